```python
import math
import jax, jax.numpy as jnp
from jax import lax
import numpy as np


D_MODEL = 1024
BATCH = 8
SEQ = 8192
DEPTH = 2
DEC_BATCH = 16
DEC_SEQ = 4096
PAST_LEN = 128

BLOCK = 128
HA = 4
DA_QK = 32
DA_V = 64
HB = 6
Q_RANK = 384
KV_RANK = 256
DB_NOPE = 64
DB_ROPE = 32
DB_V = 64
ROPE_BASE = 10000.0
HC = 6
KVH_C = 2
G_C = HC // KVH_C
DC = 64
WINDOW = 128
NUM_BUCKETS = 32
MAX_DISTANCE = 128
N_EXPERTS = 16
D_EXPERT = 1024
CAPACITY_FACTOR = 2
EPS = 1e-6

IN_SPLITS = (HA * 2 * DA_QK, HA * 2 * DA_QK, HA * DA_V, Q_RANK, KV_RANK, DB_ROPE, HC * DC, KVH_C * DC, KVH_C * DC)
IN_COLS = HA * 2 * DA_QK * 2 + HA * DA_V + Q_RANK + KV_RANK + DB_ROPE + HC * DC + 2 * KVH_C * DC
MIX_WIDTH = HA * DA_V + HB * DB_V + HC * DC

kernel_name = 'hybrid_bidir_encoder_diff_mla_swa_ec'


def _rms_norm(x, g):
    xf = x.astype(jnp.float32)
    y = xf * lax.rsqrt(jnp.mean(xf * xf, axis=-1, keepdims=True) + EPS)
    return (y * g.astype(jnp.float32)).astype(x.dtype)


def _t5_bucket(rel):
    nb = NUM_BUCKETS // 2
    max_exact = nb // 2
    ret = jnp.where(rel > 0, nb, 0)
    n = jnp.abs(rel)
    nf = jnp.maximum(n, 1).astype(jnp.float32)
    large = max_exact + (jnp.log(nf / max_exact) / math.log(MAX_DISTANCE / max_exact) * (nb - max_exact)).astype(jnp.int32)
    large = jnp.minimum(large, nb - 1)
    return ret + jnp.where(n < max_exact, n, large)


def _rotary(x, pos):
    half = DB_ROPE // 2
    inv = 1.0 / (ROPE_BASE ** (jnp.arange(half, dtype=jnp.float32) / half))
    ang = pos.astype(jnp.float32)[:, None] * inv[None, :]
    cos = jnp.cos(ang)[:, None, :].astype(x.dtype)
    sin = jnp.sin(ang)[:, None, :].astype(x.dtype)
    x1, x2 = x[..., :half], x[..., half:]
    return jnp.concatenate([x1 * cos - x2 * sin, x2 * cos + x1 * sin], axis=-1)


def _map_query_blocks(fn, q):
    B, S = q.shape[0], q.shape[1]
    nb = S // BLOCK
    qb = jnp.moveaxis(q.reshape((B, nb, BLOCK) + q.shape[2:]), 1, 0)
    out = lax.map(lambda args: fn(args[0], args[1]), (qb, jnp.arange(nb, dtype=jnp.int32)))
    out = jnp.moveaxis(out, 0, 1)
    return out.reshape((B, S) + out.shape[3:])


def _diff_attention(qa, ka, va, rel_bias, qn, kn, lq1, lk1, lq2, lk2, subln, lam_init):
    B, S = qa.shape[0], qa.shape[1]
    q = _rms_norm(qa.reshape(B, S, HA, 2, DA_QK), qn)
    k = _rms_norm(ka.reshape(B, S, HA, 2, DA_QK), kn)
    v = va.reshape(B, S, HA, DA_V)
    lam = (jnp.exp(jnp.sum(lq1.astype(jnp.float32) * lk1.astype(jnp.float32)))
           - jnp.exp(jnp.sum(lq2.astype(jnp.float32) * lk2.astype(jnp.float32))) + lam_init)
    table = rel_bias[:, :HA]
    kpos = jnp.arange(S, dtype=jnp.int32)
    scale = DA_QK ** -0.5

    def block(qb, n):
        qpos = n * BLOCK + jnp.arange(BLOCK, dtype=jnp.int32)
        bias = jnp.transpose(table[_t5_bucket(kpos[None, :] - qpos[:, None])], (2, 0, 1)).astype(jnp.float32)
        logits = jnp.einsum('bqhcd,bkhcd->bhcqk', qb, k).astype(jnp.float32) * scale + bias[None, :, None]
        p = jax.nn.softmax(logits, axis=-1)
        a = p[:, :, 0] - lam * p[:, :, 1]
        return jnp.einsum('bhqk,bkhd->bqhd', a.astype(v.dtype), v)

    o = _map_query_blocks(block, q)
    o = _rms_norm(o, subln) * (1.0 - lam_init)
    return o.reshape(B, S, HA * DA_V)


def _mla(cq, ckv, krope, g_cq, w_uq, g_ckv, w_ukv, qn, kn):
    B, S = cq.shape[0], cq.shape[1]
    qf = (_rms_norm(cq, g_cq) @ w_uq).reshape(B, S, HB, DB_NOPE + DB_ROPE)
    kv = (_rms_norm(ckv, g_ckv) @ w_ukv).reshape(B, S, HB, DB_NOPE + DB_V)
    k_nope, v = kv[..., :DB_NOPE], kv[..., DB_NOPE:]
    k_r = jnp.broadcast_to(krope[:, :, None, :], (B, S, HB, DB_ROPE))
    q = _rms_norm(qf, qn)
    k = _rms_norm(jnp.concatenate([k_nope, k_r], axis=-1), kn)
    pos = jnp.arange(S, dtype=jnp.int32)
    q = jnp.concatenate([q[..., :DB_NOPE], _rotary(q[..., DB_NOPE:], pos)], axis=-1)
    k = jnp.concatenate([k[..., :DB_NOPE], _rotary(k[..., DB_NOPE:], pos)], axis=-1)
    scale = (DB_NOPE + DB_ROPE) ** -0.5

    def block(qb, n):
        logits = jnp.einsum('bqhd,bkhd->bhqk', qb, k).astype(jnp.float32) * scale
        p = jax.nn.softmax(logits, axis=-1)
        return jnp.einsum('bhqk,bkhd->bqhd', p.astype(v.dtype), v)

    o = _map_query_blocks(block, q)
    return o.reshape(B, S, HB * DB_V)


def _window_attention(qc, kc, vc, rel_bias, qn, kn, sink):
    B, S = qc.shape[0], qc.shape[1]
    nb = S // BLOCK
    q = _rms_norm(qc.reshape(B, nb, BLOCK, KVH_C, G_C, DC), qn)
    k = _rms_norm(kc.reshape(B, S, KVH_C, DC), kn)
    v = vc.reshape(B, S, KVH_C, DC)
    pad = ((0, 0), (BLOCK, BLOCK), (0, 0), (0, 0))
    kp = jnp.pad(k, pad).reshape(B, nb + 2, BLOCK, KVH_C, DC)
    vp = jnp.pad(v, pad).reshape(B, nb + 2, BLOCK, KVH_C, DC)
    kb = jnp.concatenate([kp[:, :-2], kp[:, 1:-1], kp[:, 2:]], axis=2)
    vb = jnp.concatenate([vp[:, :-2], vp[:, 1:-1], vp[:, 2:]], axis=2)
    rel = jnp.arange(3 * BLOCK, dtype=jnp.int32)[None, :] - BLOCK - jnp.arange(BLOCK, dtype=jnp.int32)[:, None]
    bias = jnp.transpose(rel_bias[:, HA:][_t5_bucket(rel)], (2, 0, 1)).reshape(KVH_C, G_C, BLOCK, 3 * BLOCK).astype(jnp.float32)
    logits = jnp.einsum('bnqhgd,bnjhd->bnhgqj', q, kb).astype(jnp.float32) * (DC ** -0.5) + bias
    band = jnp.abs(rel) <= WINDOW
    kpos = jnp.arange(nb, dtype=jnp.int32)[:, None] * BLOCK - BLOCK + jnp.arange(3 * BLOCK, dtype=jnp.int32)[None, :]
    valid = (kpos >= 0) & (kpos < S)
    mask = band[None, None, None, None] & valid[None, :, None, None, None, :]
    logits = jnp.where(mask, logits, -jnp.inf)
    s = sink.astype(jnp.float32).reshape(KVH_C, G_C, 1, 1)
    m = jnp.maximum(jnp.max(logits, axis=-1, keepdims=True), s)
    p = jnp.exp(logits - m)
    p = p / (jnp.sum(p, axis=-1, keepdims=True) + jnp.exp(s - m))
    o = jnp.einsum('bnhgqj,bnjhd->bnqhgd', p.astype(vb.dtype), vb)
    return o.reshape(B, S, HC * DC)


def _expert_choice_ffn(h, w_router, w_gate, w_up, w_down):
    B, S, D = h.shape
    T = B * S
    cap = CAPACITY_FACTOR * T // N_EXPERTS
    xt = h.reshape(T, D)
    aff = jax.nn.softmax((xt @ w_router).astype(jnp.float32), axis=-1)
    gate, idx = lax.top_k(aff.T, cap)
    xe = xt[idx]
    hid = jax.nn.silu(jnp.einsum('ecd,edf->ecf', xe, w_gate)) * jnp.einsum('ecd,edf->ecf', xe, w_up)
    ye = jnp.einsum('ecf,efd->ecd', hid, w_down) * gate[..., None].astype(h.dtype)
    out = jnp.zeros((T, D), h.dtype).at[idx.reshape(-1)].add(ye.reshape(-1, D))
    return out.reshape(B, S, D)


def _trunk(x, rel_bias, norm_mix_g, w_in, qn_a, kn_a, lam_q1, lam_k1, lam_q2, lam_k2, subln_a,
           g_cq, w_uq, g_ckv, w_ukv, qn_b, kn_b, qn_c, kn_c, sink_c, w_out,
           norm_ffn_g, w_router, w_gate, w_up, w_down):
    splits = np.cumsum(IN_SPLITS)[:-1].tolist()
    for l in range(DEPTH):
        lam_init = 0.8 - 0.6 * math.exp(-0.3 * l)
        h = _rms_norm(x, norm_mix_g[l])
        proj = h @ w_in[l]
        qa, ka, va, cq, ckv, krope, qc, kc, vc = jnp.split(proj, splits, axis=-1)
        ya = _diff_attention(qa, ka, va, rel_bias, qn_a[l], kn_a[l], lam_q1[l], lam_k1[l], lam_q2[l], lam_k2[l], subln_a[l], lam_init)
        yb = _mla(cq, ckv, krope, g_cq[l], w_uq[l], g_ckv[l], w_ukv[l], qn_b[l], kn_b[l])
        yc = _window_attention(qc, kc, vc, rel_bias, qn_c[l], kn_c[l], sink_c[l])
        x = x + jnp.concatenate([ya, yb, yc], axis=-1) @ w_out[l]
        x = x + _expert_choice_ffn(_rms_norm(x, norm_ffn_g[l]), w_router[l], w_gate[l], w_up[l], w_down[l])
    return x


def setup_inputs(seed: int = 0) -> dict:
    key = jax.random.key(seed)
    ks = jax.random.split(key, 32)
    f32 = jnp.float32
    L, D = DEPTH, D_MODEL

    def nrm(k, shape, scale):
        return jax.random.normal(k, shape, f32) * scale

    def gain(k, shape):
        return 1.0 + 0.02 * jax.random.normal(k, shape, f32)

    return {
        'x_prompt': nrm(ks[0], (BATCH, SEQ, D), 1.0),
        'x_sample': nrm(ks[1], (DEC_BATCH, DEC_SEQ, D), 1.0),
        'rel_bias': nrm(ks[2], (NUM_BUCKETS, HA + HC), 0.2),
        'norm_mix_g': gain(ks[3], (L, D)),
        'w_in': nrm(ks[4], (L, D, IN_COLS), D ** -0.5),
        'qn_a': gain(ks[5], (L, DA_QK)),
        'kn_a': gain(ks[6], (L, DA_QK)),
        'lam_q1': nrm(ks[7], (L, DA_QK), 0.1),
        'lam_k1': nrm(ks[8], (L, DA_QK), 0.1),
        'lam_q2': nrm(ks[9], (L, DA_QK), 0.1),
        'lam_k2': nrm(ks[10], (L, DA_QK), 0.1),
        'subln_a': gain(ks[11], (L, DA_V)),
        'g_cq': gain(ks[12], (L, Q_RANK)),
        'w_uq': nrm(ks[13], (L, Q_RANK, HB * (DB_NOPE + DB_ROPE)), Q_RANK ** -0.5),
        'g_ckv': gain(ks[14], (L, KV_RANK)),
        'w_ukv': nrm(ks[15], (L, KV_RANK, HB * (DB_NOPE + DB_V)), KV_RANK ** -0.5),
        'qn_b': gain(ks[16], (L, DB_NOPE + DB_ROPE)),
        'kn_b': gain(ks[17], (L, DB_NOPE + DB_ROPE)),
        'qn_c': gain(ks[18], (L, DC)),
        'kn_c': gain(ks[19], (L, DC)),
        'sink_c': nrm(ks[20], (L, HC), 0.5),
        'w_out': nrm(ks[21], (L, MIX_WIDTH, D), MIX_WIDTH ** -0.5),
        'norm_ffn_g': gain(ks[22], (L, D)),
        'w_router': nrm(ks[23], (L, D, N_EXPERTS), D ** -0.5),
        'w_gate': nrm(ks[24], (L, N_EXPERTS, D, D_EXPERT), D ** -0.5),
        'w_up': nrm(ks[25], (L, N_EXPERTS, D, D_EXPERT), D ** -0.5),
        'w_down': nrm(ks[26], (L, N_EXPERTS, D_EXPERT, D), D_EXPERT ** -0.5),
    }


def reference(x_prompt, x_sample, rel_bias, norm_mix_g, w_in, qn_a, kn_a, lam_q1, lam_k1, lam_q2, lam_k2, subln_a,
              g_cq, w_uq, g_ckv, w_ukv, qn_b, kn_b, qn_c, kn_c, sink_c, w_out,
              norm_ffn_g, w_router, w_gate, w_up, w_down):
    y_prompt = _trunk(x_prompt, rel_bias, norm_mix_g, w_in, qn_a, kn_a, lam_q1, lam_k1, lam_q2, lam_k2, subln_a,
                      g_cq, w_uq, g_ckv, w_ukv, qn_b, kn_b, qn_c, kn_c, sink_c, w_out,
                      norm_ffn_g, w_router, w_gate, w_up, w_down)
    y_sample = _trunk(x_sample, rel_bias, norm_mix_g, w_in, qn_a, kn_a, lam_q1, lam_k1, lam_q2, lam_k2, subln_a,
                      g_cq, w_uq, g_ckv, w_ukv, qn_b, kn_b, qn_c, kn_c, sink_c, w_out,
                      norm_ffn_g, w_router, w_gate, w_up, w_down)
    return (y_prompt, y_sample)
```

```python
import functools
import math

import jax
import jax.numpy as jnp
import numpy as np
from jax import lax
from jax.experimental import pallas as pl
from jax.experimental.pallas import tpu as pltpu

D_MODEL = 1024
BLOCK = 128
HA, DA_QK, DA_V = 4, 32, 64
HB, Q_RANK, KV_RANK, DB_NOPE, DB_ROPE, DB_V = 6, 384, 256, 64, 32, 64
ROPE_BASE = 10000.0
HC, KVH_C, DC, WINDOW = 6, 2, 64, 128
G_C = HC // KVH_C
NUM_BUCKETS, MAX_DISTANCE = 32, 128
N_EXPERTS, D_EXPERT, CAPACITY_FACTOR = 16, 1024, 2
EPS = 1e-6
DEPTH = 2

DB_QK = DB_NOPE + DB_ROPE
DB_PAD = 128
ROPE_HALF = DB_ROPE // 2
IN_SPLITS = (HA * 2 * DA_QK, HA * 2 * DA_QK, HA * DA_V, Q_RANK, KV_RANK, DB_ROPE, HC * DC, KVH_C * DC, KVH_C * DC)
IN_OFFS = tuple(int(v) for v in np.cumsum((0,) + IN_SPLITS))
IN_COLS = IN_OFFS[-1]
LOG2E = 1.4426950408889634
NEG_BIG = -1e30

F32 = jnp.float32
BF16 = jnp.bfloat16

TM_IN = 512
TM_OUT = 512
BQ_A = 256
BK_A = 256
BQ_B = 256
BK_B = 256
KW_C = 3 * BLOCK
TT_MOE = 1024
CH_MOE = 128
VMEM_LIMIT = 56 * 1024 * 1024


def _nt_dot(a, b):
    return lax.dot_general(a, b, (((1,), (1,)), ((), ())), preferred_element_type=F32)


def _dot(a, b):
    return jnp.dot(a, b, preferred_element_type=F32)


def _group_norm_rows(y, gs, n_valid):
    rows, tm = y.shape
    y3 = y.reshape(rows // gs, gs, tm)
    ms = jnp.sum(y3 * y3, axis=1, keepdims=True) * (1.0 / n_valid)
    return (y3 * lax.rsqrt(ms + EPS)).reshape(rows, tm)


def _in_kernel(x_ref, g_ref, winT_ref, wuqT_ref, wukT_ref, wuvT_ref,
               gqa_ref, gka_ref, gcq_ref, gckv_ref, gqb_ref, gkb_ref, gqc_ref, gkc_ref,
               cos_ref, sin_ref,
               qaT_ref, ka_ref, vaT_ref, qbT_ref, kb_ref, vbT_ref, qcT_ref, kc_ref, vcT_ref):
    x = x_ref[...]
    h = x * lax.rsqrt(jnp.mean(x * x, axis=-1, keepdims=True) + EPS) * g_ref[...]
    hb = h.astype(BF16)

    def seg(i):
        return _nt_dot(winT_ref[IN_OFFS[i]:IN_OFFS[i + 1], :], hb)

    cos = cos_ref[...]
    sin = sin_ref[...]

    qaT_ref[...] = (_group_norm_rows(seg(0), DA_QK, DA_QK) * gqa_ref[...]).astype(BF16)
    kaT = _group_norm_rows(seg(1), DA_QK, DA_QK) * gka_ref[...]
    ka_ref[...] = kaT.T.astype(BF16)
    vaT_ref[...] = seg(2).astype(BF16)

    def rope_store(dst_rows, y, h):
        o = h * DB_PAD
        x1 = y[o + DB_NOPE:o + DB_NOPE + ROPE_HALF]
        x2 = y[o + DB_NOPE + ROPE_HALF:o + DB_QK]
        dst_rows.append(y[o:o + DB_NOPE])
        dst_rows.append(x1 * cos - x2 * sin)
        dst_rows.append(x2 * cos + x1 * sin)
        dst_rows.append(jnp.zeros((DB_PAD - DB_QK, y.shape[1]), F32))

    cqT = seg(3)
    cqn = (cqT * lax.rsqrt(jnp.mean(cqT * cqT, axis=0, keepdims=True) + EPS) * gcq_ref[...]).astype(BF16)
    qf = _group_norm_rows(_dot(wuqT_ref[...], cqn), DB_PAD, DB_QK) * gqb_ref[...]
    rows = []
    for hh in range(HB):
        rope_store(rows, qf, hh)
    qbT_ref[...] = jnp.concatenate(rows, axis=0).astype(BF16)

    ckvT = seg(4)
    ckvn = (ckvT * lax.rsqrt(jnp.mean(ckvT * ckvT, axis=0, keepdims=True) + EPS) * gckv_ref[...]).astype(BF16)
    knope = _dot(wukT_ref[...], ckvn)
    vbT_ref[...] = _dot(wuvT_ref[...], ckvn).astype(BF16)
    krope = seg(5)
    zpad = jnp.zeros((DB_PAD - DB_QK, krope.shape[1]), F32)
    rows = []
    for hh in range(HB):
        rows += [knope[hh * DB_NOPE:(hh + 1) * DB_NOPE], krope, zpad]
    kf = _group_norm_rows(jnp.concatenate(rows, axis=0), DB_PAD, DB_QK) * gkb_ref[...]
    rows = []
    for hh in range(HB):
        rope_store(rows, kf, hh)
    kb_ref[...] = jnp.concatenate(rows, axis=0).T.astype(BF16)

    qcT_ref[...] = (_group_norm_rows(seg(6), DC, DC) * gqc_ref[...]).astype(BF16)
    kcT = _group_norm_rows(seg(7), DC, DC) * gkc_ref[...]
    kc_ref[...] = kcT.T.astype(BF16)
    vcT_ref[...] = seg(8).astype(BF16)


def _in_proj(x2, lw, S):
    T = x2.shape[0]
    tm = TM_IN
    nt = T // tm
    spb = S // tm

    def full(a):
        return pl.BlockSpec(a.shape, lambda i: (0,) * a.ndim)

    consts = [lw['g_mix'], lw['w_inT'], lw['w_uqT'], lw['w_ukT'], lw['w_uvT'],
              lw['gqa'], lw['gka'], lw['gcq'], lw['gckv'], lw['gqb'], lw['gkb'], lw['gqc'], lw['gkc']]
    in_specs = ([pl.BlockSpec((tm, D_MODEL), lambda i: (i, 0))] + [full(a) for a in consts]
                + [pl.BlockSpec((ROPE_HALF, tm), lambda i: (0, i % spb))] * 2)

    def fm(rows):
        return jax.ShapeDtypeStruct((rows, T), BF16), pl.BlockSpec((rows, tm), lambda i: (0, i))

    def tk(cols):
        return jax.ShapeDtypeStruct((T, cols), BF16), pl.BlockSpec((tm, cols), lambda i: (i, 0))

    outs = [fm(HA * 2 * DA_QK), tk(HA * 2 * DA_QK), fm(HA * DA_V),
            fm(HB * DB_PAD), tk(HB * DB_PAD), fm(HB * DB_V),
            fm(HC * DC), tk(KVH_C * DC), fm(KVH_C * DC)]
    return pl.pallas_call(
        _in_kernel,
        grid=(nt,),
        in_specs=in_specs,
        out_specs=[o[1] for o in outs],
        out_shape=[o[0] for o in outs],
        compiler_params=pltpu.CompilerParams(dimension_semantics=("arbitrary",), vmem_limit_bytes=VMEM_LIMIT),
        name="in_proj",
    )(x2, *consts, lw['cosT'], lw['sinT'])


def _attn_a_kernel(sc_ref, qT_ref, k_ref, vT_ref, near_ref, subln_ref, o_ref, w_scr, m_scr, l_scr, acc_scr, *, nk):
    qb = pl.program_id(1)
    n_comb = 2 * HA
    bq = qT_ref.shape[1]

    for hc in range(n_comb):
        g = hc // 4
        qg = qT_ref[g * 128:(g + 1) * 128, :]
        row = lax.broadcasted_iota(jnp.int32, (128, bq), 0)
        keep = (row // DA_QK) == (hc % 4)
        w_scr[hc] = jnp.where(keep, qg, jnp.zeros_like(qg))
    m_scr[...] = jnp.full(m_scr.shape, NEG_BIG, F32)
    l_scr[...] = jnp.zeros(l_scr.shape, F32)
    acc_scr[...] = jnp.zeros(acc_scr.shape, F32)

    def step(kb, mode):
        ks = pl.multiple_of(kb * BK_A, BK_A)
        for hc in range(n_comb):
            h = hc // 2
            g = hc // 4
            s = _dot(k_ref[pl.ds(ks, BK_A), g * 128:(g + 1) * 128], w_scr[hc])
            if mode == 0:
                s = s + near_ref[kb - qb + 1, h]
                c = 0.0
            else:
                c = sc_ref[h] if mode < 0 else sc_ref[HA + h]
            cmax = jnp.max(s, axis=0, keepdims=True) + c
            m_old = m_scr[hc]
            m_new = jnp.maximum(m_old, cmax)
            alpha = jnp.exp2(m_old - m_new)
            p = jnp.exp2(s - (m_new - c))
            l_scr[hc] = alpha * l_scr[hc] + jnp.sum(p, axis=0, keepdims=True)
            pv = _dot(vT_ref[h * DA_V:(h + 1) * DA_V, pl.ds(ks, BK_A)], p.astype(BF16))
            acc_scr[hc] = alpha * acc_scr[hc] + pv
            m_scr[hc] = m_new

    lo = jnp.maximum(qb - 1, 0)
    hi = jnp.minimum(qb + 2, nk)
    lax.fori_loop(0, lo, lambda kb, c: (step(kb, -1), c)[1], 0)
    lax.fori_loop(lo, hi, lambda kb, c: (step(kb, 0), c)[1], 0)
    lax.fori_loop(hi, nk, lambda kb, c: (step(kb, 1), c)[1], 0)

    lam = sc_ref[2 * HA]
    outs = []
    for h in range(HA):
        a = acc_scr[2 * h] / l_scr[2 * h] - lam * (acc_scr[2 * h + 1] / l_scr[2 * h + 1])
        a = a * lax.rsqrt(jnp.mean(a * a, axis=0, keepdims=True) + EPS)
        outs.append(a)
    o = jnp.concatenate(outs, axis=0) * subln_ref[...]
    o_ref[...] = o.T.astype(BF16)


def _attn_a(qT, k, vT, lw, B, S):
    nq = S // BQ_A
    nk = S // BK_A
    n_comb = 2 * HA
    return pl.pallas_call(
        functools.partial(_attn_a_kernel, nk=nk),
        grid=(B, nq),
        in_specs=[
            pl.BlockSpec(memory_space=pltpu.SMEM),
            pl.BlockSpec((n_comb * DA_QK, BQ_A), lambda b, q: (0, b * nq + q)),
            pl.BlockSpec((S, n_comb * DA_QK), lambda b, q: (b, 0)),
            pl.BlockSpec((HA * DA_V, S), lambda b, q: (0, b)),
            pl.BlockSpec(lw['near_a'].shape, lambda b, q: (0, 0, 0, 0)),
            pl.BlockSpec((HA * DA_V, 1), lambda b, q: (0, 0)),
        ],
        out_specs=pl.BlockSpec((BQ_A, HA * DA_V), lambda b, q: (b * nq + q, 0)),
        out_shape=jax.ShapeDtypeStruct((B * S, HA * DA_V), BF16),
        scratch_shapes=[
            pltpu.VMEM((n_comb, 128, BQ_A), BF16),
            pltpu.VMEM((n_comb, 1, BQ_A), F32),
            pltpu.VMEM((n_comb, 1, BQ_A), F32),
            pltpu.VMEM((n_comb, DA_V, BQ_A), F32),
        ],
        compiler_params=pltpu.CompilerParams(dimension_semantics=("arbitrary", "arbitrary"),
                                             vmem_limit_bytes=VMEM_LIMIT),
        name="attn_a",
    )(lw['sc_a'], qT, k, vT, lw['near_a'], lw['subln'])


def _attn_b_kernel(qT_ref, k_ref, vT_ref, o_ref, m_scr, l_scr, acc_scr, *, nk):
    m_scr[...] = jnp.full(m_scr.shape, NEG_BIG, F32)
    l_scr[...] = jnp.zeros(l_scr.shape, F32)
    acc_scr[...] = jnp.zeros(acc_scr.shape, F32)

    def step(kb, carry):
        ks = pl.multiple_of(kb * BK_B, BK_B)
        for h in range(HB):
            s = _dot(k_ref[pl.ds(ks, BK_B), h * DB_PAD:(h + 1) * DB_PAD], qT_ref[h * DB_PAD:(h + 1) * DB_PAD, :])
            m_old = m_scr[h]
            m_new = jnp.maximum(m_old, jnp.max(s, axis=0, keepdims=True))
            alpha = jnp.exp2(m_old - m_new)
            p = jnp.exp2(s - m_new)
            l_scr[h] = alpha * l_scr[h] + jnp.sum(p, axis=0, keepdims=True)
            pv = _dot(vT_ref[h * DB_V:(h + 1) * DB_V, pl.ds(ks, BK_B)], p.astype(BF16))
            acc_scr[h] = alpha * acc_scr[h] + pv
            m_scr[h] = m_new
        return carry

    lax.fori_loop(0, nk, step, 0)
    o = jnp.concatenate([acc_scr[h] / l_scr[h] for h in range(HB)], axis=0)
    o_ref[...] = o.T.astype(BF16)


def _attn_b(qT, k, vT, B, S):
    nq = S // BQ_B
    nk = S // BK_B
    return pl.pallas_call(
        functools.partial(_attn_b_kernel, nk=nk),
        grid=(B, nq),
        in_specs=[
            pl.BlockSpec((HB * DB_PAD, BQ_B), lambda b, q: (0, b * nq + q)),
            pl.BlockSpec((S, HB * DB_PAD), lambda b, q: (b, 0)),
            pl.BlockSpec((HB * DB_V, S), lambda b, q: (0, b)),
        ],
        out_specs=pl.BlockSpec((BQ_B, HB * DB_V), lambda b, q: (b * nq + q, 0)),
        out_shape=jax.ShapeDtypeStruct((B * S, HB * DB_V), BF16),
        scratch_shapes=[
            pltpu.VMEM((HB, 1, BQ_B), F32),
            pltpu.VMEM((HB, 1, BQ_B), F32),
            pltpu.VMEM((HB, DB_V, BQ_B), F32),
        ],
        compiler_params=pltpu.CompilerParams(dimension_semantics=("arbitrary", "arbitrary"),
                                             vmem_limit_bytes=VMEM_LIMIT),
        name="attn_b",
    )(qT, k, vT)


def _attn_c_kernel(sc_ref, qT_ref, k_ref, vT_ref, bm_ref, o_ref, *, nb, S):
    n = pl.program_id(1)
    start = pl.multiple_of(jnp.clip((n - 1) * BLOCK, 0, S - KW_C), BLOCK)
    case = jnp.where(n == 0, 0, jnp.where(n == nb - 1, 2, 1))
    kwin = k_ref[pl.ds(start, KW_C), :]
    row = lax.broadcasted_iota(jnp.int32, (KVH_C * DC, BLOCK), 0)
    outs = []
    for h in range(HC):
        j = h // G_C
        qh = qT_ref[h * DC:(h + 1) * DC, :]
        zz = jnp.zeros_like(qh)
        w = jnp.concatenate([qh, zz] if j == 0 else [zz, qh], axis=0)
        s = _dot(kwin, w) + bm_ref[case, h]
        sink = sc_ref[h]
        m = jnp.maximum(jnp.max(s, axis=0, keepdims=True), sink)
        p = jnp.exp2(s - m)
        den = jnp.sum(p, axis=0, keepdims=True) + jnp.exp2(sink - m)
        pv = _dot(vT_ref[j * DC:(j + 1) * DC, pl.ds(start, KW_C)], p.astype(BF16))
        outs.append(pv / den)
    del row
    o_ref[...] = jnp.concatenate(outs, axis=0).T.astype(BF16)


def _attn_c(qT, k, vT, lw, B, S):
    nb = S // BLOCK
    assert nb >= 3
    return pl.pallas_call(
        functools.partial(_attn_c_kernel, nb=nb, S=S),
        grid=(B, nb),
        in_specs=[
            pl.BlockSpec(memory_space=pltpu.SMEM),
            pl.BlockSpec((HC * DC, BLOCK), lambda b, n: (0, b * nb + n)),
            pl.BlockSpec((S, KVH_C * DC), lambda b, n: (b, 0)),
            pl.BlockSpec((KVH_C * DC, S), lambda b, n: (0, b)),
            pl.BlockSpec(lw['bm_c'].shape, lambda b, n: (0, 0, 0, 0)),
        ],
        out_specs=pl.BlockSpec((BLOCK, HC * DC), lambda b, n: (b * nb + n, 0)),
        out_shape=jax.ShapeDtypeStruct((B * S, HC * DC), BF16),
        compiler_params=pltpu.CompilerParams(dimension_semantics=("arbitrary", "arbitrary"),
                                             vmem_limit_bytes=VMEM_LIMIT),
        name="attn_c",
    )(lw['sc_c'], qT, k, vT, lw['bm_c'])


def _out_kernel(x_ref, ya_ref, yb_ref, yc_ref, wo_ref, g_ref, wrT_ref, xn_ref, h2_ref, affT_ref):
    na, nb_ = HA * DA_V, HA * DA_V + HB * DB_V
    xn = (x_ref[...] + _dot(ya_ref[...], wo_ref[0:na, :]) + _dot(yb_ref[...], wo_ref[na:nb_, :])
          + _dot(yc_ref[...], wo_ref[nb_:, :]))
    xn_ref[...] = xn
    h2 = (xn * lax.rsqrt(jnp.mean(xn * xn, axis=-1, keepdims=True) + EPS) * g_ref[...]).astype(BF16)
    h2_ref[...] = h2
    lg = _nt_dot(wrT_ref[...], h2)
    e = jnp.exp(lg - jnp.max(lg, axis=0, keepdims=True))
    affT_ref[...] = e / jnp.sum(e, axis=0, keepdims=True)


def _out_proj(x2, ya, yb, yc, lw):
    T = x2.shape[0]
    tm = TM_OUT

    def full(a):
        return pl.BlockSpec(a.shape, lambda i: (0,) * a.ndim)

    return pl.pallas_call(
        _out_kernel,
        grid=(T // tm,),
        in_specs=[pl.BlockSpec((tm, D_MODEL), lambda i: (i, 0)),
                  pl.BlockSpec((tm, ya.shape[1]), lambda i: (i, 0)),
                  pl.BlockSpec((tm, yb.shape[1]), lambda i: (i, 0)),
                  pl.BlockSpec((tm, yc.shape[1]), lambda i: (i, 0)),
                  full(lw['w_out']), full(lw['g_ffn']), full(lw['w_rT'])],
        out_specs=[pl.BlockSpec((tm, D_MODEL), lambda i: (i, 0)),
                   pl.BlockSpec((tm, D_MODEL), lambda i: (i, 0)),
                   pl.BlockSpec((N_EXPERTS, tm), lambda i: (0, i))],
        out_shape=[jax.ShapeDtypeStruct((T, D_MODEL), F32),
                   jax.ShapeDtypeStruct((T, D_MODEL), BF16),
                   jax.ShapeDtypeStruct((N_EXPERTS, T), F32)],
        compiler_params=pltpu.CompilerParams(dimension_semantics=("arbitrary",), vmem_limit_bytes=VMEM_LIMIT),
        name="out_proj",
    )(x2, ya, yb, yc, lw['w_out'], lw['g_ffn'], lw['w_rT'])


def _topk_kernel(aff_ref, thr_ref, tcut_ref, *, cap, T):
    capf = float(cap)

    def bits():
        return lax.bitcast_convert_type(aff_ref[...], jnp.int32)

    def count(mask):
        return jnp.sum(jnp.where(mask, 1.0, 0.0), axis=1, keepdims=True)

    def vbody(i, v):
        cand = v | jnp.left_shift(jnp.int32(1), 30 - i)
        return jnp.where(count(bits() >= cand) >= capf, cand, v)

    thr = lax.fori_loop(0, 31, vbody, jnp.zeros((N_EXPERTS, 1), jnp.int32))
    need = capf - count(bits() > thr)
    nbits = max(1, int(math.ceil(math.log2(T))))

    def tbody(i, c):
        cand = c | jnp.left_shift(jnp.int32(1), nbits - 1 - i)
        idx = lax.broadcasted_iota(jnp.int32, (N_EXPERTS, T), 1)
        f = count((bits() == thr) & (idx < cand))
        return jnp.where(f < need, cand, c)

    tcut = lax.fori_loop(0, nbits, tbody, jnp.zeros((N_EXPERTS, 1), jnp.int32))
    thr_ref[...] = jnp.broadcast_to(thr, thr_ref.shape)
    tcut_ref[...] = jnp.broadcast_to(tcut, tcut_ref.shape)


def _topk_thresholds(affT):
    E, T = affT.shape
    cap = CAPACITY_FACTOR * T // N_EXPERTS
    return pl.pallas_call(
        functools.partial(_topk_kernel, cap=cap, T=T),
        grid=(1,),
        in_specs=[pl.BlockSpec((E, T), lambda i: (0, 0))],
        out_specs=[pl.BlockSpec((E, 128), lambda i: (0, 0))] * 2,
        out_shape=[jax.ShapeDtypeStruct((E, 128), jnp.int32)] * 2,
        compiler_params=pltpu.CompilerParams(dimension_semantics=("arbitrary",), vmem_limit_bytes=VMEM_LIMIT),
        name="topk_thr",
    )(affT)


def _moe_kernel(x_ref, h2_ref, affT_ref, thr_ref, tcut_ref, tri_ref, wg_ref, wu_ref, wd_ref, o_ref, pos_scr):
    i = pl.program_id(0)
    e = pl.program_id(1)
    tt = h2_ref.shape[0]

    @pl.when(e == 0)
    def _():
        b = lax.bitcast_convert_type(affT_ref[...], jnp.int32)
        thr = thr_ref[:, 0:1]
        tc = tcut_ref[:, 0:1]
        tg = i * tt + lax.broadcasted_iota(jnp.int32, b.shape, 1)
        sel = (b > thr) | ((b == thr) & (tg <= tc))
        pos = _dot(jnp.where(sel, 1.0, 0.0).astype(BF16), tri_ref[...])
        pos_scr[...] = jnp.where(sel, pos, -1.0)
        o_ref[...] = x_ref[...]

    posm = pos_scr[pl.ds(e, 1), :]
    gate = affT_ref[pl.ds(e, 1), :]
    n_sel = jnp.sum(jnp.where(posm >= 0.0, 1, 0))
    n_ch = (n_sel + CH_MOE - 1) // CH_MOE

    def chunk(c, carry):
        rel = posm - (c * CH_MOE).astype(F32)
        jj = lax.broadcasted_iota(jnp.int32, (CH_MOE, tt), 0).astype(F32)
        hit = jnp.broadcast_to(rel, (CH_MOE, tt)) == jj
        xe = _dot(jnp.where(hit, 1.0, 0.0).astype(BF16), h2_ref[...]).astype(BF16)
        g = _dot(xe, wg_ref[...])
        u = _dot(xe, wu_ref[...])
        hid = (g * jax.nn.sigmoid(g) * u).astype(BF16)
        ye = _dot(hid, wd_ref[...])
        gc = jnp.sum(jnp.where(hit, jnp.broadcast_to(gate, hit.shape), 0.0), axis=1, keepdims=True)
        yeb = (ye * gc).astype(BF16)
        relT = jnp.broadcast_to(rel, (CH_MOE, tt)).T
        ll = lax.broadcasted_iota(jnp.int32, (tt, CH_MOE), 1).astype(F32)
        hitT = jnp.where(relT == ll, 1.0, 0.0).astype(BF16)
        o_ref[...] += _dot(hitT, yeb)
        return carry

    lax.fori_loop(0, n_ch, chunk, 0)


def _moe(xn, h2, affT, thr, tcut, lw):
    T = xn.shape[0]
    tt = TT_MOE
    return pl.pallas_call(
        _moe_kernel,
        grid=(T // tt, N_EXPERTS),
        in_specs=[
            pl.BlockSpec((tt, D_MODEL), lambda i, e: (i, 0)),
            pl.BlockSpec((tt, D_MODEL), lambda i, e: (i, 0)),
            pl.BlockSpec((N_EXPERTS, tt), lambda i, e: (0, i)),
            pl.BlockSpec((N_EXPERTS, 128), lambda i, e: (0, 0)),
            pl.BlockSpec((N_EXPERTS, 128), lambda i, e: (0, 0)),
            pl.BlockSpec((tt, tt), lambda i, e: (0, 0)),
            pl.BlockSpec((None, D_MODEL, D_EXPERT), lambda i, e: (e, 0, 0)),
            pl.BlockSpec((None, D_MODEL, D_EXPERT), lambda i, e: (e, 0, 0)),
            pl.BlockSpec((None, D_EXPERT, D_MODEL), lambda i, e: (e, 0, 0)),
        ],
        out_specs=pl.BlockSpec((tt, D_MODEL), lambda i, e: (i, 0)),
        out_shape=jax.ShapeDtypeStruct((T, D_MODEL), F32),
        scratch_shapes=[pltpu.VMEM((N_EXPERTS, tt), F32)],
        compiler_params=pltpu.CompilerParams(dimension_semantics=("arbitrary", "arbitrary"),
                                             vmem_limit_bytes=VMEM_LIMIT),
        name="moe_ffn",
    )(xn, h2, affT, thr, tcut, lw['tri'], lw['w_gate'], lw['w_up'], lw['w_down'])


def _t5_bucket(rel):
    nb = NUM_BUCKETS // 2
    max_exact = nb // 2
    ret = jnp.where(rel > 0, nb, 0)
    n = jnp.abs(rel)
    nf = jnp.maximum(n, 1).astype(jnp.float32)
    large = max_exact + (jnp.log(nf / max_exact) / math.log(MAX_DISTANCE / max_exact) * (nb - max_exact)).astype(jnp.int32)
    large = jnp.minimum(large, nb - 1)
    return ret + jnp.where(n < max_exact, n, large)


def _col(v):
    return v.astype(F32).reshape(-1, 1)


def _pad_heads(w, n_heads, d, d_pad):
    k = w.shape[1]
    w3 = w.reshape(n_heads, d, k)
    return jnp.pad(w3, ((0, 0), (0, d_pad - d), (0, 0))).reshape(n_heads * d_pad, k)


def _prep_shared(rel_bias):
    rb = rel_bias.astype(F32) * LOG2E
    kl = jnp.arange(BK_A, dtype=jnp.int32)[:, None]
    ql = jnp.arange(BQ_A, dtype=jnp.int32)[None, :]
    tiles = []
    for d in (-1, 0, 1):
        bucket = _t5_bucket(d * BK_A + kl - ql)
        tiles.append(jnp.transpose(rb[:, :HA][bucket], (2, 0, 1)))
    near_a = jnp.stack(tiles, axis=0)
    far = _t5_bucket(jnp.array([-(1 << 20), 1 << 20], dtype=jnp.int32))
    sc_far = jnp.concatenate([rb[far[0], :HA], rb[far[1], :HA]])
    klc = jnp.arange(KW_C, dtype=jnp.int32)[:, None]
    qlc = jnp.arange(BLOCK, dtype=jnp.int32)[None, :]
    tiles = []
    for off in (0, BLOCK, 2 * BLOCK):
        rel = klc - off - qlc
        bias = jnp.transpose(rb[:, HA:][_t5_bucket(rel)], (2, 0, 1))
        tiles.append(jnp.where((jnp.abs(rel) <= WINDOW)[None], bias, NEG_BIG))
    bm_c = jnp.stack(tiles, axis=0)
    return near_a, sc_far, bm_c


def _rope_tables(S):
    inv = 1.0 / (ROPE_BASE ** (jnp.arange(ROPE_HALF, dtype=jnp.float32) / ROPE_HALF))
    ang = jnp.arange(S, dtype=jnp.int32).astype(jnp.float32)[:, None] * inv[None, :]
    return jnp.cos(ang).T, jnp.sin(ang).T


def _prep_layer(l, p, shared):
    near_a, sc_far, bm_c = shared
    lam_init = 0.8 - 0.6 * math.exp(-0.3 * l)
    lam = (jnp.exp(jnp.sum(p['lam_q1'][l].astype(F32) * p['lam_k1'][l].astype(F32)))
           - jnp.exp(jnp.sum(p['lam_q2'][l].astype(F32) * p['lam_k2'][l].astype(F32))) + lam_init)
    w_uq = p['w_uq'][l]
    w_ukv = p['w_ukv'][l].reshape(KV_RANK, HB, DB_NOPE + DB_V)
    sa = DA_QK ** -0.5 * LOG2E
    sb = DB_QK ** -0.5 * LOG2E
    scc = DC ** -0.5 * LOG2E
    zpad = jnp.zeros((DB_PAD - DB_QK,), F32)
    lw = {
        'g_mix': p['norm_mix_g'][l].astype(F32).reshape(1, D_MODEL),
        'w_inT': p['w_in'][l].T.astype(BF16),
        'w_uqT': _pad_heads(w_uq.T, HB, DB_QK, DB_PAD).astype(BF16),
        'w_ukT': w_ukv[:, :, :DB_NOPE].reshape(KV_RANK, HB * DB_NOPE).T.astype(BF16),
        'w_uvT': w_ukv[:, :, DB_NOPE:].reshape(KV_RANK, HB * DB_V).T.astype(BF16),
        'gqa': _col(jnp.tile(p['qn_a'][l], 2 * HA) * sa),
        'gka': _col(jnp.tile(p['kn_a'][l], 2 * HA)),
        'gcq': _col(p['g_cq'][l]),
        'gckv': _col(p['g_ckv'][l]),
        'gqb': _col(jnp.tile(jnp.concatenate([p['qn_b'][l].astype(F32) * sb, zpad]), HB)),
        'gkb': _col(jnp.tile(jnp.concatenate([p['kn_b'][l].astype(F32), zpad]), HB)),
        'gqc': _col(jnp.tile(p['qn_c'][l], HC) * scc),
        'gkc': _col(jnp.tile(p['kn_c'][l], KVH_C)),
        'near_a': near_a,
        'sc_a': jnp.concatenate([sc_far, lam.reshape(1)]).astype(F32),
        'subln': _col(jnp.tile(p['subln_a'][l], HA) * (1.0 - lam_init)),
        'bm_c': bm_c,
        'sc_c': p['sink_c'][l].astype(F32) * LOG2E,
        'w_out': p['w_out'][l].astype(BF16),
        'g_ffn': p['norm_ffn_g'][l].astype(F32).reshape(1, D_MODEL),
        'w_rT': p['w_router'][l].T.astype(BF16),
        'w_gate': p['w_gate'][l].astype(BF16),
        'w_up': p['w_up'][l].astype(BF16),
        'w_down': p['w_down'][l].astype(BF16),
        'tri': jnp.triu(jnp.ones((TT_MOE, TT_MOE), F32), k=1).astype(BF16),
    }
    return lw


def _trunk(x, layers):
    B, S, D = x.shape
    x2 = x.reshape(B * S, D)
    cosT, sinT = _rope_tables(S)
    for lw in layers:
        lw = dict(lw, cosT=cosT, sinT=sinT)
        qaT, ka, vaT, qbT, kb, vbT, qcT, kc, vcT = _in_proj(x2, lw, S)
        ya = _attn_a(qaT, ka, vaT, lw, B, S)
        yb = _attn_b(qbT, kb, vbT, B, S)
        yc = _attn_c(qcT, kc, vcT, lw, B, S)
        xn, h2, affT = _out_proj(x2, ya, yb, yc, lw)
        thr, tcut = _topk_thresholds(affT)
        x2 = _moe(xn, h2, affT, thr, tcut, lw)
    return x2.reshape(B, S, D)


def kernel(x_prompt, x_sample, rel_bias, norm_mix_g, w_in, qn_a, kn_a, lam_q1, lam_k1, lam_q2, lam_k2, subln_a, g_cq, w_uq, g_ckv, w_ukv, qn_b, kn_b, qn_c, kn_c, sink_c, w_out, norm_ffn_g, w_router, w_gate, w_up, w_down):
    p = dict(norm_mix_g=norm_mix_g, w_in=w_in, qn_a=qn_a, kn_a=kn_a, lam_q1=lam_q1, lam_k1=lam_k1,
             lam_q2=lam_q2, lam_k2=lam_k2, subln_a=subln_a, g_cq=g_cq, w_uq=w_uq, g_ckv=g_ckv, w_ukv=w_ukv,
             qn_b=qn_b, kn_b=kn_b, qn_c=qn_c, kn_c=kn_c, sink_c=sink_c, w_out=w_out, norm_ffn_g=norm_ffn_g,
             w_router=w_router, w_gate=w_gate, w_up=w_up, w_down=w_down)
    shared = _prep_shared(rel_bias)
    layers = [_prep_layer(l, p, shared) for l in range(w_in.shape[0])]
    return (_trunk(x_prompt, layers), _trunk(x_sample, layers))
```

```python
import functools
import math

import jax
import jax.numpy as jnp
import numpy as np
from jax import lax
from jax.experimental import pallas as pl
from jax.experimental.pallas import tpu as pltpu

D_MODEL = 1024
BLOCK = 128
HA, DA_QK, DA_V = 4, 32, 64
HB, Q_RANK, KV_RANK, DB_NOPE, DB_ROPE, DB_V = 6, 384, 256, 64, 32, 64
ROPE_BASE = 10000.0
HC, KVH_C, DC, WINDOW = 6, 2, 64, 128
G_C = HC // KVH_C
NUM_BUCKETS, MAX_DISTANCE = 32, 128
N_EXPERTS, D_EXPERT, CAPACITY_FACTOR = 16, 1024, 2
EPS = 1e-6
DEPTH = 2

DB_QK = DB_NOPE + DB_ROPE
DB_PAD = 128
ROPE_HALF = DB_ROPE // 2
IN_SPLITS = (HA * 2 * DA_QK, HA * 2 * DA_QK, HA * DA_V, Q_RANK, KV_RANK, DB_ROPE, HC * DC, KVH_C * DC, KVH_C * DC)
IN_OFFS = tuple(int(v) for v in np.cumsum((0,) + IN_SPLITS))
IN_COLS = IN_OFFS[-1]
LOG2E = 1.4426950408889634
NEG_BIG = -1e30

F32 = jnp.float32
BF16 = jnp.bfloat16

TM_IN = 512
TM_OUT = 512
BQ_A = 512
BK_A = 256
NEAR_A = BQ_A // BK_A + 2
assert BQ_A % BK_A == 0 and BK_A >= MAX_DISTANCE
BQ_B = 512
BK_B = 256
KW_C = 3 * BLOCK
TT_MOE = 1024
CH_MOE = 128
PREFETCH = 3
VMEM_LIMIT = 56 * 1024 * 1024


def _nt_dot(a, b):
    return lax.dot_general(a, b, (((1,), (1,)), ((), ())), preferred_element_type=F32)


def _dot(a, b):
    return jnp.dot(a, b, preferred_element_type=F32)


def _group_norm_rows(y, gs, n_valid):
    rows, tm = y.shape
    y3 = y.reshape(rows // gs, gs, tm)
    ms = jnp.sum(y3 * y3, axis=1, keepdims=True) * (1.0 / n_valid)
    return (y3 * lax.rsqrt(ms + EPS)).reshape(rows, tm)


def _in_kernel(x_ref, g_ref, winT_ref, wuqT_ref, wukT_ref, wuvT_ref,
               gqa_ref, gka_ref, gcq_ref, gckv_ref, gqb_ref, gkb_ref, gqc_ref, gkc_ref,
               cos_ref, sin_ref,
               qaT_ref, ka_ref, vaT_ref, qbT_ref, kb_ref, vbT_ref, qcT_ref, kc_ref, vcT_ref):
    x = x_ref[...]
    h = x * lax.rsqrt(jnp.mean(x * x, axis=-1, keepdims=True) + EPS) * g_ref[...]
    hb = h.astype(BF16)

    def seg(i):
        return _nt_dot(winT_ref[IN_OFFS[i]:IN_OFFS[i + 1], :], hb)

    cos = cos_ref[...]
    sin = sin_ref[...]

    qaT_ref[...] = (_group_norm_rows(seg(0), DA_QK, DA_QK) * gqa_ref[...]).astype(BF16)
    kaT = _group_norm_rows(seg(1), DA_QK, DA_QK) * gka_ref[...]
    ka_ref[...] = kaT.T.astype(BF16)
    vaT_ref[...] = seg(2).astype(BF16)

    def rope_store(dst_rows, y, h):
        o = h * DB_PAD
        x1 = y[o + DB_NOPE:o + DB_NOPE + ROPE_HALF]
        x2 = y[o + DB_NOPE + ROPE_HALF:o + DB_QK]
        dst_rows.append(y[o:o + DB_NOPE])
        dst_rows.append(x1 * cos - x2 * sin)
        dst_rows.append(x2 * cos + x1 * sin)
        dst_rows.append(jnp.zeros((DB_PAD - DB_QK, y.shape[1]), F32))

    cqT = seg(3)
    cqn = (cqT * lax.rsqrt(jnp.mean(cqT * cqT, axis=0, keepdims=True) + EPS) * gcq_ref[...]).astype(BF16)
    qf = _group_norm_rows(_dot(wuqT_ref[...], cqn), DB_PAD, DB_QK) * gqb_ref[...]
    rows = []
    for hh in range(HB):
        rope_store(rows, qf, hh)
    qbT_ref[...] = jnp.concatenate(rows, axis=0).astype(BF16)

    ckvT = seg(4)
    ckvn = (ckvT * lax.rsqrt(jnp.mean(ckvT * ckvT, axis=0, keepdims=True) + EPS) * gckv_ref[...]).astype(BF16)
    knope = _dot(wukT_ref[...], ckvn)
    vbT_ref[...] = _dot(wuvT_ref[...], ckvn).astype(BF16)
    krope = seg(5)
    zpad = jnp.zeros((DB_PAD - DB_QK, krope.shape[1]), F32)
    rows = []
    for hh in range(HB):
        rows += [knope[hh * DB_NOPE:(hh + 1) * DB_NOPE], krope, zpad]
    kf = _group_norm_rows(jnp.concatenate(rows, axis=0), DB_PAD, DB_QK) * gkb_ref[...]
    rows = []
    for hh in range(HB):
        rope_store(rows, kf, hh)
    kb_ref[...] = jnp.concatenate(rows, axis=0).T.astype(BF16)

    qcT_ref[...] = (_group_norm_rows(seg(6), DC, DC) * gqc_ref[...]).astype(BF16)
    kcT = _group_norm_rows(seg(7), DC, DC) * gkc_ref[...]
    kc_ref[...] = kcT.T.astype(BF16)
    vcT_ref[...] = seg(8).astype(BF16)


def _in_proj(x2, lw, S):
    T = x2.shape[0]
    tm = TM_IN
    nt = T // tm
    spb = S // tm

    def full(a):
        return pl.BlockSpec(a.shape, lambda i: (0,) * a.ndim)

    consts = [lw['g_mix'], lw['w_inT'], lw['w_uqT'], lw['w_ukT'], lw['w_uvT'],
              lw['gqa'], lw['gka'], lw['gcq'], lw['gckv'], lw['gqb'], lw['gkb'], lw['gqc'], lw['gkc']]
    in_specs = ([pl.BlockSpec((tm, D_MODEL), lambda i: (i, 0))] + [full(a) for a in consts]
                + [pl.BlockSpec((ROPE_HALF, tm), lambda i: (0, i % spb))] * 2)

    def fm(rows):
        return jax.ShapeDtypeStruct((rows, T), BF16), pl.BlockSpec((rows, tm), lambda i: (0, i))

    def tk(cols):
        return jax.ShapeDtypeStruct((T, cols), BF16), pl.BlockSpec((tm, cols), lambda i: (i, 0))

    outs = [fm(HA * 2 * DA_QK), tk(HA * 2 * DA_QK), fm(HA * DA_V),
            fm(HB * DB_PAD), tk(HB * DB_PAD), fm(HB * DB_V),
            fm(HC * DC), tk(KVH_C * DC), fm(KVH_C * DC)]
    return pl.pallas_call(
        _in_kernel,
        grid=(nt,),
        in_specs=in_specs,
        out_specs=[o[1] for o in outs],
        out_shape=[o[0] for o in outs],
        compiler_params=pltpu.CompilerParams(dimension_semantics=("arbitrary",), vmem_limit_bytes=VMEM_LIMIT),
        name="in_proj",
    )(x2, *consts, lw['cosT'], lw['sinT'])


def _attn_a_kernel(sc_ref, qT_ref, k_ref, vT_ref, near_ref, subln_ref, o_ref, w_scr, m_scr, l_scr, acc_scr, *, nk):
    qb = pl.program_id(1)
    n_comb = 2 * HA
    bq = qT_ref.shape[1]
    near_lo = (BQ_A // BK_A) * qb - 1

    for hc in range(n_comb):
        g = hc // 4
        qg = qT_ref[g * 128:(g + 1) * 128, :]
        row = lax.broadcasted_iota(jnp.int32, (128, bq), 0)
        keep = (row // DA_QK) == (hc % 4)
        w_scr[hc] = jnp.where(keep, qg, jnp.zeros_like(qg))
    m_scr[...] = jnp.full(m_scr.shape, NEG_BIG, F32)
    l_scr[...] = jnp.zeros(l_scr.shape, F32)
    acc_scr[...] = jnp.zeros(acc_scr.shape, F32)

    def step(kb, mode):
        ks = pl.multiple_of(kb * BK_A, BK_A)

        def qk(hc):
            g = hc // 4
            return _dot(k_ref[pl.ds(ks, BK_A), g * 128:(g + 1) * 128], w_scr[hc])

        ahead = [qk(i) for i in range(PREFETCH)]
        for hc in range(n_comb):
            h = hc // 2
            s = ahead.pop(0)
            if hc + PREFETCH < n_comb:
                ahead.append(qk(hc + PREFETCH))
            c = sc_ref[(1 - mode) * HA + h]
            cmax = jnp.max(s, axis=0, keepdims=True) + c
            m_old = m_scr[hc]
            m_new = jnp.maximum(m_old, cmax)
            alpha = jnp.exp2(m_old - m_new)
            p = jnp.exp2(s - (m_new - c))
            if mode == 0:
                p = p * near_ref[kb - near_lo, h]
            l_scr[hc] = alpha * l_scr[hc] + jnp.sum(p, axis=0, keepdims=True)
            pv = _dot(vT_ref[h * DA_V:(h + 1) * DA_V, pl.ds(ks, BK_A)], p.astype(BF16))
            acc_scr[hc] = alpha * acc_scr[hc] + pv
            m_scr[hc] = m_new

    lo = jnp.maximum(near_lo, 0)
    hi = jnp.minimum(near_lo + NEAR_A, nk)
    lax.fori_loop(0, lo, lambda kb, c: (step(kb, -1), c)[1], 0)
    lax.fori_loop(lo, hi, lambda kb, c: (step(kb, 0), c)[1], 0)
    lax.fori_loop(hi, nk, lambda kb, c: (step(kb, 1), c)[1], 0)

    lam = sc_ref[3 * HA]
    outs = []
    for h in range(HA):
        a = acc_scr[2 * h] / l_scr[2 * h] - lam * (acc_scr[2 * h + 1] / l_scr[2 * h + 1])
        a = a * lax.rsqrt(jnp.mean(a * a, axis=0, keepdims=True) + EPS)
        outs.append(a)
    o = jnp.concatenate(outs, axis=0) * subln_ref[...]
    o_ref[...] = o.T.astype(BF16)


def _attn_a(qT, k, vT, lw, B, S):
    nq = S // BQ_A
    nk = S // BK_A
    n_comb = 2 * HA
    return pl.pallas_call(
        functools.partial(_attn_a_kernel, nk=nk),
        grid=(B, nq),
        in_specs=[
            pl.BlockSpec(memory_space=pltpu.SMEM),
            pl.BlockSpec((n_comb * DA_QK, BQ_A), lambda b, q: (0, b * nq + q)),
            pl.BlockSpec((S, n_comb * DA_QK), lambda b, q: (b, 0)),
            pl.BlockSpec((HA * DA_V, S), lambda b, q: (0, b)),
            pl.BlockSpec(lw['near_a'].shape, lambda b, q: (0, 0, 0, 0)),
            pl.BlockSpec((HA * DA_V, 1), lambda b, q: (0, 0)),
        ],
        out_specs=pl.BlockSpec((BQ_A, HA * DA_V), lambda b, q: (b * nq + q, 0)),
        out_shape=jax.ShapeDtypeStruct((B * S, HA * DA_V), BF16),
        scratch_shapes=[
            pltpu.VMEM((n_comb, 128, BQ_A), BF16),
            pltpu.VMEM((n_comb, 1, BQ_A), F32),
            pltpu.VMEM((n_comb, 1, BQ_A), F32),
            pltpu.VMEM((n_comb, DA_V, BQ_A), F32),
        ],
        compiler_params=pltpu.CompilerParams(dimension_semantics=("arbitrary", "arbitrary"),
                                             vmem_limit_bytes=VMEM_LIMIT),
        name="attn_a",
    )(lw['sc_a'], qT, k, vT, lw['near_a'], lw['subln'])


def _attn_b_kernel(qT_ref, k_ref, vT_ref, o_ref, m_scr, l_scr, acc_scr, *, nk):
    m_scr[...] = jnp.full(m_scr.shape, NEG_BIG, F32)
    l_scr[...] = jnp.zeros(l_scr.shape, F32)
    acc_scr[...] = jnp.zeros(acc_scr.shape, F32)

    def step(kb, carry):
        ks = pl.multiple_of(kb * BK_B, BK_B)

        def qk(h):
            return _dot(k_ref[pl.ds(ks, BK_B), h * DB_PAD:(h + 1) * DB_PAD], qT_ref[h * DB_PAD:(h + 1) * DB_PAD, :])

        ahead = [qk(i) for i in range(PREFETCH)]
        for h in range(HB):
            s = ahead.pop(0)
            if h + PREFETCH < HB:
                ahead.append(qk(h + PREFETCH))
            m_old = m_scr[h]
            m_new = jnp.maximum(m_old, jnp.max(s, axis=0, keepdims=True))
            alpha = jnp.exp2(m_old - m_new)
            p = jnp.exp2(s - m_new)
            l_scr[h] = alpha * l_scr[h] + jnp.sum(p, axis=0, keepdims=True)
            pv = _dot(vT_ref[h * DB_V:(h + 1) * DB_V, pl.ds(ks, BK_B)], p.astype(BF16))
            acc_scr[h] = alpha * acc_scr[h] + pv
            m_scr[h] = m_new
        return carry

    lax.fori_loop(0, nk, step, 0)
    o = jnp.concatenate([acc_scr[h] / l_scr[h] for h in range(HB)], axis=0)
    o_ref[...] = o.T.astype(BF16)


def _attn_b(qT, k, vT, B, S):
    nq = S // BQ_B
    nk = S // BK_B
    return pl.pallas_call(
        functools.partial(_attn_b_kernel, nk=nk),
        grid=(B, nq),
        in_specs=[
            pl.BlockSpec((HB * DB_PAD, BQ_B), lambda b, q: (0, b * nq + q)),
            pl.BlockSpec((S, HB * DB_PAD), lambda b, q: (b, 0)),
            pl.BlockSpec((HB * DB_V, S), lambda b, q: (0, b)),
        ],
        out_specs=pl.BlockSpec((BQ_B, HB * DB_V), lambda b, q: (b * nq + q, 0)),
        out_shape=jax.ShapeDtypeStruct((B * S, HB * DB_V), BF16),
        scratch_shapes=[
            pltpu.VMEM((HB, 1, BQ_B), F32),
            pltpu.VMEM((HB, 1, BQ_B), F32),
            pltpu.VMEM((HB, DB_V, BQ_B), F32),
        ],
        compiler_params=pltpu.CompilerParams(dimension_semantics=("arbitrary", "arbitrary"),
                                             vmem_limit_bytes=VMEM_LIMIT),
        name="attn_b",
    )(qT, k, vT)


def _attn_c_kernel(sc_ref, qT_ref, k_ref, vT_ref, bm_ref, o_ref, *, nb, S):
    n = pl.program_id(1)
    start = pl.multiple_of(jnp.clip((n - 1) * BLOCK, 0, S - KW_C), BLOCK)
    case = jnp.where(n == 0, 0, jnp.where(n == nb - 1, 2, 1))
    kwin = k_ref[pl.ds(start, KW_C), :]
    row = lax.broadcasted_iota(jnp.int32, (KVH_C * DC, BLOCK), 0)
    outs = []
    for h in range(HC):
        j = h // G_C
        qh = qT_ref[h * DC:(h + 1) * DC, :]
        zz = jnp.zeros_like(qh)
        w = jnp.concatenate([qh, zz] if j == 0 else [zz, qh], axis=0)
        s = _dot(kwin, w) + bm_ref[case, h]
        sink = sc_ref[h]
        m = jnp.maximum(jnp.max(s, axis=0, keepdims=True), sink)
        p = jnp.exp2(s - m)
        den = jnp.sum(p, axis=0, keepdims=True) + jnp.exp2(sink - m)
        pv = _dot(vT_ref[j * DC:(j + 1) * DC, pl.ds(start, KW_C)], p.astype(BF16))
        outs.append(pv / den)
    del row
    o_ref[...] = jnp.concatenate(outs, axis=0).T.astype(BF16)


def _attn_c(qT, k, vT, lw, B, S):
    nb = S // BLOCK
    assert nb >= 3
    return pl.pallas_call(
        functools.partial(_attn_c_kernel, nb=nb, S=S),
        grid=(B, nb),
        in_specs=[
            pl.BlockSpec(memory_space=pltpu.SMEM),
            pl.BlockSpec((HC * DC, BLOCK), lambda b, n: (0, b * nb + n)),
            pl.BlockSpec((S, KVH_C * DC), lambda b, n: (b, 0)),
            pl.BlockSpec((KVH_C * DC, S), lambda b, n: (0, b)),
            pl.BlockSpec(lw['bm_c'].shape, lambda b, n: (0, 0, 0, 0)),
        ],
        out_specs=pl.BlockSpec((BLOCK, HC * DC), lambda b, n: (b * nb + n, 0)),
        out_shape=jax.ShapeDtypeStruct((B * S, HC * DC), BF16),
        compiler_params=pltpu.CompilerParams(dimension_semantics=("arbitrary", "arbitrary"),
                                             vmem_limit_bytes=VMEM_LIMIT),
        name="attn_c",
    )(lw['sc_c'], qT, k, vT, lw['bm_c'])


def _out_kernel(x_ref, ya_ref, yb_ref, yc_ref, wo_ref, g_ref, wrT_ref, xn_ref, h2_ref, affT_ref):
    na, nb_ = HA * DA_V, HA * DA_V + HB * DB_V
    xn = (x_ref[...] + _dot(ya_ref[...], wo_ref[0:na, :]) + _dot(yb_ref[...], wo_ref[na:nb_, :])
          + _dot(yc_ref[...], wo_ref[nb_:, :]))
    xn_ref[...] = xn
    h2 = (xn * lax.rsqrt(jnp.mean(xn * xn, axis=-1, keepdims=True) + EPS) * g_ref[...]).astype(BF16)
    h2_ref[...] = h2
    lg = _nt_dot(wrT_ref[...], h2)
    e = jnp.exp(lg - jnp.max(lg, axis=0, keepdims=True))
    affT_ref[...] = e / jnp.sum(e, axis=0, keepdims=True)


def _out_proj(x2, ya, yb, yc, lw):
    T = x2.shape[0]
    tm = TM_OUT

    def full(a):
        return pl.BlockSpec(a.shape, lambda i: (0,) * a.ndim)

    return pl.pallas_call(
        _out_kernel,
        grid=(T // tm,),
        in_specs=[pl.BlockSpec((tm, D_MODEL), lambda i: (i, 0)),
                  pl.BlockSpec((tm, ya.shape[1]), lambda i: (i, 0)),
                  pl.BlockSpec((tm, yb.shape[1]), lambda i: (i, 0)),
                  pl.BlockSpec((tm, yc.shape[1]), lambda i: (i, 0)),
                  full(lw['w_out']), full(lw['g_ffn']), full(lw['w_rT'])],
        out_specs=[pl.BlockSpec((tm, D_MODEL), lambda i: (i, 0)),
                   pl.BlockSpec((tm, D_MODEL), lambda i: (i, 0)),
                   pl.BlockSpec((N_EXPERTS, tm), lambda i: (0, i))],
        out_shape=[jax.ShapeDtypeStruct((T, D_MODEL), F32),
                   jax.ShapeDtypeStruct((T, D_MODEL), BF16),
                   jax.ShapeDtypeStruct((N_EXPERTS, T), F32)],
        compiler_params=pltpu.CompilerParams(dimension_semantics=("arbitrary",), vmem_limit_bytes=VMEM_LIMIT),
        name="out_proj",
    )(x2, ya, yb, yc, lw['w_out'], lw['g_ffn'], lw['w_rT'])


def _topk_kernel(aff_ref, thr_ref, tcut_ref, *, cap, T):
    capf = float(cap)

    def bits():
        return lax.bitcast_convert_type(aff_ref[...], jnp.int32)

    def count(mask):
        return jnp.sum(jnp.where(mask, 1.0, 0.0), axis=1, keepdims=True)

    def vbody(i, v):
        cand = v | jnp.left_shift(jnp.int32(1), 30 - i)
        return jnp.where(count(bits() >= cand) >= capf, cand, v)

    thr = lax.fori_loop(0, 31, vbody, jnp.zeros((N_EXPERTS, 1), jnp.int32))
    need = capf - count(bits() > thr)
    nbits = max(1, int(math.ceil(math.log2(T))))

    def tbody(i, c):
        cand = c | jnp.left_shift(jnp.int32(1), nbits - 1 - i)
        idx = lax.broadcasted_iota(jnp.int32, (N_EXPERTS, T), 1)
        f = count((bits() == thr) & (idx < cand))
        return jnp.where(f < need, cand, c)

    tcut = lax.fori_loop(0, nbits, tbody, jnp.zeros((N_EXPERTS, 1), jnp.int32))
    thr_ref[...] = jnp.broadcast_to(thr, thr_ref.shape)
    tcut_ref[...] = jnp.broadcast_to(tcut, tcut_ref.shape)


def _topk_thresholds(affT):
    E, T = affT.shape
    cap = CAPACITY_FACTOR * T // N_EXPERTS
    return pl.pallas_call(
        functools.partial(_topk_kernel, cap=cap, T=T),
        grid=(1,),
        in_specs=[pl.BlockSpec((E, T), lambda i: (0, 0))],
        out_specs=[pl.BlockSpec((E, 128), lambda i: (0, 0))] * 2,
        out_shape=[jax.ShapeDtypeStruct((E, 128), jnp.int32)] * 2,
        compiler_params=pltpu.CompilerParams(dimension_semantics=("arbitrary",), vmem_limit_bytes=VMEM_LIMIT),
        name="topk_thr",
    )(affT)


def _moe_kernel(x_ref, h2_ref, affT_ref, thr_ref, tcut_ref, tri_ref, wg_ref, wu_ref, wd_ref, o_ref, pos_scr):
    i = pl.program_id(0)
    e = pl.program_id(1)
    tt = h2_ref.shape[0]

    @pl.when(e == 0)
    def _():
        b = lax.bitcast_convert_type(affT_ref[...], jnp.int32)
        thr = thr_ref[:, 0:1]
        tc = tcut_ref[:, 0:1]
        tg = i * tt + lax.broadcasted_iota(jnp.int32, b.shape, 1)
        sel = (b > thr) | ((b == thr) & (tg <= tc))
        pos = _dot(jnp.where(sel, 1.0, 0.0).astype(BF16), tri_ref[...])
        pos_scr[...] = jnp.where(sel, pos, -1.0)
        o_ref[...] = x_ref[...]

    posm = pos_scr[pl.ds(e, 1), :]
    gate = affT_ref[pl.ds(e, 1), :]
    n_sel = jnp.sum(jnp.where(posm >= 0.0, 1, 0))
    n_ch = (n_sel + CH_MOE - 1) // CH_MOE

    def chunk(c, carry):
        rel = posm - (c * CH_MOE).astype(F32)
        jj = lax.broadcasted_iota(jnp.int32, (CH_MOE, tt), 0).astype(F32)
        hit = jnp.broadcast_to(rel, (CH_MOE, tt)) == jj
        xe = _dot(jnp.where(hit, 1.0, 0.0).astype(BF16), h2_ref[...]).astype(BF16)
        g = _dot(xe, wg_ref[...])
        u = _dot(xe, wu_ref[...])
        hid = (g * jax.nn.sigmoid(g) * u).astype(BF16)
        ye = _dot(hid, wd_ref[...])
        gc = jnp.sum(jnp.where(hit, jnp.broadcast_to(gate, hit.shape), 0.0), axis=1, keepdims=True)
        yeb = (ye * gc).astype(BF16)
        relT = jnp.broadcast_to(rel, (CH_MOE, tt)).T
        ll = lax.broadcasted_iota(jnp.int32, (tt, CH_MOE), 1).astype(F32)
        hitT = jnp.where(relT == ll, 1.0, 0.0).astype(BF16)
        o_ref[...] += _dot(hitT, yeb)
        return carry

    lax.fori_loop(0, n_ch, chunk, 0)


def _moe(xn, h2, affT, thr, tcut, lw):
    T = xn.shape[0]
    tt = TT_MOE
    return pl.pallas_call(
        _moe_kernel,
        grid=(T // tt, N_EXPERTS),
        in_specs=[
            pl.BlockSpec((tt, D_MODEL), lambda i, e: (i, 0)),
            pl.BlockSpec((tt, D_MODEL), lambda i, e: (i, 0)),
            pl.BlockSpec((N_EXPERTS, tt), lambda i, e: (0, i)),
            pl.BlockSpec((N_EXPERTS, 128), lambda i, e: (0, 0)),
            pl.BlockSpec((N_EXPERTS, 128), lambda i, e: (0, 0)),
            pl.BlockSpec((tt, tt), lambda i, e: (0, 0)),
            pl.BlockSpec((None, D_MODEL, D_EXPERT), lambda i, e: (e, 0, 0)),
            pl.BlockSpec((None, D_MODEL, D_EXPERT), lambda i, e: (e, 0, 0)),
            pl.BlockSpec((None, D_EXPERT, D_MODEL), lambda i, e: (e, 0, 0)),
        ],
        out_specs=pl.BlockSpec((tt, D_MODEL), lambda i, e: (i, 0)),
        out_shape=jax.ShapeDtypeStruct((T, D_MODEL), F32),
        scratch_shapes=[pltpu.VMEM((N_EXPERTS, tt), F32)],
        compiler_params=pltpu.CompilerParams(dimension_semantics=("arbitrary", "arbitrary"),
                                             vmem_limit_bytes=VMEM_LIMIT),
        name="moe_ffn",
    )(xn, h2, affT, thr, tcut, lw['tri'], lw['w_gate'], lw['w_up'], lw['w_down'])


def _t5_bucket(rel):
    nb = NUM_BUCKETS // 2
    max_exact = nb // 2
    ret = jnp.where(rel > 0, nb, 0)
    n = jnp.abs(rel)
    nf = jnp.maximum(n, 1).astype(jnp.float32)
    large = max_exact + (jnp.log(nf / max_exact) / math.log(MAX_DISTANCE / max_exact) * (nb - max_exact)).astype(jnp.int32)
    large = jnp.minimum(large, nb - 1)
    return ret + jnp.where(n < max_exact, n, large)


def _col(v):
    return v.astype(F32).reshape(-1, 1)


def _pad_heads(w, n_heads, d, d_pad):
    k = w.shape[1]
    w3 = w.reshape(n_heads, d, k)
    return jnp.pad(w3, ((0, 0), (0, d_pad - d), (0, 0))).reshape(n_heads * d_pad, k)


def _prep_shared(rel_bias):
    rb = rel_bias.astype(F32) * LOG2E
    kl = jnp.arange(BK_A, dtype=jnp.int32)[:, None]
    ql = jnp.arange(BQ_A, dtype=jnp.int32)[None, :]
    bmax = jnp.max(rb[:, :HA], axis=0)
    tiles = []
    for d in range(-1, NEAR_A - 1):
        bucket = _t5_bucket(d * BK_A + kl - ql)
        tiles.append(jnp.exp2(jnp.transpose(rb[:, :HA][bucket], (2, 0, 1)) - bmax[:, None, None]))
    near_a = jnp.stack(tiles, axis=0)
    far = _t5_bucket(jnp.array([-(1 << 20), 1 << 20], dtype=jnp.int32))
    sc_far = jnp.concatenate([rb[far[1], :HA], bmax, rb[far[0], :HA]])
    klc = jnp.arange(KW_C, dtype=jnp.int32)[:, None]
    qlc = jnp.arange(BLOCK, dtype=jnp.int32)[None, :]
    tiles = []
    for off in (0, BLOCK, 2 * BLOCK):
        rel = klc - off - qlc
        bias = jnp.transpose(rb[:, HA:][_t5_bucket(rel)], (2, 0, 1))
        tiles.append(jnp.where((jnp.abs(rel) <= WINDOW)[None], bias, NEG_BIG))
    bm_c = jnp.stack(tiles, axis=0)
    return near_a, sc_far, bm_c


def _rope_tables(S):
    inv = 1.0 / (ROPE_BASE ** (jnp.arange(ROPE_HALF, dtype=jnp.float32) / ROPE_HALF))
    ang = jnp.arange(S, dtype=jnp.int32).astype(jnp.float32)[:, None] * inv[None, :]
    return jnp.cos(ang).T, jnp.sin(ang).T


def _prep_layer(l, p, shared):
    near_a, sc_far, bm_c = shared
    lam_init = 0.8 - 0.6 * math.exp(-0.3 * l)
    lam = (jnp.exp(jnp.sum(p['lam_q1'][l].astype(F32) * p['lam_k1'][l].astype(F32)))
           - jnp.exp(jnp.sum(p['lam_q2'][l].astype(F32) * p['lam_k2'][l].astype(F32))) + lam_init)
    w_uq = p['w_uq'][l]
    w_ukv = p['w_ukv'][l].reshape(KV_RANK, HB, DB_NOPE + DB_V)
    sa = DA_QK ** -0.5 * LOG2E
    sb = DB_QK ** -0.5 * LOG2E
    scc = DC ** -0.5 * LOG2E
    zpad = jnp.zeros((DB_PAD - DB_QK,), F32)
    lw = {
        'g_mix': p['norm_mix_g'][l].astype(F32).reshape(1, D_MODEL),
        'w_inT': p['w_in'][l].T.astype(BF16),
        'w_uqT': _pad_heads(w_uq.T, HB, DB_QK, DB_PAD).astype(BF16),
        'w_ukT': w_ukv[:, :, :DB_NOPE].reshape(KV_RANK, HB * DB_NOPE).T.astype(BF16),
        'w_uvT': w_ukv[:, :, DB_NOPE:].reshape(KV_RANK, HB * DB_V).T.astype(BF16),
        'gqa': _col(jnp.tile(p['qn_a'][l], 2 * HA) * sa),
        'gka': _col(jnp.tile(p['kn_a'][l], 2 * HA)),
        'gcq': _col(p['g_cq'][l]),
        'gckv': _col(p['g_ckv'][l]),
        'gqb': _col(jnp.tile(jnp.concatenate([p['qn_b'][l].astype(F32) * sb, zpad]), HB)),
        'gkb': _col(jnp.tile(jnp.concatenate([p['kn_b'][l].astype(F32), zpad]), HB)),
        'gqc': _col(jnp.tile(p['qn_c'][l], HC) * scc),
        'gkc': _col(jnp.tile(p['kn_c'][l], KVH_C)),
        'near_a': near_a,
        'sc_a': jnp.concatenate([sc_far, lam.reshape(1)]).astype(F32),
        'subln': _col(jnp.tile(p['subln_a'][l], HA) * (1.0 - lam_init)),
        'bm_c': bm_c,
        'sc_c': p['sink_c'][l].astype(F32) * LOG2E,
        'w_out': p['w_out'][l].astype(BF16),
        'g_ffn': p['norm_ffn_g'][l].astype(F32).reshape(1, D_MODEL),
        'w_rT': p['w_router'][l].T.astype(BF16),
        'w_gate': p['w_gate'][l].astype(BF16),
        'w_up': p['w_up'][l].astype(BF16),
        'w_down': p['w_down'][l].astype(BF16),
        'tri': jnp.triu(jnp.ones((TT_MOE, TT_MOE), F32), k=1).astype(BF16),
    }
    return lw


def _trunk(x, layers):
    B, S, D = x.shape
    x2 = x.reshape(B * S, D)
    cosT, sinT = _rope_tables(S)
    for lw in layers:
        lw = dict(lw, cosT=cosT, sinT=sinT)
        qaT, ka, vaT, qbT, kb, vbT, qcT, kc, vcT = _in_proj(x2, lw, S)
        ya = _attn_a(qaT, ka, vaT, lw, B, S)
        yb = _attn_b(qbT, kb, vbT, B, S)
        yc = _attn_c(qcT, kc, vcT, lw, B, S)
        xn, h2, affT = _out_proj(x2, ya, yb, yc, lw)
        thr, tcut = _topk_thresholds(affT)
        x2 = _moe(xn, h2, affT, thr, tcut, lw)
    return x2.reshape(B, S, D)


def kernel(x_prompt, x_sample, rel_bias, norm_mix_g, w_in, qn_a, kn_a, lam_q1, lam_k1, lam_q2, lam_k2, subln_a, g_cq, w_uq, g_ckv, w_ukv, qn_b, kn_b, qn_c, kn_c, sink_c, w_out, norm_ffn_g, w_router, w_gate, w_up, w_down):
    p = dict(norm_mix_g=norm_mix_g, w_in=w_in, qn_a=qn_a, kn_a=kn_a, lam_q1=lam_q1, lam_k1=lam_k1,
             lam_q2=lam_q2, lam_k2=lam_k2, subln_a=subln_a, g_cq=g_cq, w_uq=w_uq, g_ckv=g_ckv, w_ukv=w_ukv,
             qn_b=qn_b, kn_b=kn_b, qn_c=qn_c, kn_c=kn_c, sink_c=sink_c, w_out=w_out, norm_ffn_g=norm_ffn_g,
             w_router=w_router, w_gate=w_gate, w_up=w_up, w_down=w_down)
    shared = _prep_shared(rel_bias)
    layers = [_prep_layer(l, p, shared) for l in range(w_in.shape[0])]
    return (_trunk(x_prompt, layers), _trunk(x_sample, layers))
```

```python
import functools
import math

import jax
import jax.numpy as jnp
import numpy as np
from jax import lax
from jax.experimental import pallas as pl
from jax.experimental.pallas import tpu as pltpu

D_MODEL = 1024
BLOCK = 128
HA, DA_QK, DA_V = 4, 32, 64
HB, Q_RANK, KV_RANK, DB_NOPE, DB_ROPE, DB_V = 6, 384, 256, 64, 32, 64
ROPE_BASE = 10000.0
HC, KVH_C, DC, WINDOW = 6, 2, 64, 128
G_C = HC // KVH_C
NUM_BUCKETS, MAX_DISTANCE = 32, 128
N_EXPERTS, D_EXPERT, CAPACITY_FACTOR = 16, 1024, 2
EPS = 1e-6
DEPTH = 2

DB_QK = DB_NOPE + DB_ROPE
DB_PAD = 128
VX = 80
BOUND_LIMIT = 80.0
ROPE_HALF = DB_ROPE // 2
IN_SPLITS = (HA * 2 * DA_QK, HA * 2 * DA_QK, HA * DA_V, Q_RANK, KV_RANK, DB_ROPE, HC * DC, KVH_C * DC, KVH_C * DC)
IN_OFFS = tuple(int(v) for v in np.cumsum((0,) + IN_SPLITS))
IN_COLS = IN_OFFS[-1]
LOG2E = 1.4426950408889634
NEG_BIG = -1e30

F32 = jnp.float32
BF16 = jnp.bfloat16

TM_IN = 512
TM_OUT = 512
BQ_A = 512
BK_A = 512
NEAR_A = BQ_A // BK_A + 2
assert BQ_A % BK_A == 0 and BK_A >= MAX_DISTANCE
BQ_B = 512
BK_B = 512
KW_C = 3 * BLOCK
TT_MOE = 1024
CH_MOE = 160
CHP_MOE = 256
PREFETCH = 3
VMEM_LIMIT = 56 * 1024 * 1024


def _nt_dot(a, b):
    return lax.dot_general(a, b, (((1,), (1,)), ((), ())), preferred_element_type=F32)


def _dot(a, b):
    return jnp.dot(a, b, preferred_element_type=F32)


def _group_norm_rows(y, gs, n_valid):
    rows, tm = y.shape
    y3 = y.reshape(rows // gs, gs, tm)
    ms = jnp.sum(y3 * y3, axis=1, keepdims=True) * (1.0 / n_valid)
    return (y3 * lax.rsqrt(ms + EPS)).reshape(rows, tm)


def _in_kernel(x_ref, g_ref, winT_ref, wuqT_ref, wukT_ref, wuvT_ref,
               gqa_ref, gka_ref, gcq_ref, gckv_ref, gqb_ref, gkb_ref, gqc_ref, gkc_ref,
               cos_ref, sin_ref,
               qaT_ref, ka_ref, vaT_ref, qbT_ref, kb_ref, vbT_ref, qcT_ref, kc_ref, vcT_ref):
    x = x_ref[...]
    h = x * lax.rsqrt(jnp.mean(x * x, axis=-1, keepdims=True) + EPS) * g_ref[...]
    hb = h.astype(BF16)

    def seg(i):
        return _nt_dot(winT_ref[IN_OFFS[i]:IN_OFFS[i + 1], :], hb)

    cos = cos_ref[...]
    sin = sin_ref[...]

    qaT_ref[...] = (_group_norm_rows(seg(0), DA_QK, DA_QK) * gqa_ref[...]).astype(BF16)
    kaT = _group_norm_rows(seg(1), DA_QK, DA_QK) * gka_ref[...]
    ka_ref[...] = kaT.T.astype(BF16)
    def with_ones_rows(v, n_heads, d):
        tm = v.shape[1]
        tail = jnp.where(lax.broadcasted_iota(jnp.int32, (VX - d, tm), 0) == 0, 1.0, 0.0)
        rows = []
        for hh in range(n_heads):
            rows += [v[hh * d:(hh + 1) * d], tail]
        return jnp.concatenate(rows, axis=0).astype(BF16)

    vaT_ref[...] = with_ones_rows(seg(2), HA, DA_V)

    def rope_store(dst_rows, y, h):
        o = h * DB_PAD
        x1 = y[o + DB_NOPE:o + DB_NOPE + ROPE_HALF]
        x2 = y[o + DB_NOPE + ROPE_HALF:o + DB_QK]
        dst_rows.append(y[o:o + DB_NOPE])
        dst_rows.append(x1 * cos - x2 * sin)
        dst_rows.append(x2 * cos + x1 * sin)
        dst_rows.append(jnp.zeros((DB_PAD - DB_QK, y.shape[1]), F32))

    cqT = seg(3)
    cqn = (cqT * lax.rsqrt(jnp.mean(cqT * cqT, axis=0, keepdims=True) + EPS) * gcq_ref[...]).astype(BF16)
    qf = _group_norm_rows(_dot(wuqT_ref[...], cqn), DB_PAD, DB_QK) * gqb_ref[...]
    rows = []
    for hh in range(HB):
        rope_store(rows, qf, hh)
    qbT_ref[...] = jnp.concatenate(rows, axis=0).astype(BF16)

    ckvT = seg(4)
    ckvn = (ckvT * lax.rsqrt(jnp.mean(ckvT * ckvT, axis=0, keepdims=True) + EPS) * gckv_ref[...]).astype(BF16)
    knope = _dot(wukT_ref[...], ckvn)
    vbT_ref[...] = with_ones_rows(_dot(wuvT_ref[...], ckvn), HB, DB_V)
    krope = seg(5)
    zpad = jnp.zeros((DB_PAD - DB_QK, krope.shape[1]), F32)
    rows = []
    for hh in range(HB):
        rows += [knope[hh * DB_NOPE:(hh + 1) * DB_NOPE], krope, zpad]
    kf = _group_norm_rows(jnp.concatenate(rows, axis=0), DB_PAD, DB_QK) * gkb_ref[...]
    rows = []
    for hh in range(HB):
        rope_store(rows, kf, hh)
    kb_ref[...] = jnp.concatenate(rows, axis=0).T.astype(BF16)

    qcT_ref[...] = (_group_norm_rows(seg(6), DC, DC) * gqc_ref[...]).astype(BF16)
    kcT = _group_norm_rows(seg(7), DC, DC) * gkc_ref[...]
    kc_ref[...] = kcT.T.astype(BF16)
    vcT_ref[...] = seg(8).astype(BF16)


def _in_proj(x2, lw, S):
    T = x2.shape[0]
    tm = TM_IN
    nt = T // tm
    spb = S // tm

    def full(a):
        return pl.BlockSpec(a.shape, lambda i: (0,) * a.ndim)

    consts = [lw['g_mix'], lw['w_inT'], lw['w_uqT'], lw['w_ukT'], lw['w_uvT'],
              lw['gqa'], lw['gka'], lw['gcq'], lw['gckv'], lw['gqb'], lw['gkb'], lw['gqc'], lw['gkc']]
    in_specs = ([pl.BlockSpec((tm, D_MODEL), lambda i: (i, 0))] + [full(a) for a in consts]
                + [pl.BlockSpec((ROPE_HALF, tm), lambda i: (0, i % spb))] * 2)

    def fm(rows):
        return jax.ShapeDtypeStruct((rows, T), BF16), pl.BlockSpec((rows, tm), lambda i: (0, i))

    def tk(cols):
        return jax.ShapeDtypeStruct((T, cols), BF16), pl.BlockSpec((tm, cols), lambda i: (i, 0))

    outs = [fm(HA * 2 * DA_QK), tk(HA * 2 * DA_QK), fm(HA * VX),
            fm(HB * DB_PAD), tk(HB * DB_PAD), fm(HB * VX),
            fm(HC * DC), tk(KVH_C * DC), fm(KVH_C * DC)]
    return pl.pallas_call(
        _in_kernel,
        grid=(nt,),
        in_specs=in_specs,
        out_specs=[o[1] for o in outs],
        out_shape=[o[0] for o in outs],
        compiler_params=pltpu.CompilerParams(dimension_semantics=("arbitrary",), vmem_limit_bytes=VMEM_LIMIT),
        name="in_proj",
    )(x2, *consts, lw['cosT'], lw['sinT'])


def _attn_a_kernel(sc_ref, qT_ref, k_ref, vT_ref, near_ref, subln_ref, o_ref, w_scr, m_scr, l_scr, acc_scr, *, nk):
    qb = pl.program_id(1)
    n_comb = 2 * HA
    bq = qT_ref.shape[1]
    near_lo = (BQ_A // BK_A) * qb - 1

    for hc in range(n_comb):
        g = hc // 4
        qg = qT_ref[g * 128:(g + 1) * 128, :]
        row = lax.broadcasted_iota(jnp.int32, (128, bq), 0)
        keep = (row // DA_QK) == (hc % 4)
        w_scr[hc] = jnp.where(keep, qg, jnp.zeros_like(qg))
    m_scr[...] = jnp.full(m_scr.shape, NEG_BIG, F32)
    l_scr[...] = jnp.zeros(l_scr.shape, F32)
    acc_scr[...] = jnp.zeros(acc_scr.shape, F32)

    def step(kb, mode):
        ks = pl.multiple_of(kb * BK_A, BK_A)

        def qk(hc):
            g = hc // 4
            return _dot(k_ref[pl.ds(ks, BK_A), g * 128:(g + 1) * 128], w_scr[hc])

        ahead = [qk(i) for i in range(PREFETCH)]
        for hc in range(n_comb):
            h = hc // 2
            s = ahead.pop(0)
            if hc + PREFETCH < n_comb:
                ahead.append(qk(hc + PREFETCH))
            c = sc_ref[(1 - mode) * HA + h]
            cmax = jnp.max(s, axis=0, keepdims=True) + c
            m_old = m_scr[hc]
            m_new = jnp.maximum(m_old, cmax)
            alpha = jnp.exp2(m_old - m_new)
            p = jnp.exp2(s - (m_new - c))
            if mode == 0:
                p = p * near_ref[kb - near_lo, h]
            l_scr[hc] = alpha * l_scr[hc] + jnp.sum(p, axis=0, keepdims=True)
            pv = _dot(vT_ref[h * VX:h * VX + DA_V, pl.ds(ks, BK_A)], p.astype(BF16))
            acc_scr[hc] = alpha * acc_scr[hc] + pv
            m_scr[hc] = m_new

    lo = jnp.maximum(near_lo, 0)
    hi = jnp.minimum(near_lo + NEAR_A, nk)
    lax.fori_loop(0, lo, lambda kb, c: (step(kb, -1), c)[1], 0)
    lax.fori_loop(lo, hi, lambda kb, c: (step(kb, 0), c)[1], 0)
    lax.fori_loop(hi, nk, lambda kb, c: (step(kb, 1), c)[1], 0)

    lam = sc_ref[3 * HA]
    outs = []
    for h in range(HA):
        a = acc_scr[2 * h] / l_scr[2 * h] - lam * (acc_scr[2 * h + 1] / l_scr[2 * h + 1])
        a = a * lax.rsqrt(jnp.mean(a * a, axis=0, keepdims=True) + EPS)
        outs.append(a)
    o = jnp.concatenate(outs, axis=0) * subln_ref[...]
    o_ref[...] = o.T.astype(BF16)


def _attn_a_bounded_kernel(sc_ref, qT_ref, k_ref, vT_ref, near_ref, subln_ref, o_ref, w_scr, acc_scr, *, nk):
    qb = pl.program_id(1)
    n_comb = 2 * HA
    bq = qT_ref.shape[1]
    near_lo = (BQ_A // BK_A) * qb - 1
    for hc in range(n_comb):
        g = hc // 4
        qg = qT_ref[g * 128:(g + 1) * 128, :]
        row = lax.broadcasted_iota(jnp.int32, (128, bq), 0)
        w_scr[hc] = jnp.where((row // DA_QK) == (hc % 4), qg, jnp.zeros_like(qg))
    acc_scr[...] = jnp.zeros(acc_scr.shape, F32)

    def step(kb, near):
        ks = pl.multiple_of(kb * BK_A, BK_A)

        def qk(hc):
            g = hc // 4
            return _dot(k_ref[pl.ds(ks, BK_A), g * 128:(g + 1) * 128], w_scr[hc])

        ahead = [qk(i) for i in range(PREFETCH)]
        for hc in range(n_comb):
            h = hc // 2
            s = ahead.pop(0)
            if hc + PREFETCH < n_comb:
                ahead.append(qk(hc + PREFETCH))
            p = jnp.exp2(s)
            if near:
                p = p * near_ref[kb - near_lo, h]
            acc_scr[hc] += _dot(vT_ref[h * VX:(h + 1) * VX, pl.ds(ks, BK_A)], p.astype(BF16))

    def rescale(off):
        for hc in range(n_comb):
            acc_scr[hc] = acc_scr[hc] * sc_ref[off + hc // 2]

    lo = jnp.maximum(near_lo, 0)
    hi = jnp.minimum(near_lo + NEAR_A, nk)
    lax.fori_loop(0, lo, lambda kb, c: (step(kb, False), c)[1], 0)
    rescale(0)
    lax.fori_loop(lo, hi, lambda kb, c: (step(kb, True), c)[1], 0)
    rescale(HA)
    lax.fori_loop(hi, nk, lambda kb, c: (step(kb, False), c)[1], 0)

    lam = sc_ref[2 * HA]
    outs = []
    for h in range(HA):
        a0, a1 = acc_scr[2 * h], acc_scr[2 * h + 1]
        a = a0[0:DA_V] / a0[DA_V:DA_V + 1] - lam * (a1[0:DA_V] / a1[DA_V:DA_V + 1])
        outs.append(a * lax.rsqrt(jnp.mean(a * a, axis=0, keepdims=True) + EPS))
    o = jnp.concatenate(outs, axis=0) * subln_ref[...]
    o_ref[...] = o.T.astype(BF16)


def _attn_a(qT, k, vT, lw, B, S, bounded):
    nq = S // BQ_A
    nk = S // BK_A
    n_comb = 2 * HA
    if bounded:
        body = functools.partial(_attn_a_bounded_kernel, nk=nk)
        scratch = [pltpu.VMEM((n_comb, 128, BQ_A), BF16), pltpu.VMEM((n_comb, VX, BQ_A), F32)]
        sc, near = lw['sc_a_bounded'], lw['near_a_bounded']
    else:
        body = functools.partial(_attn_a_kernel, nk=nk)
        scratch = [pltpu.VMEM((n_comb, 128, BQ_A), BF16), pltpu.VMEM((n_comb, 1, BQ_A), F32),
                   pltpu.VMEM((n_comb, 1, BQ_A), F32), pltpu.VMEM((n_comb, DA_V, BQ_A), F32)]
        sc, near = lw['sc_a'], lw['near_a']
    return pl.pallas_call(
        body,
        grid=(B, nq),
        in_specs=[
            pl.BlockSpec(memory_space=pltpu.SMEM),
            pl.BlockSpec((n_comb * DA_QK, BQ_A), lambda b, q: (0, b * nq + q)),
            pl.BlockSpec((S, n_comb * DA_QK), lambda b, q: (b, 0)),
            pl.BlockSpec((HA * VX, S), lambda b, q: (0, b)),
            pl.BlockSpec(near.shape, lambda b, q: (0, 0, 0, 0)),
            pl.BlockSpec((HA * DA_V, 1), lambda b, q: (0, 0)),
        ],
        out_specs=pl.BlockSpec((BQ_A, HA * DA_V), lambda b, q: (b * nq + q, 0)),
        out_shape=jax.ShapeDtypeStruct((B * S, HA * DA_V), BF16),
        scratch_shapes=scratch,
        compiler_params=pltpu.CompilerParams(dimension_semantics=("arbitrary", "arbitrary"),
                                             vmem_limit_bytes=VMEM_LIMIT),
        name="attn_a_bounded" if bounded else "attn_a",
    )(sc, qT, k, vT, near, lw['subln'])


def _attn_b_kernel(qT_ref, k_ref, vT_ref, o_ref, m_scr, l_scr, acc_scr, *, nk):
    m_scr[...] = jnp.full(m_scr.shape, NEG_BIG, F32)
    l_scr[...] = jnp.zeros(l_scr.shape, F32)
    acc_scr[...] = jnp.zeros(acc_scr.shape, F32)

    def step(kb, carry):
        ks = pl.multiple_of(kb * BK_B, BK_B)

        def qk(h):
            return _dot(k_ref[pl.ds(ks, BK_B), h * DB_PAD:(h + 1) * DB_PAD], qT_ref[h * DB_PAD:(h + 1) * DB_PAD, :])

        ahead = [qk(i) for i in range(PREFETCH)]
        for h in range(HB):
            s = ahead.pop(0)
            if h + PREFETCH < HB:
                ahead.append(qk(h + PREFETCH))
            m_old = m_scr[h]
            m_new = jnp.maximum(m_old, jnp.max(s, axis=0, keepdims=True))
            alpha = jnp.exp2(m_old - m_new)
            p = jnp.exp2(s - m_new)
            l_scr[h] = alpha * l_scr[h] + jnp.sum(p, axis=0, keepdims=True)
            pv = _dot(vT_ref[h * VX:h * VX + DB_V, pl.ds(ks, BK_B)], p.astype(BF16))
            acc_scr[h] = alpha * acc_scr[h] + pv
            m_scr[h] = m_new
        return carry

    lax.fori_loop(0, nk, step, 0)
    o = jnp.concatenate([acc_scr[h] / l_scr[h] for h in range(HB)], axis=0)
    o_ref[...] = o.T.astype(BF16)


def _attn_b_bounded_kernel(qT_ref, k_ref, vT_ref, o_ref, acc_scr, *, nk):
    acc_scr[...] = jnp.zeros(acc_scr.shape, F32)

    def step(kb, carry):
        ks = pl.multiple_of(kb * BK_B, BK_B)

        def qk(h):
            return _dot(k_ref[pl.ds(ks, BK_B), h * DB_PAD:(h + 1) * DB_PAD], qT_ref[h * DB_PAD:(h + 1) * DB_PAD, :])

        ahead = [qk(i) for i in range(PREFETCH)]
        for h in range(HB):
            s = ahead.pop(0)
            if h + PREFETCH < HB:
                ahead.append(qk(h + PREFETCH))
            p = jnp.exp2(s).astype(BF16)
            acc_scr[h] += _dot(vT_ref[h * VX:(h + 1) * VX, pl.ds(ks, BK_B)], p)
        return carry

    lax.fori_loop(0, nk, step, 0)
    o = jnp.concatenate([acc_scr[h, 0:DB_V] / acc_scr[h, DB_V:DB_V + 1] for h in range(HB)], axis=0)
    o_ref[...] = o.T.astype(BF16)


def _attn_b(qT, k, vT, B, S, bounded):
    nq = S // BQ_B
    nk = S // BK_B
    if bounded:
        body = functools.partial(_attn_b_bounded_kernel, nk=nk)
        scratch = [pltpu.VMEM((HB, VX, BQ_B), F32)]
    else:
        body = functools.partial(_attn_b_kernel, nk=nk)
        scratch = [pltpu.VMEM((HB, 1, BQ_B), F32), pltpu.VMEM((HB, 1, BQ_B), F32), pltpu.VMEM((HB, DB_V, BQ_B), F32)]
    return pl.pallas_call(
        body,
        grid=(B, nq),
        in_specs=[
            pl.BlockSpec((HB * DB_PAD, BQ_B), lambda b, q: (0, b * nq + q)),
            pl.BlockSpec((S, HB * DB_PAD), lambda b, q: (b, 0)),
            pl.BlockSpec((HB * VX, S), lambda b, q: (0, b)),
        ],
        out_specs=pl.BlockSpec((BQ_B, HB * DB_V), lambda b, q: (b * nq + q, 0)),
        out_shape=jax.ShapeDtypeStruct((B * S, HB * DB_V), BF16),
        scratch_shapes=scratch,
        compiler_params=pltpu.CompilerParams(dimension_semantics=("arbitrary", "arbitrary"),
                                             vmem_limit_bytes=VMEM_LIMIT),
        name="attn_b_bounded" if bounded else "attn_b",
    )(qT, k, vT)


def _attn_c_kernel(sc_ref, qT_ref, k_ref, vT_ref, bm_ref, o_ref, *, nb, S):
    n = pl.program_id(1)
    start = pl.multiple_of(jnp.clip((n - 1) * BLOCK, 0, S - KW_C), BLOCK)
    case = jnp.where(n == 0, 0, jnp.where(n == nb - 1, 2, 1))
    kwin = k_ref[pl.ds(start, KW_C), :]
    row = lax.broadcasted_iota(jnp.int32, (KVH_C * DC, BLOCK), 0)
    outs = []
    for h in range(HC):
        j = h // G_C
        qh = qT_ref[h * DC:(h + 1) * DC, :]
        zz = jnp.zeros_like(qh)
        w = jnp.concatenate([qh, zz] if j == 0 else [zz, qh], axis=0)
        s = _dot(kwin, w) + bm_ref[case, h]
        sink = sc_ref[h]
        m = jnp.maximum(jnp.max(s, axis=0, keepdims=True), sink)
        p = jnp.exp2(s - m)
        den = jnp.sum(p, axis=0, keepdims=True) + jnp.exp2(sink - m)
        pv = _dot(vT_ref[j * DC:(j + 1) * DC, pl.ds(start, KW_C)], p.astype(BF16))
        outs.append(pv / den)
    del row
    o_ref[...] = jnp.concatenate(outs, axis=0).T.astype(BF16)


def _attn_c(qT, k, vT, lw, B, S):
    nb = S // BLOCK
    assert nb >= 3
    return pl.pallas_call(
        functools.partial(_attn_c_kernel, nb=nb, S=S),
        grid=(B, nb),
        in_specs=[
            pl.BlockSpec(memory_space=pltpu.SMEM),
            pl.BlockSpec((HC * DC, BLOCK), lambda b, n: (0, b * nb + n)),
            pl.BlockSpec((S, KVH_C * DC), lambda b, n: (b, 0)),
            pl.BlockSpec((KVH_C * DC, S), lambda b, n: (0, b)),
            pl.BlockSpec(lw['bm_c'].shape, lambda b, n: (0, 0, 0, 0)),
        ],
        out_specs=pl.BlockSpec((BLOCK, HC * DC), lambda b, n: (b * nb + n, 0)),
        out_shape=jax.ShapeDtypeStruct((B * S, HC * DC), BF16),
        compiler_params=pltpu.CompilerParams(dimension_semantics=("arbitrary", "arbitrary"),
                                             vmem_limit_bytes=VMEM_LIMIT),
        name="attn_c",
    )(lw['sc_c'], qT, k, vT, lw['bm_c'])


def _out_kernel(x_ref, ya_ref, yb_ref, yc_ref, wo_ref, g_ref, wrT_ref, xn_ref, h2_ref, affT_ref):
    na, nb_ = HA * DA_V, HA * DA_V + HB * DB_V
    xn = (x_ref[...] + _dot(ya_ref[...], wo_ref[0:na, :]) + _dot(yb_ref[...], wo_ref[na:nb_, :])
          + _dot(yc_ref[...], wo_ref[nb_:, :]))
    xn_ref[...] = xn
    h2 = (xn * lax.rsqrt(jnp.mean(xn * xn, axis=-1, keepdims=True) + EPS) * g_ref[...]).astype(BF16)
    h2_ref[...] = h2
    lg = _nt_dot(wrT_ref[...], h2)
    e = jnp.exp(lg - jnp.max(lg, axis=0, keepdims=True))
    affT_ref[...] = e / jnp.sum(e, axis=0, keepdims=True)


def _out_proj(x2, ya, yb, yc, lw):
    T = x2.shape[0]
    tm = TM_OUT

    def full(a):
        return pl.BlockSpec(a.shape, lambda i: (0,) * a.ndim)

    return pl.pallas_call(
        _out_kernel,
        grid=(T // tm,),
        in_specs=[pl.BlockSpec((tm, D_MODEL), lambda i: (i, 0)),
                  pl.BlockSpec((tm, ya.shape[1]), lambda i: (i, 0)),
                  pl.BlockSpec((tm, yb.shape[1]), lambda i: (i, 0)),
                  pl.BlockSpec((tm, yc.shape[1]), lambda i: (i, 0)),
                  full(lw['w_out']), full(lw['g_ffn']), full(lw['w_rT'])],
        out_specs=[pl.BlockSpec((tm, D_MODEL), lambda i: (i, 0)),
                   pl.BlockSpec((tm, D_MODEL), lambda i: (i, 0)),
                   pl.BlockSpec((N_EXPERTS, tm), lambda i: (0, i))],
        out_shape=[jax.ShapeDtypeStruct((T, D_MODEL), F32),
                   jax.ShapeDtypeStruct((T, D_MODEL), BF16),
                   jax.ShapeDtypeStruct((N_EXPERTS, T), F32)],
        compiler_params=pltpu.CompilerParams(dimension_semantics=("arbitrary",), vmem_limit_bytes=VMEM_LIMIT),
        name="out_proj",
    )(x2, ya, yb, yc, lw['w_out'], lw['g_ffn'], lw['w_rT'])


def _topk_kernel(aff_ref, thr_ref, tcut_ref, *, cap, T):
    capf = float(cap)

    def bits():
        return lax.bitcast_convert_type(aff_ref[...], jnp.int32)

    def count(mask):
        return jnp.sum(jnp.where(mask, 1.0, 0.0), axis=1, keepdims=True)

    def vbody(i, v):
        cand = v | jnp.left_shift(jnp.int32(1), 30 - i)
        return jnp.where(count(bits() >= cand) >= capf, cand, v)

    thr = lax.fori_loop(0, 31, vbody, jnp.zeros((N_EXPERTS, 1), jnp.int32))
    need = capf - count(bits() > thr)
    nbits = max(1, int(math.ceil(math.log2(T))))

    def tbody(i, c):
        cand = c | jnp.left_shift(jnp.int32(1), nbits - 1 - i)
        idx = lax.broadcasted_iota(jnp.int32, (N_EXPERTS, T), 1)
        f = count((bits() == thr) & (idx < cand))
        return jnp.where(f < need, cand, c)

    tcut = lax.fori_loop(0, nbits, tbody, jnp.zeros((N_EXPERTS, 1), jnp.int32))
    thr_ref[...] = jnp.broadcast_to(thr, thr_ref.shape)
    tcut_ref[...] = jnp.broadcast_to(tcut, tcut_ref.shape)


def _topk_thresholds(affT):
    E, T = affT.shape
    cap = CAPACITY_FACTOR * T // N_EXPERTS
    return pl.pallas_call(
        functools.partial(_topk_kernel, cap=cap, T=T),
        grid=(1,),
        in_specs=[pl.BlockSpec((E, T), lambda i: (0, 0))],
        out_specs=[pl.BlockSpec((E, 128), lambda i: (0, 0))] * 2,
        out_shape=[jax.ShapeDtypeStruct((E, 128), jnp.int32)] * 2,
        compiler_params=pltpu.CompilerParams(dimension_semantics=("arbitrary",), vmem_limit_bytes=VMEM_LIMIT),
        name="topk_thr",
    )(affT)


def _moe_kernel(x_ref, h2_ref, affT_ref, thr_ref, tcut_ref, tri_ref, wg_ref, wu_ref, wd_ref, o_ref, pos_scr):
    i = pl.program_id(0)
    e = pl.program_id(1)
    tt = h2_ref.shape[0]

    @pl.when(e == 0)
    def _():
        b = lax.bitcast_convert_type(affT_ref[...], jnp.int32)
        thr = thr_ref[:, 0:1]
        tc = tcut_ref[:, 0:1]
        tg = i * tt + lax.broadcasted_iota(jnp.int32, b.shape, 1)
        sel = (b > thr) | ((b == thr) & (tg <= tc))
        pos = _dot(jnp.where(sel, 1.0, 0.0).astype(BF16), tri_ref[...])
        pos_scr[...] = jnp.where(sel, pos, -1.0)
        o_ref[...] = x_ref[...]

    posm = pos_scr[pl.ds(e, 1), :]
    gate = affT_ref[pl.ds(e, 1), :]
    n_sel = jnp.sum(jnp.where(posm >= 0.0, 1, 0))
    n_ch = (n_sel + CH_MOE - 1) // CH_MOE

    def chunk(c, carry):
        rel = posm - (c * CH_MOE).astype(F32)
        jj = lax.broadcasted_iota(jnp.int32, (CH_MOE, tt), 0).astype(F32)
        hit = jnp.broadcast_to(rel, (CH_MOE, tt)) == jj
        xe = _dot(jnp.where(hit, 1.0, 0.0).astype(BF16), h2_ref[...]).astype(BF16)
        g = _dot(xe, wg_ref[...])
        u = _dot(xe, wu_ref[...])
        hid = (g * jax.nn.sigmoid(g) * u).astype(BF16)
        ye = _dot(hid, wd_ref[...])
        gc = jnp.sum(jnp.where(hit, jnp.broadcast_to(gate, hit.shape), 0.0), axis=1, keepdims=True)
        yeb = jnp.concatenate([(ye * gc).astype(BF16), jnp.zeros((CHP_MOE - CH_MOE, ye.shape[1]), BF16)], axis=0)
        relT = jnp.broadcast_to(rel, (CHP_MOE, tt)).T
        ll = lax.broadcasted_iota(jnp.int32, (tt, CHP_MOE), 1).astype(F32)
        hitT = jnp.where(relT == ll, 1.0, 0.0).astype(BF16)
        o_ref[...] += _dot(hitT, yeb)
        return carry

    lax.fori_loop(0, n_ch, chunk, 0)


def _moe(xn, h2, affT, thr, tcut, lw):
    T = xn.shape[0]
    tt = TT_MOE
    return pl.pallas_call(
        _moe_kernel,
        grid=(T // tt, N_EXPERTS),
        in_specs=[
            pl.BlockSpec((tt, D_MODEL), lambda i, e: (i, 0)),
            pl.BlockSpec((tt, D_MODEL), lambda i, e: (i, 0)),
            pl.BlockSpec((N_EXPERTS, tt), lambda i, e: (0, i)),
            pl.BlockSpec((N_EXPERTS, 128), lambda i, e: (0, 0)),
            pl.BlockSpec((N_EXPERTS, 128), lambda i, e: (0, 0)),
            pl.BlockSpec((tt, tt), lambda i, e: (0, 0)),
            pl.BlockSpec((None, D_MODEL, D_EXPERT), lambda i, e: (e, 0, 0)),
            pl.BlockSpec((None, D_MODEL, D_EXPERT), lambda i, e: (e, 0, 0)),
            pl.BlockSpec((None, D_EXPERT, D_MODEL), lambda i, e: (e, 0, 0)),
        ],
        out_specs=pl.BlockSpec((tt, D_MODEL), lambda i, e: (i, 0)),
        out_shape=jax.ShapeDtypeStruct((T, D_MODEL), F32),
        scratch_shapes=[pltpu.VMEM((N_EXPERTS, tt), F32)],
        compiler_params=pltpu.CompilerParams(dimension_semantics=("arbitrary", "arbitrary"),
                                             vmem_limit_bytes=VMEM_LIMIT),
        name="moe_ffn",
    )(xn, h2, affT, thr, tcut, lw['tri'], lw['w_gate'], lw['w_up'], lw['w_down'])


def _t5_bucket(rel):
    nb = NUM_BUCKETS // 2
    max_exact = nb // 2
    ret = jnp.where(rel > 0, nb, 0)
    n = jnp.abs(rel)
    nf = jnp.maximum(n, 1).astype(jnp.float32)
    large = max_exact + (jnp.log(nf / max_exact) / math.log(MAX_DISTANCE / max_exact) * (nb - max_exact)).astype(jnp.int32)
    large = jnp.minimum(large, nb - 1)
    return ret + jnp.where(n < max_exact, n, large)


def _col(v):
    return v.astype(F32).reshape(-1, 1)


def _pad_heads(w, n_heads, d, d_pad):
    k = w.shape[1]
    w3 = w.reshape(n_heads, d, k)
    return jnp.pad(w3, ((0, 0), (0, d_pad - d), (0, 0))).reshape(n_heads * d_pad, k)


def _prep_shared(rel_bias):
    rb = rel_bias.astype(F32) * LOG2E
    kl = jnp.arange(BK_A, dtype=jnp.int32)[:, None]
    ql = jnp.arange(BQ_A, dtype=jnp.int32)[None, :]
    bmax = jnp.max(rb[:, :HA], axis=0)
    tiles = []
    for d in range(-1, NEAR_A - 1):
        bucket = _t5_bucket(d * BK_A + kl - ql)
        tiles.append(jnp.transpose(rb[:, :HA][bucket], (2, 0, 1)))
    near_bias = jnp.stack(tiles, axis=0)
    near_a = jnp.exp2(near_bias - bmax[None, :, None, None])
    far = _t5_bucket(jnp.array([-(1 << 20), 1 << 20], dtype=jnp.int32))
    sc_far = jnp.concatenate([rb[far[1], :HA], bmax, rb[far[0], :HA]])
    a_bounded = {
        'near': jnp.exp2(near_bias),
        'sc': jnp.concatenate([jnp.exp2(rb[far[0], :HA]), jnp.exp2(-rb[far[1], :HA])]),
        'bias_mag': jnp.max(jnp.abs(rb[:, :HA])),
    }
    klc = jnp.arange(KW_C, dtype=jnp.int32)[:, None]
    qlc = jnp.arange(BLOCK, dtype=jnp.int32)[None, :]
    tiles = []
    for off in (0, BLOCK, 2 * BLOCK):
        rel = klc - off - qlc
        bias = jnp.transpose(rb[:, HA:][_t5_bucket(rel)], (2, 0, 1))
        tiles.append(jnp.where((jnp.abs(rel) <= WINDOW)[None], bias, NEG_BIG))
    bm_c = jnp.stack(tiles, axis=0)
    return near_a, sc_far, bm_c, a_bounded


def _rope_tables(S):
    inv = 1.0 / (ROPE_BASE ** (jnp.arange(ROPE_HALF, dtype=jnp.float32) / ROPE_HALF))
    ang = jnp.arange(S, dtype=jnp.int32).astype(jnp.float32)[:, None] * inv[None, :]
    return jnp.cos(ang).T, jnp.sin(ang).T


def _logit_bounds(l, p, a_bounded, sa, sb):
    qa, ka = p['qn_a'][l].astype(F32), p['kn_a'][l].astype(F32)
    bound_a = 1.01 * DA_QK * jnp.max(jnp.abs(qa * ka)) * sa + 2.0 * a_bounded['bias_mag']
    qb, kb = p['qn_b'][l].astype(F32), p['kn_b'][l].astype(F32)
    bound_b = 1.01 * DB_QK * jnp.max(jnp.abs(qb)) * jnp.max(jnp.abs(kb)) * sb
    return bound_a <= BOUND_LIMIT, bound_b <= BOUND_LIMIT


def _prep_layer(l, p, shared):
    near_a, sc_far, bm_c, a_bounded = shared
    lam_init = 0.8 - 0.6 * math.exp(-0.3 * l)
    lam = (jnp.exp(jnp.sum(p['lam_q1'][l].astype(F32) * p['lam_k1'][l].astype(F32)))
           - jnp.exp(jnp.sum(p['lam_q2'][l].astype(F32) * p['lam_k2'][l].astype(F32))) + lam_init)
    w_uq = p['w_uq'][l]
    w_ukv = p['w_ukv'][l].reshape(KV_RANK, HB, DB_NOPE + DB_V)
    sa = DA_QK ** -0.5 * LOG2E
    sb = DB_QK ** -0.5 * LOG2E
    scc = DC ** -0.5 * LOG2E
    zpad = jnp.zeros((DB_PAD - DB_QK,), F32)
    lw = {
        'g_mix': p['norm_mix_g'][l].astype(F32).reshape(1, D_MODEL),
        'w_inT': p['w_in'][l].T.astype(BF16),
        'w_uqT': _pad_heads(w_uq.T, HB, DB_QK, DB_PAD).astype(BF16),
        'w_ukT': w_ukv[:, :, :DB_NOPE].reshape(KV_RANK, HB * DB_NOPE).T.astype(BF16),
        'w_uvT': w_ukv[:, :, DB_NOPE:].reshape(KV_RANK, HB * DB_V).T.astype(BF16),
        'gqa': _col(jnp.tile(p['qn_a'][l], 2 * HA) * sa),
        'gka': _col(jnp.tile(p['kn_a'][l], 2 * HA)),
        'gcq': _col(p['g_cq'][l]),
        'gckv': _col(p['g_ckv'][l]),
        'gqb': _col(jnp.tile(jnp.concatenate([p['qn_b'][l].astype(F32) * sb, zpad]), HB)),
        'gkb': _col(jnp.tile(jnp.concatenate([p['kn_b'][l].astype(F32), zpad]), HB)),
        'gqc': _col(jnp.tile(p['qn_c'][l], HC) * scc),
        'gkc': _col(jnp.tile(p['kn_c'][l], KVH_C)),
        'near_a': near_a,
        'sc_a': jnp.concatenate([sc_far, lam.reshape(1)]).astype(F32),
        'subln': _col(jnp.tile(p['subln_a'][l], HA) * (1.0 - lam_init)),
        'bm_c': bm_c,
        'sc_c': p['sink_c'][l].astype(F32) * LOG2E,
        'w_out': p['w_out'][l].astype(BF16),
        'g_ffn': p['norm_ffn_g'][l].astype(F32).reshape(1, D_MODEL),
        'w_rT': p['w_router'][l].T.astype(BF16),
        'w_gate': p['w_gate'][l].astype(BF16),
        'w_up': p['w_up'][l].astype(BF16),
        'w_down': p['w_down'][l].astype(BF16),
        'tri': jnp.triu(jnp.ones((TT_MOE, TT_MOE), F32), k=1).astype(BF16),
        'near_a_bounded': a_bounded['near'],
        'sc_a_bounded': jnp.concatenate([a_bounded['sc'], lam.reshape(1)]).astype(F32),
    }
    lw['bounded_a'], lw['bounded_b'] = _logit_bounds(l, p, a_bounded, sa, sb)
    return lw


def _trunk(x, layers):
    B, S, D = x.shape
    x2 = x.reshape(B * S, D)
    cosT, sinT = _rope_tables(S)
    for lw in layers:
        lw = dict(lw, cosT=cosT, sinT=sinT)
        qaT, ka, vaT, qbT, kb, vbT, qcT, kc, vcT = _in_proj(x2, lw, S)
        ya = lax.cond(lw['bounded_a'],
                      lambda q, k, v: _attn_a(q, k, v, lw, B, S, True),
                      lambda q, k, v: _attn_a(q, k, v, lw, B, S, False), qaT, ka, vaT)
        yb = lax.cond(lw['bounded_b'],
                      lambda q, k, v: _attn_b(q, k, v, B, S, True),
                      lambda q, k, v: _attn_b(q, k, v, B, S, False), qbT, kb, vbT)
        yc = _attn_c(qcT, kc, vcT, lw, B, S)
        xn, h2, affT = _out_proj(x2, ya, yb, yc, lw)
        thr, tcut = _topk_thresholds(affT)
        x2 = _moe(xn, h2, affT, thr, tcut, lw)
    return x2.reshape(B, S, D)


def kernel(x_prompt, x_sample, rel_bias, norm_mix_g, w_in, qn_a, kn_a, lam_q1, lam_k1, lam_q2, lam_k2, subln_a, g_cq, w_uq, g_ckv, w_ukv, qn_b, kn_b, qn_c, kn_c, sink_c, w_out, norm_ffn_g, w_router, w_gate, w_up, w_down):
    p = dict(norm_mix_g=norm_mix_g, w_in=w_in, qn_a=qn_a, kn_a=kn_a, lam_q1=lam_q1, lam_k1=lam_k1,
             lam_q2=lam_q2, lam_k2=lam_k2, subln_a=subln_a, g_cq=g_cq, w_uq=w_uq, g_ckv=g_ckv, w_ukv=w_ukv,
             qn_b=qn_b, kn_b=kn_b, qn_c=qn_c, kn_c=kn_c, sink_c=sink_c, w_out=w_out, norm_ffn_g=norm_ffn_g,
             w_router=w_router, w_gate=w_gate, w_up=w_up, w_down=w_down)
    shared = _prep_shared(rel_bias)
    layers = [_prep_layer(l, p, shared) for l in range(w_in.shape[0])]
    return (_trunk(x_prompt, layers), _trunk(x_sample, layers))
```

```python
import functools
import math

import jax
import jax.numpy as jnp
import numpy as np
from jax import lax
from jax.experimental import pallas as pl
from jax.experimental.pallas import tpu as pltpu

D_MODEL = 1024
BLOCK = 128
HA, DA_QK, DA_V = 4, 32, 64
HB, Q_RANK, KV_RANK, DB_NOPE, DB_ROPE, DB_V = 6, 384, 256, 64, 32, 64
ROPE_BASE = 10000.0
HC, KVH_C, DC, WINDOW = 6, 2, 64, 128
G_C = HC // KVH_C
NUM_BUCKETS, MAX_DISTANCE = 32, 128
N_EXPERTS, D_EXPERT, CAPACITY_FACTOR = 16, 1024, 2
EPS = 1e-6
DEPTH = 2

DB_QK = DB_NOPE + DB_ROPE
DB_PAD = 128
VX = 80
BOUND_LIMIT = 80.0
ROPE_HALF = DB_ROPE // 2
IN_SPLITS = (HA * 2 * DA_QK, HA * 2 * DA_QK, HA * DA_V, Q_RANK, KV_RANK, DB_ROPE, HC * DC, KVH_C * DC, KVH_C * DC)
IN_OFFS = tuple(int(v) for v in np.cumsum((0,) + IN_SPLITS))
IN_COLS = IN_OFFS[-1]
LOG2E = 1.4426950408889634
NEG_BIG = -1e30

F32 = jnp.float32
BF16 = jnp.bfloat16

TM_IN = 512
TM_OUT = 512
BQ_A = 512
BK_A = 512
NEAR_A = BQ_A // BK_A + 2
assert BQ_A % BK_A == 0 and BK_A >= MAX_DISTANCE
BQ_B = 512
BK_B = 512
UNROLL_B = 2
KW_C = 3 * BLOCK
QB_C = 256
KWB_C = QB_C + 2 * WINDOW
SUB_C = 2
TT_MOE = 1024
CH_MOE = 160
CHP_MOE = 256
PREFETCH = 3
VMEM_LIMIT = 56 * 1024 * 1024


def _nt_dot(a, b):
    return lax.dot_general(a, b, (((1,), (1,)), ((), ())), preferred_element_type=F32)


def _dot(a, b):
    return jnp.dot(a, b, preferred_element_type=F32)


def _group_norm_rows(y, gs, n_valid):
    rows, tm = y.shape
    y3 = y.reshape(rows // gs, gs, tm)
    ms = jnp.sum(y3 * y3, axis=1, keepdims=True) * (1.0 / n_valid)
    return (y3 * lax.rsqrt(ms + EPS)).reshape(rows, tm)


def _in_kernel(x_ref, g_ref, winT_ref, wuqT_ref, wukT_ref, wuvT_ref,
               gqa_ref, gka_ref, gcq_ref, gckv_ref, gqb_ref, gkb_ref, gqc_ref, gkc_ref,
               cos_ref, sin_ref,
               qaT_ref, ka_ref, vaT_ref, qbT_ref, kb_ref, vbT_ref, qcT_ref, kc_ref, vcT_ref):
    x = x_ref[...]
    h = x * lax.rsqrt(jnp.mean(x * x, axis=-1, keepdims=True) + EPS) * g_ref[...]
    hb = h.astype(BF16)

    def seg(i):
        return _nt_dot(winT_ref[IN_OFFS[i]:IN_OFFS[i + 1], :], hb)

    cos = cos_ref[...]
    sin = sin_ref[...]

    qaT_ref[...] = (_group_norm_rows(seg(0), DA_QK, DA_QK) * gqa_ref[...]).astype(BF16)
    kaT = _group_norm_rows(seg(1), DA_QK, DA_QK) * gka_ref[...]
    ka_ref[...] = kaT.T.astype(BF16)
    def with_ones_rows(v, n_heads, d):
        tm = v.shape[1]
        tail = jnp.where(lax.broadcasted_iota(jnp.int32, (VX - d, tm), 0) == 0, 1.0, 0.0)
        rows = []
        for hh in range(n_heads):
            rows += [v[hh * d:(hh + 1) * d], tail]
        return jnp.concatenate(rows, axis=0).astype(BF16)

    vaT_ref[...] = with_ones_rows(seg(2), HA, DA_V)

    def rope_store(dst_rows, y, h):
        o = h * DB_PAD
        x1 = y[o + DB_NOPE:o + DB_NOPE + ROPE_HALF]
        x2 = y[o + DB_NOPE + ROPE_HALF:o + DB_QK]
        dst_rows.append(y[o:o + DB_NOPE])
        dst_rows.append(x1 * cos - x2 * sin)
        dst_rows.append(x2 * cos + x1 * sin)
        dst_rows.append(jnp.zeros((DB_PAD - DB_QK, y.shape[1]), F32))

    cqT = seg(3)
    cqn = (cqT * lax.rsqrt(jnp.mean(cqT * cqT, axis=0, keepdims=True) + EPS) * gcq_ref[...]).astype(BF16)
    qf = _group_norm_rows(_dot(wuqT_ref[...], cqn), DB_PAD, DB_QK) * gqb_ref[...]
    rows = []
    for hh in range(HB):
        rope_store(rows, qf, hh)
    qbT_ref[...] = jnp.concatenate(rows, axis=0).astype(BF16)

    ckvT = seg(4)
    ckvn = (ckvT * lax.rsqrt(jnp.mean(ckvT * ckvT, axis=0, keepdims=True) + EPS) * gckv_ref[...]).astype(BF16)
    knope = _dot(wukT_ref[...], ckvn)
    vbT_ref[...] = with_ones_rows(_dot(wuvT_ref[...], ckvn), HB, DB_V)
    krope = seg(5)
    zpad = jnp.zeros((DB_PAD - DB_QK, krope.shape[1]), F32)
    rows = []
    for hh in range(HB):
        rows += [knope[hh * DB_NOPE:(hh + 1) * DB_NOPE], krope, zpad]
    kf = _group_norm_rows(jnp.concatenate(rows, axis=0), DB_PAD, DB_QK) * gkb_ref[...]
    rows = []
    for hh in range(HB):
        rope_store(rows, kf, hh)
    kb_ref[...] = jnp.concatenate(rows, axis=0).T.astype(BF16)

    qcT_ref[...] = (_group_norm_rows(seg(6), DC, DC) * gqc_ref[...]).astype(BF16)
    kcT = _group_norm_rows(seg(7), DC, DC) * gkc_ref[...]
    kc_ref[...] = kcT.T.astype(BF16)
    vcT_ref[...] = with_ones_rows(seg(8), KVH_C, DC)


def _in_proj(x2, lw, S):
    T = x2.shape[0]
    tm = TM_IN
    nt = T // tm
    spb = S // tm

    def full(a):
        return pl.BlockSpec(a.shape, lambda i: (0,) * a.ndim)

    consts = [lw['g_mix'], lw['w_inT'], lw['w_uqT'], lw['w_ukT'], lw['w_uvT'],
              lw['gqa'], lw['gka'], lw['gcq'], lw['gckv'], lw['gqb'], lw['gkb'], lw['gqc'], lw['gkc']]
    in_specs = ([pl.BlockSpec((tm, D_MODEL), lambda i: (i, 0))] + [full(a) for a in consts]
                + [pl.BlockSpec((ROPE_HALF, tm), lambda i: (0, i % spb))] * 2)

    def fm(rows):
        return jax.ShapeDtypeStruct((rows, T), BF16), pl.BlockSpec((rows, tm), lambda i: (0, i))

    def tk(cols):
        return jax.ShapeDtypeStruct((T, cols), BF16), pl.BlockSpec((tm, cols), lambda i: (i, 0))

    outs = [fm(HA * 2 * DA_QK), tk(HA * 2 * DA_QK), fm(HA * VX),
            fm(HB * DB_PAD), tk(HB * DB_PAD), fm(HB * VX),
            fm(HC * DC), tk(KVH_C * DC), fm(KVH_C * VX)]
    return pl.pallas_call(
        _in_kernel,
        grid=(nt,),
        in_specs=in_specs,
        out_specs=[o[1] for o in outs],
        out_shape=[o[0] for o in outs],
        compiler_params=pltpu.CompilerParams(dimension_semantics=("arbitrary",), vmem_limit_bytes=VMEM_LIMIT),
        name="in_proj",
    )(x2, *consts, lw['cosT'], lw['sinT'])


def _attn_a_kernel(sc_ref, qT_ref, k_ref, vT_ref, near_ref, subln_ref, o_ref, w_scr, m_scr, l_scr, acc_scr, *, nk):
    qb = pl.program_id(1)
    n_comb = 2 * HA
    bq = qT_ref.shape[1]
    near_lo = (BQ_A // BK_A) * qb - 1

    for hc in range(n_comb):
        g = hc // 4
        qg = qT_ref[g * 128:(g + 1) * 128, :]
        row = lax.broadcasted_iota(jnp.int32, (128, bq), 0)
        keep = (row // DA_QK) == (hc % 4)
        w_scr[hc] = jnp.where(keep, qg, jnp.zeros_like(qg))
    m_scr[...] = jnp.full(m_scr.shape, NEG_BIG, F32)
    l_scr[...] = jnp.zeros(l_scr.shape, F32)
    acc_scr[...] = jnp.zeros(acc_scr.shape, F32)

    def step(kb, mode):
        ks = pl.multiple_of(kb * BK_A, BK_A)

        def qk(hc):
            g = hc // 4
            return _dot(k_ref[pl.ds(ks, BK_A), g * 128:(g + 1) * 128], w_scr[hc])

        ahead = [qk(i) for i in range(PREFETCH)]
        for hc in range(n_comb):
            h = hc // 2
            s = ahead.pop(0)
            if hc + PREFETCH < n_comb:
                ahead.append(qk(hc + PREFETCH))
            c = sc_ref[(1 - mode) * HA + h]
            cmax = jnp.max(s, axis=0, keepdims=True) + c
            m_old = m_scr[hc]
            m_new = jnp.maximum(m_old, cmax)
            alpha = jnp.exp2(m_old - m_new)
            p = jnp.exp2(s - (m_new - c))
            if mode == 0:
                p = p * near_ref[kb - near_lo, h]
            l_scr[hc] = alpha * l_scr[hc] + jnp.sum(p, axis=0, keepdims=True)
            pv = _dot(vT_ref[h * VX:h * VX + DA_V, pl.ds(ks, BK_A)], p.astype(BF16))
            acc_scr[hc] = alpha * acc_scr[hc] + pv
            m_scr[hc] = m_new

    lo = jnp.maximum(near_lo, 0)
    hi = jnp.minimum(near_lo + NEAR_A, nk)
    lax.fori_loop(0, lo, lambda kb, c: (step(kb, -1), c)[1], 0)
    lax.fori_loop(lo, hi, lambda kb, c: (step(kb, 0), c)[1], 0)
    lax.fori_loop(hi, nk, lambda kb, c: (step(kb, 1), c)[1], 0)

    lam = sc_ref[3 * HA]
    outs = []
    for h in range(HA):
        a = acc_scr[2 * h] / l_scr[2 * h] - lam * (acc_scr[2 * h + 1] / l_scr[2 * h + 1])
        a = a * lax.rsqrt(jnp.mean(a * a, axis=0, keepdims=True) + EPS)
        outs.append(a)
    o = jnp.concatenate(outs, axis=0) * subln_ref[...]
    o_ref[...] = o.T.astype(BF16)


def _attn_a_bounded_kernel(sc_ref, qT_ref, k_ref, vT_ref, near_ref, subln_ref, o_ref, w_scr, acc_scr, *, nk):
    qb = pl.program_id(1)
    n_comb = 2 * HA
    bq = qT_ref.shape[1]
    near_lo = (BQ_A // BK_A) * qb - 1
    for hc in range(n_comb):
        g = hc // 4
        qg = qT_ref[g * 128:(g + 1) * 128, :]
        row = lax.broadcasted_iota(jnp.int32, (128, bq), 0)
        w_scr[hc] = jnp.where((row // DA_QK) == (hc % 4), qg, jnp.zeros_like(qg))
    acc_scr[...] = jnp.zeros(acc_scr.shape, F32)

    def step(kb, near):
        ks = pl.multiple_of(kb * BK_A, BK_A)

        def qk(hc):
            g = hc // 4
            return _dot(k_ref[pl.ds(ks, BK_A), g * 128:(g + 1) * 128], w_scr[hc])

        ahead = [qk(i) for i in range(PREFETCH)]
        for hc in range(n_comb):
            h = hc // 2
            s = ahead.pop(0)
            if hc + PREFETCH < n_comb:
                ahead.append(qk(hc + PREFETCH))
            p = jnp.exp2(s)
            if near:
                p = p * near_ref[kb - near_lo, h]
            acc_scr[hc] += _dot(vT_ref[h * VX:(h + 1) * VX, pl.ds(ks, BK_A)], p.astype(BF16))

    def rescale(off):
        for hc in range(n_comb):
            acc_scr[hc] = acc_scr[hc] * sc_ref[off + hc // 2]

    lo = jnp.maximum(near_lo, 0)
    hi = jnp.minimum(near_lo + NEAR_A, nk)
    lax.fori_loop(0, lo, lambda kb, c: (step(kb, False), c)[1], 0)
    rescale(0)
    lax.fori_loop(lo, hi, lambda kb, c: (step(kb, True), c)[1], 0)
    rescale(HA)
    lax.fori_loop(hi, nk, lambda kb, c: (step(kb, False), c)[1], 0)

    lam = sc_ref[2 * HA]
    outs = []
    for h in range(HA):
        a0, a1 = acc_scr[2 * h], acc_scr[2 * h + 1]
        a = a0[0:DA_V] / a0[DA_V:DA_V + 1] - lam * (a1[0:DA_V] / a1[DA_V:DA_V + 1])
        outs.append(a * lax.rsqrt(jnp.mean(a * a, axis=0, keepdims=True) + EPS))
    o = jnp.concatenate(outs, axis=0) * subln_ref[...]
    o_ref[...] = o.T.astype(BF16)


def _attn_a(qT, k, vT, lw, B, S, bounded):
    nq = S // BQ_A
    nk = S // BK_A
    n_comb = 2 * HA
    if bounded:
        body = functools.partial(_attn_a_bounded_kernel, nk=nk)
        scratch = [pltpu.VMEM((n_comb, 128, BQ_A), BF16), pltpu.VMEM((n_comb, VX, BQ_A), F32)]
        sc, near = lw['sc_a_bounded'], lw['near_a_bounded']
    else:
        body = functools.partial(_attn_a_kernel, nk=nk)
        scratch = [pltpu.VMEM((n_comb, 128, BQ_A), BF16), pltpu.VMEM((n_comb, 1, BQ_A), F32),
                   pltpu.VMEM((n_comb, 1, BQ_A), F32), pltpu.VMEM((n_comb, DA_V, BQ_A), F32)]
        sc, near = lw['sc_a'], lw['near_a']
    return pl.pallas_call(
        body,
        grid=(B, nq),
        in_specs=[
            pl.BlockSpec(memory_space=pltpu.SMEM),
            pl.BlockSpec((n_comb * DA_QK, BQ_A), lambda b, q: (0, b * nq + q)),
            pl.BlockSpec((S, n_comb * DA_QK), lambda b, q: (b, 0)),
            pl.BlockSpec((HA * VX, S), lambda b, q: (0, b)),
            pl.BlockSpec(near.shape, lambda b, q: (0, 0, 0, 0)),
            pl.BlockSpec((HA * DA_V, 1), lambda b, q: (0, 0)),
        ],
        out_specs=pl.BlockSpec((BQ_A, HA * DA_V), lambda b, q: (b * nq + q, 0)),
        out_shape=jax.ShapeDtypeStruct((B * S, HA * DA_V), BF16),
        scratch_shapes=scratch,
        compiler_params=pltpu.CompilerParams(dimension_semantics=("arbitrary", "arbitrary"),
                                             vmem_limit_bytes=VMEM_LIMIT),
        name="attn_a_bounded" if bounded else "attn_a",
    )(sc, qT, k, vT, near, lw['subln'])


def _attn_b_kernel(qT_ref, k_ref, vT_ref, o_ref, m_scr, l_scr, acc_scr, *, nk):
    m_scr[...] = jnp.full(m_scr.shape, NEG_BIG, F32)
    l_scr[...] = jnp.zeros(l_scr.shape, F32)
    acc_scr[...] = jnp.zeros(acc_scr.shape, F32)

    def step(kb, carry):
        ks = pl.multiple_of(kb * BK_B, BK_B)

        def qk(h):
            return _dot(k_ref[pl.ds(ks, BK_B), h * DB_PAD:(h + 1) * DB_PAD], qT_ref[h * DB_PAD:(h + 1) * DB_PAD, :])

        ahead = [qk(i) for i in range(PREFETCH)]
        for h in range(HB):
            s = ahead.pop(0)
            if h + PREFETCH < HB:
                ahead.append(qk(h + PREFETCH))
            m_old = m_scr[h]
            m_new = jnp.maximum(m_old, jnp.max(s, axis=0, keepdims=True))
            alpha = jnp.exp2(m_old - m_new)
            p = jnp.exp2(s - m_new)
            l_scr[h] = alpha * l_scr[h] + jnp.sum(p, axis=0, keepdims=True)
            pv = _dot(vT_ref[h * VX:h * VX + DB_V, pl.ds(ks, BK_B)], p.astype(BF16))
            acc_scr[h] = alpha * acc_scr[h] + pv
            m_scr[h] = m_new
        return carry

    lax.fori_loop(0, nk, step, 0)
    o = jnp.concatenate([acc_scr[h] / l_scr[h] for h in range(HB)], axis=0)
    o_ref[...] = o.T.astype(BF16)


def _attn_b_bounded_kernel(qT_ref, k_ref, vT_ref, o_ref, acc_scr, *, nk):
    acc_scr[...] = jnp.zeros(acc_scr.shape, F32)

    assert nk % UNROLL_B == 0
    tiles = [(j, h) for j in range(UNROLL_B) for h in range(HB)]

    def step(it, carry):
        def ks(j):
            return pl.multiple_of((it * UNROLL_B + j) * BK_B, BK_B)

        def qk(t):
            j, h = tiles[t]
            return _dot(k_ref[pl.ds(ks(j), BK_B), h * DB_PAD:(h + 1) * DB_PAD],
                        qT_ref[h * DB_PAD:(h + 1) * DB_PAD, :])

        ahead = [qk(i) for i in range(PREFETCH)]
        for t, (j, h) in enumerate(tiles):
            s = ahead.pop(0)
            if t + PREFETCH < len(tiles):
                ahead.append(qk(t + PREFETCH))
            p = jnp.exp2(s).astype(BF16)
            acc_scr[h] += _dot(vT_ref[h * VX:(h + 1) * VX, pl.ds(ks(j), BK_B)], p)
        return carry

    lax.fori_loop(0, nk // UNROLL_B, step, 0)
    o = jnp.concatenate([acc_scr[h, 0:DB_V] / acc_scr[h, DB_V:DB_V + 1] for h in range(HB)], axis=0)
    o_ref[...] = o.T.astype(BF16)


def _attn_b(qT, k, vT, B, S, bounded):
    nq = S // BQ_B
    nk = S // BK_B
    if bounded:
        body = functools.partial(_attn_b_bounded_kernel, nk=nk)
        scratch = [pltpu.VMEM((HB, VX, BQ_B), F32)]
    else:
        body = functools.partial(_attn_b_kernel, nk=nk)
        scratch = [pltpu.VMEM((HB, 1, BQ_B), F32), pltpu.VMEM((HB, 1, BQ_B), F32), pltpu.VMEM((HB, DB_V, BQ_B), F32)]
    return pl.pallas_call(
        body,
        grid=(B, nq),
        in_specs=[
            pl.BlockSpec((HB * DB_PAD, BQ_B), lambda b, q: (0, b * nq + q)),
            pl.BlockSpec((S, HB * DB_PAD), lambda b, q: (b, 0)),
            pl.BlockSpec((HB * VX, S), lambda b, q: (0, b)),
        ],
        out_specs=pl.BlockSpec((BQ_B, HB * DB_V), lambda b, q: (b * nq + q, 0)),
        out_shape=jax.ShapeDtypeStruct((B * S, HB * DB_V), BF16),
        scratch_shapes=scratch,
        compiler_params=pltpu.CompilerParams(dimension_semantics=("arbitrary", "arbitrary"),
                                             vmem_limit_bytes=VMEM_LIMIT),
        name="attn_b_bounded" if bounded else "attn_b",
    )(qT, k, vT)


def _attn_c_kernel(sc_ref, qT_ref, k_ref, vT_ref, bm_ref, o_ref, *, nb, S):
    n = pl.program_id(1)
    start = pl.multiple_of(jnp.clip((n - 1) * BLOCK, 0, S - KW_C), BLOCK)
    case = jnp.where(n == 0, 0, jnp.where(n == nb - 1, 2, 1))
    kwin = k_ref[pl.ds(start, KW_C), :]
    row = lax.broadcasted_iota(jnp.int32, (KVH_C * DC, BLOCK), 0)
    outs = []
    for h in range(HC):
        j = h // G_C
        qh = qT_ref[h * DC:(h + 1) * DC, :]
        zz = jnp.zeros_like(qh)
        w = jnp.concatenate([qh, zz] if j == 0 else [zz, qh], axis=0)
        s = _dot(kwin, w) + bm_ref[case, h]
        sink = sc_ref[h]
        m = jnp.maximum(jnp.max(s, axis=0, keepdims=True), sink)
        p = jnp.exp2(s - m)
        den = jnp.sum(p, axis=0, keepdims=True) + jnp.exp2(sink - m)
        pv = _dot(vT_ref[j * VX:j * VX + DC, pl.ds(start, KW_C)], p.astype(BF16))
        outs.append(pv / den)
    del row
    o_ref[...] = jnp.concatenate(outs, axis=0).T.astype(BF16)


def _attn_c_bounded_kernel(sc_ref, qT_ref, k_ref, vT_ref, e_ref, o_ref, *, n_sub, S):
    g = pl.program_id(1)
    tiles = [(u, h) for u in range(SUB_C) for h in range(HC)]

    def window(u):
        n = g * SUB_C + u
        start = pl.multiple_of(jnp.clip(n * QB_C - WINDOW, 0, S - KWB_C), WINDOW)
        case = jnp.where(n == 0, 0, jnp.where(n == n_sub - 1, 2, 1))
        return start, case

    def qk(t):
        u, h = tiles[t]
        start, _ = window(u)
        qh = qT_ref[h * DC:(h + 1) * DC, u * QB_C:(u + 1) * QB_C]
        zz = jnp.zeros_like(qh)
        w = jnp.concatenate([qh, zz] if h // G_C == 0 else [zz, qh], axis=0)
        return _dot(k_ref[pl.ds(start, KWB_C), :], w)

    ahead = [qk(i) for i in range(PREFETCH)]
    outs = []
    for t, (u, h) in enumerate(tiles):
        s = ahead.pop(0)
        if t + PREFETCH < len(tiles):
            ahead.append(qk(t + PREFETCH))
        start, case = window(u)
        j = h // G_C
        p = (jnp.exp2(s) * e_ref[case, h]).astype(BF16)
        pv = _dot(vT_ref[j * VX:(j + 1) * VX, pl.ds(start, KWB_C)], p)
        outs.append(pv[0:DC] / (pv[DC:DC + 1] + sc_ref[HC + h]))
        if h == HC - 1:
            o_ref[u * QB_C:(u + 1) * QB_C, :] = jnp.concatenate(outs, axis=0).T.astype(BF16)
            outs = []


def _attn_c(qT, k, vT, lw, B, S, bounded):
    if bounded:
        n_sub = S // QB_C
        assert n_sub >= 3 and n_sub % SUB_C == 0
        nb, bq = n_sub // SUB_C, QB_C * SUB_C
        body = functools.partial(_attn_c_bounded_kernel, n_sub=n_sub, S=S)
        tile, sc = lw['e_c'], lw['sc_c_bounded']
    else:
        nb, bq = S // BLOCK, BLOCK
        assert nb >= 3
        body = functools.partial(_attn_c_kernel, nb=nb, S=S)
        tile, sc = lw['bm_c'], lw['sc_c']
    return pl.pallas_call(
        body,
        grid=(B, nb),
        in_specs=[
            pl.BlockSpec(memory_space=pltpu.SMEM),
            pl.BlockSpec((HC * DC, bq), lambda b, n: (0, b * nb + n)),
            pl.BlockSpec((S, KVH_C * DC), lambda b, n: (b, 0)),
            pl.BlockSpec((KVH_C * VX, S), lambda b, n: (0, b)),
            pl.BlockSpec(tile.shape, lambda b, n: (0, 0, 0, 0)),
        ],
        out_specs=pl.BlockSpec((bq, HC * DC), lambda b, n: (b * nb + n, 0)),
        out_shape=jax.ShapeDtypeStruct((B * S, HC * DC), BF16),
        compiler_params=pltpu.CompilerParams(dimension_semantics=("arbitrary", "arbitrary"),
                                             vmem_limit_bytes=VMEM_LIMIT),
        name="attn_c_bounded" if bounded else "attn_c",
    )(sc, qT, k, vT, tile)


def _out_kernel(x_ref, ya_ref, yb_ref, yc_ref, wo_ref, g_ref, wrT_ref, xn_ref, h2_ref, affT_ref):
    na, nb_ = HA * DA_V, HA * DA_V + HB * DB_V
    xn = (x_ref[...] + _dot(ya_ref[...], wo_ref[0:na, :]) + _dot(yb_ref[...], wo_ref[na:nb_, :])
          + _dot(yc_ref[...], wo_ref[nb_:, :]))
    xn_ref[...] = xn
    h2 = (xn * lax.rsqrt(jnp.mean(xn * xn, axis=-1, keepdims=True) + EPS) * g_ref[...]).astype(BF16)
    h2_ref[...] = h2
    lg = _nt_dot(wrT_ref[...], h2)
    e = jnp.exp(lg - jnp.max(lg, axis=0, keepdims=True))
    affT_ref[...] = e / jnp.sum(e, axis=0, keepdims=True)


def _out_proj(x2, ya, yb, yc, lw):
    T = x2.shape[0]
    tm = TM_OUT

    def full(a):
        return pl.BlockSpec(a.shape, lambda i: (0,) * a.ndim)

    return pl.pallas_call(
        _out_kernel,
        grid=(T // tm,),
        in_specs=[pl.BlockSpec((tm, D_MODEL), lambda i: (i, 0)),
                  pl.BlockSpec((tm, ya.shape[1]), lambda i: (i, 0)),
                  pl.BlockSpec((tm, yb.shape[1]), lambda i: (i, 0)),
                  pl.BlockSpec((tm, yc.shape[1]), lambda i: (i, 0)),
                  full(lw['w_out']), full(lw['g_ffn']), full(lw['w_rT'])],
        out_specs=[pl.BlockSpec((tm, D_MODEL), lambda i: (i, 0)),
                   pl.BlockSpec((tm, D_MODEL), lambda i: (i, 0)),
                   pl.BlockSpec((N_EXPERTS, tm), lambda i: (0, i))],
        out_shape=[jax.ShapeDtypeStruct((T, D_MODEL), F32),
                   jax.ShapeDtypeStruct((T, D_MODEL), BF16),
                   jax.ShapeDtypeStruct((N_EXPERTS, T), F32)],
        compiler_params=pltpu.CompilerParams(dimension_semantics=("arbitrary",), vmem_limit_bytes=VMEM_LIMIT),
        name="out_proj",
    )(x2, ya, yb, yc, lw['w_out'], lw['g_ffn'], lw['w_rT'])


def _topk_kernel(aff_ref, thr_ref, tcut_ref, *, cap, T):
    capf = float(cap)

    def bits():
        return lax.bitcast_convert_type(aff_ref[...], jnp.int32)

    def count(mask):
        return jnp.sum(jnp.where(mask, 1.0, 0.0), axis=1, keepdims=True)

    def vbody(i, v):
        cand = v | jnp.left_shift(jnp.int32(1), 30 - i)
        return jnp.where(count(bits() >= cand) >= capf, cand, v)

    thr = lax.fori_loop(0, 31, vbody, jnp.zeros((N_EXPERTS, 1), jnp.int32))
    need = capf - count(bits() > thr)
    nbits = max(1, int(math.ceil(math.log2(T))))

    def tbody(i, c):
        cand = c | jnp.left_shift(jnp.int32(1), nbits - 1 - i)
        idx = lax.broadcasted_iota(jnp.int32, (N_EXPERTS, T), 1)
        f = count((bits() == thr) & (idx < cand))
        return jnp.where(f < need, cand, c)

    tcut = lax.fori_loop(0, nbits, tbody, jnp.zeros((N_EXPERTS, 1), jnp.int32))
    thr_ref[...] = jnp.broadcast_to(thr, thr_ref.shape)
    tcut_ref[...] = jnp.broadcast_to(tcut, tcut_ref.shape)


def _topk_thresholds(affT):
    E, T = affT.shape
    cap = CAPACITY_FACTOR * T // N_EXPERTS
    return pl.pallas_call(
        functools.partial(_topk_kernel, cap=cap, T=T),
        grid=(1,),
        in_specs=[pl.BlockSpec((E, T), lambda i: (0, 0))],
        out_specs=[pl.BlockSpec((E, 128), lambda i: (0, 0))] * 2,
        out_shape=[jax.ShapeDtypeStruct((E, 128), jnp.int32)] * 2,
        compiler_params=pltpu.CompilerParams(dimension_semantics=("arbitrary",), vmem_limit_bytes=VMEM_LIMIT),
        name="topk_thr",
    )(affT)


def _moe_kernel(x_ref, h2_ref, affT_ref, thr_ref, tcut_ref, tri_ref, wg_ref, wu_ref, wd_ref, o_ref, pos_scr):
    i = pl.program_id(0)
    e = pl.program_id(1)
    tt = h2_ref.shape[0]

    @pl.when(e == 0)
    def _():
        b = lax.bitcast_convert_type(affT_ref[...], jnp.int32)
        thr = thr_ref[:, 0:1]
        tc = tcut_ref[:, 0:1]
        tg = i * tt + lax.broadcasted_iota(jnp.int32, b.shape, 1)
        sel = (b > thr) | ((b == thr) & (tg <= tc))
        pos = _dot(jnp.where(sel, 1.0, 0.0).astype(BF16), tri_ref[...])
        pos_scr[...] = jnp.where(sel, pos, -1.0)
        o_ref[...] = x_ref[...]

    posm = pos_scr[pl.ds(e, 1), :]
    gate = affT_ref[pl.ds(e, 1), :]
    n_sel = jnp.sum(jnp.where(posm >= 0.0, 1, 0))
    n_ch = (n_sel + CH_MOE - 1) // CH_MOE

    def chunk(c, carry):
        rel = posm - (c * CH_MOE).astype(F32)
        jj = lax.broadcasted_iota(jnp.int32, (CH_MOE, tt), 0).astype(F32)
        hit = jnp.broadcast_to(rel, (CH_MOE, tt)) == jj
        xe = _dot(jnp.where(hit, 1.0, 0.0).astype(BF16), h2_ref[...]).astype(BF16)
        g = _dot(xe, wg_ref[...])
        u = _dot(xe, wu_ref[...])
        hid = (g * jax.nn.sigmoid(g) * u).astype(BF16)
        ye = _dot(hid, wd_ref[...])
        gc = jnp.sum(jnp.where(hit, jnp.broadcast_to(gate, hit.shape), 0.0), axis=1, keepdims=True)
        yeb = jnp.concatenate([(ye * gc).astype(BF16), jnp.zeros((CHP_MOE - CH_MOE, ye.shape[1]), BF16)], axis=0)
        relT = jnp.broadcast_to(rel, (CHP_MOE, tt)).T
        ll = lax.broadcasted_iota(jnp.int32, (tt, CHP_MOE), 1).astype(F32)
        hitT = jnp.where(relT == ll, 1.0, 0.0).astype(BF16)
        o_ref[...] += _dot(hitT, yeb)
        return carry

    lax.fori_loop(0, n_ch, chunk, 0)


def _moe(xn, h2, affT, thr, tcut, lw):
    T = xn.shape[0]
    tt = TT_MOE
    return pl.pallas_call(
        _moe_kernel,
        grid=(T // tt, N_EXPERTS),
        in_specs=[
            pl.BlockSpec((tt, D_MODEL), lambda i, e: (i, 0)),
            pl.BlockSpec((tt, D_MODEL), lambda i, e: (i, 0)),
            pl.BlockSpec((N_EXPERTS, tt), lambda i, e: (0, i)),
            pl.BlockSpec((N_EXPERTS, 128), lambda i, e: (0, 0)),
            pl.BlockSpec((N_EXPERTS, 128), lambda i, e: (0, 0)),
            pl.BlockSpec((tt, tt), lambda i, e: (0, 0)),
            pl.BlockSpec((None, D_MODEL, D_EXPERT), lambda i, e: (e, 0, 0)),
            pl.BlockSpec((None, D_MODEL, D_EXPERT), lambda i, e: (e, 0, 0)),
            pl.BlockSpec((None, D_EXPERT, D_MODEL), lambda i, e: (e, 0, 0)),
        ],
        out_specs=pl.BlockSpec((tt, D_MODEL), lambda i, e: (i, 0)),
        out_shape=jax.ShapeDtypeStruct((T, D_MODEL), F32),
        scratch_shapes=[pltpu.VMEM((N_EXPERTS, tt), F32)],
        compiler_params=pltpu.CompilerParams(dimension_semantics=("arbitrary", "arbitrary"),
                                             vmem_limit_bytes=VMEM_LIMIT),
        name="moe_ffn",
    )(xn, h2, affT, thr, tcut, lw['tri'], lw['w_gate'], lw['w_up'], lw['w_down'])


def _t5_bucket(rel):
    nb = NUM_BUCKETS // 2
    max_exact = nb // 2
    ret = jnp.where(rel > 0, nb, 0)
    n = jnp.abs(rel)
    nf = jnp.maximum(n, 1).astype(jnp.float32)
    large = max_exact + (jnp.log(nf / max_exact) / math.log(MAX_DISTANCE / max_exact) * (nb - max_exact)).astype(jnp.int32)
    large = jnp.minimum(large, nb - 1)
    return ret + jnp.where(n < max_exact, n, large)


def _col(v):
    return v.astype(F32).reshape(-1, 1)


def _pad_heads(w, n_heads, d, d_pad):
    k = w.shape[1]
    w3 = w.reshape(n_heads, d, k)
    return jnp.pad(w3, ((0, 0), (0, d_pad - d), (0, 0))).reshape(n_heads * d_pad, k)


def _toeplitz_tiles(fn, offsets, nk, nq):
    span = nk + nq
    out = []
    for off in offsets:
        u = fn(off + nk - 1 - jnp.arange(span, dtype=jnp.int32))
        rows = jnp.tile(u, (1, nk))[:, :nk * (span - 1)].reshape(u.shape[0], nk, span - 1)
        out.append(rows[:, :, nk - 1:nk - 1 + nq])
    return jnp.stack(out, axis=0)


def _prep_shared(rel_bias):
    rb = rel_bias.astype(F32) * LOG2E

    def bias_a(rel):
        return rb[:, :HA][_t5_bucket(rel)].T

    def bias_c(rel):
        return rb[:, HA:][_t5_bucket(rel)].T

    def in_band(rel):
        return (jnp.abs(rel) <= WINDOW)[None, :]

    bmax = jnp.max(rb[:, :HA], axis=0)
    near_offs = [d * BK_A for d in range(-1, NEAR_A - 1)]
    near_a = _toeplitz_tiles(lambda rel: jnp.exp2(bias_a(rel) - bmax[:, None]), near_offs, BK_A, BQ_A)
    far = _t5_bucket(jnp.array([-(1 << 20), 1 << 20], dtype=jnp.int32))
    sc_far = jnp.concatenate([rb[far[1], :HA], bmax, rb[far[0], :HA]])
    a_bounded = {
        'near': _toeplitz_tiles(lambda rel: jnp.exp2(bias_a(rel)), near_offs, BK_A, BQ_A),
        'sc': jnp.concatenate([jnp.exp2(rb[far[0], :HA]), jnp.exp2(-rb[far[1], :HA])]),
        'bias_mag': jnp.max(jnp.abs(rb[:, :HA])),
    }
    bm_c = _toeplitz_tiles(lambda rel: jnp.where(in_band(rel), bias_c(rel), NEG_BIG),
                           [0, -BLOCK, -2 * BLOCK], KW_C, BLOCK)
    c_bounded = {
        'e': _toeplitz_tiles(lambda rel: jnp.where(in_band(rel), jnp.exp2(bias_c(rel)), 0.0),
                             [0, -WINDOW, -(KWB_C - QB_C)], KWB_C, QB_C),
        'bias_mag': jnp.max(jnp.abs(rb[:, HA:])),
    }
    return near_a, sc_far, bm_c, a_bounded, c_bounded


def _rope_tables(S):
    inv = 1.0 / (ROPE_BASE ** (jnp.arange(ROPE_HALF, dtype=jnp.float32) / ROPE_HALF))
    ang = jnp.arange(S, dtype=jnp.int32).astype(jnp.float32)[:, None] * inv[None, :]
    return jnp.cos(ang).T, jnp.sin(ang).T


def _logit_bounds(l, p, a_bounded, c_bounded, sa, sb, scc):
    qa, ka = p['qn_a'][l].astype(F32), p['kn_a'][l].astype(F32)
    bound_a = 1.01 * DA_QK * jnp.max(jnp.abs(qa * ka)) * sa + 2.0 * a_bounded['bias_mag']
    qb, kb = p['qn_b'][l].astype(F32), p['kn_b'][l].astype(F32)
    bound_b = 1.01 * DB_QK * jnp.max(jnp.abs(qb)) * jnp.max(jnp.abs(kb)) * sb
    qc, kc = p['qn_c'][l].astype(F32), p['kn_c'][l].astype(F32)
    bound_c = (1.01 * DC * jnp.max(jnp.abs(qc * kc)) * scc + 2.0 * c_bounded['bias_mag']
               + jnp.max(jnp.abs(p['sink_c'][l].astype(F32))) * LOG2E)
    return bound_a <= BOUND_LIMIT, bound_b <= BOUND_LIMIT, bound_c <= BOUND_LIMIT


def _prep_layer(l, p, shared):
    near_a, sc_far, bm_c, a_bounded, c_bounded = shared
    lam_init = 0.8 - 0.6 * math.exp(-0.3 * l)
    lam = (jnp.exp(jnp.sum(p['lam_q1'][l].astype(F32) * p['lam_k1'][l].astype(F32)))
           - jnp.exp(jnp.sum(p['lam_q2'][l].astype(F32) * p['lam_k2'][l].astype(F32))) + lam_init)
    w_uq = p['w_uq'][l]
    w_ukv = p['w_ukv'][l].reshape(KV_RANK, HB, DB_NOPE + DB_V)
    sa = DA_QK ** -0.5 * LOG2E
    sb = DB_QK ** -0.5 * LOG2E
    scc = DC ** -0.5 * LOG2E
    zpad = jnp.zeros((DB_PAD - DB_QK,), F32)
    lw = {
        'g_mix': p['norm_mix_g'][l].astype(F32).reshape(1, D_MODEL),
        'w_inT': p['w_in'][l].T.astype(BF16),
        'w_uqT': _pad_heads(w_uq.T, HB, DB_QK, DB_PAD).astype(BF16),
        'w_ukT': w_ukv[:, :, :DB_NOPE].reshape(KV_RANK, HB * DB_NOPE).T.astype(BF16),
        'w_uvT': w_ukv[:, :, DB_NOPE:].reshape(KV_RANK, HB * DB_V).T.astype(BF16),
        'gqa': _col(jnp.tile(p['qn_a'][l], 2 * HA) * sa),
        'gka': _col(jnp.tile(p['kn_a'][l], 2 * HA)),
        'gcq': _col(p['g_cq'][l]),
        'gckv': _col(p['g_ckv'][l]),
        'gqb': _col(jnp.tile(jnp.concatenate([p['qn_b'][l].astype(F32) * sb, zpad]), HB)),
        'gkb': _col(jnp.tile(jnp.concatenate([p['kn_b'][l].astype(F32), zpad]), HB)),
        'gqc': _col(jnp.tile(p['qn_c'][l], HC) * scc),
        'gkc': _col(jnp.tile(p['kn_c'][l], KVH_C)),
        'near_a': near_a,
        'sc_a': jnp.concatenate([sc_far, lam.reshape(1)]).astype(F32),
        'subln': _col(jnp.tile(p['subln_a'][l], HA) * (1.0 - lam_init)),
        'bm_c': bm_c,
        'sc_c': p['sink_c'][l].astype(F32) * LOG2E,
        'w_out': p['w_out'][l].astype(BF16),
        'g_ffn': p['norm_ffn_g'][l].astype(F32).reshape(1, D_MODEL),
        'w_rT': p['w_router'][l].T.astype(BF16),
        'w_gate': p['w_gate'][l].astype(BF16),
        'w_up': p['w_up'][l].astype(BF16),
        'w_down': p['w_down'][l].astype(BF16),
        'tri': jnp.triu(jnp.ones((TT_MOE, TT_MOE), F32), k=1).astype(BF16),
        'near_a_bounded': a_bounded['near'],
        'sc_a_bounded': jnp.concatenate([a_bounded['sc'], lam.reshape(1)]).astype(F32),
    }
    lw['e_c'] = c_bounded['e']
    lw['sc_c_bounded'] = jnp.concatenate([lw['sc_c'], jnp.exp2(lw['sc_c'])])
    lw['bounded_a'], lw['bounded_b'], lw['bounded_c'] = _logit_bounds(l, p, a_bounded, c_bounded, sa, sb, scc)
    return lw


def _trunk(x, layers):
    B, S, D = x.shape
    x2 = x.reshape(B * S, D)
    cosT, sinT = _rope_tables(S)
    for lw in layers:
        lw = dict(lw, cosT=cosT, sinT=sinT)
        qaT, ka, vaT, qbT, kb, vbT, qcT, kc, vcT = _in_proj(x2, lw, S)
        ya = lax.cond(lw['bounded_a'],
                      lambda q, k, v: _attn_a(q, k, v, lw, B, S, True),
                      lambda q, k, v: _attn_a(q, k, v, lw, B, S, False), qaT, ka, vaT)
        yb = lax.cond(lw['bounded_b'],
                      lambda q, k, v: _attn_b(q, k, v, B, S, True),
                      lambda q, k, v: _attn_b(q, k, v, B, S, False), qbT, kb, vbT)
        yc = lax.cond(lw['bounded_c'],
                      lambda q, k, v: _attn_c(q, k, v, lw, B, S, True),
                      lambda q, k, v: _attn_c(q, k, v, lw, B, S, False), qcT, kc, vcT)
        xn, h2, affT = _out_proj(x2, ya, yb, yc, lw)
        thr, tcut = _topk_thresholds(affT)
        x2 = _moe(xn, h2, affT, thr, tcut, lw)
    return x2.reshape(B, S, D)


def kernel(x_prompt, x_sample, rel_bias, norm_mix_g, w_in, qn_a, kn_a, lam_q1, lam_k1, lam_q2, lam_k2, subln_a, g_cq, w_uq, g_ckv, w_ukv, qn_b, kn_b, qn_c, kn_c, sink_c, w_out, norm_ffn_g, w_router, w_gate, w_up, w_down):
    p = dict(norm_mix_g=norm_mix_g, w_in=w_in, qn_a=qn_a, kn_a=kn_a, lam_q1=lam_q1, lam_k1=lam_k1,
             lam_q2=lam_q2, lam_k2=lam_k2, subln_a=subln_a, g_cq=g_cq, w_uq=w_uq, g_ckv=g_ckv, w_ukv=w_ukv,
             qn_b=qn_b, kn_b=kn_b, qn_c=qn_c, kn_c=kn_c, sink_c=sink_c, w_out=w_out, norm_ffn_g=norm_ffn_g,
             w_router=w_router, w_gate=w_gate, w_up=w_up, w_down=w_down)
    shared = _prep_shared(rel_bias)
    layers = [_prep_layer(l, p, shared) for l in range(w_in.shape[0])]
    return (_trunk(x_prompt, layers), _trunk(x_sample, layers))
```

```python
import functools
import math

import jax
import jax.numpy as jnp
import numpy as np
from jax import lax
from jax.experimental import pallas as pl
from jax.experimental.pallas import tpu as pltpu

D_MODEL = 1024
BLOCK = 128
HA, DA_QK, DA_V = 4, 32, 64
HB, Q_RANK, KV_RANK, DB_NOPE, DB_ROPE, DB_V = 6, 384, 256, 64, 32, 64
ROPE_BASE = 10000.0
HC, KVH_C, DC, WINDOW = 6, 2, 64, 128
G_C = HC // KVH_C
NUM_BUCKETS, MAX_DISTANCE = 32, 128
N_EXPERTS, D_EXPERT, CAPACITY_FACTOR = 16, 1024, 2
EPS = 1e-6
DEPTH = 2

DB_QK = DB_NOPE + DB_ROPE
DB_PAD = 128
VX = 80
BOUND_LIMIT = 80.0
ROPE_HALF = DB_ROPE // 2
IN_SPLITS = (HA * 2 * DA_QK, HA * 2 * DA_QK, HA * DA_V, Q_RANK, KV_RANK, DB_ROPE, HC * DC, KVH_C * DC, KVH_C * DC)
IN_OFFS = tuple(int(v) for v in np.cumsum((0,) + IN_SPLITS))
IN_COLS = IN_OFFS[-1]
LOG2E = 1.4426950408889634
NEG_BIG = -1e30

F32 = jnp.float32
BF16 = jnp.bfloat16

TM_IN = 512
TM_OUT = 512
BQ_A = 512
BK_A = 512
UNROLL_A = 2
NEAR_A = BQ_A // BK_A + 2
assert BQ_A % BK_A == 0 and BK_A >= MAX_DISTANCE
BQ_B = 512
BK_B = 512
UNROLL_B = 2
KW_C = 3 * BLOCK
QB_C = 256
KWB_C = QB_C + 2 * WINDOW
SUB_C = 2
TT_MOE = 1024
SUBS_MOE = 2
CH_MOE = 160
CHP_MOE = 256
PREFETCH = 3
VMEM_LIMIT = 56 * 1024 * 1024


def _nt_dot(a, b):
    return lax.dot_general(a, b, (((1,), (1,)), ((), ())), preferred_element_type=F32)


def _dot(a, b):
    return jnp.dot(a, b, preferred_element_type=F32)


def _group_norm_rows(y, gs, n_valid):
    rows, tm = y.shape
    y3 = y.reshape(rows // gs, gs, tm)
    ms = jnp.sum(y3 * y3, axis=1, keepdims=True) * (1.0 / n_valid)
    return (y3 * lax.rsqrt(ms + EPS)).reshape(rows, tm)


def _in_kernel(x_ref, g_ref, winT_ref, wuqT_ref, wukT_ref, wuvT_ref,
               gqa_ref, gka_ref, gcq_ref, gckv_ref, gqb_ref, gkb_ref, gqc_ref, gkc_ref,
               cos_ref, sin_ref,
               qaT_ref, ka_ref, vaT_ref, qbT_ref, kb_ref, vbT_ref, qcT_ref, kc_ref, vcT_ref):
    x = x_ref[...]
    h = x * lax.rsqrt(jnp.mean(x * x, axis=-1, keepdims=True) + EPS) * g_ref[...]
    hb = h.astype(BF16)

    def seg(i):
        return _nt_dot(winT_ref[IN_OFFS[i]:IN_OFFS[i + 1], :], hb)

    cos = cos_ref[...]
    sin = sin_ref[...]

    qaT_ref[...] = (_group_norm_rows(seg(0), DA_QK, DA_QK) * gqa_ref[...]).astype(BF16)
    kaT = _group_norm_rows(seg(1), DA_QK, DA_QK) * gka_ref[...]
    ka_ref[...] = kaT.T.astype(BF16)
    def with_ones_rows(v, n_heads, d):
        tm = v.shape[1]
        tail = jnp.where(lax.broadcasted_iota(jnp.int32, (VX - d, tm), 0) == 0, 1.0, 0.0)
        rows = []
        for hh in range(n_heads):
            rows += [v[hh * d:(hh + 1) * d], tail]
        return jnp.concatenate(rows, axis=0).astype(BF16)

    vaT_ref[...] = with_ones_rows(seg(2), HA, DA_V)

    def rope_store(dst_rows, y, h):
        o = h * DB_PAD
        x1 = y[o + DB_NOPE:o + DB_NOPE + ROPE_HALF]
        x2 = y[o + DB_NOPE + ROPE_HALF:o + DB_QK]
        dst_rows.append(y[o:o + DB_NOPE])
        dst_rows.append(x1 * cos - x2 * sin)
        dst_rows.append(x2 * cos + x1 * sin)
        dst_rows.append(jnp.zeros((DB_PAD - DB_QK, y.shape[1]), F32))

    cqT = seg(3)
    cqn = (cqT * lax.rsqrt(jnp.mean(cqT * cqT, axis=0, keepdims=True) + EPS) * gcq_ref[...]).astype(BF16)
    qf = _group_norm_rows(_dot(wuqT_ref[...], cqn), DB_PAD, DB_QK) * gqb_ref[...]
    rows = []
    for hh in range(HB):
        rope_store(rows, qf, hh)
    qbT_ref[...] = jnp.concatenate(rows, axis=0).astype(BF16)

    ckvT = seg(4)
    ckvn = (ckvT * lax.rsqrt(jnp.mean(ckvT * ckvT, axis=0, keepdims=True) + EPS) * gckv_ref[...]).astype(BF16)
    knope = _dot(wukT_ref[...], ckvn)
    vbT_ref[...] = with_ones_rows(_dot(wuvT_ref[...], ckvn), HB, DB_V)
    krope = seg(5)
    zpad = jnp.zeros((DB_PAD - DB_QK, krope.shape[1]), F32)
    rows = []
    for hh in range(HB):
        rows += [knope[hh * DB_NOPE:(hh + 1) * DB_NOPE], krope, zpad]
    kf = _group_norm_rows(jnp.concatenate(rows, axis=0), DB_PAD, DB_QK) * gkb_ref[...]
    rows = []
    for hh in range(HB):
        rope_store(rows, kf, hh)
    kb_ref[...] = jnp.concatenate(rows, axis=0).T.astype(BF16)

    qcT_ref[...] = (_group_norm_rows(seg(6), DC, DC) * gqc_ref[...]).astype(BF16)
    kcT = _group_norm_rows(seg(7), DC, DC) * gkc_ref[...]
    kc_ref[...] = kcT.T.astype(BF16)
    vcT_ref[...] = with_ones_rows(seg(8), KVH_C, DC)


def _in_proj(x2, lw, S):
    T = x2.shape[0]
    tm = TM_IN
    nt = T // tm
    spb = S // tm

    def full(a):
        return pl.BlockSpec(a.shape, lambda i: (0,) * a.ndim)

    consts = [lw['g_mix'], lw['w_inT'], lw['w_uqT'], lw['w_ukT'], lw['w_uvT'],
              lw['gqa'], lw['gka'], lw['gcq'], lw['gckv'], lw['gqb'], lw['gkb'], lw['gqc'], lw['gkc']]
    in_specs = ([pl.BlockSpec((tm, D_MODEL), lambda i: (i, 0))] + [full(a) for a in consts]
                + [pl.BlockSpec((ROPE_HALF, tm), lambda i: (0, i % spb))] * 2)

    def fm(rows):
        return jax.ShapeDtypeStruct((rows, T), BF16), pl.BlockSpec((rows, tm), lambda i: (0, i))

    def tk(cols):
        return jax.ShapeDtypeStruct((T, cols), BF16), pl.BlockSpec((tm, cols), lambda i: (i, 0))

    outs = [fm(HA * 2 * DA_QK), tk(HA * 2 * DA_QK), fm(HA * VX),
            fm(HB * DB_PAD), tk(HB * DB_PAD), fm(HB * VX),
            fm(HC * DC), tk(KVH_C * DC), fm(KVH_C * VX)]
    return pl.pallas_call(
        _in_kernel,
        grid=(nt,),
        in_specs=in_specs,
        out_specs=[o[1] for o in outs],
        out_shape=[o[0] for o in outs],
        compiler_params=pltpu.CompilerParams(dimension_semantics=("arbitrary",), vmem_limit_bytes=VMEM_LIMIT),
        name="in_proj",
    )(x2, *consts, lw['cosT'], lw['sinT'])


def _attn_a_kernel(sc_ref, qT_ref, k_ref, vT_ref, near_ref, subln_ref, o_ref, w_scr, m_scr, l_scr, acc_scr, *, nk):
    qb = pl.program_id(1)
    n_comb = 2 * HA
    bq = qT_ref.shape[1]
    near_lo = (BQ_A // BK_A) * qb - 1

    for hc in range(n_comb):
        g = hc // 4
        qg = qT_ref[g * 128:(g + 1) * 128, :]
        row = lax.broadcasted_iota(jnp.int32, (128, bq), 0)
        keep = (row // DA_QK) == (hc % 4)
        w_scr[hc] = jnp.where(keep, qg, jnp.zeros_like(qg))
    m_scr[...] = jnp.full(m_scr.shape, NEG_BIG, F32)
    l_scr[...] = jnp.zeros(l_scr.shape, F32)
    acc_scr[...] = jnp.zeros(acc_scr.shape, F32)

    def step(kb, mode):
        ks = pl.multiple_of(kb * BK_A, BK_A)

        def qk(hc):
            g = hc // 4
            return _dot(k_ref[pl.ds(ks, BK_A), g * 128:(g + 1) * 128], w_scr[hc])

        ahead = [qk(i) for i in range(PREFETCH)]
        for hc in range(n_comb):
            h = hc // 2
            s = ahead.pop(0)
            if hc + PREFETCH < n_comb:
                ahead.append(qk(hc + PREFETCH))
            c = sc_ref[(1 - mode) * HA + h]
            cmax = jnp.max(s, axis=0, keepdims=True) + c
            m_old = m_scr[hc]
            m_new = jnp.maximum(m_old, cmax)
            alpha = jnp.exp2(m_old - m_new)
            p = jnp.exp2(s - (m_new - c))
            if mode == 0:
                p = p * near_ref[kb - near_lo, h]
            l_scr[hc] = alpha * l_scr[hc] + jnp.sum(p, axis=0, keepdims=True)
            pv = _dot(vT_ref[h * VX:h * VX + DA_V, pl.ds(ks, BK_A)], p.astype(BF16))
            acc_scr[hc] = alpha * acc_scr[hc] + pv
            m_scr[hc] = m_new

    lo = jnp.maximum(near_lo, 0)
    hi = jnp.minimum(near_lo + NEAR_A, nk)
    lax.fori_loop(0, lo, lambda kb, c: (step(kb, -1), c)[1], 0)
    lax.fori_loop(lo, hi, lambda kb, c: (step(kb, 0), c)[1], 0)
    lax.fori_loop(hi, nk, lambda kb, c: (step(kb, 1), c)[1], 0)

    lam = sc_ref[3 * HA]
    outs = []
    for h in range(HA):
        a = acc_scr[2 * h] / l_scr[2 * h] - lam * (acc_scr[2 * h + 1] / l_scr[2 * h + 1])
        a = a * lax.rsqrt(jnp.mean(a * a, axis=0, keepdims=True) + EPS)
        outs.append(a)
    o = jnp.concatenate(outs, axis=0) * subln_ref[...]
    o_ref[...] = o.T.astype(BF16)


def _attn_a_bounded_kernel(sc_ref, qT_ref, k_ref, vT_ref, near_ref, subln_ref, o_ref, w_scr, acc_scr, *, nk):
    qb = pl.program_id(1)
    n_comb = 2 * HA
    bq = qT_ref.shape[1]
    near_lo = (BQ_A // BK_A) * qb - 1
    for hc in range(n_comb):
        g = hc // 4
        qg = qT_ref[g * 128:(g + 1) * 128, :]
        row = lax.broadcasted_iota(jnp.int32, (128, bq), 0)
        w_scr[hc] = jnp.where((row // DA_QK) == (hc % 4), qg, jnp.zeros_like(qg))
    acc_scr[...] = jnp.zeros(acc_scr.shape, F32)

    lo = jnp.maximum(near_lo, 0)
    hi = jnp.minimum(near_lo + NEAR_A, nk)
    n_far = lo + (nk - hi)

    def step(j0, n_blocks, near):
        def block(jj):
            j = j0 + jj
            if near:
                return j, None
            right = j >= lo
            return jnp.where(right, j - lo + hi, j), jnp.where(right, HA, 0)

        tiles = [(jj, hc) for jj in range(n_blocks) for hc in range(n_comb)]

        def qk(t):
            jj, hc = tiles[t]
            g = hc // 4
            ks = pl.multiple_of(block(jj)[0] * BK_A, BK_A)
            return _dot(k_ref[pl.ds(ks, BK_A), g * 128:(g + 1) * 128], w_scr[hc])

        ahead = [qk(i) for i in range(PREFETCH)]
        for t, (jj, hc) in enumerate(tiles):
            h = hc // 2
            s = ahead.pop(0)
            if t + PREFETCH < len(tiles):
                ahead.append(qk(t + PREFETCH))
            kb, side = block(jj)
            ks = pl.multiple_of(kb * BK_A, BK_A)
            p = jnp.exp2(s)
            if near:
                p = p * near_ref[kb - near_lo, h]
            pv = _dot(vT_ref[h * VX:(h + 1) * VX, pl.ds(ks, BK_A)], p.astype(BF16))
            if near:
                acc_scr[hc] += pv
            else:
                acc_scr[hc] += pv * sc_ref[side + h]

    n_pairs = n_far // UNROLL_A
    lax.fori_loop(0, n_pairs, lambda i, c: (step(i * UNROLL_A, UNROLL_A, False), c)[1], 0)
    lax.fori_loop(n_pairs * UNROLL_A, n_far, lambda j, c: (step(j, 1, False), c)[1], 0)
    lax.fori_loop(lo, hi, lambda kb, c: (step(kb, 1, True), c)[1], 0)

    lam = sc_ref[2 * HA]
    outs = []
    for h in range(HA):
        a0, a1 = acc_scr[2 * h], acc_scr[2 * h + 1]
        a = a0[0:DA_V] / a0[DA_V:DA_V + 1] - lam * (a1[0:DA_V] / a1[DA_V:DA_V + 1])
        outs.append(a * lax.rsqrt(jnp.mean(a * a, axis=0, keepdims=True) + EPS))
    o = jnp.concatenate(outs, axis=0) * subln_ref[...]
    o_ref[...] = o.T.astype(BF16)


def _attn_a(qT, k, vT, lw, B, S, bounded):
    nq = S // BQ_A
    nk = S // BK_A
    n_comb = 2 * HA
    if bounded:
        body = functools.partial(_attn_a_bounded_kernel, nk=nk)
        scratch = [pltpu.VMEM((n_comb, 128, BQ_A), BF16), pltpu.VMEM((n_comb, VX, BQ_A), F32)]
        sc, near = lw['sc_a_bounded'], lw['near_a_bounded']
    else:
        body = functools.partial(_attn_a_kernel, nk=nk)
        scratch = [pltpu.VMEM((n_comb, 128, BQ_A), BF16), pltpu.VMEM((n_comb, 1, BQ_A), F32),
                   pltpu.VMEM((n_comb, 1, BQ_A), F32), pltpu.VMEM((n_comb, DA_V, BQ_A), F32)]
        sc, near = lw['sc_a'], lw['near_a']
    return pl.pallas_call(
        body,
        grid=(B, nq),
        in_specs=[
            pl.BlockSpec(memory_space=pltpu.SMEM),
            pl.BlockSpec((n_comb * DA_QK, BQ_A), lambda b, q: (0, b * nq + q)),
            pl.BlockSpec((S, n_comb * DA_QK), lambda b, q: (b, 0)),
            pl.BlockSpec((HA * VX, S), lambda b, q: (0, b)),
            pl.BlockSpec(near.shape, lambda b, q: (0, 0, 0, 0)),
            pl.BlockSpec((HA * DA_V, 1), lambda b, q: (0, 0)),
        ],
        out_specs=pl.BlockSpec((BQ_A, HA * DA_V), lambda b, q: (b * nq + q, 0)),
        out_shape=jax.ShapeDtypeStruct((B * S, HA * DA_V), BF16),
        scratch_shapes=scratch,
        compiler_params=pltpu.CompilerParams(dimension_semantics=("arbitrary", "arbitrary"),
                                             vmem_limit_bytes=VMEM_LIMIT),
        name="attn_a_bounded" if bounded else "attn_a",
    )(sc, qT, k, vT, near, lw['subln'])


def _attn_b_kernel(qT_ref, k_ref, vT_ref, o_ref, m_scr, l_scr, acc_scr, *, nk):
    m_scr[...] = jnp.full(m_scr.shape, NEG_BIG, F32)
    l_scr[...] = jnp.zeros(l_scr.shape, F32)
    acc_scr[...] = jnp.zeros(acc_scr.shape, F32)

    def step(kb, carry):
        ks = pl.multiple_of(kb * BK_B, BK_B)

        def qk(h):
            return _dot(k_ref[pl.ds(ks, BK_B), h * DB_PAD:(h + 1) * DB_PAD], qT_ref[h * DB_PAD:(h + 1) * DB_PAD, :])

        ahead = [qk(i) for i in range(PREFETCH)]
        for h in range(HB):
            s = ahead.pop(0)
            if h + PREFETCH < HB:
                ahead.append(qk(h + PREFETCH))
            m_old = m_scr[h]
            m_new = jnp.maximum(m_old, jnp.max(s, axis=0, keepdims=True))
            alpha = jnp.exp2(m_old - m_new)
            p = jnp.exp2(s - m_new)
            l_scr[h] = alpha * l_scr[h] + jnp.sum(p, axis=0, keepdims=True)
            pv = _dot(vT_ref[h * VX:h * VX + DB_V, pl.ds(ks, BK_B)], p.astype(BF16))
            acc_scr[h] = alpha * acc_scr[h] + pv
            m_scr[h] = m_new
        return carry

    lax.fori_loop(0, nk, step, 0)
    o = jnp.concatenate([acc_scr[h] / l_scr[h] for h in range(HB)], axis=0)
    o_ref[...] = o.T.astype(BF16)


def _attn_b_bounded_kernel(qT_ref, k_ref, vT_ref, o_ref, acc_scr, *, nk):
    acc_scr[...] = jnp.zeros(acc_scr.shape, F32)

    assert nk % UNROLL_B == 0
    tiles = [(j, h) for j in range(UNROLL_B) for h in range(HB)]

    def step(it, carry):
        def ks(j):
            return pl.multiple_of((it * UNROLL_B + j) * BK_B, BK_B)

        def qk(t):
            j, h = tiles[t]
            return _dot(k_ref[pl.ds(ks(j), BK_B), h * DB_PAD:(h + 1) * DB_PAD],
                        qT_ref[h * DB_PAD:(h + 1) * DB_PAD, :])

        ahead = [qk(i) for i in range(PREFETCH)]
        for t, (j, h) in enumerate(tiles):
            s = ahead.pop(0)
            if t + PREFETCH < len(tiles):
                ahead.append(qk(t + PREFETCH))
            p = jnp.exp2(s).astype(BF16)
            acc_scr[h] += _dot(vT_ref[h * VX:(h + 1) * VX, pl.ds(ks(j), BK_B)], p)
        return carry

    lax.fori_loop(0, nk // UNROLL_B, step, 0)
    o = jnp.concatenate([acc_scr[h, 0:DB_V] / acc_scr[h, DB_V:DB_V + 1] for h in range(HB)], axis=0)
    o_ref[...] = o.T.astype(BF16)


def _attn_b(qT, k, vT, B, S, bounded):
    nq = S // BQ_B
    nk = S // BK_B
    if bounded:
        body = functools.partial(_attn_b_bounded_kernel, nk=nk)
        scratch = [pltpu.VMEM((HB, VX, BQ_B), F32)]
    else:
        body = functools.partial(_attn_b_kernel, nk=nk)
        scratch = [pltpu.VMEM((HB, 1, BQ_B), F32), pltpu.VMEM((HB, 1, BQ_B), F32), pltpu.VMEM((HB, DB_V, BQ_B), F32)]
    return pl.pallas_call(
        body,
        grid=(B, nq),
        in_specs=[
            pl.BlockSpec((HB * DB_PAD, BQ_B), lambda b, q: (0, b * nq + q)),
            pl.BlockSpec((S, HB * DB_PAD), lambda b, q: (b, 0)),
            pl.BlockSpec((HB * VX, S), lambda b, q: (0, b)),
        ],
        out_specs=pl.BlockSpec((BQ_B, HB * DB_V), lambda b, q: (b * nq + q, 0)),
        out_shape=jax.ShapeDtypeStruct((B * S, HB * DB_V), BF16),
        scratch_shapes=scratch,
        compiler_params=pltpu.CompilerParams(dimension_semantics=("arbitrary", "arbitrary"),
                                             vmem_limit_bytes=VMEM_LIMIT),
        name="attn_b_bounded" if bounded else "attn_b",
    )(qT, k, vT)


def _attn_c_kernel(sc_ref, qT_ref, k_ref, vT_ref, bm_ref, o_ref, *, nb, S):
    n = pl.program_id(1)
    start = pl.multiple_of(jnp.clip((n - 1) * BLOCK, 0, S - KW_C), BLOCK)
    case = jnp.where(n == 0, 0, jnp.where(n == nb - 1, 2, 1))
    kwin = k_ref[pl.ds(start, KW_C), :]
    row = lax.broadcasted_iota(jnp.int32, (KVH_C * DC, BLOCK), 0)
    outs = []
    for h in range(HC):
        j = h // G_C
        qh = qT_ref[h * DC:(h + 1) * DC, :]
        zz = jnp.zeros_like(qh)
        w = jnp.concatenate([qh, zz] if j == 0 else [zz, qh], axis=0)
        s = _dot(kwin, w) + bm_ref[case, h]
        sink = sc_ref[h]
        m = jnp.maximum(jnp.max(s, axis=0, keepdims=True), sink)
        p = jnp.exp2(s - m)
        den = jnp.sum(p, axis=0, keepdims=True) + jnp.exp2(sink - m)
        pv = _dot(vT_ref[j * VX:j * VX + DC, pl.ds(start, KW_C)], p.astype(BF16))
        outs.append(pv / den)
    del row
    o_ref[...] = jnp.concatenate(outs, axis=0).T.astype(BF16)


def _attn_c_bounded_kernel(sc_ref, qT_ref, k_ref, vT_ref, e_ref, o_ref, *, n_sub, S):
    g = pl.program_id(1)
    tiles = [(u, h) for u in range(SUB_C) for h in range(HC)]

    def window(u):
        n = g * SUB_C + u
        start = pl.multiple_of(jnp.clip(n * QB_C - WINDOW, 0, S - KWB_C), WINDOW)
        case = jnp.where(n == 0, 0, jnp.where(n == n_sub - 1, 2, 1))
        return start, case

    def qk(t):
        u, h = tiles[t]
        start, _ = window(u)
        qh = qT_ref[h * DC:(h + 1) * DC, u * QB_C:(u + 1) * QB_C]
        zz = jnp.zeros_like(qh)
        w = jnp.concatenate([qh, zz] if h // G_C == 0 else [zz, qh], axis=0)
        return _dot(k_ref[pl.ds(start, KWB_C), :], w)

    ahead = [qk(i) for i in range(PREFETCH)]
    outs = []
    for t, (u, h) in enumerate(tiles):
        s = ahead.pop(0)
        if t + PREFETCH < len(tiles):
            ahead.append(qk(t + PREFETCH))
        start, case = window(u)
        j = h // G_C
        p = (jnp.exp2(s) * e_ref[case, h]).astype(BF16)
        pv = _dot(vT_ref[j * VX:(j + 1) * VX, pl.ds(start, KWB_C)], p)
        outs.append(pv[0:DC] / (pv[DC:DC + 1] + sc_ref[HC + h]))
        if h == HC - 1:
            o_ref[u * QB_C:(u + 1) * QB_C, :] = jnp.concatenate(outs, axis=0).T.astype(BF16)
            outs = []


def _attn_c(qT, k, vT, lw, B, S, bounded):
    if bounded:
        n_sub = S // QB_C
        assert n_sub >= 3 and n_sub % SUB_C == 0
        nb, bq = n_sub // SUB_C, QB_C * SUB_C
        body = functools.partial(_attn_c_bounded_kernel, n_sub=n_sub, S=S)
        tile, sc = lw['e_c'], lw['sc_c_bounded']
    else:
        nb, bq = S // BLOCK, BLOCK
        assert nb >= 3
        body = functools.partial(_attn_c_kernel, nb=nb, S=S)
        tile, sc = lw['bm_c'], lw['sc_c']
    return pl.pallas_call(
        body,
        grid=(B, nb),
        in_specs=[
            pl.BlockSpec(memory_space=pltpu.SMEM),
            pl.BlockSpec((HC * DC, bq), lambda b, n: (0, b * nb + n)),
            pl.BlockSpec((S, KVH_C * DC), lambda b, n: (b, 0)),
            pl.BlockSpec((KVH_C * VX, S), lambda b, n: (0, b)),
            pl.BlockSpec(tile.shape, lambda b, n: (0, 0, 0, 0)),
        ],
        out_specs=pl.BlockSpec((bq, HC * DC), lambda b, n: (b * nb + n, 0)),
        out_shape=jax.ShapeDtypeStruct((B * S, HC * DC), BF16),
        compiler_params=pltpu.CompilerParams(dimension_semantics=("arbitrary", "arbitrary"),
                                             vmem_limit_bytes=VMEM_LIMIT),
        name="attn_c_bounded" if bounded else "attn_c",
    )(sc, qT, k, vT, tile)


def _out_kernel(x_ref, ya_ref, yb_ref, yc_ref, wo_ref, g_ref, wrT_ref, xn_ref, h2_ref, affT_ref):
    na, nb_ = HA * DA_V, HA * DA_V + HB * DB_V
    xn = (x_ref[...] + _dot(ya_ref[...], wo_ref[0:na, :]) + _dot(yb_ref[...], wo_ref[na:nb_, :])
          + _dot(yc_ref[...], wo_ref[nb_:, :]))
    xn_ref[...] = xn
    h2 = (xn * lax.rsqrt(jnp.mean(xn * xn, axis=-1, keepdims=True) + EPS) * g_ref[...]).astype(BF16)
    h2_ref[...] = h2
    lg = _nt_dot(wrT_ref[...], h2)
    e = jnp.exp(lg - jnp.max(lg, axis=0, keepdims=True))
    affT_ref[...] = e / jnp.sum(e, axis=0, keepdims=True)


def _out_proj(x2, ya, yb, yc, lw):
    T = x2.shape[0]
    tm = TM_OUT

    def full(a):
        return pl.BlockSpec(a.shape, lambda i: (0,) * a.ndim)

    return pl.pallas_call(
        _out_kernel,
        grid=(T // tm,),
        in_specs=[pl.BlockSpec((tm, D_MODEL), lambda i: (i, 0)),
                  pl.BlockSpec((tm, ya.shape[1]), lambda i: (i, 0)),
                  pl.BlockSpec((tm, yb.shape[1]), lambda i: (i, 0)),
                  pl.BlockSpec((tm, yc.shape[1]), lambda i: (i, 0)),
                  full(lw['w_out']), full(lw['g_ffn']), full(lw['w_rT'])],
        out_specs=[pl.BlockSpec((tm, D_MODEL), lambda i: (i, 0)),
                   pl.BlockSpec((tm, D_MODEL), lambda i: (i, 0)),
                   pl.BlockSpec((N_EXPERTS, tm), lambda i: (0, i))],
        out_shape=[jax.ShapeDtypeStruct((T, D_MODEL), F32),
                   jax.ShapeDtypeStruct((T, D_MODEL), BF16),
                   jax.ShapeDtypeStruct((N_EXPERTS, T), F32)],
        compiler_params=pltpu.CompilerParams(dimension_semantics=("arbitrary",), vmem_limit_bytes=VMEM_LIMIT),
        name="out_proj",
    )(x2, ya, yb, yc, lw['w_out'], lw['g_ffn'], lw['w_rT'])


def _topk_kernel(aff_ref, thr_ref, tcut_ref, *, cap, T):
    capf = float(cap)

    def bits():
        return lax.bitcast_convert_type(aff_ref[...], jnp.int32)

    def count(mask):
        return jnp.sum(jnp.where(mask, 1.0, 0.0), axis=1, keepdims=True)

    def vbody(i, v):
        cand = v | jnp.left_shift(jnp.int32(1), 30 - i)
        return jnp.where(count(bits() >= cand) >= capf, cand, v)

    thr = lax.fori_loop(0, 31, vbody, jnp.zeros((N_EXPERTS, 1), jnp.int32))
    need = capf - count(bits() > thr)
    nbits = max(1, int(math.ceil(math.log2(T))))

    def tbody(i, c):
        cand = c | jnp.left_shift(jnp.int32(1), nbits - 1 - i)
        idx = lax.broadcasted_iota(jnp.int32, (N_EXPERTS, T), 1)
        f = count((bits() == thr) & (idx < cand))
        return jnp.where(f < need, cand, c)

    tcut = lax.fori_loop(0, nbits, tbody, jnp.zeros((N_EXPERTS, 1), jnp.int32))
    thr_ref[...] = jnp.broadcast_to(thr, thr_ref.shape)
    tcut_ref[...] = jnp.broadcast_to(tcut, tcut_ref.shape)


def _topk_thresholds(affT):
    E, T = affT.shape
    cap = CAPACITY_FACTOR * T // N_EXPERTS
    return pl.pallas_call(
        functools.partial(_topk_kernel, cap=cap, T=T),
        grid=(1,),
        in_specs=[pl.BlockSpec((E, T), lambda i: (0, 0))],
        out_specs=[pl.BlockSpec((E, 128), lambda i: (0, 0))] * 2,
        out_shape=[jax.ShapeDtypeStruct((E, 128), jnp.int32)] * 2,
        compiler_params=pltpu.CompilerParams(dimension_semantics=("arbitrary",), vmem_limit_bytes=VMEM_LIMIT),
        name="topk_thr",
    )(affT)


def _moe_kernel(x_ref, h2_ref, affT_ref, thr_ref, tcut_ref, tri_ref, wg_ref, wu_ref, wd_ref, o_ref, pos_scr):
    i = pl.program_id(0)
    e = pl.program_id(1)
    tt = TT_MOE

    @pl.when(e == 0)
    def _():
        b = lax.bitcast_convert_type(affT_ref[...], jnp.int32)
        thr = thr_ref[:, 0:1]
        tc = tcut_ref[:, 0:1]
        tg = i * (tt * SUBS_MOE) + lax.broadcasted_iota(jnp.int32, b.shape, 1)
        sel = (b > thr) | ((b == thr) & (tg <= tc))
        for sub in range(SUBS_MOE):
            sl = slice(sub * tt, (sub + 1) * tt)
            pos = _dot(jnp.where(sel[:, sl], 1.0, 0.0).astype(BF16), tri_ref[...])
            pos_scr[:, sl] = jnp.where(sel[:, sl], pos, -1.0)
        o_ref[...] = x_ref[...]

    for sub in range(SUBS_MOE):
        lanes = slice(sub * tt, (sub + 1) * tt)
        rows = pl.ds(sub * tt, tt)
        posm = pos_scr[pl.ds(e, 1), lanes]
        gate = affT_ref[pl.ds(e, 1), lanes]
        n_sel = jnp.sum(jnp.where(posm >= 0.0, 1, 0))
        n_ch = (n_sel + CH_MOE - 1) // CH_MOE

        def chunk(c, carry, posm=posm, gate=gate, rows=rows):
            rel = posm - (c * CH_MOE).astype(F32)
            jj = lax.broadcasted_iota(jnp.int32, (CH_MOE, tt), 0).astype(F32)
            hit = jnp.broadcast_to(rel, (CH_MOE, tt)) == jj
            xe = _dot(jnp.where(hit, 1.0, 0.0).astype(BF16), h2_ref[rows, :]).astype(BF16)
            g = _dot(xe, wg_ref[...])
            u = _dot(xe, wu_ref[...])
            hid = (g * jax.nn.sigmoid(g) * u).astype(BF16)
            ye = _dot(hid, wd_ref[...])
            gc = jnp.sum(jnp.where(hit, jnp.broadcast_to(gate, hit.shape), 0.0), axis=1, keepdims=True)
            yeb = jnp.concatenate([(ye * gc).astype(BF16), jnp.zeros((CHP_MOE - CH_MOE, ye.shape[1]), BF16)],
                                  axis=0)
            relT = jnp.broadcast_to(rel, (CHP_MOE, tt)).T
            ll = lax.broadcasted_iota(jnp.int32, (tt, CHP_MOE), 1).astype(F32)
            hitT = jnp.where(relT == ll, 1.0, 0.0).astype(BF16)
            o_ref[rows, :] += _dot(hitT, yeb)
            return carry

        lax.fori_loop(0, n_ch, chunk, 0)


def _moe(xn, h2, affT, thr, tcut, lw):
    T = xn.shape[0]
    tt = TT_MOE * SUBS_MOE
    once = pl.Buffered(1)
    return pl.pallas_call(
        _moe_kernel,
        grid=(T // tt, N_EXPERTS),
        in_specs=[
            pl.BlockSpec((tt, D_MODEL), lambda i, e: (i, 0), pipeline_mode=once),
            pl.BlockSpec((tt, D_MODEL), lambda i, e: (i, 0), pipeline_mode=once),
            pl.BlockSpec((N_EXPERTS, tt), lambda i, e: (0, i)),
            pl.BlockSpec((N_EXPERTS, 128), lambda i, e: (0, 0)),
            pl.BlockSpec((N_EXPERTS, 128), lambda i, e: (0, 0)),
            pl.BlockSpec((TT_MOE, TT_MOE), lambda i, e: (0, 0), pipeline_mode=once),
            pl.BlockSpec((None, D_MODEL, D_EXPERT), lambda i, e: (e, 0, 0)),
            pl.BlockSpec((None, D_MODEL, D_EXPERT), lambda i, e: (e, 0, 0)),
            pl.BlockSpec((None, D_EXPERT, D_MODEL), lambda i, e: (e, 0, 0)),
        ],
        out_specs=pl.BlockSpec((tt, D_MODEL), lambda i, e: (i, 0)),
        out_shape=jax.ShapeDtypeStruct((T, D_MODEL), F32),
        scratch_shapes=[pltpu.VMEM((N_EXPERTS, tt), F32)],
        compiler_params=pltpu.CompilerParams(dimension_semantics=("arbitrary", "arbitrary"),
                                             vmem_limit_bytes=VMEM_LIMIT),
        name="moe_ffn",
    )(xn, h2, affT, thr, tcut, lw['tri'], lw['w_gate'], lw['w_up'], lw['w_down'])


def _t5_bucket(rel):
    nb = NUM_BUCKETS // 2
    max_exact = nb // 2
    ret = jnp.where(rel > 0, nb, 0)
    n = jnp.abs(rel)
    nf = jnp.maximum(n, 1).astype(jnp.float32)
    large = max_exact + (jnp.log(nf / max_exact) / math.log(MAX_DISTANCE / max_exact) * (nb - max_exact)).astype(jnp.int32)
    large = jnp.minimum(large, nb - 1)
    return ret + jnp.where(n < max_exact, n, large)


def _col(v):
    return v.astype(F32).reshape(-1, 1)


def _pad_heads(w, n_heads, d, d_pad):
    k = w.shape[1]
    w3 = w.reshape(n_heads, d, k)
    return jnp.pad(w3, ((0, 0), (0, d_pad - d), (0, 0))).reshape(n_heads * d_pad, k)


def _toeplitz_tiles(fn, offsets, nk, nq):
    span = nk + nq
    out = []
    for off in offsets:
        u = fn(off + nk - 1 - jnp.arange(span, dtype=jnp.int32))
        rows = jnp.tile(u, (1, nk))[:, :nk * (span - 1)].reshape(u.shape[0], nk, span - 1)
        out.append(rows[:, :, nk - 1:nk - 1 + nq])
    return jnp.stack(out, axis=0)


def _prep_shared(rel_bias):
    rb = rel_bias.astype(F32) * LOG2E

    def bias_a(rel):
        return rb[:, :HA][_t5_bucket(rel)].T

    def bias_c(rel):
        return rb[:, HA:][_t5_bucket(rel)].T

    def in_band(rel):
        return (jnp.abs(rel) <= WINDOW)[None, :]

    bmax = jnp.max(rb[:, :HA], axis=0)
    near_offs = [d * BK_A for d in range(-1, NEAR_A - 1)]
    near_a = _toeplitz_tiles(lambda rel: jnp.exp2(bias_a(rel) - bmax[:, None]), near_offs, BK_A, BQ_A)
    far = _t5_bucket(jnp.array([-(1 << 20), 1 << 20], dtype=jnp.int32))
    sc_far = jnp.concatenate([rb[far[1], :HA], bmax, rb[far[0], :HA]])
    a_bounded = {
        'near': _toeplitz_tiles(lambda rel: jnp.exp2(bias_a(rel)), near_offs, BK_A, BQ_A),
        'sc': jnp.concatenate([jnp.exp2(rb[far[0], :HA]), jnp.exp2(rb[far[1], :HA])]),
        'bias_mag': jnp.max(jnp.abs(rb[:, :HA])),
    }
    bm_c = _toeplitz_tiles(lambda rel: jnp.where(in_band(rel), bias_c(rel), NEG_BIG),
                           [0, -BLOCK, -2 * BLOCK], KW_C, BLOCK)
    c_bounded = {
        'e': _toeplitz_tiles(lambda rel: jnp.where(in_band(rel), jnp.exp2(bias_c(rel)), 0.0),
                             [0, -WINDOW, -(KWB_C - QB_C)], KWB_C, QB_C),
        'bias_mag': jnp.max(jnp.abs(rb[:, HA:])),
    }
    return near_a, sc_far, bm_c, a_bounded, c_bounded


def _rope_tables(S):
    inv = 1.0 / (ROPE_BASE ** (jnp.arange(ROPE_HALF, dtype=jnp.float32) / ROPE_HALF))
    ang = jnp.arange(S, dtype=jnp.int32).astype(jnp.float32)[:, None] * inv[None, :]
    return jnp.cos(ang).T, jnp.sin(ang).T


def _logit_bounds(l, p, a_bounded, c_bounded, sa, sb, scc):
    qa, ka = p['qn_a'][l].astype(F32), p['kn_a'][l].astype(F32)
    bound_a = 1.01 * DA_QK * jnp.max(jnp.abs(qa * ka)) * sa + 2.0 * a_bounded['bias_mag']
    qb, kb = p['qn_b'][l].astype(F32), p['kn_b'][l].astype(F32)
    bound_b = 1.01 * DB_QK * jnp.max(jnp.abs(qb)) * jnp.max(jnp.abs(kb)) * sb
    qc, kc = p['qn_c'][l].astype(F32), p['kn_c'][l].astype(F32)
    bound_c = (1.01 * DC * jnp.max(jnp.abs(qc * kc)) * scc + 2.0 * c_bounded['bias_mag']
               + jnp.max(jnp.abs(p['sink_c'][l].astype(F32))) * LOG2E)
    return bound_a <= BOUND_LIMIT, bound_b <= BOUND_LIMIT, bound_c <= BOUND_LIMIT


def _prep_layer(l, p, shared):
    near_a, sc_far, bm_c, a_bounded, c_bounded = shared
    lam_init = 0.8 - 0.6 * math.exp(-0.3 * l)
    lam = (jnp.exp(jnp.sum(p['lam_q1'][l].astype(F32) * p['lam_k1'][l].astype(F32)))
           - jnp.exp(jnp.sum(p['lam_q2'][l].astype(F32) * p['lam_k2'][l].astype(F32))) + lam_init)
    w_uq = p['w_uq'][l]
    w_ukv = p['w_ukv'][l].reshape(KV_RANK, HB, DB_NOPE + DB_V)
    sa = DA_QK ** -0.5 * LOG2E
    sb = DB_QK ** -0.5 * LOG2E
    scc = DC ** -0.5 * LOG2E
    zpad = jnp.zeros((DB_PAD - DB_QK,), F32)
    lw = {
        'g_mix': p['norm_mix_g'][l].astype(F32).reshape(1, D_MODEL),
        'w_inT': p['w_in'][l].T.astype(BF16),
        'w_uqT': _pad_heads(w_uq.T, HB, DB_QK, DB_PAD).astype(BF16),
        'w_ukT': w_ukv[:, :, :DB_NOPE].reshape(KV_RANK, HB * DB_NOPE).T.astype(BF16),
        'w_uvT': w_ukv[:, :, DB_NOPE:].reshape(KV_RANK, HB * DB_V).T.astype(BF16),
        'gqa': _col(jnp.tile(p['qn_a'][l], 2 * HA) * sa),
        'gka': _col(jnp.tile(p['kn_a'][l], 2 * HA)),
        'gcq': _col(p['g_cq'][l]),
        'gckv': _col(p['g_ckv'][l]),
        'gqb': _col(jnp.tile(jnp.concatenate([p['qn_b'][l].astype(F32) * sb, zpad]), HB)),
        'gkb': _col(jnp.tile(jnp.concatenate([p['kn_b'][l].astype(F32), zpad]), HB)),
        'gqc': _col(jnp.tile(p['qn_c'][l], HC) * scc),
        'gkc': _col(jnp.tile(p['kn_c'][l], KVH_C)),
        'near_a': near_a,
        'sc_a': jnp.concatenate([sc_far, lam.reshape(1)]).astype(F32),
        'subln': _col(jnp.tile(p['subln_a'][l], HA) * (1.0 - lam_init)),
        'bm_c': bm_c,
        'sc_c': p['sink_c'][l].astype(F32) * LOG2E,
        'w_out': p['w_out'][l].astype(BF16),
        'g_ffn': p['norm_ffn_g'][l].astype(F32).reshape(1, D_MODEL),
        'w_rT': p['w_router'][l].T.astype(BF16),
        'w_gate': p['w_gate'][l].astype(BF16),
        'w_up': p['w_up'][l].astype(BF16),
        'w_down': p['w_down'][l].astype(BF16),
        'tri': jnp.triu(jnp.ones((TT_MOE, TT_MOE), F32), k=1).astype(BF16),
        'near_a_bounded': a_bounded['near'],
        'sc_a_bounded': jnp.concatenate([a_bounded['sc'], lam.reshape(1)]).astype(F32),
    }
    lw['e_c'] = c_bounded['e']
    lw['sc_c_bounded'] = jnp.concatenate([lw['sc_c'], jnp.exp2(lw['sc_c'])])
    lw['bounded_a'], lw['bounded_b'], lw['bounded_c'] = _logit_bounds(l, p, a_bounded, c_bounded, sa, sb, scc)
    return lw


def _trunk(x, layers):
    B, S, D = x.shape
    x2 = x.reshape(B * S, D)
    cosT, sinT = _rope_tables(S)
    for lw in layers:
        lw = dict(lw, cosT=cosT, sinT=sinT)
        qaT, ka, vaT, qbT, kb, vbT, qcT, kc, vcT = _in_proj(x2, lw, S)
        ya = lax.cond(lw['bounded_a'],
                      lambda q, k, v: _attn_a(q, k, v, lw, B, S, True),
                      lambda q, k, v: _attn_a(q, k, v, lw, B, S, False), qaT, ka, vaT)
        yb = lax.cond(lw['bounded_b'],
                      lambda q, k, v: _attn_b(q, k, v, B, S, True),
                      lambda q, k, v: _attn_b(q, k, v, B, S, False), qbT, kb, vbT)
        yc = lax.cond(lw['bounded_c'],
                      lambda q, k, v: _attn_c(q, k, v, lw, B, S, True),
                      lambda q, k, v: _attn_c(q, k, v, lw, B, S, False), qcT, kc, vcT)
        xn, h2, affT = _out_proj(x2, ya, yb, yc, lw)
        thr, tcut = _topk_thresholds(affT)
        x2 = _moe(xn, h2, affT, thr, tcut, lw)
    return x2.reshape(B, S, D)


def kernel(x_prompt, x_sample, rel_bias, norm_mix_g, w_in, qn_a, kn_a, lam_q1, lam_k1, lam_q2, lam_k2, subln_a, g_cq, w_uq, g_ckv, w_ukv, qn_b, kn_b, qn_c, kn_c, sink_c, w_out, norm_ffn_g, w_router, w_gate, w_up, w_down):
    p = dict(norm_mix_g=norm_mix_g, w_in=w_in, qn_a=qn_a, kn_a=kn_a, lam_q1=lam_q1, lam_k1=lam_k1,
             lam_q2=lam_q2, lam_k2=lam_k2, subln_a=subln_a, g_cq=g_cq, w_uq=w_uq, g_ckv=g_ckv, w_ukv=w_ukv,
             qn_b=qn_b, kn_b=kn_b, qn_c=qn_c, kn_c=kn_c, sink_c=sink_c, w_out=w_out, norm_ffn_g=norm_ffn_g,
             w_router=w_router, w_gate=w_gate, w_up=w_up, w_down=w_down)
    shared = _prep_shared(rel_bias)
    layers = [_prep_layer(l, p, shared) for l in range(w_in.shape[0])]
    return (_trunk(x_prompt, layers), _trunk(x_sample, layers))
```

```python
import functools
import math

import jax
import jax.numpy as jnp
import numpy as np
from jax import lax
from jax.experimental import pallas as pl
from jax.experimental.pallas import tpu as pltpu

D_MODEL = 1024
BLOCK = 128
HA, DA_QK, DA_V = 4, 32, 64
HB, Q_RANK, KV_RANK, DB_NOPE, DB_ROPE, DB_V = 6, 384, 256, 64, 32, 64
ROPE_BASE = 10000.0
HC, KVH_C, DC, WINDOW = 6, 2, 64, 128
G_C = HC // KVH_C
NUM_BUCKETS, MAX_DISTANCE = 32, 128
N_EXPERTS, D_EXPERT, CAPACITY_FACTOR = 16, 1024, 2
EPS = 1e-6
DEPTH = 2

DB_QK = DB_NOPE + DB_ROPE
DB_PAD = 128
VX = 80
BOUND_LIMIT = 80.0
ROPE_HALF = DB_ROPE // 2
IN_SPLITS = (HA * 2 * DA_QK, HA * 2 * DA_QK, HA * DA_V, Q_RANK, KV_RANK, DB_ROPE, HC * DC, KVH_C * DC, KVH_C * DC)
IN_OFFS = tuple(int(v) for v in np.cumsum((0,) + IN_SPLITS))
IN_COLS = IN_OFFS[-1]
LOG2E = 1.4426950408889634
NEG_BIG = -1e30

F32 = jnp.float32
BF16 = jnp.bfloat16

TM_IN = 512
TM_OUT = 512
BQ_A = 512
BK_A = 512
UNROLL_A = 2
NEAR_A = BQ_A // BK_A + 2
assert BQ_A % BK_A == 0 and BK_A >= MAX_DISTANCE
BQ_B = 512
BK_B = 512
UNROLL_B = 2
KW_C = 3 * BLOCK
QB_C = 256
KWB_C = QB_C + 2 * WINDOW
SUB_C = 2
TT_MOE = 1024
SUBS_MOE = 2
CH_MOE = 160
CHP_MOE = 256
PREFETCH = 3
VMEM_LIMIT = 56 * 1024 * 1024


def _nt_dot(a, b):
    return lax.dot_general(a, b, (((1,), (1,)), ((), ())), preferred_element_type=F32)


def _dot(a, b):
    return jnp.dot(a, b, preferred_element_type=F32)


def _group_norm_rows(y, gs, n_valid):
    rows, tm = y.shape
    y3 = y.reshape(rows // gs, gs, tm)
    ms = jnp.sum(y3 * y3, axis=1, keepdims=True) * (1.0 / n_valid)
    return (y3 * lax.rsqrt(ms + EPS)).reshape(rows, tm)


def _in_kernel(x_ref, g_ref, winT_ref, wuqT_ref, wukT_ref, wuvT_ref,
               gqa_ref, gka_ref, gcq_ref, gckv_ref, gqb_ref, gkb_ref, gqc_ref, gkc_ref,
               cos_ref, sin_ref,
               qaT_ref, ka_ref, vaT_ref, qbT_ref, kb_ref, vbT_ref, qcT_ref, kc_ref, vcT_ref):
    x = x_ref[...]
    h = x * lax.rsqrt(jnp.mean(x * x, axis=-1, keepdims=True) + EPS) * g_ref[...]
    projT = _dot(winT_ref[...], h.T.astype(BF16))

    def seg(i):
        return projT[IN_OFFS[i]:IN_OFFS[i + 1]]

    cos = cos_ref[...]
    sin = sin_ref[...]

    qaT_ref[...] = (_group_norm_rows(seg(0), DA_QK, DA_QK) * gqa_ref[...]).astype(BF16)
    kaT = _group_norm_rows(seg(1), DA_QK, DA_QK) * gka_ref[...]
    ka_ref[...] = kaT.T.astype(BF16)
    def with_ones_rows(v, n_heads, d):
        tm = v.shape[1]
        tail = jnp.where(lax.broadcasted_iota(jnp.int32, (VX - d, tm), 0) == 0, 1.0, 0.0)
        rows = []
        for hh in range(n_heads):
            rows += [v[hh * d:(hh + 1) * d], tail]
        return jnp.concatenate(rows, axis=0).astype(BF16)

    vaT_ref[...] = with_ones_rows(seg(2), HA, DA_V)

    def rope_store(dst_rows, y, h):
        o = h * DB_PAD
        x1 = y[o + DB_NOPE:o + DB_NOPE + ROPE_HALF]
        x2 = y[o + DB_NOPE + ROPE_HALF:o + DB_QK]
        dst_rows.append(y[o:o + DB_NOPE])
        dst_rows.append(x1 * cos - x2 * sin)
        dst_rows.append(x2 * cos + x1 * sin)
        dst_rows.append(jnp.zeros((DB_PAD - DB_QK, y.shape[1]), F32))

    cqT = seg(3)
    cqn = (cqT * lax.rsqrt(jnp.mean(cqT * cqT, axis=0, keepdims=True) + EPS) * gcq_ref[...]).astype(BF16)
    qf = _group_norm_rows(_dot(wuqT_ref[...], cqn), DB_PAD, DB_QK) * gqb_ref[...]
    rows = []
    for hh in range(HB):
        rope_store(rows, qf, hh)
    qbT_ref[...] = jnp.concatenate(rows, axis=0).astype(BF16)

    ckvT = seg(4)
    ckvn = (ckvT * lax.rsqrt(jnp.mean(ckvT * ckvT, axis=0, keepdims=True) + EPS) * gckv_ref[...]).astype(BF16)
    knope = _dot(wukT_ref[...], ckvn)
    vbT_ref[...] = with_ones_rows(_dot(wuvT_ref[...], ckvn), HB, DB_V)
    krope = seg(5)
    zpad = jnp.zeros((DB_PAD - DB_QK, krope.shape[1]), F32)
    rows = []
    for hh in range(HB):
        rows += [knope[hh * DB_NOPE:(hh + 1) * DB_NOPE], krope, zpad]
    kf = _group_norm_rows(jnp.concatenate(rows, axis=0), DB_PAD, DB_QK) * gkb_ref[...]
    rows = []
    for hh in range(HB):
        rope_store(rows, kf, hh)
    kb_ref[...] = jnp.concatenate(rows, axis=0).T.astype(BF16)

    qcT_ref[...] = (_group_norm_rows(seg(6), DC, DC) * gqc_ref[...]).astype(BF16)
    kcT = _group_norm_rows(seg(7), DC, DC) * gkc_ref[...]
    kc_ref[...] = kcT.T.astype(BF16)
    vcT_ref[...] = with_ones_rows(seg(8), KVH_C, DC)


def _in_proj(x2, lw, S):
    T = x2.shape[0]
    tm = TM_IN
    nt = T // tm
    spb = S // tm

    def full(a):
        return pl.BlockSpec(a.shape, lambda i: (0,) * a.ndim)

    consts = [lw['g_mix'], lw['w_inT'], lw['w_uqT'], lw['w_ukT'], lw['w_uvT'],
              lw['gqa'], lw['gka'], lw['gcq'], lw['gckv'], lw['gqb'], lw['gkb'], lw['gqc'], lw['gkc']]
    in_specs = ([pl.BlockSpec((tm, D_MODEL), lambda i: (i, 0))] + [full(a) for a in consts]
                + [pl.BlockSpec((ROPE_HALF, tm), lambda i: (0, i % spb))] * 2)

    def fm(rows):
        return jax.ShapeDtypeStruct((rows, T), BF16), pl.BlockSpec((rows, tm), lambda i: (0, i))

    def tk(cols):
        return jax.ShapeDtypeStruct((T, cols), BF16), pl.BlockSpec((tm, cols), lambda i: (i, 0))

    outs = [fm(HA * 2 * DA_QK), tk(HA * 2 * DA_QK), fm(HA * VX),
            fm(HB * DB_PAD), tk(HB * DB_PAD), fm(HB * VX),
            fm(HC * DC), tk(KVH_C * DC), fm(KVH_C * VX)]
    return pl.pallas_call(
        _in_kernel,
        grid=(nt,),
        in_specs=in_specs,
        out_specs=[o[1] for o in outs],
        out_shape=[o[0] for o in outs],
        compiler_params=pltpu.CompilerParams(dimension_semantics=("arbitrary",), vmem_limit_bytes=VMEM_LIMIT),
        name="in_proj",
    )(x2, *consts, lw['cosT'], lw['sinT'])


def _attn_a_kernel(sc_ref, qT_ref, k_ref, vT_ref, near_ref, subln_ref, o_ref, w_scr, m_scr, l_scr, acc_scr, *, nk):
    qb = pl.program_id(1)
    n_comb = 2 * HA
    bq = qT_ref.shape[1]
    near_lo = (BQ_A // BK_A) * qb - 1

    for hc in range(n_comb):
        g = hc // 4
        qg = qT_ref[g * 128:(g + 1) * 128, :]
        row = lax.broadcasted_iota(jnp.int32, (128, bq), 0)
        keep = (row // DA_QK) == (hc % 4)
        w_scr[hc] = jnp.where(keep, qg, jnp.zeros_like(qg))
    m_scr[...] = jnp.full(m_scr.shape, NEG_BIG, F32)
    l_scr[...] = jnp.zeros(l_scr.shape, F32)
    acc_scr[...] = jnp.zeros(acc_scr.shape, F32)

    def step(kb, mode):
        ks = pl.multiple_of(kb * BK_A, BK_A)

        def qk(hc):
            g = hc // 4
            return _dot(k_ref[pl.ds(ks, BK_A), g * 128:(g + 1) * 128], w_scr[hc])

        ahead = [qk(i) for i in range(PREFETCH)]
        for hc in range(n_comb):
            h = hc // 2
            s = ahead.pop(0)
            if hc + PREFETCH < n_comb:
                ahead.append(qk(hc + PREFETCH))
            c = sc_ref[(1 - mode) * HA + h]
            cmax = jnp.max(s, axis=0, keepdims=True) + c
            m_old = m_scr[hc]
            m_new = jnp.maximum(m_old, cmax)
            alpha = jnp.exp2(m_old - m_new)
            p = jnp.exp2(s - (m_new - c))
            if mode == 0:
                p = p * near_ref[kb - near_lo, h]
            l_scr[hc] = alpha * l_scr[hc] + jnp.sum(p, axis=0, keepdims=True)
            pv = _dot(vT_ref[h * VX:h * VX + DA_V, pl.ds(ks, BK_A)], p.astype(BF16))
            acc_scr[hc] = alpha * acc_scr[hc] + pv
            m_scr[hc] = m_new

    lo = jnp.maximum(near_lo, 0)
    hi = jnp.minimum(near_lo + NEAR_A, nk)
    lax.fori_loop(0, lo, lambda kb, c: (step(kb, -1), c)[1], 0)
    lax.fori_loop(lo, hi, lambda kb, c: (step(kb, 0), c)[1], 0)
    lax.fori_loop(hi, nk, lambda kb, c: (step(kb, 1), c)[1], 0)

    lam = sc_ref[3 * HA]
    outs = []
    for h in range(HA):
        a = acc_scr[2 * h] / l_scr[2 * h] - lam * (acc_scr[2 * h + 1] / l_scr[2 * h + 1])
        a = a * lax.rsqrt(jnp.mean(a * a, axis=0, keepdims=True) + EPS)
        outs.append(a)
    o = jnp.concatenate(outs, axis=0) * subln_ref[...]
    o_ref[...] = o.T.astype(BF16)


def _attn_a_bounded_kernel(sc_ref, qT_ref, k_ref, vT_ref, near_ref, subln_ref, o_ref, w_scr, acc_scr, *, nk):
    qb = pl.program_id(1)
    n_comb = 2 * HA
    bq = qT_ref.shape[1]
    near_lo = (BQ_A // BK_A) * qb - 1
    for hc in range(n_comb):
        g = hc // 4
        qg = qT_ref[g * 128:(g + 1) * 128, :]
        row = lax.broadcasted_iota(jnp.int32, (128, bq), 0)
        w_scr[hc] = jnp.where((row // DA_QK) == (hc % 4), qg, jnp.zeros_like(qg))
    acc_scr[...] = jnp.zeros(acc_scr.shape, F32)

    lo = jnp.maximum(near_lo, 0)
    hi = jnp.minimum(near_lo + NEAR_A, nk)
    n_far = lo + (nk - hi)

    def step(j0, n_blocks, near):
        def block(jj):
            j = j0 + jj
            if near:
                return j, None
            right = j >= lo
            return jnp.where(right, j - lo + hi, j), jnp.where(right, HA, 0)

        tiles = [(jj, hc) for jj in range(n_blocks) for hc in range(n_comb)]

        def qk(t):
            jj, hc = tiles[t]
            g = hc // 4
            ks = pl.multiple_of(block(jj)[0] * BK_A, BK_A)
            return _dot(k_ref[pl.ds(ks, BK_A), g * 128:(g + 1) * 128], w_scr[hc])

        ahead = [qk(i) for i in range(PREFETCH)]
        for t, (jj, hc) in enumerate(tiles):
            h = hc // 2
            s = ahead.pop(0)
            if t + PREFETCH < len(tiles):
                ahead.append(qk(t + PREFETCH))
            kb, side = block(jj)
            ks = pl.multiple_of(kb * BK_A, BK_A)
            p = jnp.exp2(s)
            if near:
                p = p * near_ref[kb - near_lo, h]
            pv = _dot(vT_ref[h * VX:(h + 1) * VX, pl.ds(ks, BK_A)], p.astype(BF16))
            if near:
                acc_scr[hc] += pv
            else:
                acc_scr[hc] += pv * sc_ref[side + h]

    n_pairs = n_far // UNROLL_A
    lax.fori_loop(0, n_pairs, lambda i, c: (step(i * UNROLL_A, UNROLL_A, False), c)[1], 0)
    lax.fori_loop(n_pairs * UNROLL_A, n_far, lambda j, c: (step(j, 1, False), c)[1], 0)
    lax.fori_loop(lo, hi, lambda kb, c: (step(kb, 1, True), c)[1], 0)

    lam = sc_ref[2 * HA]
    outs = []
    for h in range(HA):
        a0, a1 = acc_scr[2 * h], acc_scr[2 * h + 1]
        a = a0[0:DA_V] / a0[DA_V:DA_V + 1] - lam * (a1[0:DA_V] / a1[DA_V:DA_V + 1])
        outs.append(a * lax.rsqrt(jnp.mean(a * a, axis=0, keepdims=True) + EPS))
    o = jnp.concatenate(outs, axis=0) * subln_ref[...]
    o_ref[...] = o.T.astype(BF16)


def _attn_a(qT, k, vT, lw, B, S, bounded):
    nq = S // BQ_A
    nk = S // BK_A
    n_comb = 2 * HA
    if bounded:
        body = functools.partial(_attn_a_bounded_kernel, nk=nk)
        scratch = [pltpu.VMEM((n_comb, 128, BQ_A), BF16), pltpu.VMEM((n_comb, VX, BQ_A), F32)]
        sc, near = lw['sc_a_bounded'], lw['near_a_bounded']
    else:
        body = functools.partial(_attn_a_kernel, nk=nk)
        scratch = [pltpu.VMEM((n_comb, 128, BQ_A), BF16), pltpu.VMEM((n_comb, 1, BQ_A), F32),
                   pltpu.VMEM((n_comb, 1, BQ_A), F32), pltpu.VMEM((n_comb, DA_V, BQ_A), F32)]
        sc, near = lw['sc_a'], lw['near_a']
    return pl.pallas_call(
        body,
        grid=(B, nq),
        in_specs=[
            pl.BlockSpec(memory_space=pltpu.SMEM),
            pl.BlockSpec((n_comb * DA_QK, BQ_A), lambda b, q: (0, b * nq + q)),
            pl.BlockSpec((S, n_comb * DA_QK), lambda b, q: (b, 0)),
            pl.BlockSpec((HA * VX, S), lambda b, q: (0, b)),
            pl.BlockSpec(near.shape, lambda b, q: (0, 0, 0, 0)),
            pl.BlockSpec((HA * DA_V, 1), lambda b, q: (0, 0)),
        ],
        out_specs=pl.BlockSpec((BQ_A, HA * DA_V), lambda b, q: (b * nq + q, 0)),
        out_shape=jax.ShapeDtypeStruct((B * S, HA * DA_V), BF16),
        scratch_shapes=scratch,
        compiler_params=pltpu.CompilerParams(dimension_semantics=("arbitrary", "arbitrary"),
                                             vmem_limit_bytes=VMEM_LIMIT),
        name="attn_a_bounded" if bounded else "attn_a",
    )(sc, qT, k, vT, near, lw['subln'])


def _attn_b_kernel(qT_ref, k_ref, vT_ref, o_ref, m_scr, l_scr, acc_scr, *, nk):
    m_scr[...] = jnp.full(m_scr.shape, NEG_BIG, F32)
    l_scr[...] = jnp.zeros(l_scr.shape, F32)
    acc_scr[...] = jnp.zeros(acc_scr.shape, F32)

    def step(kb, carry):
        ks = pl.multiple_of(kb * BK_B, BK_B)

        def qk(h):
            return _dot(k_ref[pl.ds(ks, BK_B), h * DB_PAD:(h + 1) * DB_PAD], qT_ref[h * DB_PAD:(h + 1) * DB_PAD, :])

        ahead = [qk(i) for i in range(PREFETCH)]
        for h in range(HB):
            s = ahead.pop(0)
            if h + PREFETCH < HB:
                ahead.append(qk(h + PREFETCH))
            m_old = m_scr[h]
            m_new = jnp.maximum(m_old, jnp.max(s, axis=0, keepdims=True))
            alpha = jnp.exp2(m_old - m_new)
            p = jnp.exp2(s - m_new)
            l_scr[h] = alpha * l_scr[h] + jnp.sum(p, axis=0, keepdims=True)
            pv = _dot(vT_ref[h * VX:h * VX + DB_V, pl.ds(ks, BK_B)], p.astype(BF16))
            acc_scr[h] = alpha * acc_scr[h] + pv
            m_scr[h] = m_new
        return carry

    lax.fori_loop(0, nk, step, 0)
    o = jnp.concatenate([acc_scr[h] / l_scr[h] for h in range(HB)], axis=0)
    o_ref[...] = o.T.astype(BF16)


def _attn_b_bounded_kernel(qT_ref, k_ref, vT_ref, o_ref, acc_scr, *, nk):
    acc_scr[...] = jnp.zeros(acc_scr.shape, F32)

    assert nk % UNROLL_B == 0
    tiles = [(j, h) for j in range(UNROLL_B) for h in range(HB)]

    def step(it, carry):
        def ks(j):
            return pl.multiple_of((it * UNROLL_B + j) * BK_B, BK_B)

        def qk(t):
            j, h = tiles[t]
            return _dot(k_ref[pl.ds(ks(j), BK_B), h * DB_PAD:(h + 1) * DB_PAD],
                        qT_ref[h * DB_PAD:(h + 1) * DB_PAD, :])

        ahead = [qk(i) for i in range(PREFETCH)]
        for t, (j, h) in enumerate(tiles):
            s = ahead.pop(0)
            if t + PREFETCH < len(tiles):
                ahead.append(qk(t + PREFETCH))
            p = jnp.exp2(s).astype(BF16)
            acc_scr[h] += _dot(vT_ref[h * VX:(h + 1) * VX, pl.ds(ks(j), BK_B)], p)
        return carry

    lax.fori_loop(0, nk // UNROLL_B, step, 0)
    o = jnp.concatenate([acc_scr[h, 0:DB_V] / acc_scr[h, DB_V:DB_V + 1] for h in range(HB)], axis=0)
    o_ref[...] = o.T.astype(BF16)


def _attn_b(qT, k, vT, B, S, bounded):
    nq = S // BQ_B
    nk = S // BK_B
    if bounded:
        body = functools.partial(_attn_b_bounded_kernel, nk=nk)
        scratch = [pltpu.VMEM((HB, VX, BQ_B), F32)]
    else:
        body = functools.partial(_attn_b_kernel, nk=nk)
        scratch = [pltpu.VMEM((HB, 1, BQ_B), F32), pltpu.VMEM((HB, 1, BQ_B), F32), pltpu.VMEM((HB, DB_V, BQ_B), F32)]
    return pl.pallas_call(
        body,
        grid=(B, nq),
        in_specs=[
            pl.BlockSpec((HB * DB_PAD, BQ_B), lambda b, q: (0, b * nq + q)),
            pl.BlockSpec((S, HB * DB_PAD), lambda b, q: (b, 0)),
            pl.BlockSpec((HB * VX, S), lambda b, q: (0, b)),
        ],
        out_specs=pl.BlockSpec((BQ_B, HB * DB_V), lambda b, q: (b * nq + q, 0)),
        out_shape=jax.ShapeDtypeStruct((B * S, HB * DB_V), BF16),
        scratch_shapes=scratch,
        compiler_params=pltpu.CompilerParams(dimension_semantics=("arbitrary", "arbitrary"),
                                             vmem_limit_bytes=VMEM_LIMIT),
        name="attn_b_bounded" if bounded else "attn_b",
    )(qT, k, vT)


def _attn_c_kernel(sc_ref, qT_ref, k_ref, vT_ref, bm_ref, o_ref, *, nb, S):
    n = pl.program_id(1)
    start = pl.multiple_of(jnp.clip((n - 1) * BLOCK, 0, S - KW_C), BLOCK)
    case = jnp.where(n == 0, 0, jnp.where(n == nb - 1, 2, 1))
    kwin = k_ref[pl.ds(start, KW_C), :]
    row = lax.broadcasted_iota(jnp.int32, (KVH_C * DC, BLOCK), 0)
    outs = []
    for h in range(HC):
        j = h // G_C
        qh = qT_ref[h * DC:(h + 1) * DC, :]
        zz = jnp.zeros_like(qh)
        w = jnp.concatenate([qh, zz] if j == 0 else [zz, qh], axis=0)
        s = _dot(kwin, w) + bm_ref[case, h]
        sink = sc_ref[h]
        m = jnp.maximum(jnp.max(s, axis=0, keepdims=True), sink)
        p = jnp.exp2(s - m)
        den = jnp.sum(p, axis=0, keepdims=True) + jnp.exp2(sink - m)
        pv = _dot(vT_ref[j * VX:j * VX + DC, pl.ds(start, KW_C)], p.astype(BF16))
        outs.append(pv / den)
    del row
    o_ref[...] = jnp.concatenate(outs, axis=0).T.astype(BF16)


def _attn_c_bounded_kernel(sc_ref, qT_ref, k_ref, vT_ref, e_ref, o_ref, *, n_sub, S):
    g = pl.program_id(1)
    tiles = [(u, h) for u in range(SUB_C) for h in range(HC)]

    def window(u):
        n = g * SUB_C + u
        start = pl.multiple_of(jnp.clip(n * QB_C - WINDOW, 0, S - KWB_C), WINDOW)
        case = jnp.where(n == 0, 0, jnp.where(n == n_sub - 1, 2, 1))
        return start, case

    def qk(t):
        u, h = tiles[t]
        start, _ = window(u)
        qh = qT_ref[h * DC:(h + 1) * DC, u * QB_C:(u + 1) * QB_C]
        zz = jnp.zeros_like(qh)
        w = jnp.concatenate([qh, zz] if h // G_C == 0 else [zz, qh], axis=0)
        return _dot(k_ref[pl.ds(start, KWB_C), :], w)

    ahead = [qk(i) for i in range(PREFETCH)]
    outs = []
    for t, (u, h) in enumerate(tiles):
        s = ahead.pop(0)
        if t + PREFETCH < len(tiles):
            ahead.append(qk(t + PREFETCH))
        start, case = window(u)
        j = h // G_C
        p = (jnp.exp2(s) * e_ref[case, h]).astype(BF16)
        pv = _dot(vT_ref[j * VX:(j + 1) * VX, pl.ds(start, KWB_C)], p)
        outs.append(pv[0:DC] / (pv[DC:DC + 1] + sc_ref[HC + h]))
        if h == HC - 1:
            o_ref[u * QB_C:(u + 1) * QB_C, :] = jnp.concatenate(outs, axis=0).T.astype(BF16)
            outs = []


def _attn_c(qT, k, vT, lw, B, S, bounded):
    if bounded:
        n_sub = S // QB_C
        assert n_sub >= 3 and n_sub % SUB_C == 0
        nb, bq = n_sub // SUB_C, QB_C * SUB_C
        body = functools.partial(_attn_c_bounded_kernel, n_sub=n_sub, S=S)
        tile, sc = lw['e_c'], lw['sc_c_bounded']
    else:
        nb, bq = S // BLOCK, BLOCK
        assert nb >= 3
        body = functools.partial(_attn_c_kernel, nb=nb, S=S)
        tile, sc = lw['bm_c'], lw['sc_c']
    return pl.pallas_call(
        body,
        grid=(B, nb),
        in_specs=[
            pl.BlockSpec(memory_space=pltpu.SMEM),
            pl.BlockSpec((HC * DC, bq), lambda b, n: (0, b * nb + n)),
            pl.BlockSpec((S, KVH_C * DC), lambda b, n: (b, 0)),
            pl.BlockSpec((KVH_C * VX, S), lambda b, n: (0, b)),
            pl.BlockSpec(tile.shape, lambda b, n: (0, 0, 0, 0)),
        ],
        out_specs=pl.BlockSpec((bq, HC * DC), lambda b, n: (b * nb + n, 0)),
        out_shape=jax.ShapeDtypeStruct((B * S, HC * DC), BF16),
        compiler_params=pltpu.CompilerParams(dimension_semantics=("arbitrary", "arbitrary"),
                                             vmem_limit_bytes=VMEM_LIMIT),
        name="attn_c_bounded" if bounded else "attn_c",
    )(sc, qT, k, vT, tile)


def _out_kernel(x_ref, ya_ref, yb_ref, yc_ref, wo_ref, g_ref, wr_ref, xn_ref, h2_ref, affT_ref):
    na, nb_ = HA * DA_V, HA * DA_V + HB * DB_V
    xn = (x_ref[...] + _dot(ya_ref[...], wo_ref[0:na, :]) + _dot(yb_ref[...], wo_ref[na:nb_, :])
          + _dot(yc_ref[...], wo_ref[nb_:, :]))
    xn_ref[...] = xn
    h2 = (xn * lax.rsqrt(jnp.mean(xn * xn, axis=-1, keepdims=True) + EPS) * g_ref[...]).astype(BF16)
    h2_ref[...] = h2
    lg = _dot(h2, wr_ref[...]).T[0:N_EXPERTS]
    e = jnp.exp(lg - jnp.max(lg, axis=0, keepdims=True))
    affT_ref[...] = e / jnp.sum(e, axis=0, keepdims=True)


def _out_proj(x2, ya, yb, yc, lw):
    T = x2.shape[0]
    tm = TM_OUT

    def full(a):
        return pl.BlockSpec(a.shape, lambda i: (0,) * a.ndim)

    return pl.pallas_call(
        _out_kernel,
        grid=(T // tm,),
        in_specs=[pl.BlockSpec((tm, D_MODEL), lambda i: (i, 0)),
                  pl.BlockSpec((tm, ya.shape[1]), lambda i: (i, 0)),
                  pl.BlockSpec((tm, yb.shape[1]), lambda i: (i, 0)),
                  pl.BlockSpec((tm, yc.shape[1]), lambda i: (i, 0)),
                  full(lw['w_out']), full(lw['g_ffn']), full(lw['w_r'])],
        out_specs=[pl.BlockSpec((tm, D_MODEL), lambda i: (i, 0)),
                   pl.BlockSpec((tm, D_MODEL), lambda i: (i, 0)),
                   pl.BlockSpec((N_EXPERTS, tm), lambda i: (0, i))],
        out_shape=[jax.ShapeDtypeStruct((T, D_MODEL), F32),
                   jax.ShapeDtypeStruct((T, D_MODEL), BF16),
                   jax.ShapeDtypeStruct((N_EXPERTS, T), F32)],
        compiler_params=pltpu.CompilerParams(dimension_semantics=("arbitrary",), vmem_limit_bytes=VMEM_LIMIT),
        name="out_proj",
    )(x2, ya, yb, yc, lw['w_out'], lw['g_ffn'], lw['w_r'])


def _topk_kernel(aff_ref, thr_ref, tcut_ref, *, cap, T):
    capf = float(cap)

    def bits():
        return lax.bitcast_convert_type(aff_ref[...], jnp.int32)

    def count(mask):
        return jnp.sum(jnp.where(mask, 1.0, 0.0), axis=1, keepdims=True)

    def vbody(i, v):
        cand = v | jnp.left_shift(jnp.int32(1), 30 - i)
        return jnp.where(count(bits() >= cand) >= capf, cand, v)

    thr = lax.fori_loop(0, 31, vbody, jnp.zeros((N_EXPERTS, 1), jnp.int32))
    need = capf - count(bits() > thr)
    nbits = max(1, int(math.ceil(math.log2(T))))

    def tbody(i, c):
        cand = c | jnp.left_shift(jnp.int32(1), nbits - 1 - i)
        idx = lax.broadcasted_iota(jnp.int32, (N_EXPERTS, T), 1)
        f = count((bits() == thr) & (idx < cand))
        return jnp.where(f < need, cand, c)

    tcut = lax.fori_loop(0, nbits, tbody, jnp.zeros((N_EXPERTS, 1), jnp.int32))
    thr_ref[...] = jnp.broadcast_to(thr, thr_ref.shape)
    tcut_ref[...] = jnp.broadcast_to(tcut, tcut_ref.shape)


def _topk_thresholds(affT):
    E, T = affT.shape
    cap = CAPACITY_FACTOR * T // N_EXPERTS
    return pl.pallas_call(
        functools.partial(_topk_kernel, cap=cap, T=T),
        grid=(1,),
        in_specs=[pl.BlockSpec((E, T), lambda i: (0, 0))],
        out_specs=[pl.BlockSpec((E, 128), lambda i: (0, 0))] * 2,
        out_shape=[jax.ShapeDtypeStruct((E, 128), jnp.int32)] * 2,
        compiler_params=pltpu.CompilerParams(dimension_semantics=("arbitrary",), vmem_limit_bytes=VMEM_LIMIT),
        name="topk_thr",
    )(affT)


def _moe_kernel(x_ref, h2_ref, affT_ref, thr_ref, tcut_ref, tri_ref, wg_ref, wu_ref, wd_ref, o_ref, pos_scr):
    i = pl.program_id(0)
    e = pl.program_id(1)
    tt = TT_MOE

    @pl.when(e == 0)
    def _():
        b = lax.bitcast_convert_type(affT_ref[...], jnp.int32)
        thr = thr_ref[:, 0:1]
        tc = tcut_ref[:, 0:1]
        tg = i * (tt * SUBS_MOE) + lax.broadcasted_iota(jnp.int32, b.shape, 1)
        sel = (b > thr) | ((b == thr) & (tg <= tc))
        for sub in range(SUBS_MOE):
            sl = slice(sub * tt, (sub + 1) * tt)
            pos = _dot(jnp.where(sel[:, sl], 1.0, 0.0).astype(BF16), tri_ref[...])
            pos_scr[:, sl] = jnp.where(sel[:, sl], pos, -1.0)
        o_ref[...] = x_ref[...]

    posm = [pos_scr[pl.ds(e, 1), sub * tt:(sub + 1) * tt] for sub in range(SUBS_MOE)]
    gate = [affT_ref[pl.ds(e, 1), sub * tt:(sub + 1) * tt] for sub in range(SUBS_MOE)]
    n_sel = jnp.max(jnp.concatenate(posm, axis=0)) + 1.0
    n_ch = (n_sel.astype(jnp.int32) + CH_MOE - 1) // CH_MOE

    def chunk(c, carry):
        rels, hits, xes = [], [], []
        for sub in range(SUBS_MOE):
            rel = posm[sub] - (c * CH_MOE).astype(F32)
            jj = lax.broadcasted_iota(jnp.int32, (CH_MOE, tt), 0).astype(F32)
            hit = jnp.broadcast_to(rel, (CH_MOE, tt)) == jj
            xes.append(_dot(jnp.where(hit, 1.0, 0.0).astype(BF16), h2_ref[sub * tt:(sub + 1) * tt, :]).astype(BF16))
            rels.append(rel)
            hits.append(hit)
        xe = jnp.concatenate(xes, axis=0)
        g = _dot(xe, wg_ref[...])
        u = _dot(xe, wu_ref[...])
        hid = (g * jax.nn.sigmoid(g) * u).astype(BF16)
        ye = _dot(hid, wd_ref[...])
        for sub in range(SUBS_MOE):
            gc = jnp.sum(jnp.where(hits[sub], jnp.broadcast_to(gate[sub], hits[sub].shape), 0.0),
                         axis=1, keepdims=True)
            yeb = jnp.concatenate([(ye[sub * CH_MOE:(sub + 1) * CH_MOE] * gc).astype(BF16),
                                   jnp.zeros((CHP_MOE - CH_MOE, ye.shape[1]), BF16)], axis=0)
            relT = jnp.broadcast_to(rels[sub], (CHP_MOE, tt)).T
            ll = lax.broadcasted_iota(jnp.int32, (tt, CHP_MOE), 1).astype(F32)
            hitT = jnp.where(relT == ll, 1.0, 0.0).astype(BF16)
            o_ref[sub * tt:(sub + 1) * tt, :] += _dot(hitT, yeb)
        return carry

    lax.fori_loop(0, n_ch, chunk, 0)


def _moe(xn, h2, affT, thr, tcut, lw):
    T = xn.shape[0]
    tt = TT_MOE * SUBS_MOE
    once = pl.Buffered(1)
    return pl.pallas_call(
        _moe_kernel,
        grid=(T // tt, N_EXPERTS),
        in_specs=[
            pl.BlockSpec((tt, D_MODEL), lambda i, e: (i, 0), pipeline_mode=once),
            pl.BlockSpec((tt, D_MODEL), lambda i, e: (i, 0), pipeline_mode=once),
            pl.BlockSpec((N_EXPERTS, tt), lambda i, e: (0, i)),
            pl.BlockSpec((N_EXPERTS, 128), lambda i, e: (0, 0)),
            pl.BlockSpec((N_EXPERTS, 128), lambda i, e: (0, 0)),
            pl.BlockSpec((TT_MOE, TT_MOE), lambda i, e: (0, 0), pipeline_mode=once),
            pl.BlockSpec((None, D_MODEL, D_EXPERT), lambda i, e: (e, 0, 0)),
            pl.BlockSpec((None, D_MODEL, D_EXPERT), lambda i, e: (e, 0, 0)),
            pl.BlockSpec((None, D_EXPERT, D_MODEL), lambda i, e: (e, 0, 0)),
        ],
        out_specs=pl.BlockSpec((tt, D_MODEL), lambda i, e: (i, 0)),
        out_shape=jax.ShapeDtypeStruct((T, D_MODEL), F32),
        scratch_shapes=[pltpu.VMEM((N_EXPERTS, tt), F32)],
        compiler_params=pltpu.CompilerParams(dimension_semantics=("arbitrary", "arbitrary"),
                                             vmem_limit_bytes=VMEM_LIMIT),
        name="moe_ffn",
    )(xn, h2, affT, thr, tcut, lw['tri'], lw['w_gate'], lw['w_up'], lw['w_down'])


def _t5_bucket(rel):
    nb = NUM_BUCKETS // 2
    max_exact = nb // 2
    ret = jnp.where(rel > 0, nb, 0)
    n = jnp.abs(rel)
    nf = jnp.maximum(n, 1).astype(jnp.float32)
    large = max_exact + (jnp.log(nf / max_exact) / math.log(MAX_DISTANCE / max_exact) * (nb - max_exact)).astype(jnp.int32)
    large = jnp.minimum(large, nb - 1)
    return ret + jnp.where(n < max_exact, n, large)


def _col(v):
    return v.astype(F32).reshape(-1, 1)


def _pad_heads(w, n_heads, d, d_pad):
    k = w.shape[1]
    w3 = w.reshape(n_heads, d, k)
    return jnp.pad(w3, ((0, 0), (0, d_pad - d), (0, 0))).reshape(n_heads * d_pad, k)


def _toeplitz_tiles(fn, offsets, nk, nq):
    span = nk + nq
    out = []
    for off in offsets:
        u = fn(off + nk - 1 - jnp.arange(span, dtype=jnp.int32))
        rows = jnp.tile(u, (1, nk))[:, :nk * (span - 1)].reshape(u.shape[0], nk, span - 1)
        out.append(rows[:, :, nk - 1:nk - 1 + nq])
    return jnp.stack(out, axis=0)


def _prep_shared(rel_bias):
    rb = rel_bias.astype(F32) * LOG2E

    def bias_a(rel):
        return rb[:, :HA][_t5_bucket(rel)].T

    def bias_c(rel):
        return rb[:, HA:][_t5_bucket(rel)].T

    def in_band(rel):
        return (jnp.abs(rel) <= WINDOW)[None, :]

    bmax = jnp.max(rb[:, :HA], axis=0)
    near_offs = [d * BK_A for d in range(-1, NEAR_A - 1)]
    near_a = _toeplitz_tiles(lambda rel: jnp.exp2(bias_a(rel) - bmax[:, None]), near_offs, BK_A, BQ_A)
    far = _t5_bucket(jnp.array([-(1 << 20), 1 << 20], dtype=jnp.int32))
    sc_far = jnp.concatenate([rb[far[1], :HA], bmax, rb[far[0], :HA]])
    a_bounded = {
        'near': _toeplitz_tiles(lambda rel: jnp.exp2(bias_a(rel)), near_offs, BK_A, BQ_A),
        'sc': jnp.concatenate([jnp.exp2(rb[far[0], :HA]), jnp.exp2(rb[far[1], :HA])]),
        'bias_mag': jnp.max(jnp.abs(rb[:, :HA])),
    }
    bm_c = _toeplitz_tiles(lambda rel: jnp.where(in_band(rel), bias_c(rel), NEG_BIG),
                           [0, -BLOCK, -2 * BLOCK], KW_C, BLOCK)
    c_bounded = {
        'e': _toeplitz_tiles(lambda rel: jnp.where(in_band(rel), jnp.exp2(bias_c(rel)), 0.0),
                             [0, -WINDOW, -(KWB_C - QB_C)], KWB_C, QB_C),
        'bias_mag': jnp.max(jnp.abs(rb[:, HA:])),
    }
    return near_a, sc_far, bm_c, a_bounded, c_bounded


def _rope_tables(S):
    inv = 1.0 / (ROPE_BASE ** (jnp.arange(ROPE_HALF, dtype=jnp.float32) / ROPE_HALF))
    ang = jnp.arange(S, dtype=jnp.int32).astype(jnp.float32)[:, None] * inv[None, :]
    return jnp.cos(ang).T, jnp.sin(ang).T


def _logit_bounds(l, p, a_bounded, c_bounded, sa, sb, scc):
    qa, ka = p['qn_a'][l].astype(F32), p['kn_a'][l].astype(F32)
    bound_a = 1.01 * DA_QK * jnp.max(jnp.abs(qa * ka)) * sa + 2.0 * a_bounded['bias_mag']
    qb, kb = p['qn_b'][l].astype(F32), p['kn_b'][l].astype(F32)
    bound_b = 1.01 * DB_QK * jnp.max(jnp.abs(qb)) * jnp.max(jnp.abs(kb)) * sb
    qc, kc = p['qn_c'][l].astype(F32), p['kn_c'][l].astype(F32)
    bound_c = (1.01 * DC * jnp.max(jnp.abs(qc * kc)) * scc + 2.0 * c_bounded['bias_mag']
               + jnp.max(jnp.abs(p['sink_c'][l].astype(F32))) * LOG2E)
    return bound_a <= BOUND_LIMIT, bound_b <= BOUND_LIMIT, bound_c <= BOUND_LIMIT


def _prep_layer(l, p, shared):
    near_a, sc_far, bm_c, a_bounded, c_bounded = shared
    lam_init = 0.8 - 0.6 * math.exp(-0.3 * l)
    lam = (jnp.exp(jnp.sum(p['lam_q1'][l].astype(F32) * p['lam_k1'][l].astype(F32)))
           - jnp.exp(jnp.sum(p['lam_q2'][l].astype(F32) * p['lam_k2'][l].astype(F32))) + lam_init)
    w_uq = p['w_uq'][l]
    w_ukv = p['w_ukv'][l].reshape(KV_RANK, HB, DB_NOPE + DB_V)
    sa = DA_QK ** -0.5 * LOG2E
    sb = DB_QK ** -0.5 * LOG2E
    scc = DC ** -0.5 * LOG2E
    zpad = jnp.zeros((DB_PAD - DB_QK,), F32)
    lw = {
        'g_mix': p['norm_mix_g'][l].astype(F32).reshape(1, D_MODEL),
        'w_inT': p['w_in'][l].T.astype(BF16),
        'w_uqT': _pad_heads(w_uq.T, HB, DB_QK, DB_PAD).astype(BF16),
        'w_ukT': w_ukv[:, :, :DB_NOPE].reshape(KV_RANK, HB * DB_NOPE).T.astype(BF16),
        'w_uvT': w_ukv[:, :, DB_NOPE:].reshape(KV_RANK, HB * DB_V).T.astype(BF16),
        'gqa': _col(jnp.tile(p['qn_a'][l], 2 * HA) * sa),
        'gka': _col(jnp.tile(p['kn_a'][l], 2 * HA)),
        'gcq': _col(p['g_cq'][l]),
        'gckv': _col(p['g_ckv'][l]),
        'gqb': _col(jnp.tile(jnp.concatenate([p['qn_b'][l].astype(F32) * sb, zpad]), HB)),
        'gkb': _col(jnp.tile(jnp.concatenate([p['kn_b'][l].astype(F32), zpad]), HB)),
        'gqc': _col(jnp.tile(p['qn_c'][l], HC) * scc),
        'gkc': _col(jnp.tile(p['kn_c'][l], KVH_C)),
        'near_a': near_a,
        'sc_a': jnp.concatenate([sc_far, lam.reshape(1)]).astype(F32),
        'subln': _col(jnp.tile(p['subln_a'][l], HA) * (1.0 - lam_init)),
        'bm_c': bm_c,
        'sc_c': p['sink_c'][l].astype(F32) * LOG2E,
        'w_out': p['w_out'][l].astype(BF16),
        'g_ffn': p['norm_ffn_g'][l].astype(F32).reshape(1, D_MODEL),
        'w_r': jnp.pad(p['w_router'][l], ((0, 0), (0, 128 - N_EXPERTS))).astype(BF16),
        'w_gate': p['w_gate'][l].astype(BF16),
        'w_up': p['w_up'][l].astype(BF16),
        'w_down': p['w_down'][l].astype(BF16),
        'tri': jnp.triu(jnp.ones((TT_MOE, TT_MOE), F32), k=1).astype(BF16),
        'near_a_bounded': a_bounded['near'],
        'sc_a_bounded': jnp.concatenate([a_bounded['sc'], lam.reshape(1)]).astype(F32),
    }
    lw['e_c'] = c_bounded['e']
    lw['sc_c_bounded'] = jnp.concatenate([lw['sc_c'], jnp.exp2(lw['sc_c'])])
    lw['bounded_a'], lw['bounded_b'], lw['bounded_c'] = _logit_bounds(l, p, a_bounded, c_bounded, sa, sb, scc)
    return lw


def _trunk(x, layers):
    B, S, D = x.shape
    x2 = x.reshape(B * S, D)
    cosT, sinT = _rope_tables(S)
    for lw in layers:
        lw = dict(lw, cosT=cosT, sinT=sinT)
        qaT, ka, vaT, qbT, kb, vbT, qcT, kc, vcT = _in_proj(x2, lw, S)
        ya = lax.cond(lw['bounded_a'],
                      lambda q, k, v: _attn_a(q, k, v, lw, B, S, True),
                      lambda q, k, v: _attn_a(q, k, v, lw, B, S, False), qaT, ka, vaT)
        yb = lax.cond(lw['bounded_b'],
                      lambda q, k, v: _attn_b(q, k, v, B, S, True),
                      lambda q, k, v: _attn_b(q, k, v, B, S, False), qbT, kb, vbT)
        yc = lax.cond(lw['bounded_c'],
                      lambda q, k, v: _attn_c(q, k, v, lw, B, S, True),
                      lambda q, k, v: _attn_c(q, k, v, lw, B, S, False), qcT, kc, vcT)
        xn, h2, affT = _out_proj(x2, ya, yb, yc, lw)
        thr, tcut = _topk_thresholds(affT)
        x2 = _moe(xn, h2, affT, thr, tcut, lw)
    return x2.reshape(B, S, D)


def kernel(x_prompt, x_sample, rel_bias, norm_mix_g, w_in, qn_a, kn_a, lam_q1, lam_k1, lam_q2, lam_k2, subln_a, g_cq, w_uq, g_ckv, w_ukv, qn_b, kn_b, qn_c, kn_c, sink_c, w_out, norm_ffn_g, w_router, w_gate, w_up, w_down):
    p = dict(norm_mix_g=norm_mix_g, w_in=w_in, qn_a=qn_a, kn_a=kn_a, lam_q1=lam_q1, lam_k1=lam_k1,
             lam_q2=lam_q2, lam_k2=lam_k2, subln_a=subln_a, g_cq=g_cq, w_uq=w_uq, g_ckv=g_ckv, w_ukv=w_ukv,
             qn_b=qn_b, kn_b=kn_b, qn_c=qn_c, kn_c=kn_c, sink_c=sink_c, w_out=w_out, norm_ffn_g=norm_ffn_g,
             w_router=w_router, w_gate=w_gate, w_up=w_up, w_down=w_down)
    shared = _prep_shared(rel_bias)
    layers = [_prep_layer(l, p, shared) for l in range(w_in.shape[0])]
    return (_trunk(x_prompt, layers), _trunk(x_sample, layers))
```

```python
import functools
import math

import jax
import jax.numpy as jnp
import numpy as np
from jax import lax
from jax.experimental import pallas as pl
from jax.experimental.pallas import tpu as pltpu

D_MODEL = 1024
BLOCK = 128
HA, DA_QK, DA_V = 4, 32, 64
HB, Q_RANK, KV_RANK, DB_NOPE, DB_ROPE, DB_V = 6, 384, 256, 64, 32, 64
ROPE_BASE = 10000.0
HC, KVH_C, DC, WINDOW = 6, 2, 64, 128
G_C = HC // KVH_C
NUM_BUCKETS, MAX_DISTANCE = 32, 128
N_EXPERTS, D_EXPERT, CAPACITY_FACTOR = 16, 1024, 2
EPS = 1e-6
DEPTH = 2

DB_QK = DB_NOPE + DB_ROPE
DB_PAD = 128
VX = 80
BOUND_LIMIT = 80.0
ROPE_HALF = DB_ROPE // 2
IN_SPLITS = (HA * 2 * DA_QK, HA * 2 * DA_QK, HA * DA_V, Q_RANK, KV_RANK, DB_ROPE, HC * DC, KVH_C * DC, KVH_C * DC)
IN_OFFS = tuple(int(v) for v in np.cumsum((0,) + IN_SPLITS))
IN_COLS = IN_OFFS[-1]
LOG2E = 1.4426950408889634
NEG_BIG = -1e30

F32 = jnp.float32
BF16 = jnp.bfloat16

TM_IN = 512
TM_OUT = 512
BQ_A = 512
BK_A = 512
UNROLL_A = 2
NEAR_A = BQ_A // BK_A + 2
assert BQ_A % BK_A == 0 and BK_A >= MAX_DISTANCE
BQ_B = 512
BK_B = 512
UNROLL_B = 2
KW_C = 3 * BLOCK
QB_C = 256
KWB_C = QB_C + 2 * WINDOW
SUB_C = 2
TT_MOE = 1024
SUBS_MOE = 2
GATHER_GROUP = 4
CH_MOE = 160
CHP_MOE = 256
PREFETCH = 3
VMEM_LIMIT = 56 * 1024 * 1024


def _nt_dot(a, b):
    return lax.dot_general(a, b, (((1,), (1,)), ((), ())), preferred_element_type=F32)


def _dot(a, b):
    return jnp.dot(a, b, preferred_element_type=F32)


def _group_norm_rows(y, gs, n_valid):
    rows, tm = y.shape
    y3 = y.reshape(rows // gs, gs, tm)
    ms = jnp.sum(y3 * y3, axis=1, keepdims=True) * (1.0 / n_valid)
    return (y3 * lax.rsqrt(ms + EPS)).reshape(rows, tm)


def _in_kernel(x_ref, g_ref, winT_ref, wuqT_ref, wukT_ref, wuvT_ref,
               gqa_ref, gka_ref, gcq_ref, gckv_ref, gqb_ref, gkb_ref, gqc_ref, gkc_ref,
               cos_ref, sin_ref,
               qaT_ref, ka_ref, vaT_ref, qbT_ref, kb_ref, vbT_ref, qcT_ref, kc_ref, vcT_ref):
    x = x_ref[...]
    h = x * lax.rsqrt(jnp.mean(x * x, axis=-1, keepdims=True) + EPS) * g_ref[...]
    projT = _dot(winT_ref[...], h.T.astype(BF16))

    def seg(i):
        return projT[IN_OFFS[i]:IN_OFFS[i + 1]]

    cos = cos_ref[...]
    sin = sin_ref[...]

    qaT_ref[...] = (_group_norm_rows(seg(0), DA_QK, DA_QK) * gqa_ref[...]).astype(BF16)
    kaT = _group_norm_rows(seg(1), DA_QK, DA_QK) * gka_ref[...]
    ka_ref[...] = kaT.T.astype(BF16)
    def with_ones_rows(v, n_heads, d):
        tm = v.shape[1]
        tail = jnp.where(lax.broadcasted_iota(jnp.int32, (VX - d, tm), 0) == 0, 1.0, 0.0)
        rows = []
        for hh in range(n_heads):
            rows += [v[hh * d:(hh + 1) * d], tail]
        return jnp.concatenate(rows, axis=0).astype(BF16)

    vaT_ref[...] = with_ones_rows(seg(2), HA, DA_V)

    def rope_store(dst_rows, y, h):
        o = h * DB_PAD
        x1 = y[o + DB_NOPE:o + DB_NOPE + ROPE_HALF]
        x2 = y[o + DB_NOPE + ROPE_HALF:o + DB_QK]
        dst_rows.append(y[o:o + DB_NOPE])
        dst_rows.append(x1 * cos - x2 * sin)
        dst_rows.append(x2 * cos + x1 * sin)
        dst_rows.append(jnp.zeros((DB_PAD - DB_QK, y.shape[1]), F32))

    cqT = seg(3)
    cqn = (cqT * lax.rsqrt(jnp.mean(cqT * cqT, axis=0, keepdims=True) + EPS) * gcq_ref[...]).astype(BF16)
    qf = _group_norm_rows(_dot(wuqT_ref[...], cqn), DB_PAD, DB_QK) * gqb_ref[...]
    rows = []
    for hh in range(HB):
        rope_store(rows, qf, hh)
    qbT_ref[...] = jnp.concatenate(rows, axis=0).astype(BF16)

    ckvT = seg(4)
    ckvn = (ckvT * lax.rsqrt(jnp.mean(ckvT * ckvT, axis=0, keepdims=True) + EPS) * gckv_ref[...]).astype(BF16)
    knope = _dot(wukT_ref[...], ckvn)
    vbT_ref[...] = with_ones_rows(_dot(wuvT_ref[...], ckvn), HB, DB_V)
    krope = seg(5)
    zpad = jnp.zeros((DB_PAD - DB_QK, krope.shape[1]), F32)
    rows = []
    for hh in range(HB):
        rows += [knope[hh * DB_NOPE:(hh + 1) * DB_NOPE], krope, zpad]
    kf = _group_norm_rows(jnp.concatenate(rows, axis=0), DB_PAD, DB_QK) * gkb_ref[...]
    rows = []
    for hh in range(HB):
        rope_store(rows, kf, hh)
    kb_ref[...] = jnp.concatenate(rows, axis=0).T.astype(BF16)

    qcT_ref[...] = (_group_norm_rows(seg(6), DC, DC) * gqc_ref[...]).astype(BF16)
    kcT = _group_norm_rows(seg(7), DC, DC) * gkc_ref[...]
    kc_ref[...] = kcT.T.astype(BF16)
    vcT_ref[...] = with_ones_rows(seg(8), KVH_C, DC)


def _in_proj(x2, lw, S):
    T = x2.shape[0]
    tm = TM_IN
    nt = T // tm
    spb = S // tm

    def full(a):
        return pl.BlockSpec(a.shape, lambda i: (0,) * a.ndim)

    consts = [lw['g_mix'], lw['w_inT'], lw['w_uqT'], lw['w_ukT'], lw['w_uvT'],
              lw['gqa'], lw['gka'], lw['gcq'], lw['gckv'], lw['gqb'], lw['gkb'], lw['gqc'], lw['gkc']]
    in_specs = ([pl.BlockSpec((tm, D_MODEL), lambda i: (i, 0))] + [full(a) for a in consts]
                + [pl.BlockSpec((ROPE_HALF, tm), lambda i: (0, i % spb))] * 2)

    def fm(rows):
        return jax.ShapeDtypeStruct((rows, T), BF16), pl.BlockSpec((rows, tm), lambda i: (0, i))

    def tk(cols):
        return jax.ShapeDtypeStruct((T, cols), BF16), pl.BlockSpec((tm, cols), lambda i: (i, 0))

    outs = [fm(HA * 2 * DA_QK), tk(HA * 2 * DA_QK), fm(HA * VX),
            fm(HB * DB_PAD), tk(HB * DB_PAD), fm(HB * VX),
            fm(HC * DC), tk(KVH_C * DC), fm(KVH_C * VX)]
    return pl.pallas_call(
        _in_kernel,
        grid=(nt,),
        in_specs=in_specs,
        out_specs=[o[1] for o in outs],
        out_shape=[o[0] for o in outs],
        compiler_params=pltpu.CompilerParams(dimension_semantics=("arbitrary",), vmem_limit_bytes=VMEM_LIMIT),
        name="in_proj",
    )(x2, *consts, lw['cosT'], lw['sinT'])


def _attn_a_kernel(sc_ref, qT_ref, k_ref, vT_ref, near_ref, subln_ref, o_ref, w_scr, m_scr, l_scr, acc_scr, *, nk):
    qb = pl.program_id(1)
    n_comb = 2 * HA
    bq = qT_ref.shape[1]
    near_lo = (BQ_A // BK_A) * qb - 1

    for hc in range(n_comb):
        g = hc // 4
        qg = qT_ref[g * 128:(g + 1) * 128, :]
        row = lax.broadcasted_iota(jnp.int32, (128, bq), 0)
        keep = (row // DA_QK) == (hc % 4)
        w_scr[hc] = jnp.where(keep, qg, jnp.zeros_like(qg))
    m_scr[...] = jnp.full(m_scr.shape, NEG_BIG, F32)
    l_scr[...] = jnp.zeros(l_scr.shape, F32)
    acc_scr[...] = jnp.zeros(acc_scr.shape, F32)

    def step(kb, mode):
        ks = pl.multiple_of(kb * BK_A, BK_A)

        def qk(hc):
            g = hc // 4
            return _dot(k_ref[pl.ds(ks, BK_A), g * 128:(g + 1) * 128], w_scr[hc])

        ahead = [qk(i) for i in range(PREFETCH)]
        for hc in range(n_comb):
            h = hc // 2
            s = ahead.pop(0)
            if hc + PREFETCH < n_comb:
                ahead.append(qk(hc + PREFETCH))
            c = sc_ref[(1 - mode) * HA + h]
            cmax = jnp.max(s, axis=0, keepdims=True) + c
            m_old = m_scr[hc]
            m_new = jnp.maximum(m_old, cmax)
            alpha = jnp.exp2(m_old - m_new)
            p = jnp.exp2(s - (m_new - c))
            if mode == 0:
                p = p * near_ref[kb - near_lo, h]
            l_scr[hc] = alpha * l_scr[hc] + jnp.sum(p, axis=0, keepdims=True)
            pv = _dot(vT_ref[h * VX:h * VX + DA_V, pl.ds(ks, BK_A)], p.astype(BF16))
            acc_scr[hc] = alpha * acc_scr[hc] + pv
            m_scr[hc] = m_new

    lo = jnp.maximum(near_lo, 0)
    hi = jnp.minimum(near_lo + NEAR_A, nk)
    lax.fori_loop(0, lo, lambda kb, c: (step(kb, -1), c)[1], 0)
    lax.fori_loop(lo, hi, lambda kb, c: (step(kb, 0), c)[1], 0)
    lax.fori_loop(hi, nk, lambda kb, c: (step(kb, 1), c)[1], 0)

    lam = sc_ref[3 * HA]
    outs = []
    for h in range(HA):
        a = acc_scr[2 * h] / l_scr[2 * h] - lam * (acc_scr[2 * h + 1] / l_scr[2 * h + 1])
        a = a * lax.rsqrt(jnp.mean(a * a, axis=0, keepdims=True) + EPS)
        outs.append(a)
    o = jnp.concatenate(outs, axis=0) * subln_ref[...]
    o_ref[...] = o.T.astype(BF16)


def _attn_a_bounded_kernel(sc_ref, qT_ref, k_ref, vT_ref, near_ref, subln_ref, o_ref, w_scr, acc_scr, *, nk):
    qb = pl.program_id(1)
    n_comb = 2 * HA
    bq = qT_ref.shape[1]
    near_lo = (BQ_A // BK_A) * qb - 1
    for hc in range(n_comb):
        g = hc // 4
        qg = qT_ref[g * 128:(g + 1) * 128, :]
        row = lax.broadcasted_iota(jnp.int32, (128, bq), 0)
        w_scr[hc] = jnp.where((row // DA_QK) == (hc % 4), qg, jnp.zeros_like(qg))
    acc_scr[...] = jnp.zeros(acc_scr.shape, F32)

    lo = jnp.maximum(near_lo, 0)
    hi = jnp.minimum(near_lo + NEAR_A, nk)
    n_far = lo + (nk - hi)

    def step(j0, n_blocks, near):
        def block(jj):
            j = j0 + jj
            if near:
                return j, None
            right = j >= lo
            return jnp.where(right, j - lo + hi, j), jnp.where(right, HA, 0)

        tiles = [(jj, hc) for jj in range(n_blocks) for hc in range(n_comb)]

        def qk(t):
            jj, hc = tiles[t]
            g = hc // 4
            ks = pl.multiple_of(block(jj)[0] * BK_A, BK_A)
            return _dot(k_ref[pl.ds(ks, BK_A), g * 128:(g + 1) * 128], w_scr[hc])

        ahead = [qk(i) for i in range(PREFETCH)]
        for t, (jj, hc) in enumerate(tiles):
            h = hc // 2
            s = ahead.pop(0)
            if t + PREFETCH < len(tiles):
                ahead.append(qk(t + PREFETCH))
            kb, side = block(jj)
            ks = pl.multiple_of(kb * BK_A, BK_A)
            p = jnp.exp2(s)
            if near:
                p = p * near_ref[kb - near_lo, h]
            pv = _dot(vT_ref[h * VX:(h + 1) * VX, pl.ds(ks, BK_A)], p.astype(BF16))
            if near:
                acc_scr[hc] += pv
            else:
                acc_scr[hc] += pv * sc_ref[side + h]

    n_pairs = n_far // UNROLL_A
    lax.fori_loop(0, n_pairs, lambda i, c: (step(i * UNROLL_A, UNROLL_A, False), c)[1], 0)
    lax.fori_loop(n_pairs * UNROLL_A, n_far, lambda j, c: (step(j, 1, False), c)[1], 0)
    lax.fori_loop(lo, hi, lambda kb, c: (step(kb, 1, True), c)[1], 0)

    lam = sc_ref[2 * HA]
    outs = []
    for h in range(HA):
        a0, a1 = acc_scr[2 * h], acc_scr[2 * h + 1]
        a = a0[0:DA_V] / a0[DA_V:DA_V + 1] - lam * (a1[0:DA_V] / a1[DA_V:DA_V + 1])
        outs.append(a * lax.rsqrt(jnp.mean(a * a, axis=0, keepdims=True) + EPS))
    o = jnp.concatenate(outs, axis=0) * subln_ref[...]
    o_ref[...] = o.T.astype(BF16)


def _attn_a(qT, k, vT, lw, B, S, bounded):
    nq = S // BQ_A
    nk = S // BK_A
    n_comb = 2 * HA
    if bounded:
        body = functools.partial(_attn_a_bounded_kernel, nk=nk)
        scratch = [pltpu.VMEM((n_comb, 128, BQ_A), BF16), pltpu.VMEM((n_comb, VX, BQ_A), F32)]
        sc, near = lw['sc_a_bounded'], lw['near_a_bounded']
    else:
        body = functools.partial(_attn_a_kernel, nk=nk)
        scratch = [pltpu.VMEM((n_comb, 128, BQ_A), BF16), pltpu.VMEM((n_comb, 1, BQ_A), F32),
                   pltpu.VMEM((n_comb, 1, BQ_A), F32), pltpu.VMEM((n_comb, DA_V, BQ_A), F32)]
        sc, near = lw['sc_a'], lw['near_a']
    return pl.pallas_call(
        body,
        grid=(B, nq),
        in_specs=[
            pl.BlockSpec(memory_space=pltpu.SMEM),
            pl.BlockSpec((n_comb * DA_QK, BQ_A), lambda b, q: (0, b * nq + q)),
            pl.BlockSpec((S, n_comb * DA_QK), lambda b, q: (b, 0)),
            pl.BlockSpec((HA * VX, S), lambda b, q: (0, b)),
            pl.BlockSpec(near.shape, lambda b, q: (0, 0, 0, 0)),
            pl.BlockSpec((HA * DA_V, 1), lambda b, q: (0, 0)),
        ],
        out_specs=pl.BlockSpec((BQ_A, HA * DA_V), lambda b, q: (b * nq + q, 0)),
        out_shape=jax.ShapeDtypeStruct((B * S, HA * DA_V), BF16),
        scratch_shapes=scratch,
        compiler_params=pltpu.CompilerParams(dimension_semantics=("arbitrary", "arbitrary"),
                                             vmem_limit_bytes=VMEM_LIMIT),
        name="attn_a_bounded" if bounded else "attn_a",
    )(sc, qT, k, vT, near, lw['subln'])


def _attn_b_kernel(qT_ref, k_ref, vT_ref, o_ref, m_scr, l_scr, acc_scr, *, nk):
    m_scr[...] = jnp.full(m_scr.shape, NEG_BIG, F32)
    l_scr[...] = jnp.zeros(l_scr.shape, F32)
    acc_scr[...] = jnp.zeros(acc_scr.shape, F32)

    def step(kb, carry):
        ks = pl.multiple_of(kb * BK_B, BK_B)

        def qk(h):
            return _dot(k_ref[pl.ds(ks, BK_B), h * DB_PAD:(h + 1) * DB_PAD], qT_ref[h * DB_PAD:(h + 1) * DB_PAD, :])

        ahead = [qk(i) for i in range(PREFETCH)]
        for h in range(HB):
            s = ahead.pop(0)
            if h + PREFETCH < HB:
                ahead.append(qk(h + PREFETCH))
            m_old = m_scr[h]
            m_new = jnp.maximum(m_old, jnp.max(s, axis=0, keepdims=True))
            alpha = jnp.exp2(m_old - m_new)
            p = jnp.exp2(s - m_new)
            l_scr[h] = alpha * l_scr[h] + jnp.sum(p, axis=0, keepdims=True)
            pv = _dot(vT_ref[h * VX:h * VX + DB_V, pl.ds(ks, BK_B)], p.astype(BF16))
            acc_scr[h] = alpha * acc_scr[h] + pv
            m_scr[h] = m_new
        return carry

    lax.fori_loop(0, nk, step, 0)
    o = jnp.concatenate([acc_scr[h] / l_scr[h] for h in range(HB)], axis=0)
    o_ref[...] = o.T.astype(BF16)


def _attn_b_bounded_kernel(qT_ref, k_ref, vT_ref, o_ref, acc_scr, *, nk):
    acc_scr[...] = jnp.zeros(acc_scr.shape, F32)

    assert nk % UNROLL_B == 0
    tiles = [(j, h) for j in range(UNROLL_B) for h in range(HB)]

    def step(it, carry):
        def ks(j):
            return pl.multiple_of((it * UNROLL_B + j) * BK_B, BK_B)

        def qk(t):
            j, h = tiles[t]
            return _dot(k_ref[pl.ds(ks(j), BK_B), h * DB_PAD:(h + 1) * DB_PAD],
                        qT_ref[h * DB_PAD:(h + 1) * DB_PAD, :])

        ahead = [qk(i) for i in range(PREFETCH)]
        for t, (j, h) in enumerate(tiles):
            s = ahead.pop(0)
            if t + PREFETCH < len(tiles):
                ahead.append(qk(t + PREFETCH))
            p = jnp.exp2(s).astype(BF16)
            acc_scr[h] += _dot(vT_ref[h * VX:(h + 1) * VX, pl.ds(ks(j), BK_B)], p)
        return carry

    lax.fori_loop(0, nk // UNROLL_B, step, 0)
    o = jnp.concatenate([acc_scr[h, 0:DB_V] / acc_scr[h, DB_V:DB_V + 1] for h in range(HB)], axis=0)
    o_ref[...] = o.T.astype(BF16)


def _attn_b(qT, k, vT, B, S, bounded):
    nq = S // BQ_B
    nk = S // BK_B
    if bounded:
        body = functools.partial(_attn_b_bounded_kernel, nk=nk)
        scratch = [pltpu.VMEM((HB, VX, BQ_B), F32)]
    else:
        body = functools.partial(_attn_b_kernel, nk=nk)
        scratch = [pltpu.VMEM((HB, 1, BQ_B), F32), pltpu.VMEM((HB, 1, BQ_B), F32), pltpu.VMEM((HB, DB_V, BQ_B), F32)]
    return pl.pallas_call(
        body,
        grid=(B, nq),
        in_specs=[
            pl.BlockSpec((HB * DB_PAD, BQ_B), lambda b, q: (0, b * nq + q)),
            pl.BlockSpec((S, HB * DB_PAD), lambda b, q: (b, 0)),
            pl.BlockSpec((HB * VX, S), lambda b, q: (0, b)),
        ],
        out_specs=pl.BlockSpec((BQ_B, HB * DB_V), lambda b, q: (b * nq + q, 0)),
        out_shape=jax.ShapeDtypeStruct((B * S, HB * DB_V), BF16),
        scratch_shapes=scratch,
        compiler_params=pltpu.CompilerParams(dimension_semantics=("arbitrary", "arbitrary"),
                                             vmem_limit_bytes=VMEM_LIMIT),
        name="attn_b_bounded" if bounded else "attn_b",
    )(qT, k, vT)


def _attn_c_kernel(sc_ref, qT_ref, k_ref, vT_ref, bm_ref, o_ref, *, nb, S):
    n = pl.program_id(1)
    start = pl.multiple_of(jnp.clip((n - 1) * BLOCK, 0, S - KW_C), BLOCK)
    case = jnp.where(n == 0, 0, jnp.where(n == nb - 1, 2, 1))
    kwin = k_ref[pl.ds(start, KW_C), :]
    row = lax.broadcasted_iota(jnp.int32, (KVH_C * DC, BLOCK), 0)
    outs = []
    for h in range(HC):
        j = h // G_C
        qh = qT_ref[h * DC:(h + 1) * DC, :]
        zz = jnp.zeros_like(qh)
        w = jnp.concatenate([qh, zz] if j == 0 else [zz, qh], axis=0)
        s = _dot(kwin, w) + bm_ref[case, h]
        sink = sc_ref[h]
        m = jnp.maximum(jnp.max(s, axis=0, keepdims=True), sink)
        p = jnp.exp2(s - m)
        den = jnp.sum(p, axis=0, keepdims=True) + jnp.exp2(sink - m)
        pv = _dot(vT_ref[j * VX:j * VX + DC, pl.ds(start, KW_C)], p.astype(BF16))
        outs.append(pv / den)
    del row
    o_ref[...] = jnp.concatenate(outs, axis=0).T.astype(BF16)


def _attn_c_bounded_kernel(sc_ref, qT_ref, k_ref, vT_ref, e_ref, o_ref, *, n_sub, S):
    g = pl.program_id(1)
    tiles = [(u, h) for u in range(SUB_C) for h in range(HC)]

    def window(u):
        n = g * SUB_C + u
        start = pl.multiple_of(jnp.clip(n * QB_C - WINDOW, 0, S - KWB_C), WINDOW)
        case = jnp.where(n == 0, 0, jnp.where(n == n_sub - 1, 2, 1))
        return start, case

    def qk(t):
        u, h = tiles[t]
        start, _ = window(u)
        qh = qT_ref[h * DC:(h + 1) * DC, u * QB_C:(u + 1) * QB_C]
        zz = jnp.zeros_like(qh)
        w = jnp.concatenate([qh, zz] if h // G_C == 0 else [zz, qh], axis=0)
        return _dot(k_ref[pl.ds(start, KWB_C), :], w)

    ahead = [qk(i) for i in range(PREFETCH)]
    outs = []
    for t, (u, h) in enumerate(tiles):
        s = ahead.pop(0)
        if t + PREFETCH < len(tiles):
            ahead.append(qk(t + PREFETCH))
        start, case = window(u)
        j = h // G_C
        p = (jnp.exp2(s) * e_ref[case, h]).astype(BF16)
        pv = _dot(vT_ref[j * VX:(j + 1) * VX, pl.ds(start, KWB_C)], p)
        outs.append(pv[0:DC] / (pv[DC:DC + 1] + sc_ref[HC + h]))
        if h == HC - 1:
            o_ref[u * QB_C:(u + 1) * QB_C, :] = jnp.concatenate(outs, axis=0).T.astype(BF16)
            outs = []


def _attn_c(qT, k, vT, lw, B, S, bounded):
    if bounded:
        n_sub = S // QB_C
        assert n_sub >= 3 and n_sub % SUB_C == 0
        nb, bq = n_sub // SUB_C, QB_C * SUB_C
        body = functools.partial(_attn_c_bounded_kernel, n_sub=n_sub, S=S)
        tile, sc = lw['e_c'], lw['sc_c_bounded']
    else:
        nb, bq = S // BLOCK, BLOCK
        assert nb >= 3
        body = functools.partial(_attn_c_kernel, nb=nb, S=S)
        tile, sc = lw['bm_c'], lw['sc_c']
    return pl.pallas_call(
        body,
        grid=(B, nb),
        in_specs=[
            pl.BlockSpec(memory_space=pltpu.SMEM),
            pl.BlockSpec((HC * DC, bq), lambda b, n: (0, b * nb + n)),
            pl.BlockSpec((S, KVH_C * DC), lambda b, n: (b, 0)),
            pl.BlockSpec((KVH_C * VX, S), lambda b, n: (0, b)),
            pl.BlockSpec(tile.shape, lambda b, n: (0, 0, 0, 0)),
        ],
        out_specs=pl.BlockSpec((bq, HC * DC), lambda b, n: (b * nb + n, 0)),
        out_shape=jax.ShapeDtypeStruct((B * S, HC * DC), BF16),
        compiler_params=pltpu.CompilerParams(dimension_semantics=("arbitrary", "arbitrary"),
                                             vmem_limit_bytes=VMEM_LIMIT),
        name="attn_c_bounded" if bounded else "attn_c",
    )(sc, qT, k, vT, tile)


def _out_kernel(x_ref, ya_ref, yb_ref, yc_ref, wo_ref, g_ref, wr_ref, xn_ref, h2_ref, affT_ref):
    na, nb_ = HA * DA_V, HA * DA_V + HB * DB_V
    xn = (x_ref[...] + _dot(ya_ref[...], wo_ref[0:na, :]) + _dot(yb_ref[...], wo_ref[na:nb_, :])
          + _dot(yc_ref[...], wo_ref[nb_:, :]))
    xn_ref[...] = xn
    h2 = (xn * lax.rsqrt(jnp.mean(xn * xn, axis=-1, keepdims=True) + EPS) * g_ref[...]).astype(BF16)
    h2_ref[...] = h2
    lg = _dot(h2, wr_ref[...]).T[0:N_EXPERTS]
    e = jnp.exp(lg - jnp.max(lg, axis=0, keepdims=True))
    affT_ref[...] = e / jnp.sum(e, axis=0, keepdims=True)


def _out_proj(x2, ya, yb, yc, lw):
    T = x2.shape[0]
    tm = TM_OUT

    def full(a):
        return pl.BlockSpec(a.shape, lambda i: (0,) * a.ndim)

    return pl.pallas_call(
        _out_kernel,
        grid=(T // tm,),
        in_specs=[pl.BlockSpec((tm, D_MODEL), lambda i: (i, 0)),
                  pl.BlockSpec((tm, ya.shape[1]), lambda i: (i, 0)),
                  pl.BlockSpec((tm, yb.shape[1]), lambda i: (i, 0)),
                  pl.BlockSpec((tm, yc.shape[1]), lambda i: (i, 0)),
                  full(lw['w_out']), full(lw['g_ffn']), full(lw['w_r'])],
        out_specs=[pl.BlockSpec((tm, D_MODEL), lambda i: (i, 0)),
                   pl.BlockSpec((tm, D_MODEL), lambda i: (i, 0)),
                   pl.BlockSpec((N_EXPERTS, tm), lambda i: (0, i))],
        out_shape=[jax.ShapeDtypeStruct((T, D_MODEL), F32),
                   jax.ShapeDtypeStruct((T, D_MODEL), BF16),
                   jax.ShapeDtypeStruct((N_EXPERTS, T), F32)],
        compiler_params=pltpu.CompilerParams(dimension_semantics=("arbitrary",), vmem_limit_bytes=VMEM_LIMIT),
        name="out_proj",
    )(x2, ya, yb, yc, lw['w_out'], lw['g_ffn'], lw['w_r'])


def _topk_kernel(aff_ref, thr_ref, tcut_ref, *, cap, T):
    capf = float(cap)

    def bits():
        return lax.bitcast_convert_type(aff_ref[...], jnp.int32)

    def count(mask):
        return jnp.sum(jnp.where(mask, 1.0, 0.0), axis=1, keepdims=True)

    def vbody(i, v):
        cand = v | jnp.left_shift(jnp.int32(1), 30 - i)
        return jnp.where(count(bits() >= cand) >= capf, cand, v)

    thr = lax.fori_loop(0, 31, vbody, jnp.zeros((N_EXPERTS, 1), jnp.int32))
    need = capf - count(bits() > thr)
    nbits = max(1, int(math.ceil(math.log2(T))))

    def tbody(i, c):
        cand = c | jnp.left_shift(jnp.int32(1), nbits - 1 - i)
        idx = lax.broadcasted_iota(jnp.int32, (N_EXPERTS, T), 1)
        f = count((bits() == thr) & (idx < cand))
        return jnp.where(f < need, cand, c)

    tcut = lax.fori_loop(0, nbits, tbody, jnp.zeros((N_EXPERTS, 1), jnp.int32))
    thr_ref[...] = jnp.broadcast_to(thr, thr_ref.shape)
    tcut_ref[...] = jnp.broadcast_to(tcut, tcut_ref.shape)


def _topk_thresholds(affT):
    E, T = affT.shape
    cap = CAPACITY_FACTOR * T // N_EXPERTS
    return pl.pallas_call(
        functools.partial(_topk_kernel, cap=cap, T=T),
        grid=(1,),
        in_specs=[pl.BlockSpec((E, T), lambda i: (0, 0))],
        out_specs=[pl.BlockSpec((E, 128), lambda i: (0, 0))] * 2,
        out_shape=[jax.ShapeDtypeStruct((E, 128), jnp.int32)] * 2,
        compiler_params=pltpu.CompilerParams(dimension_semantics=("arbitrary",), vmem_limit_bytes=VMEM_LIMIT),
        name="topk_thr",
    )(affT)


def _moe_kernel(x_ref, h2_ref, affT_ref, thr_ref, tcut_ref, tri_ref, wg_ref, wu_ref, wd_ref, o_ref, pos_scr, xe_scr):
    i = pl.program_id(0)
    e = pl.program_id(1)
    tt = TT_MOE

    def one_hot_rows(posm, c):
        rel = posm - (c * CH_MOE).astype(F32)
        jj = lax.broadcasted_iota(jnp.int32, (CH_MOE, tt), 0).astype(F32)
        return rel, jnp.broadcast_to(rel, (CH_MOE, tt)) == jj

    @pl.when(e == 0)
    def _():
        b = lax.bitcast_convert_type(affT_ref[...], jnp.int32)
        thr = thr_ref[:, 0:1]
        tc = tcut_ref[:, 0:1]
        tg = i * (tt * SUBS_MOE) + lax.broadcasted_iota(jnp.int32, b.shape, 1)
        sel = (b > thr) | ((b == thr) & (tg <= tc))
        for sub in range(SUBS_MOE):
            sl = slice(sub * tt, (sub + 1) * tt)
            pos = _dot(jnp.where(sel[:, sl], 1.0, 0.0).astype(BF16), tri_ref[...])
            posm_all = jnp.where(sel[:, sl], pos, -1.0)
            pos_scr[:, sl] = posm_all
            for grp in range(N_EXPERTS // GATHER_GROUP):
                hits = [one_hot_rows(posm_all[ee:ee + 1], jnp.int32(0))[1]
                        for ee in range(grp * GATHER_GROUP, (grp + 1) * GATHER_GROUP)]
                onehot = jnp.where(jnp.concatenate(hits, axis=0), 1.0, 0.0).astype(BF16)
                rows = slice(grp * GATHER_GROUP * CH_MOE, (grp + 1) * GATHER_GROUP * CH_MOE)
                xe_scr[sub, rows, :] = _dot(onehot, h2_ref[sl, :]).astype(BF16)
        o_ref[...] = x_ref[...]

    posm = [pos_scr[pl.ds(e, 1), sub * tt:(sub + 1) * tt] for sub in range(SUBS_MOE)]
    gate = [affT_ref[pl.ds(e, 1), sub * tt:(sub + 1) * tt] for sub in range(SUBS_MOE)]
    n_sel = jnp.max(jnp.concatenate(posm, axis=0)) + 1.0
    n_ch = (n_sel.astype(jnp.int32) + CH_MOE - 1) // CH_MOE

    def chunk(c, pregathered):
        rels, hits, xes = [], [], []
        for sub in range(SUBS_MOE):
            rel, hit = one_hot_rows(posm[sub], c)
            if pregathered:
                xes.append(xe_scr[sub, pl.ds(pl.multiple_of(e * CH_MOE, CH_MOE), CH_MOE), :])
            else:
                xes.append(_dot(jnp.where(hit, 1.0, 0.0).astype(BF16),
                                h2_ref[sub * tt:(sub + 1) * tt, :]).astype(BF16))
            rels.append(rel)
            hits.append(hit)
        xe = jnp.concatenate(xes, axis=0)
        g = _dot(xe, wg_ref[...])
        u = _dot(xe, wu_ref[...])
        hid = (g * jax.nn.sigmoid(g) * u).astype(BF16)
        ye = _dot(hid, wd_ref[...])
        for sub in range(SUBS_MOE):
            gc = jnp.sum(jnp.where(hits[sub], jnp.broadcast_to(gate[sub], hits[sub].shape), 0.0),
                         axis=1, keepdims=True)
            yeb = jnp.concatenate([(ye[sub * CH_MOE:(sub + 1) * CH_MOE] * gc).astype(BF16),
                                   jnp.zeros((CHP_MOE - CH_MOE, ye.shape[1]), BF16)], axis=0)
            relT = jnp.broadcast_to(rels[sub], (CHP_MOE, tt)).T
            ll = lax.broadcasted_iota(jnp.int32, (tt, CHP_MOE), 1).astype(F32)
            hitT = jnp.where(relT == ll, 1.0, 0.0).astype(BF16)
            o_ref[sub * tt:(sub + 1) * tt, :] += _dot(hitT, yeb)

    chunk(jnp.int32(0), True)
    lax.fori_loop(1, n_ch, lambda c, carry: (chunk(c, False), carry)[1], 0)


def _moe(xn, h2, affT, thr, tcut, lw):
    T = xn.shape[0]
    tt = TT_MOE * SUBS_MOE
    once = pl.Buffered(1)
    return pl.pallas_call(
        _moe_kernel,
        grid=(T // tt, N_EXPERTS),
        in_specs=[
            pl.BlockSpec((tt, D_MODEL), lambda i, e: (i, 0), pipeline_mode=once),
            pl.BlockSpec((tt, D_MODEL), lambda i, e: (i, 0), pipeline_mode=once),
            pl.BlockSpec((N_EXPERTS, tt), lambda i, e: (0, i)),
            pl.BlockSpec((N_EXPERTS, 128), lambda i, e: (0, 0)),
            pl.BlockSpec((N_EXPERTS, 128), lambda i, e: (0, 0)),
            pl.BlockSpec((TT_MOE, TT_MOE), lambda i, e: (0, 0), pipeline_mode=once),
            pl.BlockSpec((None, D_MODEL, D_EXPERT), lambda i, e: (e, 0, 0)),
            pl.BlockSpec((None, D_MODEL, D_EXPERT), lambda i, e: (e, 0, 0)),
            pl.BlockSpec((None, D_EXPERT, D_MODEL), lambda i, e: (e, 0, 0)),
        ],
        out_specs=pl.BlockSpec((tt, D_MODEL), lambda i, e: (i, 0), pipeline_mode=once),
        out_shape=jax.ShapeDtypeStruct((T, D_MODEL), F32),
        scratch_shapes=[pltpu.VMEM((N_EXPERTS, tt), F32),
                        pltpu.VMEM((SUBS_MOE, N_EXPERTS * CH_MOE, D_MODEL), BF16)],
        compiler_params=pltpu.CompilerParams(dimension_semantics=("arbitrary", "arbitrary"),
                                             vmem_limit_bytes=VMEM_LIMIT),
        name="moe_ffn",
    )(xn, h2, affT, thr, tcut, lw['tri'], lw['w_gate'], lw['w_up'], lw['w_down'])


def _t5_bucket(rel):
    nb = NUM_BUCKETS // 2
    max_exact = nb // 2
    ret = jnp.where(rel > 0, nb, 0)
    n = jnp.abs(rel)
    nf = jnp.maximum(n, 1).astype(jnp.float32)
    large = max_exact + (jnp.log(nf / max_exact) / math.log(MAX_DISTANCE / max_exact) * (nb - max_exact)).astype(jnp.int32)
    large = jnp.minimum(large, nb - 1)
    return ret + jnp.where(n < max_exact, n, large)


def _col(v):
    return v.astype(F32).reshape(-1, 1)


def _pad_heads(w, n_heads, d, d_pad):
    k = w.shape[1]
    w3 = w.reshape(n_heads, d, k)
    return jnp.pad(w3, ((0, 0), (0, d_pad - d), (0, 0))).reshape(n_heads * d_pad, k)


def _toeplitz_tiles(fn, offsets, nk, nq):
    span = nk + nq
    out = []
    for off in offsets:
        u = fn(off + nk - 1 - jnp.arange(span, dtype=jnp.int32))
        rows = jnp.tile(u, (1, nk))[:, :nk * (span - 1)].reshape(u.shape[0], nk, span - 1)
        out.append(rows[:, :, nk - 1:nk - 1 + nq])
    return jnp.stack(out, axis=0)


def _prep_shared(rel_bias):
    rb = rel_bias.astype(F32) * LOG2E

    def bias_a(rel):
        return rb[:, :HA][_t5_bucket(rel)].T

    def bias_c(rel):
        return rb[:, HA:][_t5_bucket(rel)].T

    def in_band(rel):
        return (jnp.abs(rel) <= WINDOW)[None, :]

    bmax = jnp.max(rb[:, :HA], axis=0)
    near_offs = [d * BK_A for d in range(-1, NEAR_A - 1)]
    near_a = _toeplitz_tiles(lambda rel: jnp.exp2(bias_a(rel) - bmax[:, None]), near_offs, BK_A, BQ_A)
    far = _t5_bucket(jnp.array([-(1 << 20), 1 << 20], dtype=jnp.int32))
    sc_far = jnp.concatenate([rb[far[1], :HA], bmax, rb[far[0], :HA]])
    a_bounded = {
        'near': _toeplitz_tiles(lambda rel: jnp.exp2(bias_a(rel)), near_offs, BK_A, BQ_A),
        'sc': jnp.concatenate([jnp.exp2(rb[far[0], :HA]), jnp.exp2(rb[far[1], :HA])]),
        'bias_mag': jnp.max(jnp.abs(rb[:, :HA])),
    }
    bm_c = _toeplitz_tiles(lambda rel: jnp.where(in_band(rel), bias_c(rel), NEG_BIG),
                           [0, -BLOCK, -2 * BLOCK], KW_C, BLOCK)
    c_bounded = {
        'e': _toeplitz_tiles(lambda rel: jnp.where(in_band(rel), jnp.exp2(bias_c(rel)), 0.0),
                             [0, -WINDOW, -(KWB_C - QB_C)], KWB_C, QB_C),
        'bias_mag': jnp.max(jnp.abs(rb[:, HA:])),
    }
    return near_a, sc_far, bm_c, a_bounded, c_bounded


def _rope_tables(S):
    inv = 1.0 / (ROPE_BASE ** (jnp.arange(ROPE_HALF, dtype=jnp.float32) / ROPE_HALF))
    ang = jnp.arange(S, dtype=jnp.int32).astype(jnp.float32)[:, None] * inv[None, :]
    return jnp.cos(ang).T, jnp.sin(ang).T


def _logit_bounds(l, p, a_bounded, c_bounded, sa, sb, scc):
    qa, ka = p['qn_a'][l].astype(F32), p['kn_a'][l].astype(F32)
    bound_a = 1.01 * DA_QK * jnp.max(jnp.abs(qa * ka)) * sa + 2.0 * a_bounded['bias_mag']
    qb, kb = p['qn_b'][l].astype(F32), p['kn_b'][l].astype(F32)
    bound_b = 1.01 * DB_QK * jnp.max(jnp.abs(qb)) * jnp.max(jnp.abs(kb)) * sb
    qc, kc = p['qn_c'][l].astype(F32), p['kn_c'][l].astype(F32)
    bound_c = (1.01 * DC * jnp.max(jnp.abs(qc * kc)) * scc + 2.0 * c_bounded['bias_mag']
               + jnp.max(jnp.abs(p['sink_c'][l].astype(F32))) * LOG2E)
    return bound_a <= BOUND_LIMIT, bound_b <= BOUND_LIMIT, bound_c <= BOUND_LIMIT


def _prep_layer(l, p, shared):
    near_a, sc_far, bm_c, a_bounded, c_bounded = shared
    lam_init = 0.8 - 0.6 * math.exp(-0.3 * l)
    lam = (jnp.exp(jnp.sum(p['lam_q1'][l].astype(F32) * p['lam_k1'][l].astype(F32)))
           - jnp.exp(jnp.sum(p['lam_q2'][l].astype(F32) * p['lam_k2'][l].astype(F32))) + lam_init)
    w_uq = p['w_uq'][l]
    w_ukv = p['w_ukv'][l].reshape(KV_RANK, HB, DB_NOPE + DB_V)
    sa = DA_QK ** -0.5 * LOG2E
    sb = DB_QK ** -0.5 * LOG2E
    scc = DC ** -0.5 * LOG2E
    zpad = jnp.zeros((DB_PAD - DB_QK,), F32)
    lw = {
        'g_mix': p['norm_mix_g'][l].astype(F32).reshape(1, D_MODEL),
        'w_inT': p['w_in'][l].T.astype(BF16),
        'w_uqT': _pad_heads(w_uq.T, HB, DB_QK, DB_PAD).astype(BF16),
        'w_ukT': w_ukv[:, :, :DB_NOPE].reshape(KV_RANK, HB * DB_NOPE).T.astype(BF16),
        'w_uvT': w_ukv[:, :, DB_NOPE:].reshape(KV_RANK, HB * DB_V).T.astype(BF16),
        'gqa': _col(jnp.tile(p['qn_a'][l], 2 * HA) * sa),
        'gka': _col(jnp.tile(p['kn_a'][l], 2 * HA)),
        'gcq': _col(p['g_cq'][l]),
        'gckv': _col(p['g_ckv'][l]),
        'gqb': _col(jnp.tile(jnp.concatenate([p['qn_b'][l].astype(F32) * sb, zpad]), HB)),
        'gkb': _col(jnp.tile(jnp.concatenate([p['kn_b'][l].astype(F32), zpad]), HB)),
        'gqc': _col(jnp.tile(p['qn_c'][l], HC) * scc),
        'gkc': _col(jnp.tile(p['kn_c'][l], KVH_C)),
        'near_a': near_a,
        'sc_a': jnp.concatenate([sc_far, lam.reshape(1)]).astype(F32),
        'subln': _col(jnp.tile(p['subln_a'][l], HA) * (1.0 - lam_init)),
        'bm_c': bm_c,
        'sc_c': p['sink_c'][l].astype(F32) * LOG2E,
        'w_out': p['w_out'][l].astype(BF16),
        'g_ffn': p['norm_ffn_g'][l].astype(F32).reshape(1, D_MODEL),
        'w_r': jnp.pad(p['w_router'][l], ((0, 0), (0, 128 - N_EXPERTS))).astype(BF16),
        'w_gate': p['w_gate'][l].astype(BF16),
        'w_up': p['w_up'][l].astype(BF16),
        'w_down': p['w_down'][l].astype(BF16),
        'tri': jnp.triu(jnp.ones((TT_MOE, TT_MOE), F32), k=1).astype(BF16),
        'near_a_bounded': a_bounded['near'],
        'sc_a_bounded': jnp.concatenate([a_bounded['sc'], lam.reshape(1)]).astype(F32),
    }
    lw['e_c'] = c_bounded['e']
    lw['sc_c_bounded'] = jnp.concatenate([lw['sc_c'], jnp.exp2(lw['sc_c'])])
    lw['bounded_a'], lw['bounded_b'], lw['bounded_c'] = _logit_bounds(l, p, a_bounded, c_bounded, sa, sb, scc)
    return lw


def _trunk(x, layers):
    B, S, D = x.shape
    x2 = x.reshape(B * S, D)
    cosT, sinT = _rope_tables(S)
    for lw in layers:
        lw = dict(lw, cosT=cosT, sinT=sinT)
        qaT, ka, vaT, qbT, kb, vbT, qcT, kc, vcT = _in_proj(x2, lw, S)
        ya = lax.cond(lw['bounded_a'],
                      lambda q, k, v: _attn_a(q, k, v, lw, B, S, True),
                      lambda q, k, v: _attn_a(q, k, v, lw, B, S, False), qaT, ka, vaT)
        yb = lax.cond(lw['bounded_b'],
                      lambda q, k, v: _attn_b(q, k, v, B, S, True),
                      lambda q, k, v: _attn_b(q, k, v, B, S, False), qbT, kb, vbT)
        yc = lax.cond(lw['bounded_c'],
                      lambda q, k, v: _attn_c(q, k, v, lw, B, S, True),
                      lambda q, k, v: _attn_c(q, k, v, lw, B, S, False), qcT, kc, vcT)
        xn, h2, affT = _out_proj(x2, ya, yb, yc, lw)
        thr, tcut = _topk_thresholds(affT)
        x2 = _moe(xn, h2, affT, thr, tcut, lw)
    return x2.reshape(B, S, D)


def kernel(x_prompt, x_sample, rel_bias, norm_mix_g, w_in, qn_a, kn_a, lam_q1, lam_k1, lam_q2, lam_k2, subln_a, g_cq, w_uq, g_ckv, w_ukv, qn_b, kn_b, qn_c, kn_c, sink_c, w_out, norm_ffn_g, w_router, w_gate, w_up, w_down):
    p = dict(norm_mix_g=norm_mix_g, w_in=w_in, qn_a=qn_a, kn_a=kn_a, lam_q1=lam_q1, lam_k1=lam_k1,
             lam_q2=lam_q2, lam_k2=lam_k2, subln_a=subln_a, g_cq=g_cq, w_uq=w_uq, g_ckv=g_ckv, w_ukv=w_ukv,
             qn_b=qn_b, kn_b=kn_b, qn_c=qn_c, kn_c=kn_c, sink_c=sink_c, w_out=w_out, norm_ffn_g=norm_ffn_g,
             w_router=w_router, w_gate=w_gate, w_up=w_up, w_down=w_down)
    shared = _prep_shared(rel_bias)
    layers = [_prep_layer(l, p, shared) for l in range(w_in.shape[0])]
    return (_trunk(x_prompt, layers), _trunk(x_sample, layers))
```

```python
import functools
import math

import jax
import jax.numpy as jnp
import numpy as np
from jax import lax
from jax.experimental import pallas as pl
from jax.experimental.pallas import tpu as pltpu

D_MODEL = 1024
BLOCK = 128
HA, DA_QK, DA_V = 4, 32, 64
HB, Q_RANK, KV_RANK, DB_NOPE, DB_ROPE, DB_V = 6, 384, 256, 64, 32, 64
ROPE_BASE = 10000.0
HC, KVH_C, DC, WINDOW = 6, 2, 64, 128
G_C = HC // KVH_C
NUM_BUCKETS, MAX_DISTANCE = 32, 128
N_EXPERTS, D_EXPERT, CAPACITY_FACTOR = 16, 1024, 2
EPS = 1e-6
DEPTH = 2

DB_QK = DB_NOPE + DB_ROPE
DB_PAD = 128
VX = 80
BOUND_LIMIT = 80.0
ROPE_HALF = DB_ROPE // 2
IN_SPLITS = (HA * 2 * DA_QK, HA * 2 * DA_QK, HA * DA_V, Q_RANK, KV_RANK, DB_ROPE, HC * DC, KVH_C * DC, KVH_C * DC)
IN_OFFS = tuple(int(v) for v in np.cumsum((0,) + IN_SPLITS))
IN_COLS = IN_OFFS[-1]
LOG2E = 1.4426950408889634
NEG_BIG = -1e30

F32 = jnp.float32
BF16 = jnp.bfloat16

TM_IN = 512
TM_OUT = 512
BQ_A = 512
BK_A = 512
UNROLL_A = 2
NEAR_A = BQ_A // BK_A + 2
assert BQ_A % BK_A == 0 and BK_A >= MAX_DISTANCE
BQ_B = 512
BK_B = 512
UNROLL_B = 2
KW_C = 3 * BLOCK
QB_C = 256
KWB_C = QB_C + 2 * WINDOW
SUB_C = 2
TT_MOE = 1024
SUBS_MOE = 2
GATHER_GROUP = 4
CH_MOE = 160
CHP_MOE = 256
PREFETCH = 3
VMEM_LIMIT = 56 * 1024 * 1024


def _nt_dot(a, b):
    return lax.dot_general(a, b, (((1,), (1,)), ((), ())), preferred_element_type=F32)


def _dot(a, b):
    return jnp.dot(a, b, preferred_element_type=F32)


def _group_norm_rows(y, gs, n_valid):
    rows, tm = y.shape
    y3 = y.reshape(rows // gs, gs, tm)
    ms = jnp.sum(y3 * y3, axis=1, keepdims=True) * (1.0 / n_valid)
    return (y3 * lax.rsqrt(ms + EPS)).reshape(rows, tm)


def _in_kernel(x_ref, g_ref, winT_ref, wuqT_ref, wukT_ref, wuvT_ref,
               gqa_ref, gka_ref, gcq_ref, gckv_ref, gqb_ref, gkb_ref, gqc_ref, gkc_ref,
               cos_ref, sin_ref,
               qaT_ref, ka_ref, vaT_ref, qbT_ref, kb_ref, vbT_ref, qcT_ref, kc_ref, vcT_ref):
    x = x_ref[...]
    h = x * lax.rsqrt(jnp.mean(x * x, axis=-1, keepdims=True) + EPS) * g_ref[...]
    projT = _dot(winT_ref[...], h.T.astype(BF16))

    def seg(i):
        return projT[IN_OFFS[i]:IN_OFFS[i + 1]]

    cos = cos_ref[...]
    sin = sin_ref[...]

    qaT_ref[...] = (_group_norm_rows(seg(0), DA_QK, DA_QK) * gqa_ref[...]).astype(BF16)
    kaT = _group_norm_rows(seg(1), DA_QK, DA_QK) * gka_ref[...]
    ka_ref[...] = kaT.T.astype(BF16)
    def with_ones_rows(v, n_heads, d):
        tm = v.shape[1]
        tail = jnp.where(lax.broadcasted_iota(jnp.int32, (VX - d, tm), 0) == 0, 1.0, 0.0)
        rows = []
        for hh in range(n_heads):
            rows += [v[hh * d:(hh + 1) * d], tail]
        return jnp.concatenate(rows, axis=0).astype(BF16)

    vaT_ref[...] = with_ones_rows(seg(2), HA, DA_V)

    def rope_store(dst_rows, y, h):
        o = h * DB_PAD
        x1 = y[o + DB_NOPE:o + DB_NOPE + ROPE_HALF]
        x2 = y[o + DB_NOPE + ROPE_HALF:o + DB_QK]
        dst_rows.append(y[o:o + DB_NOPE])
        dst_rows.append(x1 * cos - x2 * sin)
        dst_rows.append(x2 * cos + x1 * sin)
        dst_rows.append(jnp.zeros((DB_PAD - DB_QK, y.shape[1]), F32))

    cqT = seg(3)
    cqn = (cqT * lax.rsqrt(jnp.mean(cqT * cqT, axis=0, keepdims=True) + EPS) * gcq_ref[...]).astype(BF16)
    qf = _group_norm_rows(_dot(wuqT_ref[...], cqn), DB_PAD, DB_QK) * gqb_ref[...]
    rows = []
    for hh in range(HB):
        rope_store(rows, qf, hh)
    qbT_ref[...] = jnp.concatenate(rows, axis=0).astype(BF16)

    ckvT = seg(4)
    ckvn = (ckvT * lax.rsqrt(jnp.mean(ckvT * ckvT, axis=0, keepdims=True) + EPS) * gckv_ref[...]).astype(BF16)
    knope = _dot(wukT_ref[...], ckvn)
    vbT_ref[...] = with_ones_rows(_dot(wuvT_ref[...], ckvn), HB, DB_V)
    krope = seg(5)
    zpad = jnp.zeros((DB_PAD - DB_QK, krope.shape[1]), F32)
    rows = []
    for hh in range(HB):
        rows += [knope[hh * DB_NOPE:(hh + 1) * DB_NOPE], krope, zpad]
    kf = _group_norm_rows(jnp.concatenate(rows, axis=0), DB_PAD, DB_QK) * gkb_ref[...]
    rows = []
    for hh in range(HB):
        rope_store(rows, kf, hh)
    kb_ref[...] = jnp.concatenate(rows, axis=0).T.astype(BF16)

    qcT_ref[...] = (_group_norm_rows(seg(6), DC, DC) * gqc_ref[...]).astype(BF16)
    kcT = _group_norm_rows(seg(7), DC, DC) * gkc_ref[...]
    kc_ref[...] = kcT.T.astype(BF16)
    vcT_ref[...] = with_ones_rows(seg(8), KVH_C, DC)


def _in_proj(x2, lw, S):
    T = x2.shape[0]
    tm = TM_IN
    nt = T // tm
    spb = S // tm

    def full(a):
        return pl.BlockSpec(a.shape, lambda i: (0,) * a.ndim)

    consts = [lw['g_mix'], lw['w_inT'], lw['w_uqT'], lw['w_ukT'], lw['w_uvT'],
              lw['gqa'], lw['gka'], lw['gcq'], lw['gckv'], lw['gqb'], lw['gkb'], lw['gqc'], lw['gkc']]
    in_specs = ([pl.BlockSpec((tm, D_MODEL), lambda i: (i, 0))] + [full(a) for a in consts]
                + [pl.BlockSpec((ROPE_HALF, tm), lambda i: (0, i % spb))] * 2)

    def fm(rows):
        return jax.ShapeDtypeStruct((rows, T), BF16), pl.BlockSpec((rows, tm), lambda i: (0, i))

    def tk(cols):
        return jax.ShapeDtypeStruct((T, cols), BF16), pl.BlockSpec((tm, cols), lambda i: (i, 0))

    outs = [fm(HA * 2 * DA_QK), tk(HA * 2 * DA_QK), fm(HA * VX),
            fm(HB * DB_PAD), tk(HB * DB_PAD), fm(HB * VX),
            fm(HC * DC), tk(KVH_C * DC), fm(KVH_C * VX)]
    return pl.pallas_call(
        _in_kernel,
        grid=(nt,),
        in_specs=in_specs,
        out_specs=[o[1] for o in outs],
        out_shape=[o[0] for o in outs],
        compiler_params=pltpu.CompilerParams(dimension_semantics=("arbitrary",), vmem_limit_bytes=VMEM_LIMIT),
        name="in_proj",
    )(x2, *consts, lw['cosT'], lw['sinT'])


def _attn_a_kernel(sc_ref, qT_ref, k_ref, vT_ref, near_ref, subln_ref, o_ref, w_scr, m_scr, l_scr, acc_scr, *, nk):
    qb = pl.program_id(1)
    n_comb = 2 * HA
    bq = qT_ref.shape[1]
    near_lo = (BQ_A // BK_A) * qb - 1

    for hc in range(n_comb):
        g = hc // 4
        qg = qT_ref[g * 128:(g + 1) * 128, :]
        row = lax.broadcasted_iota(jnp.int32, (128, bq), 0)
        keep = (row // DA_QK) == (hc % 4)
        w_scr[hc] = jnp.where(keep, qg, jnp.zeros_like(qg))
    m_scr[...] = jnp.full(m_scr.shape, NEG_BIG, F32)
    l_scr[...] = jnp.zeros(l_scr.shape, F32)
    acc_scr[...] = jnp.zeros(acc_scr.shape, F32)

    def step(kb, mode):
        ks = pl.multiple_of(kb * BK_A, BK_A)

        def qk(hc):
            g = hc // 4
            return _dot(k_ref[pl.ds(ks, BK_A), g * 128:(g + 1) * 128], w_scr[hc])

        ahead = [qk(i) for i in range(PREFETCH)]
        for hc in range(n_comb):
            h = hc // 2
            s = ahead.pop(0)
            if hc + PREFETCH < n_comb:
                ahead.append(qk(hc + PREFETCH))
            c = sc_ref[(1 - mode) * HA + h]
            cmax = jnp.max(s, axis=0, keepdims=True) + c
            m_old = m_scr[hc]
            m_new = jnp.maximum(m_old, cmax)
            alpha = jnp.exp2(m_old - m_new)
            p = jnp.exp2(s - (m_new - c))
            if mode == 0:
                p = p * near_ref[kb - near_lo, h]
            l_scr[hc] = alpha * l_scr[hc] + jnp.sum(p, axis=0, keepdims=True)
            pv = _dot(vT_ref[h * VX:h * VX + DA_V, pl.ds(ks, BK_A)], p.astype(BF16))
            acc_scr[hc] = alpha * acc_scr[hc] + pv
            m_scr[hc] = m_new

    lo = jnp.maximum(near_lo, 0)
    hi = jnp.minimum(near_lo + NEAR_A, nk)
    lax.fori_loop(0, lo, lambda kb, c: (step(kb, -1), c)[1], 0)
    lax.fori_loop(lo, hi, lambda kb, c: (step(kb, 0), c)[1], 0)
    lax.fori_loop(hi, nk, lambda kb, c: (step(kb, 1), c)[1], 0)

    lam = sc_ref[3 * HA]
    outs = []
    for h in range(HA):
        a = acc_scr[2 * h] / l_scr[2 * h] - lam * (acc_scr[2 * h + 1] / l_scr[2 * h + 1])
        a = a * lax.rsqrt(jnp.mean(a * a, axis=0, keepdims=True) + EPS)
        outs.append(a)
    o = jnp.concatenate(outs, axis=0) * subln_ref[...]
    o_ref[...] = o.T.astype(BF16)


def _sublane_partial_sum(p):
    n, w = p.shape
    return jnp.sum(p.reshape(n // 8, 8, w), axis=0)


def _attn_a_bounded_kernel(sc_ref, qT_ref, k_ref, vT_ref, near_ref, subln_ref, o_ref, w_scr, acc_scr, l_scr, *, nk):
    qb = pl.program_id(1)
    n_comb = 2 * HA
    bq = qT_ref.shape[1]
    near_lo = (BQ_A // BK_A) * qb - 1
    for hc in range(n_comb):
        g = hc // 4
        qg = qT_ref[g * 128:(g + 1) * 128, :]
        row = lax.broadcasted_iota(jnp.int32, (128, bq), 0)
        w_scr[hc] = jnp.where((row // DA_QK) == (hc % 4), qg, jnp.zeros_like(qg))
    acc_scr[...] = jnp.zeros(acc_scr.shape, F32)
    l_scr[...] = jnp.zeros(l_scr.shape, F32)

    lo = jnp.maximum(near_lo, 0)
    hi = jnp.minimum(near_lo + NEAR_A, nk)
    n_far = lo + (nk - hi)

    def step(j0, n_blocks, near):
        def block(jj):
            j = j0 + jj
            if near:
                return j, None
            right = j >= lo
            return jnp.where(right, j - lo + hi, j), jnp.where(right, HA, 0)

        tiles = [(jj, hc) for jj in range(n_blocks) for hc in range(n_comb)]

        def qk(t):
            jj, hc = tiles[t]
            g = hc // 4
            ks = pl.multiple_of(block(jj)[0] * BK_A, BK_A)
            return _dot(k_ref[pl.ds(ks, BK_A), g * 128:(g + 1) * 128], w_scr[hc])

        ahead = [qk(i) for i in range(PREFETCH)]
        for t, (jj, hc) in enumerate(tiles):
            h = hc // 2
            s = ahead.pop(0)
            if t + PREFETCH < len(tiles):
                ahead.append(qk(t + PREFETCH))
            kb, side = block(jj)
            ks = pl.multiple_of(kb * BK_A, BK_A)
            p = jnp.exp2(s)
            if near:
                p = p * near_ref[kb - near_lo, h]
            psum = _sublane_partial_sum(p)
            pv = _dot(vT_ref[h * VX:h * VX + DA_V, pl.ds(ks, BK_A)], p.astype(BF16))
            if near:
                acc_scr[hc] += pv
                l_scr[hc] += psum
            else:
                f = sc_ref[side + h]
                acc_scr[hc] += pv * f
                l_scr[hc] += psum * f

    n_pairs = n_far // UNROLL_A
    lax.fori_loop(0, n_pairs, lambda i, c: (step(i * UNROLL_A, UNROLL_A, False), c)[1], 0)
    lax.fori_loop(n_pairs * UNROLL_A, n_far, lambda j, c: (step(j, 1, False), c)[1], 0)
    lax.fori_loop(lo, hi, lambda kb, c: (step(kb, 1, True), c)[1], 0)

    lam = sc_ref[2 * HA]
    outs = []
    for h in range(HA):
        l0 = jnp.sum(l_scr[2 * h], axis=0, keepdims=True)
        l1 = jnp.sum(l_scr[2 * h + 1], axis=0, keepdims=True)
        a = acc_scr[2 * h] / l0 - lam * (acc_scr[2 * h + 1] / l1)
        outs.append(a * lax.rsqrt(jnp.mean(a * a, axis=0, keepdims=True) + EPS))
    o = jnp.concatenate(outs, axis=0) * subln_ref[...]
    o_ref[...] = o.T.astype(BF16)


def _attn_a(qT, k, vT, lw, B, S, bounded):
    nq = S // BQ_A
    nk = S // BK_A
    n_comb = 2 * HA
    if bounded:
        body = functools.partial(_attn_a_bounded_kernel, nk=nk)
        scratch = [pltpu.VMEM((n_comb, 128, BQ_A), BF16), pltpu.VMEM((n_comb, DA_V, BQ_A), F32),
                   pltpu.VMEM((n_comb, 8, BQ_A), F32)]
        sc, near = lw['sc_a_bounded'], lw['near_a_bounded']
    else:
        body = functools.partial(_attn_a_kernel, nk=nk)
        scratch = [pltpu.VMEM((n_comb, 128, BQ_A), BF16), pltpu.VMEM((n_comb, 1, BQ_A), F32),
                   pltpu.VMEM((n_comb, 1, BQ_A), F32), pltpu.VMEM((n_comb, DA_V, BQ_A), F32)]
        sc, near = lw['sc_a'], lw['near_a']
    return pl.pallas_call(
        body,
        grid=(B, nq),
        in_specs=[
            pl.BlockSpec(memory_space=pltpu.SMEM),
            pl.BlockSpec((n_comb * DA_QK, BQ_A), lambda b, q: (0, b * nq + q)),
            pl.BlockSpec((S, n_comb * DA_QK), lambda b, q: (b, 0)),
            pl.BlockSpec((HA * VX, S), lambda b, q: (0, b)),
            pl.BlockSpec(near.shape, lambda b, q: (0, 0, 0, 0)),
            pl.BlockSpec((HA * DA_V, 1), lambda b, q: (0, 0)),
        ],
        out_specs=pl.BlockSpec((BQ_A, HA * DA_V), lambda b, q: (b * nq + q, 0)),
        out_shape=jax.ShapeDtypeStruct((B * S, HA * DA_V), BF16),
        scratch_shapes=scratch,
        compiler_params=pltpu.CompilerParams(dimension_semantics=("arbitrary", "arbitrary"),
                                             vmem_limit_bytes=VMEM_LIMIT),
        name="attn_a_bounded" if bounded else "attn_a",
    )(sc, qT, k, vT, near, lw['subln'])


def _attn_b_kernel(qT_ref, k_ref, vT_ref, o_ref, m_scr, l_scr, acc_scr, *, nk):
    m_scr[...] = jnp.full(m_scr.shape, NEG_BIG, F32)
    l_scr[...] = jnp.zeros(l_scr.shape, F32)
    acc_scr[...] = jnp.zeros(acc_scr.shape, F32)

    def step(kb, carry):
        ks = pl.multiple_of(kb * BK_B, BK_B)

        def qk(h):
            return _dot(k_ref[pl.ds(ks, BK_B), h * DB_PAD:(h + 1) * DB_PAD], qT_ref[h * DB_PAD:(h + 1) * DB_PAD, :])

        ahead = [qk(i) for i in range(PREFETCH)]
        for h in range(HB):
            s = ahead.pop(0)
            if h + PREFETCH < HB:
                ahead.append(qk(h + PREFETCH))
            m_old = m_scr[h]
            m_new = jnp.maximum(m_old, jnp.max(s, axis=0, keepdims=True))
            alpha = jnp.exp2(m_old - m_new)
            p = jnp.exp2(s - m_new)
            l_scr[h] = alpha * l_scr[h] + jnp.sum(p, axis=0, keepdims=True)
            pv = _dot(vT_ref[h * VX:h * VX + DB_V, pl.ds(ks, BK_B)], p.astype(BF16))
            acc_scr[h] = alpha * acc_scr[h] + pv
            m_scr[h] = m_new
        return carry

    lax.fori_loop(0, nk, step, 0)
    o = jnp.concatenate([acc_scr[h] / l_scr[h] for h in range(HB)], axis=0)
    o_ref[...] = o.T.astype(BF16)


def _attn_b_bounded_kernel(qT_ref, k_ref, vT_ref, o_ref, acc_scr, l_scr, *, nk):
    acc_scr[...] = jnp.zeros(acc_scr.shape, F32)
    l_scr[...] = jnp.zeros(l_scr.shape, F32)

    assert nk % UNROLL_B == 0
    tiles = [(j, h) for j in range(UNROLL_B) for h in range(HB)]

    def step(it, carry):
        def ks(j):
            return pl.multiple_of((it * UNROLL_B + j) * BK_B, BK_B)

        def qk(t):
            j, h = tiles[t]
            return _dot(k_ref[pl.ds(ks(j), BK_B), h * DB_PAD:(h + 1) * DB_PAD],
                        qT_ref[h * DB_PAD:(h + 1) * DB_PAD, :])

        ahead = [qk(i) for i in range(PREFETCH)]
        for t, (j, h) in enumerate(tiles):
            s = ahead.pop(0)
            if t + PREFETCH < len(tiles):
                ahead.append(qk(t + PREFETCH))
            p = jnp.exp2(s)
            l_scr[h] += _sublane_partial_sum(p)
            acc_scr[h] += _dot(vT_ref[h * VX:h * VX + DB_V, pl.ds(ks(j), BK_B)], p.astype(BF16))
        return carry

    lax.fori_loop(0, nk // UNROLL_B, step, 0)
    o = jnp.concatenate([acc_scr[h] / jnp.sum(l_scr[h], axis=0, keepdims=True) for h in range(HB)], axis=0)
    o_ref[...] = o.T.astype(BF16)


def _attn_b(qT, k, vT, B, S, bounded):
    nq = S // BQ_B
    nk = S // BK_B
    if bounded:
        body = functools.partial(_attn_b_bounded_kernel, nk=nk)
        scratch = [pltpu.VMEM((HB, DB_V, BQ_B), F32), pltpu.VMEM((HB, 8, BQ_B), F32)]
    else:
        body = functools.partial(_attn_b_kernel, nk=nk)
        scratch = [pltpu.VMEM((HB, 1, BQ_B), F32), pltpu.VMEM((HB, 1, BQ_B), F32), pltpu.VMEM((HB, DB_V, BQ_B), F32)]
    return pl.pallas_call(
        body,
        grid=(B, nq),
        in_specs=[
            pl.BlockSpec((HB * DB_PAD, BQ_B), lambda b, q: (0, b * nq + q)),
            pl.BlockSpec((S, HB * DB_PAD), lambda b, q: (b, 0)),
            pl.BlockSpec((HB * VX, S), lambda b, q: (0, b)),
        ],
        out_specs=pl.BlockSpec((BQ_B, HB * DB_V), lambda b, q: (b * nq + q, 0)),
        out_shape=jax.ShapeDtypeStruct((B * S, HB * DB_V), BF16),
        scratch_shapes=scratch,
        compiler_params=pltpu.CompilerParams(dimension_semantics=("arbitrary", "arbitrary"),
                                             vmem_limit_bytes=VMEM_LIMIT),
        name="attn_b_bounded" if bounded else "attn_b",
    )(qT, k, vT)


def _attn_c_kernel(sc_ref, qT_ref, k_ref, vT_ref, bm_ref, o_ref, *, nb, S):
    n = pl.program_id(1)
    start = pl.multiple_of(jnp.clip((n - 1) * BLOCK, 0, S - KW_C), BLOCK)
    case = jnp.where(n == 0, 0, jnp.where(n == nb - 1, 2, 1))
    kwin = k_ref[pl.ds(start, KW_C), :]
    row = lax.broadcasted_iota(jnp.int32, (KVH_C * DC, BLOCK), 0)
    outs = []
    for h in range(HC):
        j = h // G_C
        qh = qT_ref[h * DC:(h + 1) * DC, :]
        zz = jnp.zeros_like(qh)
        w = jnp.concatenate([qh, zz] if j == 0 else [zz, qh], axis=0)
        s = _dot(kwin, w) + bm_ref[case, h]
        sink = sc_ref[h]
        m = jnp.maximum(jnp.max(s, axis=0, keepdims=True), sink)
        p = jnp.exp2(s - m)
        den = jnp.sum(p, axis=0, keepdims=True) + jnp.exp2(sink - m)
        pv = _dot(vT_ref[j * VX:j * VX + DC, pl.ds(start, KW_C)], p.astype(BF16))
        outs.append(pv / den)
    del row
    o_ref[...] = jnp.concatenate(outs, axis=0).T.astype(BF16)


def _attn_c_bounded_kernel(sc_ref, qT_ref, k_ref, vT_ref, e_ref, o_ref, *, n_sub, S):
    g = pl.program_id(1)
    tiles = [(u, h) for u in range(SUB_C) for h in range(HC)]

    def window(u):
        n = g * SUB_C + u
        start = pl.multiple_of(jnp.clip(n * QB_C - WINDOW, 0, S - KWB_C), WINDOW)
        case = jnp.where(n == 0, 0, jnp.where(n == n_sub - 1, 2, 1))
        return start, case

    def qk(t):
        u, h = tiles[t]
        start, _ = window(u)
        qh = qT_ref[h * DC:(h + 1) * DC, u * QB_C:(u + 1) * QB_C]
        zz = jnp.zeros_like(qh)
        w = jnp.concatenate([qh, zz] if h // G_C == 0 else [zz, qh], axis=0)
        return _dot(k_ref[pl.ds(start, KWB_C), :], w)

    ahead = [qk(i) for i in range(PREFETCH)]
    outs = []
    for t, (u, h) in enumerate(tiles):
        s = ahead.pop(0)
        if t + PREFETCH < len(tiles):
            ahead.append(qk(t + PREFETCH))
        start, case = window(u)
        j = h // G_C
        p = (jnp.exp2(s) * e_ref[case, h]).astype(BF16)
        pv = _dot(vT_ref[j * VX:(j + 1) * VX, pl.ds(start, KWB_C)], p)
        outs.append(pv[0:DC] / (pv[DC:DC + 1] + sc_ref[HC + h]))
        if h == HC - 1:
            o_ref[u * QB_C:(u + 1) * QB_C, :] = jnp.concatenate(outs, axis=0).T.astype(BF16)
            outs = []


def _attn_c(qT, k, vT, lw, B, S, bounded):
    if bounded:
        n_sub = S // QB_C
        assert n_sub >= 3 and n_sub % SUB_C == 0
        nb, bq = n_sub // SUB_C, QB_C * SUB_C
        body = functools.partial(_attn_c_bounded_kernel, n_sub=n_sub, S=S)
        tile, sc = lw['e_c'], lw['sc_c_bounded']
    else:
        nb, bq = S // BLOCK, BLOCK
        assert nb >= 3
        body = functools.partial(_attn_c_kernel, nb=nb, S=S)
        tile, sc = lw['bm_c'], lw['sc_c']
    return pl.pallas_call(
        body,
        grid=(B, nb),
        in_specs=[
            pl.BlockSpec(memory_space=pltpu.SMEM),
            pl.BlockSpec((HC * DC, bq), lambda b, n: (0, b * nb + n)),
            pl.BlockSpec((S, KVH_C * DC), lambda b, n: (b, 0)),
            pl.BlockSpec((KVH_C * VX, S), lambda b, n: (0, b)),
            pl.BlockSpec(tile.shape, lambda b, n: (0, 0, 0, 0)),
        ],
        out_specs=pl.BlockSpec((bq, HC * DC), lambda b, n: (b * nb + n, 0)),
        out_shape=jax.ShapeDtypeStruct((B * S, HC * DC), BF16),
        compiler_params=pltpu.CompilerParams(dimension_semantics=("arbitrary", "arbitrary"),
                                             vmem_limit_bytes=VMEM_LIMIT),
        name="attn_c_bounded" if bounded else "attn_c",
    )(sc, qT, k, vT, tile)


def _out_kernel(x_ref, ya_ref, yb_ref, yc_ref, wo_ref, g_ref, wr_ref, xn_ref, h2_ref, affT_ref):
    na, nb_ = HA * DA_V, HA * DA_V + HB * DB_V
    xn = (x_ref[...] + _dot(ya_ref[...], wo_ref[0:na, :]) + _dot(yb_ref[...], wo_ref[na:nb_, :])
          + _dot(yc_ref[...], wo_ref[nb_:, :]))
    xn_ref[...] = xn
    h2 = (xn * lax.rsqrt(jnp.mean(xn * xn, axis=-1, keepdims=True) + EPS) * g_ref[...]).astype(BF16)
    h2_ref[...] = h2
    lg = _dot(h2, wr_ref[...]).T[0:N_EXPERTS]
    e = jnp.exp(lg - jnp.max(lg, axis=0, keepdims=True))
    affT_ref[...] = e / jnp.sum(e, axis=0, keepdims=True)


def _out_proj(x2, ya, yb, yc, lw):
    T = x2.shape[0]
    tm = TM_OUT

    def full(a):
        return pl.BlockSpec(a.shape, lambda i: (0,) * a.ndim)

    return pl.pallas_call(
        _out_kernel,
        grid=(T // tm,),
        in_specs=[pl.BlockSpec((tm, D_MODEL), lambda i: (i, 0)),
                  pl.BlockSpec((tm, ya.shape[1]), lambda i: (i, 0)),
                  pl.BlockSpec((tm, yb.shape[1]), lambda i: (i, 0)),
                  pl.BlockSpec((tm, yc.shape[1]), lambda i: (i, 0)),
                  full(lw['w_out']), full(lw['g_ffn']), full(lw['w_r'])],
        out_specs=[pl.BlockSpec((tm, D_MODEL), lambda i: (i, 0)),
                   pl.BlockSpec((tm, D_MODEL), lambda i: (i, 0)),
                   pl.BlockSpec((N_EXPERTS, tm), lambda i: (0, i))],
        out_shape=[jax.ShapeDtypeStruct((T, D_MODEL), F32),
                   jax.ShapeDtypeStruct((T, D_MODEL), BF16),
                   jax.ShapeDtypeStruct((N_EXPERTS, T), F32)],
        compiler_params=pltpu.CompilerParams(dimension_semantics=("arbitrary",), vmem_limit_bytes=VMEM_LIMIT),
        name="out_proj",
    )(x2, ya, yb, yc, lw['w_out'], lw['g_ffn'], lw['w_r'])


def _topk_kernel(aff_ref, thr_ref, tcut_ref, *, cap, T):
    capf = float(cap)

    def bits():
        return lax.bitcast_convert_type(aff_ref[...], jnp.int32)

    def count(mask):
        return jnp.sum(jnp.where(mask, 1.0, 0.0), axis=1, keepdims=True)

    def vbody(i, v):
        cand = v | jnp.left_shift(jnp.int32(1), 30 - i)
        return jnp.where(count(bits() >= cand) >= capf, cand, v)

    thr = lax.fori_loop(0, 31, vbody, jnp.zeros((N_EXPERTS, 1), jnp.int32))
    need = capf - count(bits() > thr)
    nbits = max(1, int(math.ceil(math.log2(T))))

    def tbody(i, c):
        cand = c | jnp.left_shift(jnp.int32(1), nbits - 1 - i)
        idx = lax.broadcasted_iota(jnp.int32, (N_EXPERTS, T), 1)
        f = count((bits() == thr) & (idx < cand))
        return jnp.where(f < need, cand, c)

    tcut = lax.fori_loop(0, nbits, tbody, jnp.zeros((N_EXPERTS, 1), jnp.int32))
    thr_ref[...] = jnp.broadcast_to(thr, thr_ref.shape)
    tcut_ref[...] = jnp.broadcast_to(tcut, tcut_ref.shape)


def _topk_thresholds(affT):
    E, T = affT.shape
    cap = CAPACITY_FACTOR * T // N_EXPERTS
    return pl.pallas_call(
        functools.partial(_topk_kernel, cap=cap, T=T),
        grid=(1,),
        in_specs=[pl.BlockSpec((E, T), lambda i: (0, 0))],
        out_specs=[pl.BlockSpec((E, 128), lambda i: (0, 0))] * 2,
        out_shape=[jax.ShapeDtypeStruct((E, 128), jnp.int32)] * 2,
        compiler_params=pltpu.CompilerParams(dimension_semantics=("arbitrary",), vmem_limit_bytes=VMEM_LIMIT),
        name="topk_thr",
    )(affT)


def _moe_kernel(x_ref, h2_ref, affT_ref, thr_ref, tcut_ref, tri_ref, wg_ref, wu_ref, wd_ref, o_ref, pos_scr, xe_scr):
    i = pl.program_id(0)
    e = pl.program_id(1)
    tt = TT_MOE

    def one_hot_rows(posm, c):
        rel = posm - (c * CH_MOE).astype(F32)
        jj = lax.broadcasted_iota(jnp.int32, (CH_MOE, tt), 0).astype(F32)
        return rel, jnp.broadcast_to(rel, (CH_MOE, tt)) == jj

    @pl.when(e == 0)
    def _():
        b = lax.bitcast_convert_type(affT_ref[...], jnp.int32)
        thr = thr_ref[:, 0:1]
        tc = tcut_ref[:, 0:1]
        tg = i * (tt * SUBS_MOE) + lax.broadcasted_iota(jnp.int32, b.shape, 1)
        sel = (b > thr) | ((b == thr) & (tg <= tc))
        for sub in range(SUBS_MOE):
            sl = slice(sub * tt, (sub + 1) * tt)
            pos = _dot(jnp.where(sel[:, sl], 1.0, 0.0).astype(BF16), tri_ref[...])
            posm_all = jnp.where(sel[:, sl], pos, -1.0)
            pos_scr[:, sl] = posm_all
            for grp in range(N_EXPERTS // GATHER_GROUP):
                hits = [one_hot_rows(posm_all[ee:ee + 1], jnp.int32(0))[1]
                        for ee in range(grp * GATHER_GROUP, (grp + 1) * GATHER_GROUP)]
                onehot = jnp.where(jnp.concatenate(hits, axis=0), 1.0, 0.0).astype(BF16)
                rows = slice(grp * GATHER_GROUP * CH_MOE, (grp + 1) * GATHER_GROUP * CH_MOE)
                xe_scr[sub, rows, :] = _dot(onehot, h2_ref[sl, :]).astype(BF16)
        o_ref[...] = x_ref[...]

    posm = [pos_scr[pl.ds(e, 1), sub * tt:(sub + 1) * tt] for sub in range(SUBS_MOE)]
    gate = [affT_ref[pl.ds(e, 1), sub * tt:(sub + 1) * tt] for sub in range(SUBS_MOE)]
    n_sel = jnp.max(jnp.concatenate(posm, axis=0)) + 1.0
    n_ch = (n_sel.astype(jnp.int32) + CH_MOE - 1) // CH_MOE

    def chunk(c, pregathered):
        rels, hits, xes = [], [], []
        for sub in range(SUBS_MOE):
            rel, hit = one_hot_rows(posm[sub], c)
            if pregathered:
                xes.append(xe_scr[sub, pl.ds(pl.multiple_of(e * CH_MOE, CH_MOE), CH_MOE), :])
            else:
                xes.append(_dot(jnp.where(hit, 1.0, 0.0).astype(BF16),
                                h2_ref[sub * tt:(sub + 1) * tt, :]).astype(BF16))
            rels.append(rel)
            hits.append(hit)
        xe = jnp.concatenate(xes, axis=0)
        g = _dot(xe, wg_ref[...])
        u = _dot(xe, wu_ref[...])
        hid = (g * jax.nn.sigmoid(g) * u).astype(BF16)
        ye = _dot(hid, wd_ref[...])
        for sub in range(SUBS_MOE):
            gc = jnp.sum(jnp.where(hits[sub], jnp.broadcast_to(gate[sub], hits[sub].shape), 0.0),
                         axis=1, keepdims=True)
            yeb = jnp.concatenate([(ye[sub * CH_MOE:(sub + 1) * CH_MOE] * gc).astype(BF16),
                                   jnp.zeros((CHP_MOE - CH_MOE, ye.shape[1]), BF16)], axis=0)
            relT = jnp.broadcast_to(rels[sub], (CHP_MOE, tt)).T
            ll = lax.broadcasted_iota(jnp.int32, (tt, CHP_MOE), 1).astype(F32)
            hitT = jnp.where(relT == ll, 1.0, 0.0).astype(BF16)
            o_ref[sub * tt:(sub + 1) * tt, :] += _dot(hitT, yeb)

    chunk(jnp.int32(0), True)
    lax.fori_loop(1, n_ch, lambda c, carry: (chunk(c, False), carry)[1], 0)


def _moe(xn, h2, affT, thr, tcut, lw):
    T = xn.shape[0]
    tt = TT_MOE * SUBS_MOE
    once = pl.Buffered(1)
    return pl.pallas_call(
        _moe_kernel,
        grid=(T // tt, N_EXPERTS),
        in_specs=[
            pl.BlockSpec((tt, D_MODEL), lambda i, e: (i, 0), pipeline_mode=once),
            pl.BlockSpec((tt, D_MODEL), lambda i, e: (i, 0), pipeline_mode=once),
            pl.BlockSpec((N_EXPERTS, tt), lambda i, e: (0, i)),
            pl.BlockSpec((N_EXPERTS, 128), lambda i, e: (0, 0)),
            pl.BlockSpec((N_EXPERTS, 128), lambda i, e: (0, 0)),
            pl.BlockSpec((TT_MOE, TT_MOE), lambda i, e: (0, 0), pipeline_mode=once),
            pl.BlockSpec((None, D_MODEL, D_EXPERT), lambda i, e: (e, 0, 0)),
            pl.BlockSpec((None, D_MODEL, D_EXPERT), lambda i, e: (e, 0, 0)),
            pl.BlockSpec((None, D_EXPERT, D_MODEL), lambda i, e: (e, 0, 0)),
        ],
        out_specs=pl.BlockSpec((tt, D_MODEL), lambda i, e: (i, 0), pipeline_mode=once),
        out_shape=jax.ShapeDtypeStruct((T, D_MODEL), F32),
        scratch_shapes=[pltpu.VMEM((N_EXPERTS, tt), F32),
                        pltpu.VMEM((SUBS_MOE, N_EXPERTS * CH_MOE, D_MODEL), BF16)],
        compiler_params=pltpu.CompilerParams(dimension_semantics=("arbitrary", "arbitrary"),
                                             vmem_limit_bytes=VMEM_LIMIT),
        name="moe_ffn",
    )(xn, h2, affT, thr, tcut, lw['tri'], lw['w_gate'], lw['w_up'], lw['w_down'])


def _t5_bucket(rel):
    nb = NUM_BUCKETS // 2
    max_exact = nb // 2
    ret = jnp.where(rel > 0, nb, 0)
    n = jnp.abs(rel)
    nf = jnp.maximum(n, 1).astype(jnp.float32)
    large = max_exact + (jnp.log(nf / max_exact) / math.log(MAX_DISTANCE / max_exact) * (nb - max_exact)).astype(jnp.int32)
    large = jnp.minimum(large, nb - 1)
    return ret + jnp.where(n < max_exact, n, large)


def _col(v):
    return v.astype(F32).reshape(-1, 1)


def _pad_heads(w, n_heads, d, d_pad):
    k = w.shape[1]
    w3 = w.reshape(n_heads, d, k)
    return jnp.pad(w3, ((0, 0), (0, d_pad - d), (0, 0))).reshape(n_heads * d_pad, k)


def _toeplitz_tiles(fn, offsets, nk, nq):
    span = nk + nq
    out = []
    for off in offsets:
        u = fn(off + nk - 1 - jnp.arange(span, dtype=jnp.int32))
        rows = jnp.tile(u, (1, nk))[:, :nk * (span - 1)].reshape(u.shape[0], nk, span - 1)
        out.append(rows[:, :, nk - 1:nk - 1 + nq])
    return jnp.stack(out, axis=0)


def _prep_shared(rel_bias):
    rb = rel_bias.astype(F32) * LOG2E

    def bias_a(rel):
        return rb[:, :HA][_t5_bucket(rel)].T

    def bias_c(rel):
        return rb[:, HA:][_t5_bucket(rel)].T

    def in_band(rel):
        return (jnp.abs(rel) <= WINDOW)[None, :]

    bmax = jnp.max(rb[:, :HA], axis=0)
    near_offs = [d * BK_A for d in range(-1, NEAR_A - 1)]
    near_a = _toeplitz_tiles(lambda rel: jnp.exp2(bias_a(rel) - bmax[:, None]), near_offs, BK_A, BQ_A)
    far = _t5_bucket(jnp.array([-(1 << 20), 1 << 20], dtype=jnp.int32))
    sc_far = jnp.concatenate([rb[far[1], :HA], bmax, rb[far[0], :HA]])
    a_bounded = {
        'near': _toeplitz_tiles(lambda rel: jnp.exp2(bias_a(rel)), near_offs, BK_A, BQ_A),
        'sc': jnp.concatenate([jnp.exp2(rb[far[0], :HA]), jnp.exp2(rb[far[1], :HA])]),
        'bias_mag': jnp.max(jnp.abs(rb[:, :HA])),
    }
    bm_c = _toeplitz_tiles(lambda rel: jnp.where(in_band(rel), bias_c(rel), NEG_BIG),
                           [0, -BLOCK, -2 * BLOCK], KW_C, BLOCK)
    c_bounded = {
        'e': _toeplitz_tiles(lambda rel: jnp.where(in_band(rel), jnp.exp2(bias_c(rel)), 0.0),
                             [0, -WINDOW, -(KWB_C - QB_C)], KWB_C, QB_C),
        'bias_mag': jnp.max(jnp.abs(rb[:, HA:])),
    }
    return near_a, sc_far, bm_c, a_bounded, c_bounded


def _rope_tables(S):
    inv = 1.0 / (ROPE_BASE ** (jnp.arange(ROPE_HALF, dtype=jnp.float32) / ROPE_HALF))
    ang = jnp.arange(S, dtype=jnp.int32).astype(jnp.float32)[:, None] * inv[None, :]
    return jnp.cos(ang).T, jnp.sin(ang).T


def _logit_bounds(l, p, a_bounded, c_bounded, sa, sb, scc):
    qa, ka = p['qn_a'][l].astype(F32), p['kn_a'][l].astype(F32)
    bound_a = 1.01 * DA_QK * jnp.max(jnp.abs(qa * ka)) * sa + 2.0 * a_bounded['bias_mag']
    qb, kb = p['qn_b'][l].astype(F32), p['kn_b'][l].astype(F32)
    bound_b = 1.01 * DB_QK * jnp.max(jnp.abs(qb)) * jnp.max(jnp.abs(kb)) * sb
    qc, kc = p['qn_c'][l].astype(F32), p['kn_c'][l].astype(F32)
    bound_c = (1.01 * DC * jnp.max(jnp.abs(qc * kc)) * scc + 2.0 * c_bounded['bias_mag']
               + jnp.max(jnp.abs(p['sink_c'][l].astype(F32))) * LOG2E)
    return bound_a <= BOUND_LIMIT, bound_b <= BOUND_LIMIT, bound_c <= BOUND_LIMIT


def _prep_layer(l, p, shared):
    near_a, sc_far, bm_c, a_bounded, c_bounded = shared
    lam_init = 0.8 - 0.6 * math.exp(-0.3 * l)
    lam = (jnp.exp(jnp.sum(p['lam_q1'][l].astype(F32) * p['lam_k1'][l].astype(F32)))
           - jnp.exp(jnp.sum(p['lam_q2'][l].astype(F32) * p['lam_k2'][l].astype(F32))) + lam_init)
    w_uq = p['w_uq'][l]
    w_ukv = p['w_ukv'][l].reshape(KV_RANK, HB, DB_NOPE + DB_V)
    sa = DA_QK ** -0.5 * LOG2E
    sb = DB_QK ** -0.5 * LOG2E
    scc = DC ** -0.5 * LOG2E
    zpad = jnp.zeros((DB_PAD - DB_QK,), F32)
    lw = {
        'g_mix': p['norm_mix_g'][l].astype(F32).reshape(1, D_MODEL),
        'w_inT': p['w_in'][l].T.astype(BF16),
        'w_uqT': _pad_heads(w_uq.T, HB, DB_QK, DB_PAD).astype(BF16),
        'w_ukT': w_ukv[:, :, :DB_NOPE].reshape(KV_RANK, HB * DB_NOPE).T.astype(BF16),
        'w_uvT': w_ukv[:, :, DB_NOPE:].reshape(KV_RANK, HB * DB_V).T.astype(BF16),
        'gqa': _col(jnp.tile(p['qn_a'][l], 2 * HA) * sa),
        'gka': _col(jnp.tile(p['kn_a'][l], 2 * HA)),
        'gcq': _col(p['g_cq'][l]),
        'gckv': _col(p['g_ckv'][l]),
        'gqb': _col(jnp.tile(jnp.concatenate([p['qn_b'][l].astype(F32) * sb, zpad]), HB)),
        'gkb': _col(jnp.tile(jnp.concatenate([p['kn_b'][l].astype(F32), zpad]), HB)),
        'gqc': _col(jnp.tile(p['qn_c'][l], HC) * scc),
        'gkc': _col(jnp.tile(p['kn_c'][l], KVH_C)),
        'near_a': near_a,
        'sc_a': jnp.concatenate([sc_far, lam.reshape(1)]).astype(F32),
        'subln': _col(jnp.tile(p['subln_a'][l], HA) * (1.0 - lam_init)),
        'bm_c': bm_c,
        'sc_c': p['sink_c'][l].astype(F32) * LOG2E,
        'w_out': p['w_out'][l].astype(BF16),
        'g_ffn': p['norm_ffn_g'][l].astype(F32).reshape(1, D_MODEL),
        'w_r': jnp.pad(p['w_router'][l], ((0, 0), (0, 128 - N_EXPERTS))).astype(BF16),
        'w_gate': p['w_gate'][l].astype(BF16),
        'w_up': p['w_up'][l].astype(BF16),
        'w_down': p['w_down'][l].astype(BF16),
        'tri': jnp.triu(jnp.ones((TT_MOE, TT_MOE), F32), k=1).astype(BF16),
        'near_a_bounded': a_bounded['near'],
        'sc_a_bounded': jnp.concatenate([a_bounded['sc'], lam.reshape(1)]).astype(F32),
    }
    lw['e_c'] = c_bounded['e']
    lw['sc_c_bounded'] = jnp.concatenate([lw['sc_c'], jnp.exp2(lw['sc_c'])])
    lw['bounded_a'], lw['bounded_b'], lw['bounded_c'] = _logit_bounds(l, p, a_bounded, c_bounded, sa, sb, scc)
    return lw


def _trunk(x, layers):
    B, S, D = x.shape
    x2 = x.reshape(B * S, D)
    cosT, sinT = _rope_tables(S)
    for lw in layers:
        lw = dict(lw, cosT=cosT, sinT=sinT)
        qaT, ka, vaT, qbT, kb, vbT, qcT, kc, vcT = _in_proj(x2, lw, S)
        ya = lax.cond(lw['bounded_a'],
                      lambda q, k, v: _attn_a(q, k, v, lw, B, S, True),
                      lambda q, k, v: _attn_a(q, k, v, lw, B, S, False), qaT, ka, vaT)
        yb = lax.cond(lw['bounded_b'],
                      lambda q, k, v: _attn_b(q, k, v, B, S, True),
                      lambda q, k, v: _attn_b(q, k, v, B, S, False), qbT, kb, vbT)
        yc = lax.cond(lw['bounded_c'],
                      lambda q, k, v: _attn_c(q, k, v, lw, B, S, True),
                      lambda q, k, v: _attn_c(q, k, v, lw, B, S, False), qcT, kc, vcT)
        xn, h2, affT = _out_proj(x2, ya, yb, yc, lw)
        thr, tcut = _topk_thresholds(affT)
        x2 = _moe(xn, h2, affT, thr, tcut, lw)
    return x2.reshape(B, S, D)


def kernel(x_prompt, x_sample, rel_bias, norm_mix_g, w_in, qn_a, kn_a, lam_q1, lam_k1, lam_q2, lam_k2, subln_a, g_cq, w_uq, g_ckv, w_ukv, qn_b, kn_b, qn_c, kn_c, sink_c, w_out, norm_ffn_g, w_router, w_gate, w_up, w_down):
    p = dict(norm_mix_g=norm_mix_g, w_in=w_in, qn_a=qn_a, kn_a=kn_a, lam_q1=lam_q1, lam_k1=lam_k1,
             lam_q2=lam_q2, lam_k2=lam_k2, subln_a=subln_a, g_cq=g_cq, w_uq=w_uq, g_ckv=g_ckv, w_ukv=w_ukv,
             qn_b=qn_b, kn_b=kn_b, qn_c=qn_c, kn_c=kn_c, sink_c=sink_c, w_out=w_out, norm_ffn_g=norm_ffn_g,
             w_router=w_router, w_gate=w_gate, w_up=w_up, w_down=w_down)
    shared = _prep_shared(rel_bias)
    layers = [_prep_layer(l, p, shared) for l in range(w_in.shape[0])]
    return (_trunk(x_prompt, layers), _trunk(x_sample, layers))
```

```python
import functools
import math

import jax
import jax.numpy as jnp
import numpy as np
from jax import lax
from jax.experimental import pallas as pl
from jax.experimental.pallas import tpu as pltpu

D_MODEL = 1024
BLOCK = 128
HA, DA_QK, DA_V = 4, 32, 64
HB, Q_RANK, KV_RANK, DB_NOPE, DB_ROPE, DB_V = 6, 384, 256, 64, 32, 64
ROPE_BASE = 10000.0
HC, KVH_C, DC, WINDOW = 6, 2, 64, 128
G_C = HC // KVH_C
NUM_BUCKETS, MAX_DISTANCE = 32, 128
N_EXPERTS, D_EXPERT, CAPACITY_FACTOR = 16, 1024, 2
EPS = 1e-6
DEPTH = 2

DB_QK = DB_NOPE + DB_ROPE
DB_PAD = 128
VX = 80
BOUND_LIMIT = 80.0
ROPE_HALF = DB_ROPE // 2
IN_SPLITS = (HA * 2 * DA_QK, HA * 2 * DA_QK, HA * DA_V, Q_RANK, KV_RANK, DB_ROPE, HC * DC, KVH_C * DC, KVH_C * DC)
IN_OFFS = tuple(int(v) for v in np.cumsum((0,) + IN_SPLITS))
IN_COLS = IN_OFFS[-1]
LOG2E = 1.4426950408889634
NEG_BIG = -1e30

F32 = jnp.float32
BF16 = jnp.bfloat16

TM_IN = 512
TM_OUT = 512
BQ_A = 512
BK_A = 512
UNROLL_A = 4
NEAR_A = BQ_A // BK_A + 2
assert BQ_A % BK_A == 0 and BK_A >= MAX_DISTANCE
BQ_B = 512
BK_B = 512
UNROLL_B = 4
KW_C = 3 * BLOCK
QB_C = 256
KWB_C = QB_C + 2 * WINDOW
SUB_C = 2
TT_MOE = 1024
SUBS_MOE = 2
GATHER_GROUP = 4
CH_MOE = 160
CHP_MOE = 256
PREFETCH = 3
VMEM_LIMIT = 56 * 1024 * 1024


def _nt_dot(a, b):
    return lax.dot_general(a, b, (((1,), (1,)), ((), ())), preferred_element_type=F32)


def _dot(a, b):
    return jnp.dot(a, b, preferred_element_type=F32)


def _group_norm_rows(y, gs, n_valid):
    rows, tm = y.shape
    y3 = y.reshape(rows // gs, gs, tm)
    ms = jnp.sum(y3 * y3, axis=1, keepdims=True) * (1.0 / n_valid)
    return (y3 * lax.rsqrt(ms + EPS)).reshape(rows, tm)


def _in_kernel(x_ref, g_ref, winT_ref, wuqT_ref, wukT_ref, wuvT_ref,
               gqa_ref, gka_ref, gcq_ref, gckv_ref, gqb_ref, gkb_ref, gqc_ref, gkc_ref,
               cos_ref, sin_ref,
               qaT_ref, ka_ref, vaT_ref, qbT_ref, kb_ref, vbT_ref, qcT_ref, kc_ref, vcT_ref):
    x = x_ref[...]
    h = x * lax.rsqrt(jnp.mean(x * x, axis=-1, keepdims=True) + EPS) * g_ref[...]
    projT = _dot(winT_ref[...], h.T.astype(BF16))

    def seg(i):
        return projT[IN_OFFS[i]:IN_OFFS[i + 1]]

    cos = cos_ref[...]
    sin = sin_ref[...]

    qaT_ref[...] = (_group_norm_rows(seg(0), DA_QK, DA_QK) * gqa_ref[...]).astype(BF16)
    kaT = _group_norm_rows(seg(1), DA_QK, DA_QK) * gka_ref[...]
    ka_ref[...] = kaT.T.astype(BF16)
    def with_ones_rows(v, n_heads, d):
        tm = v.shape[1]
        tail = jnp.where(lax.broadcasted_iota(jnp.int32, (VX - d, tm), 0) == 0, 1.0, 0.0)
        rows = []
        for hh in range(n_heads):
            rows += [v[hh * d:(hh + 1) * d], tail]
        return jnp.concatenate(rows, axis=0).astype(BF16)

    vaT_ref[...] = with_ones_rows(seg(2), HA, DA_V)

    def rope_store(dst_rows, y, h):
        o = h * DB_PAD
        x1 = y[o + DB_NOPE:o + DB_NOPE + ROPE_HALF]
        x2 = y[o + DB_NOPE + ROPE_HALF:o + DB_QK]
        dst_rows.append(y[o:o + DB_NOPE])
        dst_rows.append(x1 * cos - x2 * sin)
        dst_rows.append(x2 * cos + x1 * sin)
        dst_rows.append(jnp.zeros((DB_PAD - DB_QK, y.shape[1]), F32))

    cqT = seg(3)
    cqn = (cqT * lax.rsqrt(jnp.mean(cqT * cqT, axis=0, keepdims=True) + EPS) * gcq_ref[...]).astype(BF16)
    qf = _group_norm_rows(_dot(wuqT_ref[...], cqn), DB_PAD, DB_QK) * gqb_ref[...]
    rows = []
    for hh in range(HB):
        rope_store(rows, qf, hh)
    qbT_ref[...] = jnp.concatenate(rows, axis=0).astype(BF16)

    ckvT = seg(4)
    ckvn = (ckvT * lax.rsqrt(jnp.mean(ckvT * ckvT, axis=0, keepdims=True) + EPS) * gckv_ref[...]).astype(BF16)
    knope = _dot(wukT_ref[...], ckvn)
    vbT_ref[...] = with_ones_rows(_dot(wuvT_ref[...], ckvn), HB, DB_V)
    krope = seg(5)
    zpad = jnp.zeros((DB_PAD - DB_QK, krope.shape[1]), F32)
    rows = []
    for hh in range(HB):
        rows += [knope[hh * DB_NOPE:(hh + 1) * DB_NOPE], krope, zpad]
    kf = _group_norm_rows(jnp.concatenate(rows, axis=0), DB_PAD, DB_QK) * gkb_ref[...]
    rows = []
    for hh in range(HB):
        rope_store(rows, kf, hh)
    kb_ref[...] = jnp.concatenate(rows, axis=0).T.astype(BF16)

    qcT_ref[...] = (_group_norm_rows(seg(6), DC, DC) * gqc_ref[...]).astype(BF16)
    kcT = _group_norm_rows(seg(7), DC, DC) * gkc_ref[...]
    kc_ref[...] = kcT.T.astype(BF16)
    vcT_ref[...] = with_ones_rows(seg(8), KVH_C, DC)


def _in_proj(x2, lw, S):
    T = x2.shape[0]
    tm = TM_IN
    nt = T // tm
    spb = S // tm

    def full(a):
        return pl.BlockSpec(a.shape, lambda i: (0,) * a.ndim)

    consts = [lw['g_mix'], lw['w_inT'], lw['w_uqT'], lw['w_ukT'], lw['w_uvT'],
              lw['gqa'], lw['gka'], lw['gcq'], lw['gckv'], lw['gqb'], lw['gkb'], lw['gqc'], lw['gkc']]
    in_specs = ([pl.BlockSpec((tm, D_MODEL), lambda i: (i, 0))] + [full(a) for a in consts]
                + [pl.BlockSpec((ROPE_HALF, tm), lambda i: (0, i % spb))] * 2)

    def fm(rows):
        return jax.ShapeDtypeStruct((rows, T), BF16), pl.BlockSpec((rows, tm), lambda i: (0, i))

    def tk(cols):
        return jax.ShapeDtypeStruct((T, cols), BF16), pl.BlockSpec((tm, cols), lambda i: (i, 0))

    outs = [fm(HA * 2 * DA_QK), tk(HA * 2 * DA_QK), fm(HA * VX),
            fm(HB * DB_PAD), tk(HB * DB_PAD), fm(HB * VX),
            fm(HC * DC), tk(KVH_C * DC), fm(KVH_C * VX)]
    return pl.pallas_call(
        _in_kernel,
        grid=(nt,),
        in_specs=in_specs,
        out_specs=[o[1] for o in outs],
        out_shape=[o[0] for o in outs],
        compiler_params=pltpu.CompilerParams(dimension_semantics=("arbitrary",), vmem_limit_bytes=VMEM_LIMIT),
        name="in_proj",
    )(x2, *consts, lw['cosT'], lw['sinT'])


def _attn_a_kernel(sc_ref, qT_ref, k_ref, vT_ref, near_ref, subln_ref, o_ref, w_scr, m_scr, l_scr, acc_scr, *, nk):
    qb = pl.program_id(1)
    n_comb = 2 * HA
    bq = qT_ref.shape[1]
    near_lo = (BQ_A // BK_A) * qb - 1

    for hc in range(n_comb):
        g = hc // 4
        qg = qT_ref[g * 128:(g + 1) * 128, :]
        row = lax.broadcasted_iota(jnp.int32, (128, bq), 0)
        keep = (row // DA_QK) == (hc % 4)
        w_scr[hc] = jnp.where(keep, qg, jnp.zeros_like(qg))
    m_scr[...] = jnp.full(m_scr.shape, NEG_BIG, F32)
    l_scr[...] = jnp.zeros(l_scr.shape, F32)
    acc_scr[...] = jnp.zeros(acc_scr.shape, F32)

    def step(kb, mode):
        ks = pl.multiple_of(kb * BK_A, BK_A)

        def qk(hc):
            g = hc // 4
            return _dot(k_ref[pl.ds(ks, BK_A), g * 128:(g + 1) * 128], w_scr[hc])

        ahead = [qk(i) for i in range(PREFETCH)]
        for hc in range(n_comb):
            h = hc // 2
            s = ahead.pop(0)
            if hc + PREFETCH < n_comb:
                ahead.append(qk(hc + PREFETCH))
            c = sc_ref[(1 - mode) * HA + h]
            cmax = jnp.max(s, axis=0, keepdims=True) + c
            m_old = m_scr[hc]
            m_new = jnp.maximum(m_old, cmax)
            alpha = jnp.exp2(m_old - m_new)
            p = jnp.exp2(s - (m_new - c))
            if mode == 0:
                p = p * near_ref[kb - near_lo, h]
            l_scr[hc] = alpha * l_scr[hc] + jnp.sum(p, axis=0, keepdims=True)
            pv = _dot(vT_ref[h * VX:h * VX + DA_V, pl.ds(ks, BK_A)], p.astype(BF16))
            acc_scr[hc] = alpha * acc_scr[hc] + pv
            m_scr[hc] = m_new

    lo = jnp.maximum(near_lo, 0)
    hi = jnp.minimum(near_lo + NEAR_A, nk)
    lax.fori_loop(0, lo, lambda kb, c: (step(kb, -1), c)[1], 0)
    lax.fori_loop(lo, hi, lambda kb, c: (step(kb, 0), c)[1], 0)
    lax.fori_loop(hi, nk, lambda kb, c: (step(kb, 1), c)[1], 0)

    lam = sc_ref[3 * HA]
    outs = []
    for h in range(HA):
        a = acc_scr[2 * h] / l_scr[2 * h] - lam * (acc_scr[2 * h + 1] / l_scr[2 * h + 1])
        a = a * lax.rsqrt(jnp.mean(a * a, axis=0, keepdims=True) + EPS)
        outs.append(a)
    o = jnp.concatenate(outs, axis=0) * subln_ref[...]
    o_ref[...] = o.T.astype(BF16)


def _sublane_partial_sum(p):
    n, w = p.shape
    return jnp.sum(p.reshape(n // 8, 8, w), axis=0)


def _attn_a_bounded_kernel(sc_ref, qT_ref, k_ref, vT_ref, near_ref, subln_ref, o_ref, w_scr, acc_scr, l_scr, *, nk):
    qb = pl.program_id(1)
    n_comb = 2 * HA
    bq = qT_ref.shape[1]
    near_lo = (BQ_A // BK_A) * qb - 1
    for hc in range(n_comb):
        g = hc // 4
        qg = qT_ref[g * 128:(g + 1) * 128, :]
        row = lax.broadcasted_iota(jnp.int32, (128, bq), 0)
        w_scr[hc] = jnp.where((row // DA_QK) == (hc % 4), qg, jnp.zeros_like(qg))
    acc_scr[...] = jnp.zeros(acc_scr.shape, F32)
    l_scr[...] = jnp.zeros(l_scr.shape, F32)

    lo = jnp.maximum(near_lo, 0)
    hi = jnp.minimum(near_lo + NEAR_A, nk)
    n_far = lo + (nk - hi)

    def step(j0, n_blocks, near):
        def block(jj):
            j = j0 + jj
            if near:
                return j, None
            right = j >= lo
            return jnp.where(right, j - lo + hi, j), jnp.where(right, HA, 0)

        tiles = [(jj, hc) for jj in range(n_blocks) for hc in range(n_comb)]

        def qk(t):
            jj, hc = tiles[t]
            g = hc // 4
            ks = pl.multiple_of(block(jj)[0] * BK_A, BK_A)
            return _dot(k_ref[pl.ds(ks, BK_A), g * 128:(g + 1) * 128], w_scr[hc])

        ahead = [qk(i) for i in range(PREFETCH)]
        for t, (jj, hc) in enumerate(tiles):
            h = hc // 2
            s = ahead.pop(0)
            if t + PREFETCH < len(tiles):
                ahead.append(qk(t + PREFETCH))
            kb, side = block(jj)
            ks = pl.multiple_of(kb * BK_A, BK_A)
            p = jnp.exp2(s)
            if near:
                p = p * near_ref[kb - near_lo, h]
            psum = _sublane_partial_sum(p)
            pv = _dot(vT_ref[h * VX:h * VX + DA_V, pl.ds(ks, BK_A)], p.astype(BF16))
            if near:
                acc_scr[hc] += pv
                l_scr[hc] += psum
            else:
                f = sc_ref[side + h]
                acc_scr[hc] += pv * f
                l_scr[hc] += psum * f

    n_pairs = n_far // UNROLL_A
    lax.fori_loop(0, n_pairs, lambda i, c: (step(i * UNROLL_A, UNROLL_A, False), c)[1], 0)
    lax.fori_loop(n_pairs * UNROLL_A, n_far, lambda j, c: (step(j, 1, False), c)[1], 0)
    lax.fori_loop(lo, hi, lambda kb, c: (step(kb, 1, True), c)[1], 0)

    lam = sc_ref[2 * HA]
    outs = []
    for h in range(HA):
        l0 = jnp.sum(l_scr[2 * h], axis=0, keepdims=True)
        l1 = jnp.sum(l_scr[2 * h + 1], axis=0, keepdims=True)
        a = acc_scr[2 * h] / l0 - lam * (acc_scr[2 * h + 1] / l1)
        outs.append(a * lax.rsqrt(jnp.mean(a * a, axis=0, keepdims=True) + EPS))
    o = jnp.concatenate(outs, axis=0) * subln_ref[...]
    o_ref[...] = o.T.astype(BF16)


def _attn_a(qT, k, vT, lw, B, S, bounded):
    nq = S // BQ_A
    nk = S // BK_A
    n_comb = 2 * HA
    if bounded:
        body = functools.partial(_attn_a_bounded_kernel, nk=nk)
        scratch = [pltpu.VMEM((n_comb, 128, BQ_A), BF16), pltpu.VMEM((n_comb, DA_V, BQ_A), F32),
                   pltpu.VMEM((n_comb, 8, BQ_A), F32)]
        sc, near = lw['sc_a_bounded'], lw['near_a_bounded']
    else:
        body = functools.partial(_attn_a_kernel, nk=nk)
        scratch = [pltpu.VMEM((n_comb, 128, BQ_A), BF16), pltpu.VMEM((n_comb, 1, BQ_A), F32),
                   pltpu.VMEM((n_comb, 1, BQ_A), F32), pltpu.VMEM((n_comb, DA_V, BQ_A), F32)]
        sc, near = lw['sc_a'], lw['near_a']
    return pl.pallas_call(
        body,
        grid=(B, nq),
        in_specs=[
            pl.BlockSpec(memory_space=pltpu.SMEM),
            pl.BlockSpec((n_comb * DA_QK, BQ_A), lambda b, q: (0, b * nq + q)),
            pl.BlockSpec((S, n_comb * DA_QK), lambda b, q: (b, 0)),
            pl.BlockSpec((HA * VX, S), lambda b, q: (0, b)),
            pl.BlockSpec(near.shape, lambda b, q: (0, 0, 0, 0)),
            pl.BlockSpec((HA * DA_V, 1), lambda b, q: (0, 0)),
        ],
        out_specs=pl.BlockSpec((BQ_A, HA * DA_V), lambda b, q: (b * nq + q, 0)),
        out_shape=jax.ShapeDtypeStruct((B * S, HA * DA_V), BF16),
        scratch_shapes=scratch,
        compiler_params=pltpu.CompilerParams(dimension_semantics=("arbitrary", "arbitrary"),
                                             vmem_limit_bytes=VMEM_LIMIT),
        name="attn_a_bounded" if bounded else "attn_a",
    )(sc, qT, k, vT, near, lw['subln'])


def _attn_b_kernel(qT_ref, k_ref, vT_ref, o_ref, m_scr, l_scr, acc_scr, *, nk):
    m_scr[...] = jnp.full(m_scr.shape, NEG_BIG, F32)
    l_scr[...] = jnp.zeros(l_scr.shape, F32)
    acc_scr[...] = jnp.zeros(acc_scr.shape, F32)

    def step(kb, carry):
        ks = pl.multiple_of(kb * BK_B, BK_B)

        def qk(h):
            return _dot(k_ref[pl.ds(ks, BK_B), h * DB_PAD:(h + 1) * DB_PAD], qT_ref[h * DB_PAD:(h + 1) * DB_PAD, :])

        ahead = [qk(i) for i in range(PREFETCH)]
        for h in range(HB):
            s = ahead.pop(0)
            if h + PREFETCH < HB:
                ahead.append(qk(h + PREFETCH))
            m_old = m_scr[h]
            m_new = jnp.maximum(m_old, jnp.max(s, axis=0, keepdims=True))
            alpha = jnp.exp2(m_old - m_new)
            p = jnp.exp2(s - m_new)
            l_scr[h] = alpha * l_scr[h] + jnp.sum(p, axis=0, keepdims=True)
            pv = _dot(vT_ref[h * VX:h * VX + DB_V, pl.ds(ks, BK_B)], p.astype(BF16))
            acc_scr[h] = alpha * acc_scr[h] + pv
            m_scr[h] = m_new
        return carry

    lax.fori_loop(0, nk, step, 0)
    o = jnp.concatenate([acc_scr[h] / l_scr[h] for h in range(HB)], axis=0)
    o_ref[...] = o.T.astype(BF16)


def _attn_b_bounded_kernel(qT_ref, k_ref, vT_ref, o_ref, acc_scr, l_scr, *, nk):
    acc_scr[...] = jnp.zeros(acc_scr.shape, F32)
    l_scr[...] = jnp.zeros(l_scr.shape, F32)

    assert nk % UNROLL_B == 0
    tiles = [(j, h) for j in range(UNROLL_B) for h in range(HB)]

    def step(it, carry):
        def ks(j):
            return pl.multiple_of((it * UNROLL_B + j) * BK_B, BK_B)

        def qk(t):
            j, h = tiles[t]
            return _dot(k_ref[pl.ds(ks(j), BK_B), h * DB_PAD:(h + 1) * DB_PAD],
                        qT_ref[h * DB_PAD:(h + 1) * DB_PAD, :])

        ahead = [qk(i) for i in range(PREFETCH)]
        for t, (j, h) in enumerate(tiles):
            s = ahead.pop(0)
            if t + PREFETCH < len(tiles):
                ahead.append(qk(t + PREFETCH))
            p = jnp.exp2(s)
            l_scr[h] += _sublane_partial_sum(p)
            acc_scr[h] += _dot(vT_ref[h * VX:h * VX + DB_V, pl.ds(ks(j), BK_B)], p.astype(BF16))
        return carry

    lax.fori_loop(0, nk // UNROLL_B, step, 0)
    o = jnp.concatenate([acc_scr[h] / jnp.sum(l_scr[h], axis=0, keepdims=True) for h in range(HB)], axis=0)
    o_ref[...] = o.T.astype(BF16)


def _attn_b(qT, k, vT, B, S, bounded):
    nq = S // BQ_B
    nk = S // BK_B
    if bounded:
        body = functools.partial(_attn_b_bounded_kernel, nk=nk)
        scratch = [pltpu.VMEM((HB, DB_V, BQ_B), F32), pltpu.VMEM((HB, 8, BQ_B), F32)]
    else:
        body = functools.partial(_attn_b_kernel, nk=nk)
        scratch = [pltpu.VMEM((HB, 1, BQ_B), F32), pltpu.VMEM((HB, 1, BQ_B), F32), pltpu.VMEM((HB, DB_V, BQ_B), F32)]
    return pl.pallas_call(
        body,
        grid=(B, nq),
        in_specs=[
            pl.BlockSpec((HB * DB_PAD, BQ_B), lambda b, q: (0, b * nq + q)),
            pl.BlockSpec((S, HB * DB_PAD), lambda b, q: (b, 0)),
            pl.BlockSpec((HB * VX, S), lambda b, q: (0, b)),
        ],
        out_specs=pl.BlockSpec((BQ_B, HB * DB_V), lambda b, q: (b * nq + q, 0)),
        out_shape=jax.ShapeDtypeStruct((B * S, HB * DB_V), BF16),
        scratch_shapes=scratch,
        compiler_params=pltpu.CompilerParams(dimension_semantics=("arbitrary", "arbitrary"),
                                             vmem_limit_bytes=VMEM_LIMIT),
        name="attn_b_bounded" if bounded else "attn_b",
    )(qT, k, vT)


def _attn_c_kernel(sc_ref, qT_ref, k_ref, vT_ref, bm_ref, o_ref, *, nb, S):
    n = pl.program_id(1)
    start = pl.multiple_of(jnp.clip((n - 1) * BLOCK, 0, S - KW_C), BLOCK)
    case = jnp.where(n == 0, 0, jnp.where(n == nb - 1, 2, 1))
    kwin = k_ref[pl.ds(start, KW_C), :]
    row = lax.broadcasted_iota(jnp.int32, (KVH_C * DC, BLOCK), 0)
    outs = []
    for h in range(HC):
        j = h // G_C
        qh = qT_ref[h * DC:(h + 1) * DC, :]
        zz = jnp.zeros_like(qh)
        w = jnp.concatenate([qh, zz] if j == 0 else [zz, qh], axis=0)
        s = _dot(kwin, w) + bm_ref[case, h]
        sink = sc_ref[h]
        m = jnp.maximum(jnp.max(s, axis=0, keepdims=True), sink)
        p = jnp.exp2(s - m)
        den = jnp.sum(p, axis=0, keepdims=True) + jnp.exp2(sink - m)
        pv = _dot(vT_ref[j * VX:j * VX + DC, pl.ds(start, KW_C)], p.astype(BF16))
        outs.append(pv / den)
    del row
    o_ref[...] = jnp.concatenate(outs, axis=0).T.astype(BF16)


def _attn_c_bounded_kernel(sc_ref, qT_ref, k_ref, vT_ref, e_ref, o_ref, *, n_sub, S):
    g = pl.program_id(1)
    tiles = [(u, h) for u in range(SUB_C) for h in range(HC)]

    def window(u):
        n = g * SUB_C + u
        start = pl.multiple_of(jnp.clip(n * QB_C - WINDOW, 0, S - KWB_C), WINDOW)
        case = jnp.where(n == 0, 0, jnp.where(n == n_sub - 1, 2, 1))
        return start, case

    def qk(t):
        u, h = tiles[t]
        start, _ = window(u)
        qh = qT_ref[h * DC:(h + 1) * DC, u * QB_C:(u + 1) * QB_C]
        zz = jnp.zeros_like(qh)
        w = jnp.concatenate([qh, zz] if h // G_C == 0 else [zz, qh], axis=0)
        return _dot(k_ref[pl.ds(start, KWB_C), :], w)

    ahead = [qk(i) for i in range(PREFETCH)]
    outs = []
    for t, (u, h) in enumerate(tiles):
        s = ahead.pop(0)
        if t + PREFETCH < len(tiles):
            ahead.append(qk(t + PREFETCH))
        start, case = window(u)
        j = h // G_C
        p = (jnp.exp2(s) * e_ref[case, h]).astype(BF16)
        pv = _dot(vT_ref[j * VX:(j + 1) * VX, pl.ds(start, KWB_C)], p)
        outs.append(pv[0:DC] / (pv[DC:DC + 1] + sc_ref[HC + h]))
        if h == HC - 1:
            o_ref[u * QB_C:(u + 1) * QB_C, :] = jnp.concatenate(outs, axis=0).T.astype(BF16)
            outs = []


def _attn_c(qT, k, vT, lw, B, S, bounded):
    if bounded:
        n_sub = S // QB_C
        assert n_sub >= 3 and n_sub % SUB_C == 0
        nb, bq = n_sub // SUB_C, QB_C * SUB_C
        body = functools.partial(_attn_c_bounded_kernel, n_sub=n_sub, S=S)
        tile, sc = lw['e_c'], lw['sc_c_bounded']
    else:
        nb, bq = S // BLOCK, BLOCK
        assert nb >= 3
        body = functools.partial(_attn_c_kernel, nb=nb, S=S)
        tile, sc = lw['bm_c'], lw['sc_c']
    return pl.pallas_call(
        body,
        grid=(B, nb),
        in_specs=[
            pl.BlockSpec(memory_space=pltpu.SMEM),
            pl.BlockSpec((HC * DC, bq), lambda b, n: (0, b * nb + n)),
            pl.BlockSpec((S, KVH_C * DC), lambda b, n: (b, 0)),
            pl.BlockSpec((KVH_C * VX, S), lambda b, n: (0, b)),
            pl.BlockSpec(tile.shape, lambda b, n: (0, 0, 0, 0)),
        ],
        out_specs=pl.BlockSpec((bq, HC * DC), lambda b, n: (b * nb + n, 0)),
        out_shape=jax.ShapeDtypeStruct((B * S, HC * DC), BF16),
        compiler_params=pltpu.CompilerParams(dimension_semantics=("arbitrary", "arbitrary"),
                                             vmem_limit_bytes=VMEM_LIMIT),
        name="attn_c_bounded" if bounded else "attn_c",
    )(sc, qT, k, vT, tile)


def _out_kernel(x_ref, ya_ref, yb_ref, yc_ref, wo_ref, g_ref, wr_ref, xn_ref, h2_ref, affT_ref):
    na, nb_ = HA * DA_V, HA * DA_V + HB * DB_V
    xn = (x_ref[...] + _dot(ya_ref[...], wo_ref[0:na, :]) + _dot(yb_ref[...], wo_ref[na:nb_, :])
          + _dot(yc_ref[...], wo_ref[nb_:, :]))
    xn_ref[...] = xn
    h2 = (xn * lax.rsqrt(jnp.mean(xn * xn, axis=-1, keepdims=True) + EPS) * g_ref[...]).astype(BF16)
    h2_ref[...] = h2
    lg = _dot(h2, wr_ref[...]).T[0:N_EXPERTS]
    e = jnp.exp(lg - jnp.max(lg, axis=0, keepdims=True))
    affT_ref[...] = e / jnp.sum(e, axis=0, keepdims=True)


def _out_proj(x2, ya, yb, yc, lw):
    T = x2.shape[0]
    tm = TM_OUT

    def full(a):
        return pl.BlockSpec(a.shape, lambda i: (0,) * a.ndim)

    return pl.pallas_call(
        _out_kernel,
        grid=(T // tm,),
        in_specs=[pl.BlockSpec((tm, D_MODEL), lambda i: (i, 0)),
                  pl.BlockSpec((tm, ya.shape[1]), lambda i: (i, 0)),
                  pl.BlockSpec((tm, yb.shape[1]), lambda i: (i, 0)),
                  pl.BlockSpec((tm, yc.shape[1]), lambda i: (i, 0)),
                  full(lw['w_out']), full(lw['g_ffn']), full(lw['w_r'])],
        out_specs=[pl.BlockSpec((tm, D_MODEL), lambda i: (i, 0)),
                   pl.BlockSpec((tm, D_MODEL), lambda i: (i, 0)),
                   pl.BlockSpec((N_EXPERTS, tm), lambda i: (0, i))],
        out_shape=[jax.ShapeDtypeStruct((T, D_MODEL), F32),
                   jax.ShapeDtypeStruct((T, D_MODEL), BF16),
                   jax.ShapeDtypeStruct((N_EXPERTS, T), F32)],
        compiler_params=pltpu.CompilerParams(dimension_semantics=("arbitrary",), vmem_limit_bytes=VMEM_LIMIT),
        name="out_proj",
    )(x2, ya, yb, yc, lw['w_out'], lw['g_ffn'], lw['w_r'])


def _topk_kernel(aff_ref, thr_ref, tcut_ref, *, cap, T):
    capf = float(cap)

    def bits():
        return lax.bitcast_convert_type(aff_ref[...], jnp.int32)

    def count(mask):
        return jnp.sum(jnp.where(mask, 1.0, 0.0), axis=1, keepdims=True)

    def vbody(i, v):
        cand = v | jnp.left_shift(jnp.int32(1), 30 - i)
        return jnp.where(count(bits() >= cand) >= capf, cand, v)

    thr = lax.fori_loop(0, 31, vbody, jnp.zeros((N_EXPERTS, 1), jnp.int32))
    need = capf - count(bits() > thr)
    nbits = max(1, int(math.ceil(math.log2(T))))

    def tbody(i, c):
        cand = c | jnp.left_shift(jnp.int32(1), nbits - 1 - i)
        idx = lax.broadcasted_iota(jnp.int32, (N_EXPERTS, T), 1)
        f = count((bits() == thr) & (idx < cand))
        return jnp.where(f < need, cand, c)

    tcut = lax.fori_loop(0, nbits, tbody, jnp.zeros((N_EXPERTS, 1), jnp.int32))
    thr_ref[...] = jnp.broadcast_to(thr, thr_ref.shape)
    tcut_ref[...] = jnp.broadcast_to(tcut, tcut_ref.shape)


def _topk_thresholds(affT):
    E, T = affT.shape
    cap = CAPACITY_FACTOR * T // N_EXPERTS
    return pl.pallas_call(
        functools.partial(_topk_kernel, cap=cap, T=T),
        grid=(1,),
        in_specs=[pl.BlockSpec((E, T), lambda i: (0, 0))],
        out_specs=[pl.BlockSpec((E, 128), lambda i: (0, 0))] * 2,
        out_shape=[jax.ShapeDtypeStruct((E, 128), jnp.int32)] * 2,
        compiler_params=pltpu.CompilerParams(dimension_semantics=("arbitrary",), vmem_limit_bytes=VMEM_LIMIT),
        name="topk_thr",
    )(affT)


def _moe_kernel(x_ref, h2_ref, affT_ref, thr_ref, tcut_ref, tri_ref, wg_ref, wu_ref, wd_ref, o_ref, pos_scr, xe_scr):
    i = pl.program_id(0)
    e = pl.program_id(1)
    tt = TT_MOE

    def one_hot_rows(posm, c):
        rel = posm - (c * CH_MOE).astype(F32)
        jj = lax.broadcasted_iota(jnp.int32, (CH_MOE, tt), 0).astype(F32)
        return rel, jnp.broadcast_to(rel, (CH_MOE, tt)) == jj

    @pl.when(e == 0)
    def _():
        b = lax.bitcast_convert_type(affT_ref[...], jnp.int32)
        thr = thr_ref[:, 0:1]
        tc = tcut_ref[:, 0:1]
        tg = i * (tt * SUBS_MOE) + lax.broadcasted_iota(jnp.int32, b.shape, 1)
        sel = (b > thr) | ((b == thr) & (tg <= tc))
        for sub in range(SUBS_MOE):
            sl = slice(sub * tt, (sub + 1) * tt)
            pos = _dot(jnp.where(sel[:, sl], 1.0, 0.0).astype(BF16), tri_ref[...])
            posm_all = jnp.where(sel[:, sl], pos, -1.0)
            pos_scr[:, sl] = posm_all
            for grp in range(N_EXPERTS // GATHER_GROUP):
                hits = [one_hot_rows(posm_all[ee:ee + 1], jnp.int32(0))[1]
                        for ee in range(grp * GATHER_GROUP, (grp + 1) * GATHER_GROUP)]
                onehot = jnp.where(jnp.concatenate(hits, axis=0), 1.0, 0.0).astype(BF16)
                rows = slice(grp * GATHER_GROUP * CH_MOE, (grp + 1) * GATHER_GROUP * CH_MOE)
                xe_scr[sub, rows, :] = _dot(onehot, h2_ref[sl, :]).astype(BF16)
        o_ref[...] = x_ref[...]

    posm = [pos_scr[pl.ds(e, 1), sub * tt:(sub + 1) * tt] for sub in range(SUBS_MOE)]
    gate = [affT_ref[pl.ds(e, 1), sub * tt:(sub + 1) * tt] for sub in range(SUBS_MOE)]
    n_sel = jnp.max(jnp.concatenate(posm, axis=0)) + 1.0
    n_ch = (n_sel.astype(jnp.int32) + CH_MOE - 1) // CH_MOE

    def chunk(c, pregathered):
        rels, hits, xes = [], [], []
        for sub in range(SUBS_MOE):
            rel, hit = one_hot_rows(posm[sub], c)
            if pregathered:
                xes.append(xe_scr[sub, pl.ds(pl.multiple_of(e * CH_MOE, CH_MOE), CH_MOE), :])
            else:
                xes.append(_dot(jnp.where(hit, 1.0, 0.0).astype(BF16),
                                h2_ref[sub * tt:(sub + 1) * tt, :]).astype(BF16))
            rels.append(rel)
            hits.append(hit)
        xe = jnp.concatenate(xes, axis=0)
        g = _dot(xe, wg_ref[...])
        u = _dot(xe, wu_ref[...])
        hid = (g * jax.nn.sigmoid(g) * u).astype(BF16)
        ye = _dot(hid, wd_ref[...])
        for sub in range(SUBS_MOE):
            gc = jnp.sum(jnp.where(hits[sub], jnp.broadcast_to(gate[sub], hits[sub].shape), 0.0),
                         axis=1, keepdims=True)
            yeb = jnp.concatenate([(ye[sub * CH_MOE:(sub + 1) * CH_MOE] * gc).astype(BF16),
                                   jnp.zeros((CHP_MOE - CH_MOE, ye.shape[1]), BF16)], axis=0)
            relT = jnp.broadcast_to(rels[sub], (CHP_MOE, tt)).T
            ll = lax.broadcasted_iota(jnp.int32, (tt, CHP_MOE), 1).astype(F32)
            hitT = jnp.where(relT == ll, 1.0, 0.0).astype(BF16)
            o_ref[sub * tt:(sub + 1) * tt, :] += _dot(hitT, yeb)

    chunk(jnp.int32(0), True)
    lax.fori_loop(1, n_ch, lambda c, carry: (chunk(c, False), carry)[1], 0)


def _moe(xn, h2, affT, thr, tcut, lw):
    T = xn.shape[0]
    tt = TT_MOE * SUBS_MOE
    once = pl.Buffered(1)
    return pl.pallas_call(
        _moe_kernel,
        grid=(T // tt, N_EXPERTS),
        in_specs=[
            pl.BlockSpec((tt, D_MODEL), lambda i, e: (i, 0), pipeline_mode=once),
            pl.BlockSpec((tt, D_MODEL), lambda i, e: (i, 0), pipeline_mode=once),
            pl.BlockSpec((N_EXPERTS, tt), lambda i, e: (0, i)),
            pl.BlockSpec((N_EXPERTS, 128), lambda i, e: (0, 0)),
            pl.BlockSpec((N_EXPERTS, 128), lambda i, e: (0, 0)),
            pl.BlockSpec((TT_MOE, TT_MOE), lambda i, e: (0, 0), pipeline_mode=once),
            pl.BlockSpec((None, D_MODEL, D_EXPERT), lambda i, e: (e, 0, 0)),
            pl.BlockSpec((None, D_MODEL, D_EXPERT), lambda i, e: (e, 0, 0)),
            pl.BlockSpec((None, D_EXPERT, D_MODEL), lambda i, e: (e, 0, 0)),
        ],
        out_specs=pl.BlockSpec((tt, D_MODEL), lambda i, e: (i, 0), pipeline_mode=once),
        out_shape=jax.ShapeDtypeStruct((T, D_MODEL), F32),
        scratch_shapes=[pltpu.VMEM((N_EXPERTS, tt), F32),
                        pltpu.VMEM((SUBS_MOE, N_EXPERTS * CH_MOE, D_MODEL), BF16)],
        compiler_params=pltpu.CompilerParams(dimension_semantics=("arbitrary", "arbitrary"),
                                             vmem_limit_bytes=VMEM_LIMIT),
        name="moe_ffn",
    )(xn, h2, affT, thr, tcut, lw['tri'], lw['w_gate'], lw['w_up'], lw['w_down'])


def _t5_bucket(rel):
    nb = NUM_BUCKETS // 2
    max_exact = nb // 2
    ret = jnp.where(rel > 0, nb, 0)
    n = jnp.abs(rel)
    nf = jnp.maximum(n, 1).astype(jnp.float32)
    large = max_exact + (jnp.log(nf / max_exact) / math.log(MAX_DISTANCE / max_exact) * (nb - max_exact)).astype(jnp.int32)
    large = jnp.minimum(large, nb - 1)
    return ret + jnp.where(n < max_exact, n, large)


def _col(v):
    return v.astype(F32).reshape(-1, 1)


def _pad_heads(w, n_heads, d, d_pad):
    k = w.shape[1]
    w3 = w.reshape(n_heads, d, k)
    return jnp.pad(w3, ((0, 0), (0, d_pad - d), (0, 0))).reshape(n_heads * d_pad, k)


def _toeplitz_tiles(fn, offsets, nk, nq):
    span = nk + nq
    out = []
    for off in offsets:
        u = fn(off + nk - 1 - jnp.arange(span, dtype=jnp.int32))
        rows = jnp.tile(u, (1, nk))[:, :nk * (span - 1)].reshape(u.shape[0], nk, span - 1)
        out.append(rows[:, :, nk - 1:nk - 1 + nq])
    return jnp.stack(out, axis=0)


def _prep_shared(rel_bias):
    rb = rel_bias.astype(F32) * LOG2E

    def bias_a(rel):
        return rb[:, :HA][_t5_bucket(rel)].T

    def bias_c(rel):
        return rb[:, HA:][_t5_bucket(rel)].T

    def in_band(rel):
        return (jnp.abs(rel) <= WINDOW)[None, :]

    bmax = jnp.max(rb[:, :HA], axis=0)
    near_offs = [d * BK_A for d in range(-1, NEAR_A - 1)]
    near_a = _toeplitz_tiles(lambda rel: jnp.exp2(bias_a(rel) - bmax[:, None]), near_offs, BK_A, BQ_A)
    far = _t5_bucket(jnp.array([-(1 << 20), 1 << 20], dtype=jnp.int32))
    sc_far = jnp.concatenate([rb[far[1], :HA], bmax, rb[far[0], :HA]])
    a_bounded = {
        'near': _toeplitz_tiles(lambda rel: jnp.exp2(bias_a(rel)), near_offs, BK_A, BQ_A),
        'sc': jnp.concatenate([jnp.exp2(rb[far[0], :HA]), jnp.exp2(rb[far[1], :HA])]),
        'bias_mag': jnp.max(jnp.abs(rb[:, :HA])),
    }
    bm_c = _toeplitz_tiles(lambda rel: jnp.where(in_band(rel), bias_c(rel), NEG_BIG),
                           [0, -BLOCK, -2 * BLOCK], KW_C, BLOCK)
    c_bounded = {
        'e': _toeplitz_tiles(lambda rel: jnp.where(in_band(rel), jnp.exp2(bias_c(rel)), 0.0),
                             [0, -WINDOW, -(KWB_C - QB_C)], KWB_C, QB_C),
        'bias_mag': jnp.max(jnp.abs(rb[:, HA:])),
    }
    return near_a, sc_far, bm_c, a_bounded, c_bounded


def _rope_tables(S):
    inv = 1.0 / (ROPE_BASE ** (jnp.arange(ROPE_HALF, dtype=jnp.float32) / ROPE_HALF))
    ang = jnp.arange(S, dtype=jnp.int32).astype(jnp.float32)[:, None] * inv[None, :]
    return jnp.cos(ang).T, jnp.sin(ang).T


def _logit_bounds(l, p, a_bounded, c_bounded, sa, sb, scc):
    qa, ka = p['qn_a'][l].astype(F32), p['kn_a'][l].astype(F32)
    bound_a = 1.01 * DA_QK * jnp.max(jnp.abs(qa * ka)) * sa + 2.0 * a_bounded['bias_mag']
    qb, kb = p['qn_b'][l].astype(F32), p['kn_b'][l].astype(F32)
    bound_b = 1.01 * DB_QK * jnp.max(jnp.abs(qb)) * jnp.max(jnp.abs(kb)) * sb
    qc, kc = p['qn_c'][l].astype(F32), p['kn_c'][l].astype(F32)
    bound_c = (1.01 * DC * jnp.max(jnp.abs(qc * kc)) * scc + 2.0 * c_bounded['bias_mag']
               + jnp.max(jnp.abs(p['sink_c'][l].astype(F32))) * LOG2E)
    return bound_a <= BOUND_LIMIT, bound_b <= BOUND_LIMIT, bound_c <= BOUND_LIMIT


def _prep_layer(l, p, shared):
    near_a, sc_far, bm_c, a_bounded, c_bounded = shared
    lam_init = 0.8 - 0.6 * math.exp(-0.3 * l)
    lam = (jnp.exp(jnp.sum(p['lam_q1'][l].astype(F32) * p['lam_k1'][l].astype(F32)))
           - jnp.exp(jnp.sum(p['lam_q2'][l].astype(F32) * p['lam_k2'][l].astype(F32))) + lam_init)
    w_uq = p['w_uq'][l]
    w_ukv = p['w_ukv'][l].reshape(KV_RANK, HB, DB_NOPE + DB_V)
    sa = DA_QK ** -0.5 * LOG2E
    sb = DB_QK ** -0.5 * LOG2E
    scc = DC ** -0.5 * LOG2E
    zpad = jnp.zeros((DB_PAD - DB_QK,), F32)
    lw = {
        'g_mix': p['norm_mix_g'][l].astype(F32).reshape(1, D_MODEL),
        'w_inT': p['w_in'][l].T.astype(BF16),
        'w_uqT': _pad_heads(w_uq.T, HB, DB_QK, DB_PAD).astype(BF16),
        'w_ukT': w_ukv[:, :, :DB_NOPE].reshape(KV_RANK, HB * DB_NOPE).T.astype(BF16),
        'w_uvT': w_ukv[:, :, DB_NOPE:].reshape(KV_RANK, HB * DB_V).T.astype(BF16),
        'gqa': _col(jnp.tile(p['qn_a'][l], 2 * HA) * sa),
        'gka': _col(jnp.tile(p['kn_a'][l], 2 * HA)),
        'gcq': _col(p['g_cq'][l]),
        'gckv': _col(p['g_ckv'][l]),
        'gqb': _col(jnp.tile(jnp.concatenate([p['qn_b'][l].astype(F32) * sb, zpad]), HB)),
        'gkb': _col(jnp.tile(jnp.concatenate([p['kn_b'][l].astype(F32), zpad]), HB)),
        'gqc': _col(jnp.tile(p['qn_c'][l], HC) * scc),
        'gkc': _col(jnp.tile(p['kn_c'][l], KVH_C)),
        'near_a': near_a,
        'sc_a': jnp.concatenate([sc_far, lam.reshape(1)]).astype(F32),
        'subln': _col(jnp.tile(p['subln_a'][l], HA) * (1.0 - lam_init)),
        'bm_c': bm_c,
        'sc_c': p['sink_c'][l].astype(F32) * LOG2E,
        'w_out': p['w_out'][l].astype(BF16),
        'g_ffn': p['norm_ffn_g'][l].astype(F32).reshape(1, D_MODEL),
        'w_r': jnp.pad(p['w_router'][l], ((0, 0), (0, 128 - N_EXPERTS))).astype(BF16),
        'w_gate': p['w_gate'][l].astype(BF16),
        'w_up': p['w_up'][l].astype(BF16),
        'w_down': p['w_down'][l].astype(BF16),
        'tri': jnp.triu(jnp.ones((TT_MOE, TT_MOE), F32), k=1).astype(BF16),
        'near_a_bounded': a_bounded['near'],
        'sc_a_bounded': jnp.concatenate([a_bounded['sc'], lam.reshape(1)]).astype(F32),
    }
    lw['e_c'] = c_bounded['e']
    lw['sc_c_bounded'] = jnp.concatenate([lw['sc_c'], jnp.exp2(lw['sc_c'])])
    lw['bounded_a'], lw['bounded_b'], lw['bounded_c'] = _logit_bounds(l, p, a_bounded, c_bounded, sa, sb, scc)
    return lw


def _trunk(x, layers):
    B, S, D = x.shape
    x2 = x.reshape(B * S, D)
    cosT, sinT = _rope_tables(S)
    for lw in layers:
        lw = dict(lw, cosT=cosT, sinT=sinT)
        qaT, ka, vaT, qbT, kb, vbT, qcT, kc, vcT = _in_proj(x2, lw, S)
        ya = lax.cond(lw['bounded_a'],
                      lambda q, k, v: _attn_a(q, k, v, lw, B, S, True),
                      lambda q, k, v: _attn_a(q, k, v, lw, B, S, False), qaT, ka, vaT)
        yb = lax.cond(lw['bounded_b'],
                      lambda q, k, v: _attn_b(q, k, v, B, S, True),
                      lambda q, k, v: _attn_b(q, k, v, B, S, False), qbT, kb, vbT)
        yc = lax.cond(lw['bounded_c'],
                      lambda q, k, v: _attn_c(q, k, v, lw, B, S, True),
                      lambda q, k, v: _attn_c(q, k, v, lw, B, S, False), qcT, kc, vcT)
        xn, h2, affT = _out_proj(x2, ya, yb, yc, lw)
        thr, tcut = _topk_thresholds(affT)
        x2 = _moe(xn, h2, affT, thr, tcut, lw)
    return x2.reshape(B, S, D)


def kernel(x_prompt, x_sample, rel_bias, norm_mix_g, w_in, qn_a, kn_a, lam_q1, lam_k1, lam_q2, lam_k2, subln_a, g_cq, w_uq, g_ckv, w_ukv, qn_b, kn_b, qn_c, kn_c, sink_c, w_out, norm_ffn_g, w_router, w_gate, w_up, w_down):
    p = dict(norm_mix_g=norm_mix_g, w_in=w_in, qn_a=qn_a, kn_a=kn_a, lam_q1=lam_q1, lam_k1=lam_k1,
             lam_q2=lam_q2, lam_k2=lam_k2, subln_a=subln_a, g_cq=g_cq, w_uq=w_uq, g_ckv=g_ckv, w_ukv=w_ukv,
             qn_b=qn_b, kn_b=kn_b, qn_c=qn_c, kn_c=kn_c, sink_c=sink_c, w_out=w_out, norm_ffn_g=norm_ffn_g,
             w_router=w_router, w_gate=w_gate, w_up=w_up, w_down=w_down)
    shared = _prep_shared(rel_bias)
    layers = [_prep_layer(l, p, shared) for l in range(w_in.shape[0])]
    return (_trunk(x_prompt, layers), _trunk(x_sample, layers))
```

```python
import functools
import math

import jax
import jax.numpy as jnp
import numpy as np
from jax import lax
from jax.experimental import pallas as pl
from jax.experimental.pallas import tpu as pltpu

D_MODEL = 1024
BLOCK = 128
HA, DA_QK, DA_V = 4, 32, 64
HB, Q_RANK, KV_RANK, DB_NOPE, DB_ROPE, DB_V = 6, 384, 256, 64, 32, 64
ROPE_BASE = 10000.0
HC, KVH_C, DC, WINDOW = 6, 2, 64, 128
G_C = HC // KVH_C
NUM_BUCKETS, MAX_DISTANCE = 32, 128
N_EXPERTS, D_EXPERT, CAPACITY_FACTOR = 16, 1024, 2
EPS = 1e-6

DB_QK = DB_NOPE + DB_ROPE
DB_PAD = 128
VX = 80
BOUND_LIMIT = 80.0
ROPE_HALF = DB_ROPE // 2
IN_SPLITS = (HA * 2 * DA_QK, HA * 2 * DA_QK, HA * DA_V, Q_RANK, KV_RANK, DB_ROPE, HC * DC, KVH_C * DC, KVH_C * DC)
IN_OFFS = tuple(int(v) for v in np.cumsum((0,) + IN_SPLITS))
IN_COLS = IN_OFFS[-1]
LOG2E = 1.4426950408889634
NEG_BIG = -1e30

F32 = jnp.float32
BF16 = jnp.bfloat16

TM_IN = 512
TM_OUT = 512
BQ_A = 512
BK_A = 512
UNROLL_A = 4
NEAR_A = BQ_A // BK_A + 2
assert BQ_A % BK_A == 0 and BK_A >= MAX_DISTANCE
BQ_B = 512
BK_B = 512
UNROLL_B = 4
KW_C = 3 * BLOCK
QB_C = 256
KWB_C = QB_C + 2 * WINDOW
SUB_C = 2
TT_MOE = 1024
SUBS_MOE = 2
GATHER_GROUP = 4
CH_MOE = 160
CHP_MOE = 256
PREFETCH = 3
VMEM_LIMIT = 56 * 1024 * 1024


def _dot(a, b):
    return jnp.dot(a, b, preferred_element_type=F32)


def _group_norm_rows(y, gs, n_valid):
    rows, tm = y.shape
    y3 = y.reshape(rows // gs, gs, tm)
    ms = jnp.sum(y3 * y3, axis=1, keepdims=True) * (1.0 / n_valid)
    return (y3 * lax.rsqrt(ms + EPS)).reshape(rows, tm)


def _in_kernel(x_ref, g_ref, winT_ref, wuqT_ref, wukT_ref, wuvT_ref,
               gqa_ref, gka_ref, gcq_ref, gckv_ref, gqb_ref, gkb_ref, gqc_ref, gkc_ref,
               cos_ref, sin_ref,
               qaT_ref, ka_ref, vaT_ref, qbT_ref, kb_ref, vbT_ref, qcT_ref, kc_ref, vcT_ref):
    x = x_ref[...]
    h = x * lax.rsqrt(jnp.mean(x * x, axis=-1, keepdims=True) + EPS) * g_ref[...]
    projT = _dot(winT_ref[...], h.T.astype(BF16))

    def seg(i):
        return projT[IN_OFFS[i]:IN_OFFS[i + 1]]

    cos = cos_ref[...]
    sin = sin_ref[...]

    def with_ones_rows(v, n_heads, d):
        tm = v.shape[1]
        tail = jnp.where(lax.broadcasted_iota(jnp.int32, (VX - d, tm), 0) == 0, 1.0, 0.0)
        rows = []
        for hh in range(n_heads):
            rows += [v[hh * d:(hh + 1) * d], tail]
        return jnp.concatenate(rows, axis=0).astype(BF16)

    qaT_ref[...] = (_group_norm_rows(seg(0), DA_QK, DA_QK) * gqa_ref[...]).astype(BF16)
    kaT = _group_norm_rows(seg(1), DA_QK, DA_QK) * gka_ref[...]
    ka_ref[...] = kaT.T.astype(BF16)
    vaT_ref[...] = with_ones_rows(seg(2), HA, DA_V)

    def rope_store(dst_rows, y, h):
        o = h * DB_PAD
        x1 = y[o + DB_NOPE:o + DB_NOPE + ROPE_HALF]
        x2 = y[o + DB_NOPE + ROPE_HALF:o + DB_QK]
        dst_rows.append(y[o:o + DB_NOPE])
        dst_rows.append(x1 * cos - x2 * sin)
        dst_rows.append(x2 * cos + x1 * sin)
        dst_rows.append(jnp.zeros((DB_PAD - DB_QK, y.shape[1]), F32))

    cqT = seg(3)
    cqn = (cqT * lax.rsqrt(jnp.mean(cqT * cqT, axis=0, keepdims=True) + EPS) * gcq_ref[...]).astype(BF16)
    qf = _group_norm_rows(_dot(wuqT_ref[...], cqn), DB_PAD, DB_QK) * gqb_ref[...]
    rows = []
    for hh in range(HB):
        rope_store(rows, qf, hh)
    qbT_ref[...] = jnp.concatenate(rows, axis=0).astype(BF16)

    ckvT = seg(4)
    ckvn = (ckvT * lax.rsqrt(jnp.mean(ckvT * ckvT, axis=0, keepdims=True) + EPS) * gckv_ref[...]).astype(BF16)
    knope = _dot(wukT_ref[...], ckvn)
    vbT_ref[...] = with_ones_rows(_dot(wuvT_ref[...], ckvn), HB, DB_V)
    krope = seg(5)
    zpad = jnp.zeros((DB_PAD - DB_QK, krope.shape[1]), F32)
    rows = []
    for hh in range(HB):
        rows += [knope[hh * DB_NOPE:(hh + 1) * DB_NOPE], krope, zpad]
    kf = _group_norm_rows(jnp.concatenate(rows, axis=0), DB_PAD, DB_QK) * gkb_ref[...]
    rows = []
    for hh in range(HB):
        rope_store(rows, kf, hh)
    kb_ref[...] = jnp.concatenate(rows, axis=0).T.astype(BF16)

    qcT_ref[...] = (_group_norm_rows(seg(6), DC, DC) * gqc_ref[...]).astype(BF16)
    kcT = _group_norm_rows(seg(7), DC, DC) * gkc_ref[...]
    kc_ref[...] = kcT.T.astype(BF16)
    vcT_ref[...] = with_ones_rows(seg(8), KVH_C, DC)


def _in_proj(x2, lw, S):
    T = x2.shape[0]
    tm = TM_IN
    nt = T // tm
    spb = S // tm

    def full(a):
        return pl.BlockSpec(a.shape, lambda i: (0,) * a.ndim)

    consts = [lw['g_mix'], lw['w_inT'], lw['w_uqT'], lw['w_ukT'], lw['w_uvT'],
              lw['gqa'], lw['gka'], lw['gcq'], lw['gckv'], lw['gqb'], lw['gkb'], lw['gqc'], lw['gkc']]
    in_specs = ([pl.BlockSpec((tm, D_MODEL), lambda i: (i, 0))] + [full(a) for a in consts]
                + [pl.BlockSpec((ROPE_HALF, tm), lambda i: (0, i % spb))] * 2)

    def fm(rows):
        return jax.ShapeDtypeStruct((rows, T), BF16), pl.BlockSpec((rows, tm), lambda i: (0, i))

    def tk(cols):
        return jax.ShapeDtypeStruct((T, cols), BF16), pl.BlockSpec((tm, cols), lambda i: (i, 0))

    outs = [fm(HA * 2 * DA_QK), tk(HA * 2 * DA_QK), fm(HA * VX),
            fm(HB * DB_PAD), tk(HB * DB_PAD), fm(HB * VX),
            fm(HC * DC), tk(KVH_C * DC), fm(KVH_C * VX)]
    return pl.pallas_call(
        _in_kernel,
        grid=(nt,),
        in_specs=in_specs,
        out_specs=[o[1] for o in outs],
        out_shape=[o[0] for o in outs],
        compiler_params=pltpu.CompilerParams(dimension_semantics=("arbitrary",), vmem_limit_bytes=VMEM_LIMIT),
        name="in_proj",
    )(x2, *consts, lw['cosT'], lw['sinT'])


def _attn_a_kernel(sc_ref, qT_ref, k_ref, vT_ref, near_ref, subln_ref, o_ref, w_scr, m_scr, l_scr, acc_scr, *, nk):
    qb = pl.program_id(1)
    n_comb = 2 * HA
    bq = qT_ref.shape[1]
    near_lo = (BQ_A // BK_A) * qb - 1

    for hc in range(n_comb):
        g = hc // 4
        qg = qT_ref[g * 128:(g + 1) * 128, :]
        row = lax.broadcasted_iota(jnp.int32, (128, bq), 0)
        keep = (row // DA_QK) == (hc % 4)
        w_scr[hc] = jnp.where(keep, qg, jnp.zeros_like(qg))
    m_scr[...] = jnp.full(m_scr.shape, NEG_BIG, F32)
    l_scr[...] = jnp.zeros(l_scr.shape, F32)
    acc_scr[...] = jnp.zeros(acc_scr.shape, F32)

    def step(kb, mode):
        ks = pl.multiple_of(kb * BK_A, BK_A)

        def qk(hc):
            g = hc // 4
            return _dot(k_ref[pl.ds(ks, BK_A), g * 128:(g + 1) * 128], w_scr[hc])

        ahead = [qk(i) for i in range(PREFETCH)]
        for hc in range(n_comb):
            h = hc // 2
            s = ahead.pop(0)
            if hc + PREFETCH < n_comb:
                ahead.append(qk(hc + PREFETCH))
            c = sc_ref[(1 - mode) * HA + h]
            cmax = jnp.max(s, axis=0, keepdims=True) + c
            m_old = m_scr[hc]
            m_new = jnp.maximum(m_old, cmax)
            alpha = jnp.exp2(m_old - m_new)
            p = jnp.exp2(s - (m_new - c))
            if mode == 0:
                p = p * near_ref[kb - near_lo, h]
            l_scr[hc] = alpha * l_scr[hc] + jnp.sum(p, axis=0, keepdims=True)
            pv = _dot(vT_ref[h * VX:h * VX + DA_V, pl.ds(ks, BK_A)], p.astype(BF16))
            acc_scr[hc] = alpha * acc_scr[hc] + pv
            m_scr[hc] = m_new

    lo = jnp.maximum(near_lo, 0)
    hi = jnp.minimum(near_lo + NEAR_A, nk)
    lax.fori_loop(0, lo, lambda kb, c: (step(kb, -1), c)[1], 0)
    lax.fori_loop(lo, hi, lambda kb, c: (step(kb, 0), c)[1], 0)
    lax.fori_loop(hi, nk, lambda kb, c: (step(kb, 1), c)[1], 0)

    lam = sc_ref[3 * HA]
    outs = []
    for h in range(HA):
        a = acc_scr[2 * h] / l_scr[2 * h] - lam * (acc_scr[2 * h + 1] / l_scr[2 * h + 1])
        a = a * lax.rsqrt(jnp.mean(a * a, axis=0, keepdims=True) + EPS)
        outs.append(a)
    o = jnp.concatenate(outs, axis=0) * subln_ref[...]
    o_ref[...] = o.T.astype(BF16)


def _sublane_partial_sum(p):
    n, w = p.shape
    return jnp.sum(p.reshape(n // 8, 8, w), axis=0)


def _attn_a_bounded_kernel(sc_ref, qT_ref, k_ref, vT_ref, near_ref, subln_ref, o_ref, w_scr, acc_scr, l_scr, *, nk):
    qb = pl.program_id(1)
    n_comb = 2 * HA
    bq = qT_ref.shape[1]
    near_lo = (BQ_A // BK_A) * qb - 1
    for hc in range(n_comb):
        g = hc // 4
        qg = qT_ref[g * 128:(g + 1) * 128, :]
        row = lax.broadcasted_iota(jnp.int32, (128, bq), 0)
        w_scr[hc] = jnp.where((row // DA_QK) == (hc % 4), qg, jnp.zeros_like(qg))
    acc_scr[...] = jnp.zeros(acc_scr.shape, F32)
    l_scr[...] = jnp.zeros(l_scr.shape, F32)

    lo = jnp.maximum(near_lo, 0)
    hi = jnp.minimum(near_lo + NEAR_A, nk)
    n_far = lo + (nk - hi)

    def step(j0, n_blocks, near):
        def block(jj):
            j = j0 + jj
            if near:
                return j, None
            right = j >= lo
            return jnp.where(right, j - lo + hi, j), jnp.where(right, HA, 0)

        tiles = [(jj, hc) for jj in range(n_blocks) for hc in range(n_comb)]

        def qk(t):
            jj, hc = tiles[t]
            g = hc // 4
            ks = pl.multiple_of(block(jj)[0] * BK_A, BK_A)
            return _dot(k_ref[pl.ds(ks, BK_A), g * 128:(g + 1) * 128], w_scr[hc])

        ahead = [qk(i) for i in range(PREFETCH)]
        for t, (jj, hc) in enumerate(tiles):
            h = hc // 2
            s = ahead.pop(0)
            if t + PREFETCH < len(tiles):
                ahead.append(qk(t + PREFETCH))
            kb, side = block(jj)
            ks = pl.multiple_of(kb * BK_A, BK_A)
            p = jnp.exp2(s)
            if near:
                p = p * near_ref[kb - near_lo, h]
            psum = _sublane_partial_sum(p)
            pv = _dot(vT_ref[h * VX:h * VX + DA_V, pl.ds(ks, BK_A)], p.astype(BF16))
            if near:
                acc_scr[hc] += pv
                l_scr[hc] += psum
            else:
                f = sc_ref[side + h]
                acc_scr[hc] += pv * f
                l_scr[hc] += psum * f

    n_pairs = n_far // UNROLL_A
    lax.fori_loop(0, n_pairs, lambda i, c: (step(i * UNROLL_A, UNROLL_A, False), c)[1], 0)
    lax.fori_loop(n_pairs * UNROLL_A, n_far, lambda j, c: (step(j, 1, False), c)[1], 0)
    lax.fori_loop(lo, hi, lambda kb, c: (step(kb, 1, True), c)[1], 0)

    lam = sc_ref[2 * HA]
    outs = []
    for h in range(HA):
        l0 = jnp.sum(l_scr[2 * h], axis=0, keepdims=True)
        l1 = jnp.sum(l_scr[2 * h + 1], axis=0, keepdims=True)
        a = acc_scr[2 * h] / l0 - lam * (acc_scr[2 * h + 1] / l1)
        outs.append(a * lax.rsqrt(jnp.mean(a * a, axis=0, keepdims=True) + EPS))
    o = jnp.concatenate(outs, axis=0) * subln_ref[...]
    o_ref[...] = o.T.astype(BF16)


def _attn_a(qT, k, vT, lw, B, S, bounded):
    nq = S // BQ_A
    nk = S // BK_A
    n_comb = 2 * HA
    if bounded:
        body = functools.partial(_attn_a_bounded_kernel, nk=nk)
        scratch = [pltpu.VMEM((n_comb, 128, BQ_A), BF16), pltpu.VMEM((n_comb, DA_V, BQ_A), F32),
                   pltpu.VMEM((n_comb, 8, BQ_A), F32)]
        sc, near = lw['sc_a_bounded'], lw['near_a_bounded']
    else:
        body = functools.partial(_attn_a_kernel, nk=nk)
        scratch = [pltpu.VMEM((n_comb, 128, BQ_A), BF16), pltpu.VMEM((n_comb, 1, BQ_A), F32),
                   pltpu.VMEM((n_comb, 1, BQ_A), F32), pltpu.VMEM((n_comb, DA_V, BQ_A), F32)]
        sc, near = lw['sc_a'], lw['near_a']
    return pl.pallas_call(
        body,
        grid=(B, nq),
        in_specs=[
            pl.BlockSpec(memory_space=pltpu.SMEM),
            pl.BlockSpec((n_comb * DA_QK, BQ_A), lambda b, q: (0, b * nq + q)),
            pl.BlockSpec((S, n_comb * DA_QK), lambda b, q: (b, 0)),
            pl.BlockSpec((HA * VX, S), lambda b, q: (0, b)),
            pl.BlockSpec(near.shape, lambda b, q: (0, 0, 0, 0)),
            pl.BlockSpec((HA * DA_V, 1), lambda b, q: (0, 0)),
        ],
        out_specs=pl.BlockSpec((BQ_A, HA * DA_V), lambda b, q: (b * nq + q, 0)),
        out_shape=jax.ShapeDtypeStruct((B * S, HA * DA_V), BF16),
        scratch_shapes=scratch,
        compiler_params=pltpu.CompilerParams(dimension_semantics=("arbitrary", "arbitrary"),
                                             vmem_limit_bytes=VMEM_LIMIT),
        name="attn_a_bounded" if bounded else "attn_a",
    )(sc, qT, k, vT, near, lw['subln'])


def _attn_b_kernel(qT_ref, k_ref, vT_ref, o_ref, m_scr, l_scr, acc_scr, *, nk):
    m_scr[...] = jnp.full(m_scr.shape, NEG_BIG, F32)
    l_scr[...] = jnp.zeros(l_scr.shape, F32)
    acc_scr[...] = jnp.zeros(acc_scr.shape, F32)

    def step(kb, carry):
        ks = pl.multiple_of(kb * BK_B, BK_B)

        def qk(h):
            return _dot(k_ref[pl.ds(ks, BK_B), h * DB_PAD:(h + 1) * DB_PAD], qT_ref[h * DB_PAD:(h + 1) * DB_PAD, :])

        ahead = [qk(i) for i in range(PREFETCH)]
        for h in range(HB):
            s = ahead.pop(0)
            if h + PREFETCH < HB:
                ahead.append(qk(h + PREFETCH))
            m_old = m_scr[h]
            m_new = jnp.maximum(m_old, jnp.max(s, axis=0, keepdims=True))
            alpha = jnp.exp2(m_old - m_new)
            p = jnp.exp2(s - m_new)
            l_scr[h] = alpha * l_scr[h] + jnp.sum(p, axis=0, keepdims=True)
            pv = _dot(vT_ref[h * VX:h * VX + DB_V, pl.ds(ks, BK_B)], p.astype(BF16))
            acc_scr[h] = alpha * acc_scr[h] + pv
            m_scr[h] = m_new
        return carry

    lax.fori_loop(0, nk, step, 0)
    o = jnp.concatenate([acc_scr[h] / l_scr[h] for h in range(HB)], axis=0)
    o_ref[...] = o.T.astype(BF16)


def _attn_b_bounded_kernel(qT_ref, k_ref, vT_ref, o_ref, acc_scr, l_scr, *, nk):
    acc_scr[...] = jnp.zeros(acc_scr.shape, F32)
    l_scr[...] = jnp.zeros(l_scr.shape, F32)

    assert nk % UNROLL_B == 0
    tiles = [(j, h) for j in range(UNROLL_B) for h in range(HB)]

    def step(it, carry):
        def ks(j):
            return pl.multiple_of((it * UNROLL_B + j) * BK_B, BK_B)

        def qk(t):
            j, h = tiles[t]
            return _dot(k_ref[pl.ds(ks(j), BK_B), h * DB_PAD:(h + 1) * DB_PAD],
                        qT_ref[h * DB_PAD:(h + 1) * DB_PAD, :])

        ahead = [qk(i) for i in range(PREFETCH)]
        for t, (j, h) in enumerate(tiles):
            s = ahead.pop(0)
            if t + PREFETCH < len(tiles):
                ahead.append(qk(t + PREFETCH))
            p = jnp.exp2(s)
            l_scr[h] += _sublane_partial_sum(p)
            acc_scr[h] += _dot(vT_ref[h * VX:h * VX + DB_V, pl.ds(ks(j), BK_B)], p.astype(BF16))
        return carry

    lax.fori_loop(0, nk // UNROLL_B, step, 0)
    o = jnp.concatenate([acc_scr[h] / jnp.sum(l_scr[h], axis=0, keepdims=True) for h in range(HB)], axis=0)
    o_ref[...] = o.T.astype(BF16)


def _attn_b(qT, k, vT, B, S, bounded):
    nq = S // BQ_B
    nk = S // BK_B
    if bounded:
        body = functools.partial(_attn_b_bounded_kernel, nk=nk)
        scratch = [pltpu.VMEM((HB, DB_V, BQ_B), F32), pltpu.VMEM((HB, 8, BQ_B), F32)]
    else:
        body = functools.partial(_attn_b_kernel, nk=nk)
        scratch = [pltpu.VMEM((HB, 1, BQ_B), F32), pltpu.VMEM((HB, 1, BQ_B), F32), pltpu.VMEM((HB, DB_V, BQ_B), F32)]
    return pl.pallas_call(
        body,
        grid=(B, nq),
        in_specs=[
            pl.BlockSpec((HB * DB_PAD, BQ_B), lambda b, q: (0, b * nq + q)),
            pl.BlockSpec((S, HB * DB_PAD), lambda b, q: (b, 0)),
            pl.BlockSpec((HB * VX, S), lambda b, q: (0, b)),
        ],
        out_specs=pl.BlockSpec((BQ_B, HB * DB_V), lambda b, q: (b * nq + q, 0)),
        out_shape=jax.ShapeDtypeStruct((B * S, HB * DB_V), BF16),
        scratch_shapes=scratch,
        compiler_params=pltpu.CompilerParams(dimension_semantics=("arbitrary", "arbitrary"),
                                             vmem_limit_bytes=VMEM_LIMIT),
        name="attn_b_bounded" if bounded else "attn_b",
    )(qT, k, vT)


def _attn_c_kernel(sc_ref, qT_ref, k_ref, vT_ref, bm_ref, o_ref, *, nb, S):
    n = pl.program_id(1)
    start = pl.multiple_of(jnp.clip((n - 1) * BLOCK, 0, S - KW_C), BLOCK)
    case = jnp.where(n == 0, 0, jnp.where(n == nb - 1, 2, 1))
    kwin = k_ref[pl.ds(start, KW_C), :]
    outs = []
    for h in range(HC):
        j = h // G_C
        qh = qT_ref[h * DC:(h + 1) * DC, :]
        zz = jnp.zeros_like(qh)
        w = jnp.concatenate([qh, zz] if j == 0 else [zz, qh], axis=0)
        s = _dot(kwin, w) + bm_ref[case, h]
        sink = sc_ref[h]
        m = jnp.maximum(jnp.max(s, axis=0, keepdims=True), sink)
        p = jnp.exp2(s - m)
        den = jnp.sum(p, axis=0, keepdims=True) + jnp.exp2(sink - m)
        pv = _dot(vT_ref[j * VX:j * VX + DC, pl.ds(start, KW_C)], p.astype(BF16))
        outs.append(pv / den)
    o_ref[...] = jnp.concatenate(outs, axis=0).T.astype(BF16)


def _attn_c_bounded_kernel(sc_ref, qT_ref, k_ref, vT_ref, e_ref, o_ref, *, n_sub, S):
    g = pl.program_id(1)
    tiles = [(u, h) for u in range(SUB_C) for h in range(HC)]

    def window(u):
        n = g * SUB_C + u
        start = pl.multiple_of(jnp.clip(n * QB_C - WINDOW, 0, S - KWB_C), WINDOW)
        case = jnp.where(n == 0, 0, jnp.where(n == n_sub - 1, 2, 1))
        return start, case

    def qk(t):
        u, h = tiles[t]
        start, _ = window(u)
        qh = qT_ref[h * DC:(h + 1) * DC, u * QB_C:(u + 1) * QB_C]
        zz = jnp.zeros_like(qh)
        w = jnp.concatenate([qh, zz] if h // G_C == 0 else [zz, qh], axis=0)
        return _dot(k_ref[pl.ds(start, KWB_C), :], w)

    ahead = [qk(i) for i in range(PREFETCH)]
    outs = []
    for t, (u, h) in enumerate(tiles):
        s = ahead.pop(0)
        if t + PREFETCH < len(tiles):
            ahead.append(qk(t + PREFETCH))
        start, case = window(u)
        j = h // G_C
        p = (jnp.exp2(s) * e_ref[case, h]).astype(BF16)
        pv = _dot(vT_ref[j * VX:(j + 1) * VX, pl.ds(start, KWB_C)], p)
        outs.append(pv[0:DC] / (pv[DC:DC + 1] + sc_ref[HC + h]))
        if h == HC - 1:
            o_ref[u * QB_C:(u + 1) * QB_C, :] = jnp.concatenate(outs, axis=0).T.astype(BF16)
            outs = []


def _attn_c(qT, k, vT, lw, B, S, bounded):
    if bounded:
        n_sub = S // QB_C
        assert n_sub >= 3 and n_sub % SUB_C == 0
        nb, bq = n_sub // SUB_C, QB_C * SUB_C
        body = functools.partial(_attn_c_bounded_kernel, n_sub=n_sub, S=S)
        tile, sc = lw['e_c'], lw['sc_c_bounded']
    else:
        nb, bq = S // BLOCK, BLOCK
        assert nb >= 3
        body = functools.partial(_attn_c_kernel, nb=nb, S=S)
        tile, sc = lw['bm_c'], lw['sc_c']
    return pl.pallas_call(
        body,
        grid=(B, nb),
        in_specs=[
            pl.BlockSpec(memory_space=pltpu.SMEM),
            pl.BlockSpec((HC * DC, bq), lambda b, n: (0, b * nb + n)),
            pl.BlockSpec((S, KVH_C * DC), lambda b, n: (b, 0)),
            pl.BlockSpec((KVH_C * VX, S), lambda b, n: (0, b)),
            pl.BlockSpec(tile.shape, lambda b, n: (0, 0, 0, 0)),
        ],
        out_specs=pl.BlockSpec((bq, HC * DC), lambda b, n: (b * nb + n, 0)),
        out_shape=jax.ShapeDtypeStruct((B * S, HC * DC), BF16),
        compiler_params=pltpu.CompilerParams(dimension_semantics=("arbitrary", "arbitrary"),
                                             vmem_limit_bytes=VMEM_LIMIT),
        name="attn_c_bounded" if bounded else "attn_c",
    )(sc, qT, k, vT, tile)


def _out_kernel(x_ref, ya_ref, yb_ref, yc_ref, wo_ref, g_ref, wr_ref, xn_ref, h2_ref, affT_ref):
    na, nb_ = HA * DA_V, HA * DA_V + HB * DB_V
    xn = (x_ref[...] + _dot(ya_ref[...], wo_ref[0:na, :]) + _dot(yb_ref[...], wo_ref[na:nb_, :])
          + _dot(yc_ref[...], wo_ref[nb_:, :]))
    xn_ref[...] = xn
    h2 = (xn * lax.rsqrt(jnp.mean(xn * xn, axis=-1, keepdims=True) + EPS) * g_ref[...]).astype(BF16)
    h2_ref[...] = h2
    lg = _dot(h2, wr_ref[...]).T[0:N_EXPERTS]
    e = jnp.exp(lg - jnp.max(lg, axis=0, keepdims=True))
    affT_ref[...] = e / jnp.sum(e, axis=0, keepdims=True)


def _out_proj(x2, ya, yb, yc, lw):
    T = x2.shape[0]
    tm = TM_OUT

    def full(a):
        return pl.BlockSpec(a.shape, lambda i: (0,) * a.ndim)

    return pl.pallas_call(
        _out_kernel,
        grid=(T // tm,),
        in_specs=[pl.BlockSpec((tm, D_MODEL), lambda i: (i, 0)),
                  pl.BlockSpec((tm, ya.shape[1]), lambda i: (i, 0)),
                  pl.BlockSpec((tm, yb.shape[1]), lambda i: (i, 0)),
                  pl.BlockSpec((tm, yc.shape[1]), lambda i: (i, 0)),
                  full(lw['w_out']), full(lw['g_ffn']), full(lw['w_r'])],
        out_specs=[pl.BlockSpec((tm, D_MODEL), lambda i: (i, 0)),
                   pl.BlockSpec((tm, D_MODEL), lambda i: (i, 0)),
                   pl.BlockSpec((N_EXPERTS, tm), lambda i: (0, i))],
        out_shape=[jax.ShapeDtypeStruct((T, D_MODEL), F32),
                   jax.ShapeDtypeStruct((T, D_MODEL), BF16),
                   jax.ShapeDtypeStruct((N_EXPERTS, T), F32)],
        compiler_params=pltpu.CompilerParams(dimension_semantics=("arbitrary",), vmem_limit_bytes=VMEM_LIMIT),
        name="out_proj",
    )(x2, ya, yb, yc, lw['w_out'], lw['g_ffn'], lw['w_r'])


def _topk_kernel(aff_ref, thr_ref, tcut_ref, *, cap, T):
    capf = float(cap)

    def bits():
        return lax.bitcast_convert_type(aff_ref[...], jnp.int32)

    def count(mask):
        return jnp.sum(jnp.where(mask, 1.0, 0.0), axis=1, keepdims=True)

    def vbody(i, v):
        cand = v | jnp.left_shift(jnp.int32(1), 30 - i)
        return jnp.where(count(bits() >= cand) >= capf, cand, v)

    thr = lax.fori_loop(0, 31, vbody, jnp.zeros((N_EXPERTS, 1), jnp.int32))
    need = capf - count(bits() > thr)
    nbits = max(1, int(math.ceil(math.log2(T))))

    def tbody(i, c):
        cand = c | jnp.left_shift(jnp.int32(1), nbits - 1 - i)
        idx = lax.broadcasted_iota(jnp.int32, (N_EXPERTS, T), 1)
        f = count((bits() == thr) & (idx < cand))
        return jnp.where(f < need, cand, c)

    tcut = lax.fori_loop(0, nbits, tbody, jnp.zeros((N_EXPERTS, 1), jnp.int32))
    thr_ref[...] = jnp.broadcast_to(thr, thr_ref.shape)
    tcut_ref[...] = jnp.broadcast_to(tcut, tcut_ref.shape)


def _topk_thresholds(affT):
    E, T = affT.shape
    cap = CAPACITY_FACTOR * T // N_EXPERTS
    return pl.pallas_call(
        functools.partial(_topk_kernel, cap=cap, T=T),
        grid=(1,),
        in_specs=[pl.BlockSpec((E, T), lambda i: (0, 0))],
        out_specs=[pl.BlockSpec((E, 128), lambda i: (0, 0))] * 2,
        out_shape=[jax.ShapeDtypeStruct((E, 128), jnp.int32)] * 2,
        compiler_params=pltpu.CompilerParams(dimension_semantics=("arbitrary",), vmem_limit_bytes=VMEM_LIMIT),
        name="topk_thr",
    )(affT)


def _moe_kernel(x_ref, h2_ref, affT_ref, thr_ref, tcut_ref, tri_ref, wg_ref, wu_ref, wd_ref, o_ref, pos_scr, xe_scr):
    i = pl.program_id(0)
    e = pl.program_id(1)
    tt = TT_MOE

    def one_hot_rows(posm, c):
        rel = posm - (c * CH_MOE).astype(F32)
        jj = lax.broadcasted_iota(jnp.int32, (CH_MOE, tt), 0).astype(F32)
        return rel, jnp.broadcast_to(rel, (CH_MOE, tt)) == jj

    @pl.when(e == 0)
    def _():
        b = lax.bitcast_convert_type(affT_ref[...], jnp.int32)
        thr = thr_ref[:, 0:1]
        tc = tcut_ref[:, 0:1]
        tg = i * (tt * SUBS_MOE) + lax.broadcasted_iota(jnp.int32, b.shape, 1)
        sel = (b > thr) | ((b == thr) & (tg <= tc))
        for sub in range(SUBS_MOE):
            sl = slice(sub * tt, (sub + 1) * tt)
            pos = _dot(jnp.where(sel[:, sl], 1.0, 0.0).astype(BF16), tri_ref[...])
            posm_all = jnp.where(sel[:, sl], pos, -1.0)
            pos_scr[:, sl] = posm_all
            for grp in range(N_EXPERTS // GATHER_GROUP):
                hits = [one_hot_rows(posm_all[ee:ee + 1], jnp.int32(0))[1]
                        for ee in range(grp * GATHER_GROUP, (grp + 1) * GATHER_GROUP)]
                onehot = jnp.where(jnp.concatenate(hits, axis=0), 1.0, 0.0).astype(BF16)
                rows = slice(grp * GATHER_GROUP * CH_MOE, (grp + 1) * GATHER_GROUP * CH_MOE)
                xe_scr[sub, rows, :] = _dot(onehot, h2_ref[sl, :]).astype(BF16)
        o_ref[...] = x_ref[...]

    posm = [pos_scr[pl.ds(e, 1), sub * tt:(sub + 1) * tt] for sub in range(SUBS_MOE)]
    gate = [affT_ref[pl.ds(e, 1), sub * tt:(sub + 1) * tt] for sub in range(SUBS_MOE)]
    n_sel = jnp.max(jnp.concatenate(posm, axis=0)) + 1.0
    n_ch = (n_sel.astype(jnp.int32) + CH_MOE - 1) // CH_MOE

    def chunk(c, pregathered):
        rels, hits, xes = [], [], []
        for sub in range(SUBS_MOE):
            rel, hit = one_hot_rows(posm[sub], c)
            if pregathered:
                xes.append(xe_scr[sub, pl.ds(pl.multiple_of(e * CH_MOE, CH_MOE), CH_MOE), :])
            else:
                xes.append(_dot(jnp.where(hit, 1.0, 0.0).astype(BF16),
                                h2_ref[sub * tt:(sub + 1) * tt, :]).astype(BF16))
            rels.append(rel)
            hits.append(hit)
        xe = jnp.concatenate(xes, axis=0)
        g = _dot(xe, wg_ref[...])
        u = _dot(xe, wu_ref[...])
        hid = (g * jax.nn.sigmoid(g) * u).astype(BF16)
        ye = _dot(hid, wd_ref[...])
        for sub in range(SUBS_MOE):
            gc = jnp.sum(jnp.where(hits[sub], jnp.broadcast_to(gate[sub], hits[sub].shape), 0.0),
                         axis=1, keepdims=True)
            yeb = jnp.concatenate([(ye[sub * CH_MOE:(sub + 1) * CH_MOE] * gc).astype(BF16),
                                   jnp.zeros((CHP_MOE - CH_MOE, ye.shape[1]), BF16)], axis=0)
            relT = jnp.broadcast_to(rels[sub], (CHP_MOE, tt)).T
            ll = lax.broadcasted_iota(jnp.int32, (tt, CHP_MOE), 1).astype(F32)
            hitT = jnp.where(relT == ll, 1.0, 0.0).astype(BF16)
            o_ref[sub * tt:(sub + 1) * tt, :] += _dot(hitT, yeb)

    chunk(jnp.int32(0), True)
    lax.fori_loop(1, n_ch, lambda c, carry: (chunk(c, False), carry)[1], 0)


def _moe(xn, h2, affT, thr, tcut, lw):
    T = xn.shape[0]
    tt = TT_MOE * SUBS_MOE
    once = pl.Buffered(1)
    return pl.pallas_call(
        _moe_kernel,
        grid=(T // tt, N_EXPERTS),
        in_specs=[
            pl.BlockSpec((tt, D_MODEL), lambda i, e: (i, 0), pipeline_mode=once),
            pl.BlockSpec((tt, D_MODEL), lambda i, e: (i, 0), pipeline_mode=once),
            pl.BlockSpec((N_EXPERTS, tt), lambda i, e: (0, i)),
            pl.BlockSpec((N_EXPERTS, 128), lambda i, e: (0, 0)),
            pl.BlockSpec((N_EXPERTS, 128), lambda i, e: (0, 0)),
            pl.BlockSpec((TT_MOE, TT_MOE), lambda i, e: (0, 0), pipeline_mode=once),
            pl.BlockSpec((None, D_MODEL, D_EXPERT), lambda i, e: (e, 0, 0)),
            pl.BlockSpec((None, D_MODEL, D_EXPERT), lambda i, e: (e, 0, 0)),
            pl.BlockSpec((None, D_EXPERT, D_MODEL), lambda i, e: (e, 0, 0)),
        ],
        out_specs=pl.BlockSpec((tt, D_MODEL), lambda i, e: (i, 0), pipeline_mode=once),
        out_shape=jax.ShapeDtypeStruct((T, D_MODEL), F32),
        scratch_shapes=[pltpu.VMEM((N_EXPERTS, tt), F32),
                        pltpu.VMEM((SUBS_MOE, N_EXPERTS * CH_MOE, D_MODEL), BF16)],
        compiler_params=pltpu.CompilerParams(dimension_semantics=("arbitrary", "arbitrary"),
                                             vmem_limit_bytes=VMEM_LIMIT),
        name="moe_ffn",
    )(xn, h2, affT, thr, tcut, lw['tri'], lw['w_gate'], lw['w_up'], lw['w_down'])


def _t5_bucket(rel):
    nb = NUM_BUCKETS // 2
    max_exact = nb // 2
    ret = jnp.where(rel > 0, nb, 0)
    n = jnp.abs(rel)
    nf = jnp.maximum(n, 1).astype(jnp.float32)
    large = max_exact + (jnp.log(nf / max_exact) / math.log(MAX_DISTANCE / max_exact) * (nb - max_exact)).astype(jnp.int32)
    large = jnp.minimum(large, nb - 1)
    return ret + jnp.where(n < max_exact, n, large)


def _col(v):
    return v.astype(F32).reshape(-1, 1)


def _pad_heads(w, n_heads, d, d_pad):
    k = w.shape[1]
    w3 = w.reshape(n_heads, d, k)
    return jnp.pad(w3, ((0, 0), (0, d_pad - d), (0, 0))).reshape(n_heads * d_pad, k)


def _toeplitz_tiles(fn, offsets, nk, nq):
    span = nk + nq
    out = []
    for off in offsets:
        u = fn(off + nk - 1 - jnp.arange(span, dtype=jnp.int32))
        rows = jnp.tile(u, (1, nk))[:, :nk * (span - 1)].reshape(u.shape[0], nk, span - 1)
        out.append(rows[:, :, nk - 1:nk - 1 + nq])
    return jnp.stack(out, axis=0)


def _prep_shared(rel_bias):
    rb = rel_bias.astype(F32) * LOG2E

    def bias_a(rel):
        return rb[:, :HA][_t5_bucket(rel)].T

    def bias_c(rel):
        return rb[:, HA:][_t5_bucket(rel)].T

    def in_band(rel):
        return (jnp.abs(rel) <= WINDOW)[None, :]

    bmax = jnp.max(rb[:, :HA], axis=0)
    near_offs = [d * BK_A for d in range(-1, NEAR_A - 1)]
    near_a = _toeplitz_tiles(lambda rel: jnp.exp2(bias_a(rel) - bmax[:, None]), near_offs, BK_A, BQ_A)
    far = _t5_bucket(jnp.array([-(1 << 20), 1 << 20], dtype=jnp.int32))
    sc_far = jnp.concatenate([rb[far[1], :HA], bmax, rb[far[0], :HA]])
    a_bounded = {
        'near': _toeplitz_tiles(lambda rel: jnp.exp2(bias_a(rel)), near_offs, BK_A, BQ_A),
        'sc': jnp.concatenate([jnp.exp2(rb[far[0], :HA]), jnp.exp2(rb[far[1], :HA])]),
        'bias_mag': jnp.max(jnp.abs(rb[:, :HA])),
    }
    bm_c = _toeplitz_tiles(lambda rel: jnp.where(in_band(rel), bias_c(rel), NEG_BIG),
                           [0, -BLOCK, -2 * BLOCK], KW_C, BLOCK)
    c_bounded = {
        'e': _toeplitz_tiles(lambda rel: jnp.where(in_band(rel), jnp.exp2(bias_c(rel)), 0.0),
                             [0, -WINDOW, -(KWB_C - QB_C)], KWB_C, QB_C),
        'bias_mag': jnp.max(jnp.abs(rb[:, HA:])),
    }
    return near_a, sc_far, bm_c, a_bounded, c_bounded


def _rope_tables(S):
    inv = 1.0 / (ROPE_BASE ** (jnp.arange(ROPE_HALF, dtype=jnp.float32) / ROPE_HALF))
    ang = jnp.arange(S, dtype=jnp.int32).astype(jnp.float32)[:, None] * inv[None, :]
    return jnp.cos(ang).T, jnp.sin(ang).T


def _logit_bounds(l, p, a_bounded, c_bounded, sa, sb, scc):
    qa, ka = p['qn_a'][l].astype(F32), p['kn_a'][l].astype(F32)
    bound_a = 1.01 * DA_QK * jnp.max(jnp.abs(qa * ka)) * sa + 2.0 * a_bounded['bias_mag']
    qb, kb = p['qn_b'][l].astype(F32), p['kn_b'][l].astype(F32)
    bound_b = 1.01 * DB_QK * jnp.max(jnp.abs(qb)) * jnp.max(jnp.abs(kb)) * sb
    qc, kc = p['qn_c'][l].astype(F32), p['kn_c'][l].astype(F32)
    bound_c = (1.01 * DC * jnp.max(jnp.abs(qc * kc)) * scc + 2.0 * c_bounded['bias_mag']
               + jnp.max(jnp.abs(p['sink_c'][l].astype(F32))) * LOG2E)
    return bound_a <= BOUND_LIMIT, bound_b <= BOUND_LIMIT, bound_c <= BOUND_LIMIT


def _prep_layer(l, p, shared):
    near_a, sc_far, bm_c, a_bounded, c_bounded = shared
    lam_init = 0.8 - 0.6 * math.exp(-0.3 * l)
    lam = (jnp.exp(jnp.sum(p['lam_q1'][l].astype(F32) * p['lam_k1'][l].astype(F32)))
           - jnp.exp(jnp.sum(p['lam_q2'][l].astype(F32) * p['lam_k2'][l].astype(F32))) + lam_init)
    w_uq = p['w_uq'][l]
    w_ukv = p['w_ukv'][l].reshape(KV_RANK, HB, DB_NOPE + DB_V)
    sa = DA_QK ** -0.5 * LOG2E
    sb = DB_QK ** -0.5 * LOG2E
    scc = DC ** -0.5 * LOG2E
    zpad = jnp.zeros((DB_PAD - DB_QK,), F32)
    lw = {
        'g_mix': p['norm_mix_g'][l].astype(F32).reshape(1, D_MODEL),
        'w_inT': p['w_in'][l].T.astype(BF16),
        'w_uqT': _pad_heads(w_uq.T, HB, DB_QK, DB_PAD).astype(BF16),
        'w_ukT': w_ukv[:, :, :DB_NOPE].reshape(KV_RANK, HB * DB_NOPE).T.astype(BF16),
        'w_uvT': w_ukv[:, :, DB_NOPE:].reshape(KV_RANK, HB * DB_V).T.astype(BF16),
        'gqa': _col(jnp.tile(p['qn_a'][l], 2 * HA) * sa),
        'gka': _col(jnp.tile(p['kn_a'][l], 2 * HA)),
        'gcq': _col(p['g_cq'][l]),
        'gckv': _col(p['g_ckv'][l]),
        'gqb': _col(jnp.tile(jnp.concatenate([p['qn_b'][l].astype(F32) * sb, zpad]), HB)),
        'gkb': _col(jnp.tile(jnp.concatenate([p['kn_b'][l].astype(F32), zpad]), HB)),
        'gqc': _col(jnp.tile(p['qn_c'][l], HC) * scc),
        'gkc': _col(jnp.tile(p['kn_c'][l], KVH_C)),
        'near_a': near_a,
        'sc_a': jnp.concatenate([sc_far, lam.reshape(1)]).astype(F32),
        'subln': _col(jnp.tile(p['subln_a'][l], HA) * (1.0 - lam_init)),
        'bm_c': bm_c,
        'sc_c': p['sink_c'][l].astype(F32) * LOG2E,
        'w_out': p['w_out'][l].astype(BF16),
        'g_ffn': p['norm_ffn_g'][l].astype(F32).reshape(1, D_MODEL),
        'w_r': jnp.pad(p['w_router'][l], ((0, 0), (0, 128 - N_EXPERTS))).astype(BF16),
        'w_gate': p['w_gate'][l].astype(BF16),
        'w_up': p['w_up'][l].astype(BF16),
        'w_down': p['w_down'][l].astype(BF16),
        'tri': jnp.triu(jnp.ones((TT_MOE, TT_MOE), F32), k=1).astype(BF16),
        'near_a_bounded': a_bounded['near'],
        'sc_a_bounded': jnp.concatenate([a_bounded['sc'], lam.reshape(1)]).astype(F32),
    }
    lw['e_c'] = c_bounded['e']
    lw['sc_c_bounded'] = jnp.concatenate([lw['sc_c'], jnp.exp2(lw['sc_c'])])
    lw['bounded_a'], lw['bounded_b'], lw['bounded_c'] = _logit_bounds(l, p, a_bounded, c_bounded, sa, sb, scc)
    return lw


def _trunk(x, layers):
    B, S, D = x.shape
    x2 = x.reshape(B * S, D)
    cosT, sinT = _rope_tables(S)
    for lw in layers:
        lw = dict(lw, cosT=cosT, sinT=sinT)
        qaT, ka, vaT, qbT, kb, vbT, qcT, kc, vcT = _in_proj(x2, lw, S)
        ya = lax.cond(lw['bounded_a'],
                      lambda q, k, v: _attn_a(q, k, v, lw, B, S, True),
                      lambda q, k, v: _attn_a(q, k, v, lw, B, S, False), qaT, ka, vaT)
        yb = lax.cond(lw['bounded_b'],
                      lambda q, k, v: _attn_b(q, k, v, B, S, True),
                      lambda q, k, v: _attn_b(q, k, v, B, S, False), qbT, kb, vbT)
        yc = lax.cond(lw['bounded_c'],
                      lambda q, k, v: _attn_c(q, k, v, lw, B, S, True),
                      lambda q, k, v: _attn_c(q, k, v, lw, B, S, False), qcT, kc, vcT)
        xn, h2, affT = _out_proj(x2, ya, yb, yc, lw)
        thr, tcut = _topk_thresholds(affT)
        x2 = _moe(xn, h2, affT, thr, tcut, lw)
    return x2.reshape(B, S, D)


def kernel(x_prompt, x_sample, rel_bias, norm_mix_g, w_in, qn_a, kn_a, lam_q1, lam_k1, lam_q2, lam_k2, subln_a, g_cq, w_uq, g_ckv, w_ukv, qn_b, kn_b, qn_c, kn_c, sink_c, w_out, norm_ffn_g, w_router, w_gate, w_up, w_down):
    p = dict(norm_mix_g=norm_mix_g, w_in=w_in, qn_a=qn_a, kn_a=kn_a, lam_q1=lam_q1, lam_k1=lam_k1,
             lam_q2=lam_q2, lam_k2=lam_k2, subln_a=subln_a, g_cq=g_cq, w_uq=w_uq, g_ckv=g_ckv, w_ukv=w_ukv,
             qn_b=qn_b, kn_b=kn_b, qn_c=qn_c, kn_c=kn_c, sink_c=sink_c, w_out=w_out, norm_ffn_g=norm_ffn_g,
             w_router=w_router, w_gate=w_gate, w_up=w_up, w_down=w_down)
    shared = _prep_shared(rel_bias)
    layers = [_prep_layer(l, p, shared) for l in range(w_in.shape[0])]
    return (_trunk(x_prompt, layers), _trunk(x_sample, layers))
```

```python
import functools
import math

import jax
import jax.numpy as jnp
import numpy as np
from jax import lax
from jax.experimental import pallas as pl
from jax.experimental.pallas import tpu as pltpu

D_MODEL = 1024
BLOCK = 128
HA, DA_QK, DA_V = 4, 32, 64
HB, Q_RANK, KV_RANK, DB_NOPE, DB_ROPE, DB_V = 6, 384, 256, 64, 32, 64
ROPE_BASE = 10000.0
HC, KVH_C, DC, WINDOW = 6, 2, 64, 128
G_C = HC // KVH_C
NUM_BUCKETS, MAX_DISTANCE = 32, 128
N_EXPERTS, D_EXPERT, CAPACITY_FACTOR = 16, 1024, 2
EPS = 1e-6

DB_QK = DB_NOPE + DB_ROPE
DB_PAD = 128
VX = 80
BOUND_LIMIT = 80.0
ROPE_HALF = DB_ROPE // 2
IN_SPLITS = (HA * 2 * DA_QK, HA * 2 * DA_QK, HA * DA_V, Q_RANK, KV_RANK, DB_ROPE, HC * DC, KVH_C * DC, KVH_C * DC)
IN_OFFS = tuple(int(v) for v in np.cumsum((0,) + IN_SPLITS))
IN_COLS = IN_OFFS[-1]
LOG2E = 1.4426950408889634
NEG_BIG = -1e30

F32 = jnp.float32
BF16 = jnp.bfloat16

TM_IN = 512
TM_OUT = 1024
BQ_A = 512
BK_A = 512
UNROLL_A = 4
NEAR_A = BQ_A // BK_A + 2
assert BQ_A % BK_A == 0 and BK_A >= MAX_DISTANCE
BQ_B = 512
BK_B = 512
UNROLL_B = 4
KW_C = 3 * BLOCK
QB_C = 256
KWB_C = QB_C + 2 * WINDOW
SUB_C = 2
TT_MOE = 1024
SUBS_MOE = 2
GATHER_GROUP = 4
CH_MOE = 160
CHP_MOE = 256
PREFETCH = 3
VMEM_LIMIT = 56 * 1024 * 1024


def _dot(a, b):
    return jnp.dot(a, b, preferred_element_type=F32)


def _group_norm_rows(y, gs, n_valid):
    rows, tm = y.shape
    y3 = y.reshape(rows // gs, gs, tm)
    ms = jnp.sum(y3 * y3, axis=1, keepdims=True) * (1.0 / n_valid)
    return (y3 * lax.rsqrt(ms + EPS)).reshape(rows, tm)


def _in_kernel(x_ref, g_ref, winT_ref, wuqT_ref, wukT_ref, wuvT_ref,
               gqa_ref, gka_ref, gcq_ref, gckv_ref, gqb_ref, gkb_ref, gqc_ref, gkc_ref,
               cos_ref, sin_ref,
               qaT_ref, ka_ref, vaT_ref, qbT_ref, kb_ref, vbT_ref, qcT_ref, kc_ref, vcT_ref):
    x = x_ref[...]
    h = x * lax.rsqrt(jnp.mean(x * x, axis=-1, keepdims=True) + EPS) * g_ref[...]
    projT = _dot(winT_ref[...], h.T.astype(BF16))

    def seg(i):
        return projT[IN_OFFS[i]:IN_OFFS[i + 1]]

    cos = cos_ref[...]
    sin = sin_ref[...]

    def with_ones_rows(v, n_heads, d):
        tm = v.shape[1]
        tail = jnp.where(lax.broadcasted_iota(jnp.int32, (VX - d, tm), 0) == 0, 1.0, 0.0)
        rows = []
        for hh in range(n_heads):
            rows += [v[hh * d:(hh + 1) * d], tail]
        return jnp.concatenate(rows, axis=0).astype(BF16)

    qaT_ref[...] = (_group_norm_rows(seg(0), DA_QK, DA_QK) * gqa_ref[...]).astype(BF16)
    kaT = _group_norm_rows(seg(1), DA_QK, DA_QK) * gka_ref[...]
    ka_ref[...] = kaT.T.astype(BF16)
    vaT_ref[...] = with_ones_rows(seg(2), HA, DA_V)

    def rope_store(dst_rows, y, h):
        o = h * DB_PAD
        x1 = y[o + DB_NOPE:o + DB_NOPE + ROPE_HALF]
        x2 = y[o + DB_NOPE + ROPE_HALF:o + DB_QK]
        dst_rows.append(y[o:o + DB_NOPE])
        dst_rows.append(x1 * cos - x2 * sin)
        dst_rows.append(x2 * cos + x1 * sin)
        dst_rows.append(jnp.zeros((DB_PAD - DB_QK, y.shape[1]), F32))

    cqT = seg(3)
    cqn = (cqT * lax.rsqrt(jnp.mean(cqT * cqT, axis=0, keepdims=True) + EPS) * gcq_ref[...]).astype(BF16)
    qf = _group_norm_rows(_dot(wuqT_ref[...], cqn), DB_PAD, DB_QK) * gqb_ref[...]
    rows = []
    for hh in range(HB):
        rope_store(rows, qf, hh)
    qbT_ref[...] = jnp.concatenate(rows, axis=0).astype(BF16)

    ckvT = seg(4)
    ckvn = (ckvT * lax.rsqrt(jnp.mean(ckvT * ckvT, axis=0, keepdims=True) + EPS) * gckv_ref[...]).astype(BF16)
    knope = _dot(wukT_ref[...], ckvn)
    vbT_ref[...] = with_ones_rows(_dot(wuvT_ref[...], ckvn), HB, DB_V)
    krope = seg(5)
    zpad = jnp.zeros((DB_PAD - DB_QK, krope.shape[1]), F32)
    rows = []
    for hh in range(HB):
        rows += [knope[hh * DB_NOPE:(hh + 1) * DB_NOPE], krope, zpad]
    kf = _group_norm_rows(jnp.concatenate(rows, axis=0), DB_PAD, DB_QK) * gkb_ref[...]
    rows = []
    for hh in range(HB):
        rope_store(rows, kf, hh)
    kb_ref[...] = jnp.concatenate(rows, axis=0).T.astype(BF16)

    qcT_ref[...] = (_group_norm_rows(seg(6), DC, DC) * gqc_ref[...]).astype(BF16)
    kcT = _group_norm_rows(seg(7), DC, DC) * gkc_ref[...]
    kc_ref[...] = kcT.T.astype(BF16)
    vcT_ref[...] = with_ones_rows(seg(8), KVH_C, DC)


def _in_proj(x2, lw, S):
    T = x2.shape[0]
    tm = TM_IN
    nt = T // tm
    spb = S // tm

    def full(a):
        return pl.BlockSpec(a.shape, lambda i: (0,) * a.ndim)

    consts = [lw['g_mix'], lw['w_inT'], lw['w_uqT'], lw['w_ukT'], lw['w_uvT'],
              lw['gqa'], lw['gka'], lw['gcq'], lw['gckv'], lw['gqb'], lw['gkb'], lw['gqc'], lw['gkc']]
    in_specs = ([pl.BlockSpec((tm, D_MODEL), lambda i: (i, 0))] + [full(a) for a in consts]
                + [pl.BlockSpec((ROPE_HALF, tm), lambda i: (0, i % spb))] * 2)

    def fm(rows):
        return jax.ShapeDtypeStruct((rows, T), BF16), pl.BlockSpec((rows, tm), lambda i: (0, i))

    def tk(cols):
        return jax.ShapeDtypeStruct((T, cols), BF16), pl.BlockSpec((tm, cols), lambda i: (i, 0))

    outs = [fm(HA * 2 * DA_QK), tk(HA * 2 * DA_QK), fm(HA * VX),
            fm(HB * DB_PAD), tk(HB * DB_PAD), fm(HB * VX),
            fm(HC * DC), tk(KVH_C * DC), fm(KVH_C * VX)]
    return pl.pallas_call(
        _in_kernel,
        grid=(nt,),
        in_specs=in_specs,
        out_specs=[o[1] for o in outs],
        out_shape=[o[0] for o in outs],
        compiler_params=pltpu.CompilerParams(dimension_semantics=("arbitrary",), vmem_limit_bytes=VMEM_LIMIT),
        name="in_proj",
    )(x2, *consts, lw['cosT'], lw['sinT'])


def _attn_a_kernel(sc_ref, qT_ref, k_ref, vT_ref, near_ref, subln_ref, o_ref, w_scr, m_scr, l_scr, acc_scr, *, nk):
    qb = pl.program_id(1)
    n_comb = 2 * HA
    bq = qT_ref.shape[1]
    near_lo = (BQ_A // BK_A) * qb - 1

    for hc in range(n_comb):
        g = hc // 4
        qg = qT_ref[g * 128:(g + 1) * 128, :]
        row = lax.broadcasted_iota(jnp.int32, (128, bq), 0)
        keep = (row // DA_QK) == (hc % 4)
        w_scr[hc] = jnp.where(keep, qg, jnp.zeros_like(qg))
    m_scr[...] = jnp.full(m_scr.shape, NEG_BIG, F32)
    l_scr[...] = jnp.zeros(l_scr.shape, F32)
    acc_scr[...] = jnp.zeros(acc_scr.shape, F32)

    def step(kb, mode):
        ks = pl.multiple_of(kb * BK_A, BK_A)

        def qk(hc):
            g = hc // 4
            return _dot(k_ref[pl.ds(ks, BK_A), g * 128:(g + 1) * 128], w_scr[hc])

        ahead = [qk(i) for i in range(PREFETCH)]
        for hc in range(n_comb):
            h = hc // 2
            s = ahead.pop(0)
            if hc + PREFETCH < n_comb:
                ahead.append(qk(hc + PREFETCH))
            c = sc_ref[(1 - mode) * HA + h]
            cmax = jnp.max(s, axis=0, keepdims=True) + c
            m_old = m_scr[hc]
            m_new = jnp.maximum(m_old, cmax)
            alpha = jnp.exp2(m_old - m_new)
            p = jnp.exp2(s - (m_new - c))
            if mode == 0:
                p = p * near_ref[kb - near_lo, h]
            l_scr[hc] = alpha * l_scr[hc] + jnp.sum(p, axis=0, keepdims=True)
            pv = _dot(vT_ref[h * VX:h * VX + DA_V, pl.ds(ks, BK_A)], p.astype(BF16))
            acc_scr[hc] = alpha * acc_scr[hc] + pv
            m_scr[hc] = m_new

    lo = jnp.maximum(near_lo, 0)
    hi = jnp.minimum(near_lo + NEAR_A, nk)
    lax.fori_loop(0, lo, lambda kb, c: (step(kb, -1), c)[1], 0)
    lax.fori_loop(lo, hi, lambda kb, c: (step(kb, 0), c)[1], 0)
    lax.fori_loop(hi, nk, lambda kb, c: (step(kb, 1), c)[1], 0)

    lam = sc_ref[3 * HA]
    outs = []
    for h in range(HA):
        a = acc_scr[2 * h] / l_scr[2 * h] - lam * (acc_scr[2 * h + 1] / l_scr[2 * h + 1])
        a = a * lax.rsqrt(jnp.mean(a * a, axis=0, keepdims=True) + EPS)
        outs.append(a)
    o = jnp.concatenate(outs, axis=0) * subln_ref[...]
    o_ref[...] = o.T.astype(BF16)


def _sublane_partial_sum(p):
    n, w = p.shape
    return jnp.sum(p.reshape(n // 8, 8, w), axis=0)


def _attn_a_bounded_kernel(sc_ref, qT_ref, k_ref, vT_ref, near_ref, subln_ref, o_ref, w_scr, acc_scr, l_scr, *, nk):
    qb = pl.program_id(1)
    n_comb = 2 * HA
    bq = qT_ref.shape[1]
    near_lo = (BQ_A // BK_A) * qb - 1
    for hc in range(n_comb):
        g = hc // 4
        qg = qT_ref[g * 128:(g + 1) * 128, :]
        row = lax.broadcasted_iota(jnp.int32, (128, bq), 0)
        w_scr[hc] = jnp.where((row // DA_QK) == (hc % 4), qg, jnp.zeros_like(qg))
    acc_scr[...] = jnp.zeros(acc_scr.shape, F32)
    l_scr[...] = jnp.zeros(l_scr.shape, F32)

    lo = jnp.maximum(near_lo, 0)
    hi = jnp.minimum(near_lo + NEAR_A, nk)
    n_far = lo + (nk - hi)

    def step(j0, n_blocks, near):
        def block(jj):
            j = j0 + jj
            if near:
                return j, None
            right = j >= lo
            return jnp.where(right, j - lo + hi, j), jnp.where(right, HA, 0)

        tiles = [(jj, hc) for jj in range(n_blocks) for hc in range(n_comb)]

        def qk(t):
            jj, hc = tiles[t]
            g = hc // 4
            ks = pl.multiple_of(block(jj)[0] * BK_A, BK_A)
            return _dot(k_ref[pl.ds(ks, BK_A), g * 128:(g + 1) * 128], w_scr[hc])

        ahead = [qk(i) for i in range(PREFETCH)]
        for t, (jj, hc) in enumerate(tiles):
            h = hc // 2
            s = ahead.pop(0)
            if t + PREFETCH < len(tiles):
                ahead.append(qk(t + PREFETCH))
            kb, side = block(jj)
            ks = pl.multiple_of(kb * BK_A, BK_A)
            p = jnp.exp2(s)
            if near:
                p = p * near_ref[kb - near_lo, h]
            psum = _sublane_partial_sum(p)
            pv = _dot(vT_ref[h * VX:h * VX + DA_V, pl.ds(ks, BK_A)], p.astype(BF16))
            if near:
                acc_scr[hc] += pv
                l_scr[hc] += psum
            else:
                f = sc_ref[side + h]
                acc_scr[hc] += pv * f
                l_scr[hc] += psum * f

    n_pairs = n_far // UNROLL_A
    lax.fori_loop(0, n_pairs, lambda i, c: (step(i * UNROLL_A, UNROLL_A, False), c)[1], 0)
    lax.fori_loop(n_pairs * UNROLL_A, n_far, lambda j, c: (step(j, 1, False), c)[1], 0)
    lax.fori_loop(lo, hi, lambda kb, c: (step(kb, 1, True), c)[1], 0)

    lam = sc_ref[2 * HA]
    outs = []
    for h in range(HA):
        l0 = jnp.sum(l_scr[2 * h], axis=0, keepdims=True)
        l1 = jnp.sum(l_scr[2 * h + 1], axis=0, keepdims=True)
        a = acc_scr[2 * h] / l0 - lam * (acc_scr[2 * h + 1] / l1)
        outs.append(a * lax.rsqrt(jnp.mean(a * a, axis=0, keepdims=True) + EPS))
    o = jnp.concatenate(outs, axis=0) * subln_ref[...]
    o_ref[...] = o.T.astype(BF16)


def _attn_a(qT, k, vT, lw, B, S, bounded):
    nq = S // BQ_A
    nk = S // BK_A
    n_comb = 2 * HA
    if bounded:
        body = functools.partial(_attn_a_bounded_kernel, nk=nk)
        scratch = [pltpu.VMEM((n_comb, 128, BQ_A), BF16), pltpu.VMEM((n_comb, DA_V, BQ_A), F32),
                   pltpu.VMEM((n_comb, 8, BQ_A), F32)]
        sc, near = lw['sc_a_bounded'], lw['near_a_bounded']
    else:
        body = functools.partial(_attn_a_kernel, nk=nk)
        scratch = [pltpu.VMEM((n_comb, 128, BQ_A), BF16), pltpu.VMEM((n_comb, 1, BQ_A), F32),
                   pltpu.VMEM((n_comb, 1, BQ_A), F32), pltpu.VMEM((n_comb, DA_V, BQ_A), F32)]
        sc, near = lw['sc_a'], lw['near_a']
    return pl.pallas_call(
        body,
        grid=(B, nq),
        in_specs=[
            pl.BlockSpec(memory_space=pltpu.SMEM),
            pl.BlockSpec((n_comb * DA_QK, BQ_A), lambda b, q: (0, b * nq + q)),
            pl.BlockSpec((S, n_comb * DA_QK), lambda b, q: (b, 0)),
            pl.BlockSpec((HA * VX, S), lambda b, q: (0, b)),
            pl.BlockSpec(near.shape, lambda b, q: (0, 0, 0, 0)),
            pl.BlockSpec((HA * DA_V, 1), lambda b, q: (0, 0)),
        ],
        out_specs=pl.BlockSpec((BQ_A, HA * DA_V), lambda b, q: (b * nq + q, 0)),
        out_shape=jax.ShapeDtypeStruct((B * S, HA * DA_V), BF16),
        scratch_shapes=scratch,
        compiler_params=pltpu.CompilerParams(dimension_semantics=("arbitrary", "arbitrary"),
                                             vmem_limit_bytes=VMEM_LIMIT),
        name="attn_a_bounded" if bounded else "attn_a",
    )(sc, qT, k, vT, near, lw['subln'])


def _attn_b_kernel(qT_ref, k_ref, vT_ref, o_ref, m_scr, l_scr, acc_scr, *, nk):
    m_scr[...] = jnp.full(m_scr.shape, NEG_BIG, F32)
    l_scr[...] = jnp.zeros(l_scr.shape, F32)
    acc_scr[...] = jnp.zeros(acc_scr.shape, F32)

    def step(kb, carry):
        ks = pl.multiple_of(kb * BK_B, BK_B)

        def qk(h):
            return _dot(k_ref[pl.ds(ks, BK_B), h * DB_PAD:(h + 1) * DB_PAD], qT_ref[h * DB_PAD:(h + 1) * DB_PAD, :])

        ahead = [qk(i) for i in range(PREFETCH)]
        for h in range(HB):
            s = ahead.pop(0)
            if h + PREFETCH < HB:
                ahead.append(qk(h + PREFETCH))
            m_old = m_scr[h]
            m_new = jnp.maximum(m_old, jnp.max(s, axis=0, keepdims=True))
            alpha = jnp.exp2(m_old - m_new)
            p = jnp.exp2(s - m_new)
            l_scr[h] = alpha * l_scr[h] + jnp.sum(p, axis=0, keepdims=True)
            pv = _dot(vT_ref[h * VX:h * VX + DB_V, pl.ds(ks, BK_B)], p.astype(BF16))
            acc_scr[h] = alpha * acc_scr[h] + pv
            m_scr[h] = m_new
        return carry

    lax.fori_loop(0, nk, step, 0)
    o = jnp.concatenate([acc_scr[h] / l_scr[h] for h in range(HB)], axis=0)
    o_ref[...] = o.T.astype(BF16)


def _attn_b_bounded_kernel(qT_ref, k_ref, vT_ref, o_ref, acc_scr, l_scr, *, nk):
    acc_scr[...] = jnp.zeros(acc_scr.shape, F32)
    l_scr[...] = jnp.zeros(l_scr.shape, F32)

    assert nk % UNROLL_B == 0
    tiles = [(j, h) for j in range(UNROLL_B) for h in range(HB)]

    def step(it, carry):
        def ks(j):
            return pl.multiple_of((it * UNROLL_B + j) * BK_B, BK_B)

        def qk(t):
            j, h = tiles[t]
            return _dot(k_ref[pl.ds(ks(j), BK_B), h * DB_PAD:(h + 1) * DB_PAD],
                        qT_ref[h * DB_PAD:(h + 1) * DB_PAD, :])

        ahead = [qk(i) for i in range(PREFETCH)]
        for t, (j, h) in enumerate(tiles):
            s = ahead.pop(0)
            if t + PREFETCH < len(tiles):
                ahead.append(qk(t + PREFETCH))
            p = jnp.exp2(s)
            l_scr[h] += _sublane_partial_sum(p)
            acc_scr[h] += _dot(vT_ref[h * VX:h * VX + DB_V, pl.ds(ks(j), BK_B)], p.astype(BF16))
        return carry

    lax.fori_loop(0, nk // UNROLL_B, step, 0)
    o = jnp.concatenate([acc_scr[h] / jnp.sum(l_scr[h], axis=0, keepdims=True) for h in range(HB)], axis=0)
    o_ref[...] = o.T.astype(BF16)


def _attn_b(qT, k, vT, B, S, bounded):
    nq = S // BQ_B
    nk = S // BK_B
    if bounded:
        body = functools.partial(_attn_b_bounded_kernel, nk=nk)
        scratch = [pltpu.VMEM((HB, DB_V, BQ_B), F32), pltpu.VMEM((HB, 8, BQ_B), F32)]
    else:
        body = functools.partial(_attn_b_kernel, nk=nk)
        scratch = [pltpu.VMEM((HB, 1, BQ_B), F32), pltpu.VMEM((HB, 1, BQ_B), F32), pltpu.VMEM((HB, DB_V, BQ_B), F32)]
    return pl.pallas_call(
        body,
        grid=(B, nq),
        in_specs=[
            pl.BlockSpec((HB * DB_PAD, BQ_B), lambda b, q: (0, b * nq + q)),
            pl.BlockSpec((S, HB * DB_PAD), lambda b, q: (b, 0)),
            pl.BlockSpec((HB * VX, S), lambda b, q: (0, b)),
        ],
        out_specs=pl.BlockSpec((BQ_B, HB * DB_V), lambda b, q: (b * nq + q, 0)),
        out_shape=jax.ShapeDtypeStruct((B * S, HB * DB_V), BF16),
        scratch_shapes=scratch,
        compiler_params=pltpu.CompilerParams(dimension_semantics=("arbitrary", "arbitrary"),
                                             vmem_limit_bytes=VMEM_LIMIT),
        name="attn_b_bounded" if bounded else "attn_b",
    )(qT, k, vT)


def _attn_c_kernel(sc_ref, qT_ref, k_ref, vT_ref, bm_ref, o_ref, *, nb, S):
    n = pl.program_id(1)
    start = pl.multiple_of(jnp.clip((n - 1) * BLOCK, 0, S - KW_C), BLOCK)
    case = jnp.where(n == 0, 0, jnp.where(n == nb - 1, 2, 1))
    kwin = k_ref[pl.ds(start, KW_C), :]
    outs = []
    for h in range(HC):
        j = h // G_C
        qh = qT_ref[h * DC:(h + 1) * DC, :]
        zz = jnp.zeros_like(qh)
        w = jnp.concatenate([qh, zz] if j == 0 else [zz, qh], axis=0)
        s = _dot(kwin, w) + bm_ref[case, h]
        sink = sc_ref[h]
        m = jnp.maximum(jnp.max(s, axis=0, keepdims=True), sink)
        p = jnp.exp2(s - m)
        den = jnp.sum(p, axis=0, keepdims=True) + jnp.exp2(sink - m)
        pv = _dot(vT_ref[j * VX:j * VX + DC, pl.ds(start, KW_C)], p.astype(BF16))
        outs.append(pv / den)
    o_ref[...] = jnp.concatenate(outs, axis=0).T.astype(BF16)


def _attn_c_bounded_kernel(sc_ref, qT_ref, k_ref, vT_ref, e_ref, o_ref, *, n_sub, S):
    g = pl.program_id(1)
    tiles = [(u, h) for u in range(SUB_C) for h in range(HC)]

    def window(u):
        n = g * SUB_C + u
        start = pl.multiple_of(jnp.clip(n * QB_C - WINDOW, 0, S - KWB_C), WINDOW)
        case = jnp.where(n == 0, 0, jnp.where(n == n_sub - 1, 2, 1))
        return start, case

    def qk(t):
        u, h = tiles[t]
        start, _ = window(u)
        qh = qT_ref[h * DC:(h + 1) * DC, u * QB_C:(u + 1) * QB_C]
        zz = jnp.zeros_like(qh)
        w = jnp.concatenate([qh, zz] if h // G_C == 0 else [zz, qh], axis=0)
        return _dot(k_ref[pl.ds(start, KWB_C), :], w)

    ahead = [qk(i) for i in range(PREFETCH)]
    outs = []
    for t, (u, h) in enumerate(tiles):
        s = ahead.pop(0)
        if t + PREFETCH < len(tiles):
            ahead.append(qk(t + PREFETCH))
        start, case = window(u)
        j = h // G_C
        p = (jnp.exp2(s) * e_ref[case, h]).astype(BF16)
        pv = _dot(vT_ref[j * VX:(j + 1) * VX, pl.ds(start, KWB_C)], p)
        outs.append(pv[0:DC] / (pv[DC:DC + 1] + sc_ref[HC + h]))
        if h == HC - 1:
            o_ref[u * QB_C:(u + 1) * QB_C, :] = jnp.concatenate(outs, axis=0).T.astype(BF16)
            outs = []


def _attn_c(qT, k, vT, lw, B, S, bounded):
    if bounded:
        n_sub = S // QB_C
        assert n_sub >= 3 and n_sub % SUB_C == 0
        nb, bq = n_sub // SUB_C, QB_C * SUB_C
        body = functools.partial(_attn_c_bounded_kernel, n_sub=n_sub, S=S)
        tile, sc = lw['e_c'], lw['sc_c_bounded']
    else:
        nb, bq = S // BLOCK, BLOCK
        assert nb >= 3
        body = functools.partial(_attn_c_kernel, nb=nb, S=S)
        tile, sc = lw['bm_c'], lw['sc_c']
    return pl.pallas_call(
        body,
        grid=(B, nb),
        in_specs=[
            pl.BlockSpec(memory_space=pltpu.SMEM),
            pl.BlockSpec((HC * DC, bq), lambda b, n: (0, b * nb + n)),
            pl.BlockSpec((S, KVH_C * DC), lambda b, n: (b, 0)),
            pl.BlockSpec((KVH_C * VX, S), lambda b, n: (0, b)),
            pl.BlockSpec(tile.shape, lambda b, n: (0, 0, 0, 0)),
        ],
        out_specs=pl.BlockSpec((bq, HC * DC), lambda b, n: (b * nb + n, 0)),
        out_shape=jax.ShapeDtypeStruct((B * S, HC * DC), BF16),
        compiler_params=pltpu.CompilerParams(dimension_semantics=("arbitrary", "arbitrary"),
                                             vmem_limit_bytes=VMEM_LIMIT),
        name="attn_c_bounded" if bounded else "attn_c",
    )(sc, qT, k, vT, tile)


def _out_kernel(x_ref, ya_ref, yb_ref, yc_ref, wo_ref, g_ref, wr_ref, xn_ref, h2_ref, affT_ref):
    na, nb_ = HA * DA_V, HA * DA_V + HB * DB_V
    xn = (x_ref[...] + _dot(ya_ref[...], wo_ref[0:na, :]) + _dot(yb_ref[...], wo_ref[na:nb_, :])
          + _dot(yc_ref[...], wo_ref[nb_:, :]))
    xn_ref[...] = xn
    h2 = (xn * lax.rsqrt(jnp.mean(xn * xn, axis=-1, keepdims=True) + EPS) * g_ref[...]).astype(BF16)
    h2_ref[...] = h2
    lg = _dot(h2, wr_ref[...]).T[0:N_EXPERTS]
    e = jnp.exp(lg - jnp.max(lg, axis=0, keepdims=True))
    affT_ref[...] = e / jnp.sum(e, axis=0, keepdims=True)


def _out_proj(x2, ya, yb, yc, lw):
    T = x2.shape[0]
    tm = TM_OUT

    def full(a):
        return pl.BlockSpec(a.shape, lambda i: (0,) * a.ndim)

    return pl.pallas_call(
        _out_kernel,
        grid=(T // tm,),
        in_specs=[pl.BlockSpec((tm, D_MODEL), lambda i: (i, 0)),
                  pl.BlockSpec((tm, ya.shape[1]), lambda i: (i, 0)),
                  pl.BlockSpec((tm, yb.shape[1]), lambda i: (i, 0)),
                  pl.BlockSpec((tm, yc.shape[1]), lambda i: (i, 0)),
                  full(lw['w_out']), full(lw['g_ffn']), full(lw['w_r'])],
        out_specs=[pl.BlockSpec((tm, D_MODEL), lambda i: (i, 0)),
                   pl.BlockSpec((tm, D_MODEL), lambda i: (i, 0)),
                   pl.BlockSpec((N_EXPERTS, tm), lambda i: (0, i))],
        out_shape=[jax.ShapeDtypeStruct((T, D_MODEL), F32),
                   jax.ShapeDtypeStruct((T, D_MODEL), BF16),
                   jax.ShapeDtypeStruct((N_EXPERTS, T), F32)],
        compiler_params=pltpu.CompilerParams(dimension_semantics=("arbitrary",), vmem_limit_bytes=VMEM_LIMIT),
        name="out_proj",
    )(x2, ya, yb, yc, lw['w_out'], lw['g_ffn'], lw['w_r'])


def _topk_kernel(aff_ref, thr_ref, tcut_ref, *, cap, T):
    capf = float(cap)

    def bits():
        return lax.bitcast_convert_type(aff_ref[...], jnp.int32)

    def count(mask):
        return jnp.sum(jnp.where(mask, 1.0, 0.0), axis=1, keepdims=True)

    def vbody(i, v):
        cand = v | jnp.left_shift(jnp.int32(1), 30 - i)
        return jnp.where(count(bits() >= cand) >= capf, cand, v)

    thr = lax.fori_loop(0, 31, vbody, jnp.zeros((N_EXPERTS, 1), jnp.int32))
    need = capf - count(bits() > thr)
    nbits = max(1, int(math.ceil(math.log2(T))))

    def tbody(i, c):
        cand = c | jnp.left_shift(jnp.int32(1), nbits - 1 - i)
        idx = lax.broadcasted_iota(jnp.int32, (N_EXPERTS, T), 1)
        f = count((bits() == thr) & (idx < cand))
        return jnp.where(f < need, cand, c)

    tcut = lax.fori_loop(0, nbits, tbody, jnp.zeros((N_EXPERTS, 1), jnp.int32))
    thr_ref[...] = jnp.broadcast_to(thr, thr_ref.shape)
    tcut_ref[...] = jnp.broadcast_to(tcut, tcut_ref.shape)


def _topk_thresholds(affT):
    E, T = affT.shape
    cap = CAPACITY_FACTOR * T // N_EXPERTS
    return pl.pallas_call(
        functools.partial(_topk_kernel, cap=cap, T=T),
        grid=(1,),
        in_specs=[pl.BlockSpec((E, T), lambda i: (0, 0))],
        out_specs=[pl.BlockSpec((E, 128), lambda i: (0, 0))] * 2,
        out_shape=[jax.ShapeDtypeStruct((E, 128), jnp.int32)] * 2,
        compiler_params=pltpu.CompilerParams(dimension_semantics=("arbitrary",), vmem_limit_bytes=VMEM_LIMIT),
        name="topk_thr",
    )(affT)


def _exclusive_prefix_count(sel, tri):
    e, n = sel.shape
    ones = jnp.where(sel, 1.0, 0.0)
    out, before = [], jnp.zeros((e, 1), F32)
    for kt in range(n // 128):
        blk = ones[:, kt * 128:(kt + 1) * 128]
        out.append(_dot(blk.astype(BF16), tri) + before)
        before = before + jnp.sum(blk, axis=1, keepdims=True)
    return jnp.concatenate(out, axis=1)


def _moe_kernel(x_ref, h2_ref, affT_ref, thr_ref, tcut_ref, tri_ref, wg_ref, wu_ref, wd_ref, o_ref, pos_scr, xe_scr):
    i = pl.program_id(0)
    e = pl.program_id(1)
    tt = TT_MOE

    def one_hot_rows(posm, c):
        rel = posm - (c * CH_MOE).astype(F32)
        jj = lax.broadcasted_iota(jnp.int32, (CH_MOE, tt), 0).astype(F32)
        return jnp.broadcast_to(rel, (CH_MOE, tt)) == jj

    @pl.when(e == 0)
    def _():
        b = lax.bitcast_convert_type(affT_ref[...], jnp.int32)
        thr = thr_ref[:, 0:1]
        tc = tcut_ref[:, 0:1]
        tg = i * (tt * SUBS_MOE) + lax.broadcasted_iota(jnp.int32, b.shape, 1)
        sel = (b > thr) | ((b == thr) & (tg <= tc))
        for sub in range(SUBS_MOE):
            sl = slice(sub * tt, (sub + 1) * tt)
            pos = _exclusive_prefix_count(sel[:, sl], tri_ref[...])
            posm_all = jnp.where(sel[:, sl], pos, -1.0)
            pos_scr[:, sl] = posm_all
            for grp in range(N_EXPERTS // GATHER_GROUP):
                hits = [one_hot_rows(posm_all[ee:ee + 1], jnp.int32(0))
                        for ee in range(grp * GATHER_GROUP, (grp + 1) * GATHER_GROUP)]
                onehot = jnp.where(jnp.concatenate(hits, axis=0), 1.0, 0.0).astype(BF16)
                rows = slice(grp * GATHER_GROUP * CH_MOE, (grp + 1) * GATHER_GROUP * CH_MOE)
                xe_scr[sub, rows, :] = _dot(onehot, h2_ref[sl, :]).astype(BF16)
        o_ref[...] = x_ref[...]

    posm = [pos_scr[pl.ds(e, 1), sub * tt:(sub + 1) * tt] for sub in range(SUBS_MOE)]
    gate = [affT_ref[pl.ds(e, 1), sub * tt:(sub + 1) * tt] for sub in range(SUBS_MOE)]
    n_sel = jnp.max(jnp.concatenate(posm, axis=0)) + 1.0
    n_ch = (n_sel.astype(jnp.int32) + CH_MOE - 1) // CH_MOE

    def chunk(c, pregathered):
        hits, xes = [], []
        for sub in range(SUBS_MOE):
            hit = one_hot_rows(posm[sub], c)
            if pregathered:
                xes.append(xe_scr[sub, pl.ds(pl.multiple_of(e * CH_MOE, CH_MOE), CH_MOE), :])
            else:
                xes.append(_dot(jnp.where(hit, 1.0, 0.0).astype(BF16),
                                h2_ref[sub * tt:(sub + 1) * tt, :]).astype(BF16))
            hits.append(hit)
        xe = jnp.concatenate(xes, axis=0)
        g = _dot(xe, wg_ref[...])
        u = _dot(xe, wu_ref[...])
        hid = (g * jax.nn.sigmoid(g) * u).astype(BF16)
        ye = _dot(hid, wd_ref[...])
        for sub in range(SUBS_MOE):
            gc = jnp.sum(jnp.where(hits[sub], jnp.broadcast_to(gate[sub], hits[sub].shape), 0.0),
                         axis=1, keepdims=True)
            yeb = jnp.concatenate([(ye[sub * CH_MOE:(sub + 1) * CH_MOE] * gc).astype(BF16),
                                   jnp.zeros((CHP_MOE - CH_MOE, ye.shape[1]), BF16)], axis=0)
            onehot = jnp.concatenate([jnp.where(hits[sub], 1.0, 0.0).astype(BF16),
                                      jnp.zeros((CHP_MOE - CH_MOE, tt), BF16)], axis=0)
            o_ref[sub * tt:(sub + 1) * tt, :] += _dot(onehot.T, yeb)

    chunk(jnp.int32(0), True)
    lax.fori_loop(1, n_ch, lambda c, carry: (chunk(c, False), carry)[1], 0)


def _moe(xn, h2, affT, thr, tcut, lw):
    T = xn.shape[0]
    tt = TT_MOE * SUBS_MOE
    once = pl.Buffered(1)
    return pl.pallas_call(
        _moe_kernel,
        grid=(T // tt, N_EXPERTS),
        in_specs=[
            pl.BlockSpec((tt, D_MODEL), lambda i, e: (i, 0), pipeline_mode=once),
            pl.BlockSpec((tt, D_MODEL), lambda i, e: (i, 0), pipeline_mode=once),
            pl.BlockSpec((N_EXPERTS, tt), lambda i, e: (0, i)),
            pl.BlockSpec((N_EXPERTS, 128), lambda i, e: (0, 0)),
            pl.BlockSpec((N_EXPERTS, 128), lambda i, e: (0, 0)),
            pl.BlockSpec((128, 128), lambda i, e: (0, 0), pipeline_mode=once),
            pl.BlockSpec((None, D_MODEL, D_EXPERT), lambda i, e: (e, 0, 0)),
            pl.BlockSpec((None, D_MODEL, D_EXPERT), lambda i, e: (e, 0, 0)),
            pl.BlockSpec((None, D_EXPERT, D_MODEL), lambda i, e: (e, 0, 0)),
        ],
        out_specs=pl.BlockSpec((tt, D_MODEL), lambda i, e: (i, 0), pipeline_mode=once),
        out_shape=jax.ShapeDtypeStruct((T, D_MODEL), F32),
        scratch_shapes=[pltpu.VMEM((N_EXPERTS, tt), F32),
                        pltpu.VMEM((SUBS_MOE, N_EXPERTS * CH_MOE, D_MODEL), BF16)],
        compiler_params=pltpu.CompilerParams(dimension_semantics=("arbitrary", "arbitrary"),
                                             vmem_limit_bytes=VMEM_LIMIT),
        name="moe_ffn",
    )(xn, h2, affT, thr, tcut, lw['tri'], lw['w_gate'], lw['w_up'], lw['w_down'])


def _t5_bucket(rel):
    nb = NUM_BUCKETS // 2
    max_exact = nb // 2
    ret = jnp.where(rel > 0, nb, 0)
    n = jnp.abs(rel)
    nf = jnp.maximum(n, 1).astype(jnp.float32)
    large = max_exact + (jnp.log(nf / max_exact) / math.log(MAX_DISTANCE / max_exact) * (nb - max_exact)).astype(jnp.int32)
    large = jnp.minimum(large, nb - 1)
    return ret + jnp.where(n < max_exact, n, large)


def _col(v):
    return v.astype(F32).reshape(-1, 1)


def _pad_heads(w, n_heads, d, d_pad):
    k = w.shape[1]
    w3 = w.reshape(n_heads, d, k)
    return jnp.pad(w3, ((0, 0), (0, d_pad - d), (0, 0))).reshape(n_heads * d_pad, k)


def _toeplitz_tiles(fn, offsets, nk, nq):
    span = nk + nq
    out = []
    for off in offsets:
        u = fn(off + nk - 1 - jnp.arange(span, dtype=jnp.int32))
        rows = jnp.tile(u, (1, nk))[:, :nk * (span - 1)].reshape(u.shape[0], nk, span - 1)
        out.append(rows[:, :, nk - 1:nk - 1 + nq])
    return jnp.stack(out, axis=0)


def _prep_shared(rel_bias):
    rb = rel_bias.astype(F32) * LOG2E

    def bias_a(rel):
        return rb[:, :HA][_t5_bucket(rel)].T

    def bias_c(rel):
        return rb[:, HA:][_t5_bucket(rel)].T

    def in_band(rel):
        return (jnp.abs(rel) <= WINDOW)[None, :]

    bmax = jnp.max(rb[:, :HA], axis=0)
    near_offs = [d * BK_A for d in range(-1, NEAR_A - 1)]
    near_a = _toeplitz_tiles(lambda rel: jnp.exp2(bias_a(rel) - bmax[:, None]), near_offs, BK_A, BQ_A)
    far = _t5_bucket(jnp.array([-(1 << 20), 1 << 20], dtype=jnp.int32))
    sc_far = jnp.concatenate([rb[far[1], :HA], bmax, rb[far[0], :HA]])
    a_bounded = {
        'near': _toeplitz_tiles(lambda rel: jnp.exp2(bias_a(rel)), near_offs, BK_A, BQ_A),
        'sc': jnp.concatenate([jnp.exp2(rb[far[0], :HA]), jnp.exp2(rb[far[1], :HA])]),
        'bias_mag': jnp.max(jnp.abs(rb[:, :HA])),
    }
    bm_c = _toeplitz_tiles(lambda rel: jnp.where(in_band(rel), bias_c(rel), NEG_BIG),
                           [0, -BLOCK, -2 * BLOCK], KW_C, BLOCK)
    c_bounded = {
        'e': _toeplitz_tiles(lambda rel: jnp.where(in_band(rel), jnp.exp2(bias_c(rel)), 0.0),
                             [0, -WINDOW, -(KWB_C - QB_C)], KWB_C, QB_C),
        'bias_mag': jnp.max(jnp.abs(rb[:, HA:])),
    }
    return near_a, sc_far, bm_c, a_bounded, c_bounded


def _rope_tables(S):
    inv = 1.0 / (ROPE_BASE ** (jnp.arange(ROPE_HALF, dtype=jnp.float32) / ROPE_HALF))
    ang = jnp.arange(S, dtype=jnp.int32).astype(jnp.float32)[:, None] * inv[None, :]
    return jnp.cos(ang).T, jnp.sin(ang).T


def _logit_bounds(l, p, a_bounded, c_bounded, sa, sb, scc):
    qa, ka = p['qn_a'][l].astype(F32), p['kn_a'][l].astype(F32)
    bound_a = 1.01 * DA_QK * jnp.max(jnp.abs(qa * ka)) * sa + 2.0 * a_bounded['bias_mag']
    qb, kb = p['qn_b'][l].astype(F32), p['kn_b'][l].astype(F32)
    bound_b = 1.01 * DB_QK * jnp.max(jnp.abs(qb)) * jnp.max(jnp.abs(kb)) * sb
    qc, kc = p['qn_c'][l].astype(F32), p['kn_c'][l].astype(F32)
    bound_c = (1.01 * DC * jnp.max(jnp.abs(qc * kc)) * scc + 2.0 * c_bounded['bias_mag']
               + jnp.max(jnp.abs(p['sink_c'][l].astype(F32))) * LOG2E)
    return bound_a <= BOUND_LIMIT, bound_b <= BOUND_LIMIT, bound_c <= BOUND_LIMIT


def _prep_layer(l, p, shared):
    near_a, sc_far, bm_c, a_bounded, c_bounded = shared
    lam_init = 0.8 - 0.6 * math.exp(-0.3 * l)
    lam = (jnp.exp(jnp.sum(p['lam_q1'][l].astype(F32) * p['lam_k1'][l].astype(F32)))
           - jnp.exp(jnp.sum(p['lam_q2'][l].astype(F32) * p['lam_k2'][l].astype(F32))) + lam_init)
    w_uq = p['w_uq'][l]
    w_ukv = p['w_ukv'][l].reshape(KV_RANK, HB, DB_NOPE + DB_V)
    sa = DA_QK ** -0.5 * LOG2E
    sb = DB_QK ** -0.5 * LOG2E
    scc = DC ** -0.5 * LOG2E
    zpad = jnp.zeros((DB_PAD - DB_QK,), F32)
    lw = {
        'g_mix': p['norm_mix_g'][l].astype(F32).reshape(1, D_MODEL),
        'w_inT': p['w_in'][l].T.astype(BF16),
        'w_uqT': _pad_heads(w_uq.T, HB, DB_QK, DB_PAD).astype(BF16),
        'w_ukT': w_ukv[:, :, :DB_NOPE].reshape(KV_RANK, HB * DB_NOPE).T.astype(BF16),
        'w_uvT': w_ukv[:, :, DB_NOPE:].reshape(KV_RANK, HB * DB_V).T.astype(BF16),
        'gqa': _col(jnp.tile(p['qn_a'][l], 2 * HA) * sa),
        'gka': _col(jnp.tile(p['kn_a'][l], 2 * HA)),
        'gcq': _col(p['g_cq'][l]),
        'gckv': _col(p['g_ckv'][l]),
        'gqb': _col(jnp.tile(jnp.concatenate([p['qn_b'][l].astype(F32) * sb, zpad]), HB)),
        'gkb': _col(jnp.tile(jnp.concatenate([p['kn_b'][l].astype(F32), zpad]), HB)),
        'gqc': _col(jnp.tile(p['qn_c'][l], HC) * scc),
        'gkc': _col(jnp.tile(p['kn_c'][l], KVH_C)),
        'near_a': near_a,
        'sc_a': jnp.concatenate([sc_far, lam.reshape(1)]).astype(F32),
        'subln': _col(jnp.tile(p['subln_a'][l], HA) * (1.0 - lam_init)),
        'bm_c': bm_c,
        'sc_c': p['sink_c'][l].astype(F32) * LOG2E,
        'w_out': p['w_out'][l].astype(BF16),
        'g_ffn': p['norm_ffn_g'][l].astype(F32).reshape(1, D_MODEL),
        'w_r': jnp.pad(p['w_router'][l], ((0, 0), (0, 128 - N_EXPERTS))).astype(BF16),
        'w_gate': p['w_gate'][l].astype(BF16),
        'w_up': p['w_up'][l].astype(BF16),
        'w_down': p['w_down'][l].astype(BF16),
        'tri': jnp.triu(jnp.ones((128, 128), F32), k=1).astype(BF16),
        'near_a_bounded': a_bounded['near'],
        'sc_a_bounded': jnp.concatenate([a_bounded['sc'], lam.reshape(1)]).astype(F32),
    }
    lw['e_c'] = c_bounded['e']
    lw['sc_c_bounded'] = jnp.concatenate([lw['sc_c'], jnp.exp2(lw['sc_c'])])
    lw['bounded_a'], lw['bounded_b'], lw['bounded_c'] = _logit_bounds(l, p, a_bounded, c_bounded, sa, sb, scc)
    return lw


def _trunk(x, layers):
    B, S, D = x.shape
    x2 = x.reshape(B * S, D)
    cosT, sinT = _rope_tables(S)
    for lw in layers:
        lw = dict(lw, cosT=cosT, sinT=sinT)
        qaT, ka, vaT, qbT, kb, vbT, qcT, kc, vcT = _in_proj(x2, lw, S)
        ya = lax.cond(lw['bounded_a'],
                      lambda q, k, v: _attn_a(q, k, v, lw, B, S, True),
                      lambda q, k, v: _attn_a(q, k, v, lw, B, S, False), qaT, ka, vaT)
        yb = lax.cond(lw['bounded_b'],
                      lambda q, k, v: _attn_b(q, k, v, B, S, True),
                      lambda q, k, v: _attn_b(q, k, v, B, S, False), qbT, kb, vbT)
        yc = lax.cond(lw['bounded_c'],
                      lambda q, k, v: _attn_c(q, k, v, lw, B, S, True),
                      lambda q, k, v: _attn_c(q, k, v, lw, B, S, False), qcT, kc, vcT)
        xn, h2, affT = _out_proj(x2, ya, yb, yc, lw)
        thr, tcut = _topk_thresholds(affT)
        x2 = _moe(xn, h2, affT, thr, tcut, lw)
    return x2.reshape(B, S, D)


def kernel(x_prompt, x_sample, rel_bias, norm_mix_g, w_in, qn_a, kn_a, lam_q1, lam_k1, lam_q2, lam_k2, subln_a, g_cq, w_uq, g_ckv, w_ukv, qn_b, kn_b, qn_c, kn_c, sink_c, w_out, norm_ffn_g, w_router, w_gate, w_up, w_down):
    p = dict(norm_mix_g=norm_mix_g, w_in=w_in, qn_a=qn_a, kn_a=kn_a, lam_q1=lam_q1, lam_k1=lam_k1,
             lam_q2=lam_q2, lam_k2=lam_k2, subln_a=subln_a, g_cq=g_cq, w_uq=w_uq, g_ckv=g_ckv, w_ukv=w_ukv,
             qn_b=qn_b, kn_b=kn_b, qn_c=qn_c, kn_c=kn_c, sink_c=sink_c, w_out=w_out, norm_ffn_g=norm_ffn_g,
             w_router=w_router, w_gate=w_gate, w_up=w_up, w_down=w_down)
    shared = _prep_shared(rel_bias)
    layers = [_prep_layer(l, p, shared) for l in range(w_in.shape[0])]
    return (_trunk(x_prompt, layers), _trunk(x_sample, layers))
```

```python
import functools
import math

import jax
import jax.numpy as jnp
import numpy as np
from jax import lax
from jax.experimental import pallas as pl
from jax.experimental.pallas import tpu as pltpu

D_MODEL = 1024
BLOCK = 128
HA, DA_QK, DA_V = 4, 32, 64
HB, Q_RANK, KV_RANK, DB_NOPE, DB_ROPE, DB_V = 6, 384, 256, 64, 32, 64
ROPE_BASE = 10000.0
HC, KVH_C, DC, WINDOW = 6, 2, 64, 128
G_C = HC // KVH_C
NUM_BUCKETS, MAX_DISTANCE = 32, 128
N_EXPERTS, D_EXPERT, CAPACITY_FACTOR = 16, 1024, 2
EPS = 1e-6

DB_QK = DB_NOPE + DB_ROPE
DB_PAD = 128
VX = 80
BOUND_LIMIT = 80.0
ROPE_HALF = DB_ROPE // 2
IN_SPLITS = (HA * 2 * DA_QK, HA * 2 * DA_QK, HA * DA_V, Q_RANK, KV_RANK, DB_ROPE, HC * DC, KVH_C * DC, KVH_C * DC)
IN_OFFS = tuple(int(v) for v in np.cumsum((0,) + IN_SPLITS))
IN_COLS = IN_OFFS[-1]
LOG2E = 1.4426950408889634
NEG_BIG = -1e30

F32 = jnp.float32
BF16 = jnp.bfloat16

TM_IN = 1024
TM_OUT = 1024
BQ_A = 512
BK_A = 512
UNROLL_A = 4
NEAR_A = BQ_A // BK_A + 2
assert BQ_A % BK_A == 0 and BK_A >= MAX_DISTANCE
BQ_B = 512
BK_B = 512
UNROLL_B = 4
KW_C = 3 * BLOCK
QB_C = 256
KWB_C = QB_C + 2 * WINDOW
SUB_C = 2
TT_MOE = 1024
SUBS_MOE = 2
GATHER_GROUP = 8
CH_MOE = 160
CHP_MOE = 256
PREFETCH = 3
VMEM_LIMIT = 56 * 1024 * 1024


def _dot(a, b):
    return jnp.dot(a, b, preferred_element_type=F32)


def _group_norm_rows(y, gs, n_valid):
    rows, tm = y.shape
    y3 = y.reshape(rows // gs, gs, tm)
    ms = jnp.sum(y3 * y3, axis=1, keepdims=True) * (1.0 / n_valid)
    return (y3 * lax.rsqrt(ms + EPS)).reshape(rows, tm)


def _in_kernel(x_ref, g_ref, winT_ref, wuqT_ref, wukT_ref, wuvT_ref,
               gqa_ref, gka_ref, gcq_ref, gckv_ref, gqb_ref, gkb_ref, gqc_ref, gkc_ref,
               cos_ref, sin_ref,
               qaT_ref, ka_ref, vaT_ref, qbT_ref, kb_ref, vbT_ref, qcT_ref, kc_ref, vcT_ref):
    x = x_ref[...]
    h = x * lax.rsqrt(jnp.mean(x * x, axis=-1, keepdims=True) + EPS) * g_ref[...]
    projT = _dot(winT_ref[...], h.T.astype(BF16))

    def seg(i):
        return projT[IN_OFFS[i]:IN_OFFS[i + 1]]

    cos = cos_ref[...]
    sin = sin_ref[...]

    def with_ones_rows(v, n_heads, d):
        tm = v.shape[1]
        tail = jnp.where(lax.broadcasted_iota(jnp.int32, (VX - d, tm), 0) == 0, 1.0, 0.0)
        rows = []
        for hh in range(n_heads):
            rows += [v[hh * d:(hh + 1) * d], tail]
        return jnp.concatenate(rows, axis=0).astype(BF16)

    qaT_ref[...] = (_group_norm_rows(seg(0), DA_QK, DA_QK) * gqa_ref[...]).astype(BF16)
    kaT = _group_norm_rows(seg(1), DA_QK, DA_QK) * gka_ref[...]
    ka_ref[...] = kaT.T.astype(BF16)
    vaT_ref[...] = with_ones_rows(seg(2), HA, DA_V)

    def rope_store(dst_rows, y, h):
        o = h * DB_PAD
        x1 = y[o + DB_NOPE:o + DB_NOPE + ROPE_HALF]
        x2 = y[o + DB_NOPE + ROPE_HALF:o + DB_QK]
        dst_rows.append(y[o:o + DB_NOPE])
        dst_rows.append(x1 * cos - x2 * sin)
        dst_rows.append(x2 * cos + x1 * sin)
        dst_rows.append(jnp.zeros((DB_PAD - DB_QK, y.shape[1]), F32))

    cqT = seg(3)
    cqn = (cqT * lax.rsqrt(jnp.mean(cqT * cqT, axis=0, keepdims=True) + EPS) * gcq_ref[...]).astype(BF16)
    qf = _group_norm_rows(_dot(wuqT_ref[...], cqn), DB_PAD, DB_QK) * gqb_ref[...]
    rows = []
    for hh in range(HB):
        rope_store(rows, qf, hh)
    qbT_ref[...] = jnp.concatenate(rows, axis=0).astype(BF16)

    ckvT = seg(4)
    ckvn = (ckvT * lax.rsqrt(jnp.mean(ckvT * ckvT, axis=0, keepdims=True) + EPS) * gckv_ref[...]).astype(BF16)
    knope = _dot(wukT_ref[...], ckvn)
    vbT_ref[...] = with_ones_rows(_dot(wuvT_ref[...], ckvn), HB, DB_V)
    krope = seg(5)
    zpad = jnp.zeros((DB_PAD - DB_QK, krope.shape[1]), F32)
    rows = []
    for hh in range(HB):
        rows += [knope[hh * DB_NOPE:(hh + 1) * DB_NOPE], krope, zpad]
    kf = _group_norm_rows(jnp.concatenate(rows, axis=0), DB_PAD, DB_QK) * gkb_ref[...]
    rows = []
    for hh in range(HB):
        rope_store(rows, kf, hh)
    kb_ref[...] = jnp.concatenate(rows, axis=0).T.astype(BF16)

    qcT_ref[...] = (_group_norm_rows(seg(6), DC, DC) * gqc_ref[...]).astype(BF16)
    kcT = _group_norm_rows(seg(7), DC, DC) * gkc_ref[...]
    kc_ref[...] = kcT.T.astype(BF16)
    vcT_ref[...] = with_ones_rows(seg(8), KVH_C, DC)


def _in_proj(x2, lw, S):
    T = x2.shape[0]
    tm = TM_IN
    nt = T // tm
    spb = S // tm

    def full(a):
        return pl.BlockSpec(a.shape, lambda i: (0,) * a.ndim)

    consts = [lw['g_mix'], lw['w_inT'], lw['w_uqT'], lw['w_ukT'], lw['w_uvT'],
              lw['gqa'], lw['gka'], lw['gcq'], lw['gckv'], lw['gqb'], lw['gkb'], lw['gqc'], lw['gkc']]
    in_specs = ([pl.BlockSpec((tm, D_MODEL), lambda i: (i, 0))] + [full(a) for a in consts]
                + [pl.BlockSpec((ROPE_HALF, tm), lambda i: (0, i % spb))] * 2)

    def fm(rows):
        return jax.ShapeDtypeStruct((rows, T), BF16), pl.BlockSpec((rows, tm), lambda i: (0, i))

    def tk(cols):
        return jax.ShapeDtypeStruct((T, cols), BF16), pl.BlockSpec((tm, cols), lambda i: (i, 0))

    outs = [fm(HA * 2 * DA_QK), tk(HA * 2 * DA_QK), fm(HA * VX),
            fm(HB * DB_PAD), tk(HB * DB_PAD), fm(HB * VX),
            fm(HC * DC), tk(KVH_C * DC), fm(KVH_C * VX)]
    return pl.pallas_call(
        _in_kernel,
        grid=(nt,),
        in_specs=in_specs,
        out_specs=[o[1] for o in outs],
        out_shape=[o[0] for o in outs],
        compiler_params=pltpu.CompilerParams(dimension_semantics=("arbitrary",), vmem_limit_bytes=VMEM_LIMIT),
        name="in_proj",
    )(x2, *consts, lw['cosT'], lw['sinT'])


def _attn_a_kernel(sc_ref, qT_ref, k_ref, vT_ref, near_ref, subln_ref, o_ref, w_scr, m_scr, l_scr, acc_scr, *, nk):
    qb = pl.program_id(1)
    n_comb = 2 * HA
    bq = qT_ref.shape[1]
    near_lo = (BQ_A // BK_A) * qb - 1

    for hc in range(n_comb):
        g = hc // 4
        qg = qT_ref[g * 128:(g + 1) * 128, :]
        row = lax.broadcasted_iota(jnp.int32, (128, bq), 0)
        keep = (row // DA_QK) == (hc % 4)
        w_scr[hc] = jnp.where(keep, qg, jnp.zeros_like(qg))
    m_scr[...] = jnp.full(m_scr.shape, NEG_BIG, F32)
    l_scr[...] = jnp.zeros(l_scr.shape, F32)
    acc_scr[...] = jnp.zeros(acc_scr.shape, F32)

    def step(kb, mode):
        ks = pl.multiple_of(kb * BK_A, BK_A)

        def qk(hc):
            g = hc // 4
            return _dot(k_ref[pl.ds(ks, BK_A), g * 128:(g + 1) * 128], w_scr[hc])

        ahead = [qk(i) for i in range(PREFETCH)]
        for hc in range(n_comb):
            h = hc // 2
            s = ahead.pop(0)
            if hc + PREFETCH < n_comb:
                ahead.append(qk(hc + PREFETCH))
            c = sc_ref[(1 - mode) * HA + h]
            cmax = jnp.max(s, axis=0, keepdims=True) + c
            m_old = m_scr[hc]
            m_new = jnp.maximum(m_old, cmax)
            alpha = jnp.exp2(m_old - m_new)
            p = jnp.exp2(s - (m_new - c))
            if mode == 0:
                p = p * near_ref[kb - near_lo, h]
            l_scr[hc] = alpha * l_scr[hc] + jnp.sum(p, axis=0, keepdims=True)
            pv = _dot(vT_ref[h * VX:h * VX + DA_V, pl.ds(ks, BK_A)], p.astype(BF16))
            acc_scr[hc] = alpha * acc_scr[hc] + pv
            m_scr[hc] = m_new

    lo = jnp.maximum(near_lo, 0)
    hi = jnp.minimum(near_lo + NEAR_A, nk)
    lax.fori_loop(0, lo, lambda kb, c: (step(kb, -1), c)[1], 0)
    lax.fori_loop(lo, hi, lambda kb, c: (step(kb, 0), c)[1], 0)
    lax.fori_loop(hi, nk, lambda kb, c: (step(kb, 1), c)[1], 0)

    lam = sc_ref[3 * HA]
    outs = []
    for h in range(HA):
        a = acc_scr[2 * h] / l_scr[2 * h] - lam * (acc_scr[2 * h + 1] / l_scr[2 * h + 1])
        a = a * lax.rsqrt(jnp.mean(a * a, axis=0, keepdims=True) + EPS)
        outs.append(a)
    o = jnp.concatenate(outs, axis=0) * subln_ref[...]
    o_ref[...] = o.T.astype(BF16)


def _sublane_partial_sum(p):
    n, w = p.shape
    return jnp.sum(p.reshape(n // 8, 8, w), axis=0)


def _attn_a_bounded_kernel(sc_ref, qT_ref, k_ref, vT_ref, near_ref, subln_ref, o_ref, w_scr, acc_scr, l_scr, *, nk):
    qb = pl.program_id(1)
    n_comb = 2 * HA
    bq = qT_ref.shape[1]
    near_lo = (BQ_A // BK_A) * qb - 1
    for hc in range(n_comb):
        g = hc // 4
        qg = qT_ref[g * 128:(g + 1) * 128, :]
        row = lax.broadcasted_iota(jnp.int32, (128, bq), 0)
        w_scr[hc] = jnp.where((row // DA_QK) == (hc % 4), qg, jnp.zeros_like(qg))
    acc_scr[...] = jnp.zeros(acc_scr.shape, F32)
    l_scr[...] = jnp.zeros(l_scr.shape, F32)

    lo = jnp.maximum(near_lo, 0)
    hi = jnp.minimum(near_lo + NEAR_A, nk)
    n_far = lo + (nk - hi)

    def step(j0, n_blocks, near):
        def block(jj):
            j = j0 + jj
            if near:
                return j, None
            right = j >= lo
            return jnp.where(right, j - lo + hi, j), jnp.where(right, HA, 0)

        tiles = [(jj, hc) for jj in range(n_blocks) for hc in range(n_comb)]

        def qk(t):
            jj, hc = tiles[t]
            g = hc // 4
            ks = pl.multiple_of(block(jj)[0] * BK_A, BK_A)
            return _dot(k_ref[pl.ds(ks, BK_A), g * 128:(g + 1) * 128], w_scr[hc])

        ahead = [qk(i) for i in range(PREFETCH)]
        for t, (jj, hc) in enumerate(tiles):
            h = hc // 2
            s = ahead.pop(0)
            if t + PREFETCH < len(tiles):
                ahead.append(qk(t + PREFETCH))
            kb, side = block(jj)
            ks = pl.multiple_of(kb * BK_A, BK_A)
            p = jnp.exp2(s)
            if near:
                p = p * near_ref[kb - near_lo, h]
            psum = _sublane_partial_sum(p)
            pv = _dot(vT_ref[h * VX:h * VX + DA_V, pl.ds(ks, BK_A)], p.astype(BF16))
            if near:
                acc_scr[hc] += pv
                l_scr[hc] += psum
            else:
                f = sc_ref[side + h]
                acc_scr[hc] += pv * f
                l_scr[hc] += psum * f

    n_pairs = n_far // UNROLL_A
    lax.fori_loop(0, n_pairs, lambda i, c: (step(i * UNROLL_A, UNROLL_A, False), c)[1], 0)
    lax.fori_loop(n_pairs * UNROLL_A, n_far, lambda j, c: (step(j, 1, False), c)[1], 0)
    lax.fori_loop(lo, hi, lambda kb, c: (step(kb, 1, True), c)[1], 0)

    lam = sc_ref[2 * HA]
    outs = []
    for h in range(HA):
        l0 = jnp.sum(l_scr[2 * h], axis=0, keepdims=True)
        l1 = jnp.sum(l_scr[2 * h + 1], axis=0, keepdims=True)
        a = acc_scr[2 * h] / l0 - lam * (acc_scr[2 * h + 1] / l1)
        outs.append(a * lax.rsqrt(jnp.mean(a * a, axis=0, keepdims=True) + EPS))
    o = jnp.concatenate(outs, axis=0) * subln_ref[...]
    o_ref[...] = o.T.astype(BF16)


def _attn_a(qT, k, vT, lw, B, S, bounded):
    nq = S // BQ_A
    nk = S // BK_A
    n_comb = 2 * HA
    if bounded:
        body = functools.partial(_attn_a_bounded_kernel, nk=nk)
        scratch = [pltpu.VMEM((n_comb, 128, BQ_A), BF16), pltpu.VMEM((n_comb, DA_V, BQ_A), F32),
                   pltpu.VMEM((n_comb, 8, BQ_A), F32)]
        sc, near = lw['sc_a_bounded'], lw['near_a_bounded']
    else:
        body = functools.partial(_attn_a_kernel, nk=nk)
        scratch = [pltpu.VMEM((n_comb, 128, BQ_A), BF16), pltpu.VMEM((n_comb, 1, BQ_A), F32),
                   pltpu.VMEM((n_comb, 1, BQ_A), F32), pltpu.VMEM((n_comb, DA_V, BQ_A), F32)]
        sc, near = lw['sc_a'], lw['near_a']
    return pl.pallas_call(
        body,
        grid=(B, nq),
        in_specs=[
            pl.BlockSpec(memory_space=pltpu.SMEM),
            pl.BlockSpec((n_comb * DA_QK, BQ_A), lambda b, q: (0, b * nq + q)),
            pl.BlockSpec((S, n_comb * DA_QK), lambda b, q: (b, 0)),
            pl.BlockSpec((HA * VX, S), lambda b, q: (0, b)),
            pl.BlockSpec(near.shape, lambda b, q: (0, 0, 0, 0)),
            pl.BlockSpec((HA * DA_V, 1), lambda b, q: (0, 0)),
        ],
        out_specs=pl.BlockSpec((BQ_A, HA * DA_V), lambda b, q: (b * nq + q, 0)),
        out_shape=jax.ShapeDtypeStruct((B * S, HA * DA_V), BF16),
        scratch_shapes=scratch,
        compiler_params=pltpu.CompilerParams(dimension_semantics=("arbitrary", "arbitrary"),
                                             vmem_limit_bytes=VMEM_LIMIT),
        name="attn_a_bounded" if bounded else "attn_a",
    )(sc, qT, k, vT, near, lw['subln'])


def _attn_b_kernel(qT_ref, k_ref, vT_ref, o_ref, m_scr, l_scr, acc_scr, *, nk):
    m_scr[...] = jnp.full(m_scr.shape, NEG_BIG, F32)
    l_scr[...] = jnp.zeros(l_scr.shape, F32)
    acc_scr[...] = jnp.zeros(acc_scr.shape, F32)

    def step(kb, carry):
        ks = pl.multiple_of(kb * BK_B, BK_B)

        def qk(h):
            return _dot(k_ref[pl.ds(ks, BK_B), h * DB_PAD:(h + 1) * DB_PAD], qT_ref[h * DB_PAD:(h + 1) * DB_PAD, :])

        ahead = [qk(i) for i in range(PREFETCH)]
        for h in range(HB):
            s = ahead.pop(0)
            if h + PREFETCH < HB:
                ahead.append(qk(h + PREFETCH))
            m_old = m_scr[h]
            m_new = jnp.maximum(m_old, jnp.max(s, axis=0, keepdims=True))
            alpha = jnp.exp2(m_old - m_new)
            p = jnp.exp2(s - m_new)
            l_scr[h] = alpha * l_scr[h] + jnp.sum(p, axis=0, keepdims=True)
            pv = _dot(vT_ref[h * VX:h * VX + DB_V, pl.ds(ks, BK_B)], p.astype(BF16))
            acc_scr[h] = alpha * acc_scr[h] + pv
            m_scr[h] = m_new
        return carry

    lax.fori_loop(0, nk, step, 0)
    o = jnp.concatenate([acc_scr[h] / l_scr[h] for h in range(HB)], axis=0)
    o_ref[...] = o.T.astype(BF16)


def _attn_b_bounded_kernel(qT_ref, k_ref, vT_ref, o_ref, acc_scr, l_scr, *, nk):
    acc_scr[...] = jnp.zeros(acc_scr.shape, F32)
    l_scr[...] = jnp.zeros(l_scr.shape, F32)

    assert nk % UNROLL_B == 0
    tiles = [(j, h) for j in range(UNROLL_B) for h in range(HB)]

    def step(it, carry):
        def ks(j):
            return pl.multiple_of((it * UNROLL_B + j) * BK_B, BK_B)

        def qk(t):
            j, h = tiles[t]
            return _dot(k_ref[pl.ds(ks(j), BK_B), h * DB_PAD:(h + 1) * DB_PAD],
                        qT_ref[h * DB_PAD:(h + 1) * DB_PAD, :])

        ahead = [qk(i) for i in range(PREFETCH)]
        for t, (j, h) in enumerate(tiles):
            s = ahead.pop(0)
            if t + PREFETCH < len(tiles):
                ahead.append(qk(t + PREFETCH))
            p = jnp.exp2(s)
            l_scr[h] += _sublane_partial_sum(p)
            acc_scr[h] += _dot(vT_ref[h * VX:h * VX + DB_V, pl.ds(ks(j), BK_B)], p.astype(BF16))
        return carry

    lax.fori_loop(0, nk // UNROLL_B, step, 0)
    o = jnp.concatenate([acc_scr[h] / jnp.sum(l_scr[h], axis=0, keepdims=True) for h in range(HB)], axis=0)
    o_ref[...] = o.T.astype(BF16)


def _attn_b(qT, k, vT, B, S, bounded):
    nq = S // BQ_B
    nk = S // BK_B
    if bounded:
        body = functools.partial(_attn_b_bounded_kernel, nk=nk)
        scratch = [pltpu.VMEM((HB, DB_V, BQ_B), F32), pltpu.VMEM((HB, 8, BQ_B), F32)]
    else:
        body = functools.partial(_attn_b_kernel, nk=nk)
        scratch = [pltpu.VMEM((HB, 1, BQ_B), F32), pltpu.VMEM((HB, 1, BQ_B), F32), pltpu.VMEM((HB, DB_V, BQ_B), F32)]
    return pl.pallas_call(
        body,
        grid=(B, nq),
        in_specs=[
            pl.BlockSpec((HB * DB_PAD, BQ_B), lambda b, q: (0, b * nq + q)),
            pl.BlockSpec((S, HB * DB_PAD), lambda b, q: (b, 0)),
            pl.BlockSpec((HB * VX, S), lambda b, q: (0, b)),
        ],
        out_specs=pl.BlockSpec((BQ_B, HB * DB_V), lambda b, q: (b * nq + q, 0)),
        out_shape=jax.ShapeDtypeStruct((B * S, HB * DB_V), BF16),
        scratch_shapes=scratch,
        compiler_params=pltpu.CompilerParams(dimension_semantics=("arbitrary", "arbitrary"),
                                             vmem_limit_bytes=VMEM_LIMIT),
        name="attn_b_bounded" if bounded else "attn_b",
    )(qT, k, vT)


def _attn_c_kernel(sc_ref, qT_ref, k_ref, vT_ref, bm_ref, o_ref, *, nb, S):
    n = pl.program_id(1)
    start = pl.multiple_of(jnp.clip((n - 1) * BLOCK, 0, S - KW_C), BLOCK)
    case = jnp.where(n == 0, 0, jnp.where(n == nb - 1, 2, 1))
    kwin = k_ref[pl.ds(start, KW_C), :]
    outs = []
    for h in range(HC):
        j = h // G_C
        qh = qT_ref[h * DC:(h + 1) * DC, :]
        zz = jnp.zeros_like(qh)
        w = jnp.concatenate([qh, zz] if j == 0 else [zz, qh], axis=0)
        s = _dot(kwin, w) + bm_ref[case, h]
        sink = sc_ref[h]
        m = jnp.maximum(jnp.max(s, axis=0, keepdims=True), sink)
        p = jnp.exp2(s - m)
        den = jnp.sum(p, axis=0, keepdims=True) + jnp.exp2(sink - m)
        pv = _dot(vT_ref[j * VX:j * VX + DC, pl.ds(start, KW_C)], p.astype(BF16))
        outs.append(pv / den)
    o_ref[...] = jnp.concatenate(outs, axis=0).T.astype(BF16)


def _attn_c_bounded_kernel(sc_ref, qT_ref, k_ref, vT_ref, e_ref, o_ref, *, n_sub, S):
    g = pl.program_id(1)
    tiles = [(u, h) for u in range(SUB_C) for h in range(HC)]

    def window(u):
        n = g * SUB_C + u
        start = pl.multiple_of(jnp.clip(n * QB_C - WINDOW, 0, S - KWB_C), WINDOW)
        case = jnp.where(n == 0, 0, jnp.where(n == n_sub - 1, 2, 1))
        return start, case

    def qk(t):
        u, h = tiles[t]
        start, _ = window(u)
        qh = qT_ref[h * DC:(h + 1) * DC, u * QB_C:(u + 1) * QB_C]
        zz = jnp.zeros_like(qh)
        w = jnp.concatenate([qh, zz] if h // G_C == 0 else [zz, qh], axis=0)
        return _dot(k_ref[pl.ds(start, KWB_C), :], w)

    ahead = [qk(i) for i in range(PREFETCH)]
    outs = []
    for t, (u, h) in enumerate(tiles):
        s = ahead.pop(0)
        if t + PREFETCH < len(tiles):
            ahead.append(qk(t + PREFETCH))
        start, case = window(u)
        j = h // G_C
        p = (jnp.exp2(s) * e_ref[case, h]).astype(BF16)
        pv = _dot(vT_ref[j * VX:(j + 1) * VX, pl.ds(start, KWB_C)], p)
        outs.append(pv[0:DC] / (pv[DC:DC + 1] + sc_ref[HC + h]))
        if h == HC - 1:
            o_ref[u * QB_C:(u + 1) * QB_C, :] = jnp.concatenate(outs, axis=0).T.astype(BF16)
            outs = []


def _attn_c(qT, k, vT, lw, B, S, bounded):
    if bounded:
        n_sub = S // QB_C
        assert n_sub >= 3 and n_sub % SUB_C == 0
        nb, bq = n_sub // SUB_C, QB_C * SUB_C
        body = functools.partial(_attn_c_bounded_kernel, n_sub=n_sub, S=S)
        tile, sc = lw['e_c'], lw['sc_c_bounded']
    else:
        nb, bq = S // BLOCK, BLOCK
        assert nb >= 3
        body = functools.partial(_attn_c_kernel, nb=nb, S=S)
        tile, sc = lw['bm_c'], lw['sc_c']
    return pl.pallas_call(
        body,
        grid=(B, nb),
        in_specs=[
            pl.BlockSpec(memory_space=pltpu.SMEM),
            pl.BlockSpec((HC * DC, bq), lambda b, n: (0, b * nb + n)),
            pl.BlockSpec((S, KVH_C * DC), lambda b, n: (b, 0)),
            pl.BlockSpec((KVH_C * VX, S), lambda b, n: (0, b)),
            pl.BlockSpec(tile.shape, lambda b, n: (0, 0, 0, 0)),
        ],
        out_specs=pl.BlockSpec((bq, HC * DC), lambda b, n: (b * nb + n, 0)),
        out_shape=jax.ShapeDtypeStruct((B * S, HC * DC), BF16),
        compiler_params=pltpu.CompilerParams(dimension_semantics=("arbitrary", "arbitrary"),
                                             vmem_limit_bytes=VMEM_LIMIT),
        name="attn_c_bounded" if bounded else "attn_c",
    )(sc, qT, k, vT, tile)


def _out_kernel(x_ref, ya_ref, yb_ref, yc_ref, wo_ref, g_ref, wr_ref, xn_ref, h2_ref, affT_ref):
    na, nb_ = HA * DA_V, HA * DA_V + HB * DB_V
    xn = (x_ref[...] + _dot(ya_ref[...], wo_ref[0:na, :]) + _dot(yb_ref[...], wo_ref[na:nb_, :])
          + _dot(yc_ref[...], wo_ref[nb_:, :]))
    xn_ref[...] = xn
    h2 = (xn * lax.rsqrt(jnp.mean(xn * xn, axis=-1, keepdims=True) + EPS) * g_ref[...]).astype(BF16)
    h2_ref[...] = h2
    lg = _dot(h2, wr_ref[...]).T[0:N_EXPERTS]
    e = jnp.exp(lg - jnp.max(lg, axis=0, keepdims=True))
    affT_ref[...] = e / jnp.sum(e, axis=0, keepdims=True)


def _out_proj(x2, ya, yb, yc, lw):
    T = x2.shape[0]
    tm = TM_OUT

    def full(a):
        return pl.BlockSpec(a.shape, lambda i: (0,) * a.ndim)

    return pl.pallas_call(
        _out_kernel,
        grid=(T // tm,),
        in_specs=[pl.BlockSpec((tm, D_MODEL), lambda i: (i, 0)),
                  pl.BlockSpec((tm, ya.shape[1]), lambda i: (i, 0)),
                  pl.BlockSpec((tm, yb.shape[1]), lambda i: (i, 0)),
                  pl.BlockSpec((tm, yc.shape[1]), lambda i: (i, 0)),
                  full(lw['w_out']), full(lw['g_ffn']), full(lw['w_r'])],
        out_specs=[pl.BlockSpec((tm, D_MODEL), lambda i: (i, 0)),
                   pl.BlockSpec((tm, D_MODEL), lambda i: (i, 0)),
                   pl.BlockSpec((N_EXPERTS, tm), lambda i: (0, i))],
        out_shape=[jax.ShapeDtypeStruct((T, D_MODEL), F32),
                   jax.ShapeDtypeStruct((T, D_MODEL), BF16),
                   jax.ShapeDtypeStruct((N_EXPERTS, T), F32)],
        compiler_params=pltpu.CompilerParams(dimension_semantics=("arbitrary",), vmem_limit_bytes=VMEM_LIMIT),
        name="out_proj",
    )(x2, ya, yb, yc, lw['w_out'], lw['g_ffn'], lw['w_r'])


def _topk_kernel(aff_ref, thr_ref, tcut_ref, *, cap, T):
    capf = float(cap)

    def bits():
        return lax.bitcast_convert_type(aff_ref[...], jnp.int32)

    def count(mask):
        return jnp.sum(jnp.where(mask, 1.0, 0.0), axis=1, keepdims=True)

    def vbody(i, v):
        cand = v | jnp.left_shift(jnp.int32(1), 30 - i)
        return jnp.where(count(bits() >= cand) >= capf, cand, v)

    thr = lax.fori_loop(0, 31, vbody, jnp.zeros((N_EXPERTS, 1), jnp.int32))
    need = capf - count(bits() > thr)
    nbits = max(1, int(math.ceil(math.log2(T))))

    def tbody(i, c):
        cand = c | jnp.left_shift(jnp.int32(1), nbits - 1 - i)
        idx = lax.broadcasted_iota(jnp.int32, (N_EXPERTS, T), 1)
        f = count((bits() == thr) & (idx < cand))
        return jnp.where(f < need, cand, c)

    tcut = lax.fori_loop(0, nbits, tbody, jnp.zeros((N_EXPERTS, 1), jnp.int32))
    thr_ref[...] = jnp.broadcast_to(thr, thr_ref.shape)
    tcut_ref[...] = jnp.broadcast_to(tcut, tcut_ref.shape)


def _topk_thresholds(affT):
    E, T = affT.shape
    cap = CAPACITY_FACTOR * T // N_EXPERTS
    return pl.pallas_call(
        functools.partial(_topk_kernel, cap=cap, T=T),
        grid=(1,),
        in_specs=[pl.BlockSpec((E, T), lambda i: (0, 0))],
        out_specs=[pl.BlockSpec((E, 128), lambda i: (0, 0))] * 2,
        out_shape=[jax.ShapeDtypeStruct((E, 128), jnp.int32)] * 2,
        compiler_params=pltpu.CompilerParams(dimension_semantics=("arbitrary",), vmem_limit_bytes=VMEM_LIMIT),
        name="topk_thr",
    )(affT)


def _exclusive_prefix_count(sel, tri):
    e, n = sel.shape
    ones = jnp.where(sel, 1.0, 0.0)
    out, before = [], jnp.zeros((e, 1), F32)
    for kt in range(n // 128):
        blk = ones[:, kt * 128:(kt + 1) * 128]
        out.append(_dot(blk.astype(BF16), tri) + before)
        before = before + jnp.sum(blk, axis=1, keepdims=True)
    return jnp.concatenate(out, axis=1)


def _moe_kernel(x_ref, h2_ref, affT_ref, thr_ref, tcut_ref, tri_ref, wg_ref, wu_ref, wd_ref, o_ref, pos_scr, xe_scr):
    i = pl.program_id(0)
    e = pl.program_id(1)
    tt = TT_MOE

    def one_hot_rows(posm, c):
        rel = posm - (c * CH_MOE).astype(F32)
        jj = lax.broadcasted_iota(jnp.int32, (CH_MOE, tt), 0).astype(F32)
        return jnp.broadcast_to(rel, (CH_MOE, tt)) == jj

    @pl.when(e == 0)
    def _():
        b = lax.bitcast_convert_type(affT_ref[...], jnp.int32)
        thr = thr_ref[:, 0:1]
        tc = tcut_ref[:, 0:1]
        tg = i * (tt * SUBS_MOE) + lax.broadcasted_iota(jnp.int32, b.shape, 1)
        sel = (b > thr) | ((b == thr) & (tg <= tc))
        for sub in range(SUBS_MOE):
            sl = slice(sub * tt, (sub + 1) * tt)
            pos = _exclusive_prefix_count(sel[:, sl], tri_ref[...])
            posm_all = jnp.where(sel[:, sl], pos, -1.0)
            pos_scr[:, sl] = posm_all
            for grp in range(N_EXPERTS // GATHER_GROUP):
                hits = [one_hot_rows(posm_all[ee:ee + 1], jnp.int32(0))
                        for ee in range(grp * GATHER_GROUP, (grp + 1) * GATHER_GROUP)]
                onehot = jnp.where(jnp.concatenate(hits, axis=0), 1.0, 0.0).astype(BF16)
                rows = slice(grp * GATHER_GROUP * CH_MOE, (grp + 1) * GATHER_GROUP * CH_MOE)
                xe_scr[sub, rows, :] = _dot(onehot, h2_ref[sl, :]).astype(BF16)
        o_ref[...] = x_ref[...]

    posm = [pos_scr[pl.ds(e, 1), sub * tt:(sub + 1) * tt] for sub in range(SUBS_MOE)]
    gate = [affT_ref[pl.ds(e, 1), sub * tt:(sub + 1) * tt] for sub in range(SUBS_MOE)]
    n_sel = jnp.max(jnp.concatenate(posm, axis=0)) + 1.0
    n_ch = (n_sel.astype(jnp.int32) + CH_MOE - 1) // CH_MOE

    def chunk(c, pregathered):
        hits, xes = [], []
        for sub in range(SUBS_MOE):
            hit = one_hot_rows(posm[sub], c)
            if pregathered:
                xes.append(xe_scr[sub, pl.ds(pl.multiple_of(e * CH_MOE, CH_MOE), CH_MOE), :])
            else:
                xes.append(_dot(jnp.where(hit, 1.0, 0.0).astype(BF16),
                                h2_ref[sub * tt:(sub + 1) * tt, :]).astype(BF16))
            hits.append(hit)
        xe = jnp.concatenate(xes, axis=0)
        g = _dot(xe, wg_ref[...])
        u = _dot(xe, wu_ref[...])
        hid = (g * jax.nn.sigmoid(g) * u).astype(BF16)
        ye = _dot(hid, wd_ref[...])
        for sub in range(SUBS_MOE):
            gc = jnp.sum(jnp.where(hits[sub], jnp.broadcast_to(gate[sub], hits[sub].shape), 0.0),
                         axis=1, keepdims=True)
            yeb = jnp.concatenate([(ye[sub * CH_MOE:(sub + 1) * CH_MOE] * gc).astype(BF16),
                                   jnp.zeros((CHP_MOE - CH_MOE, ye.shape[1]), BF16)], axis=0)
            onehot = jnp.concatenate([jnp.where(hits[sub], 1.0, 0.0).astype(BF16),
                                      jnp.zeros((CHP_MOE - CH_MOE, tt), BF16)], axis=0)
            o_ref[sub * tt:(sub + 1) * tt, :] += _dot(onehot.T, yeb)

    chunk(jnp.int32(0), True)
    lax.fori_loop(1, n_ch, lambda c, carry: (chunk(c, False), carry)[1], 0)


def _moe(xn, h2, affT, thr, tcut, lw):
    T = xn.shape[0]
    tt = TT_MOE * SUBS_MOE
    once = pl.Buffered(1)
    return pl.pallas_call(
        _moe_kernel,
        grid=(T // tt, N_EXPERTS),
        in_specs=[
            pl.BlockSpec((tt, D_MODEL), lambda i, e: (i, 0), pipeline_mode=once),
            pl.BlockSpec((tt, D_MODEL), lambda i, e: (i, 0), pipeline_mode=once),
            pl.BlockSpec((N_EXPERTS, tt), lambda i, e: (0, i)),
            pl.BlockSpec((N_EXPERTS, 128), lambda i, e: (0, 0)),
            pl.BlockSpec((N_EXPERTS, 128), lambda i, e: (0, 0)),
            pl.BlockSpec((128, 128), lambda i, e: (0, 0), pipeline_mode=once),
            pl.BlockSpec((None, D_MODEL, D_EXPERT), lambda i, e: (e, 0, 0)),
            pl.BlockSpec((None, D_MODEL, D_EXPERT), lambda i, e: (e, 0, 0)),
            pl.BlockSpec((None, D_EXPERT, D_MODEL), lambda i, e: (e, 0, 0)),
        ],
        out_specs=pl.BlockSpec((tt, D_MODEL), lambda i, e: (i, 0), pipeline_mode=once),
        out_shape=jax.ShapeDtypeStruct((T, D_MODEL), F32),
        scratch_shapes=[pltpu.VMEM((N_EXPERTS, tt), F32),
                        pltpu.VMEM((SUBS_MOE, N_EXPERTS * CH_MOE, D_MODEL), BF16)],
        compiler_params=pltpu.CompilerParams(dimension_semantics=("arbitrary", "arbitrary"),
                                             vmem_limit_bytes=VMEM_LIMIT),
        name="moe_ffn",
    )(xn, h2, affT, thr, tcut, lw['tri'], lw['w_gate'], lw['w_up'], lw['w_down'])


def _t5_bucket(rel):
    nb = NUM_BUCKETS // 2
    max_exact = nb // 2
    ret = jnp.where(rel > 0, nb, 0)
    n = jnp.abs(rel)
    nf = jnp.maximum(n, 1).astype(jnp.float32)
    large = max_exact + (jnp.log(nf / max_exact) / math.log(MAX_DISTANCE / max_exact) * (nb - max_exact)).astype(jnp.int32)
    large = jnp.minimum(large, nb - 1)
    return ret + jnp.where(n < max_exact, n, large)


def _col(v):
    return v.astype(F32).reshape(-1, 1)


def _pad_heads(w, n_heads, d, d_pad):
    k = w.shape[1]
    w3 = w.reshape(n_heads, d, k)
    return jnp.pad(w3, ((0, 0), (0, d_pad - d), (0, 0))).reshape(n_heads * d_pad, k)


def _toeplitz_tiles(fn, offsets, nk, nq):
    span = nk + nq
    out = []
    for off in offsets:
        u = fn(off + nk - 1 - jnp.arange(span, dtype=jnp.int32))
        rows = jnp.tile(u, (1, nk))[:, :nk * (span - 1)].reshape(u.shape[0], nk, span - 1)
        out.append(rows[:, :, nk - 1:nk - 1 + nq])
    return jnp.stack(out, axis=0)


def _prep_shared(rel_bias):
    rb = rel_bias.astype(F32) * LOG2E

    def bias_a(rel):
        return rb[:, :HA][_t5_bucket(rel)].T

    def bias_c(rel):
        return rb[:, HA:][_t5_bucket(rel)].T

    def in_band(rel):
        return (jnp.abs(rel) <= WINDOW)[None, :]

    bmax = jnp.max(rb[:, :HA], axis=0)
    near_offs = [d * BK_A for d in range(-1, NEAR_A - 1)]
    near_a = _toeplitz_tiles(lambda rel: jnp.exp2(bias_a(rel) - bmax[:, None]), near_offs, BK_A, BQ_A)
    far = _t5_bucket(jnp.array([-(1 << 20), 1 << 20], dtype=jnp.int32))
    sc_far = jnp.concatenate([rb[far[1], :HA], bmax, rb[far[0], :HA]])
    a_bounded = {
        'near': _toeplitz_tiles(lambda rel: jnp.exp2(bias_a(rel)), near_offs, BK_A, BQ_A),
        'sc': jnp.concatenate([jnp.exp2(rb[far[0], :HA]), jnp.exp2(rb[far[1], :HA])]),
        'bias_mag': jnp.max(jnp.abs(rb[:, :HA])),
    }
    bm_c = _toeplitz_tiles(lambda rel: jnp.where(in_band(rel), bias_c(rel), NEG_BIG),
                           [0, -BLOCK, -2 * BLOCK], KW_C, BLOCK)
    c_bounded = {
        'e': _toeplitz_tiles(lambda rel: jnp.where(in_band(rel), jnp.exp2(bias_c(rel)), 0.0),
                             [0, -WINDOW, -(KWB_C - QB_C)], KWB_C, QB_C),
        'bias_mag': jnp.max(jnp.abs(rb[:, HA:])),
    }
    return near_a, sc_far, bm_c, a_bounded, c_bounded


def _rope_tables(S):
    inv = 1.0 / (ROPE_BASE ** (jnp.arange(ROPE_HALF, dtype=jnp.float32) / ROPE_HALF))
    ang = jnp.arange(S, dtype=jnp.int32).astype(jnp.float32)[:, None] * inv[None, :]
    return jnp.cos(ang).T, jnp.sin(ang).T


def _logit_bounds(l, p, a_bounded, c_bounded, sa, sb, scc):
    qa, ka = p['qn_a'][l].astype(F32), p['kn_a'][l].astype(F32)
    bound_a = 1.01 * DA_QK * jnp.max(jnp.abs(qa * ka)) * sa + 2.0 * a_bounded['bias_mag']
    qb, kb = p['qn_b'][l].astype(F32), p['kn_b'][l].astype(F32)
    bound_b = 1.01 * DB_QK * jnp.max(jnp.abs(qb)) * jnp.max(jnp.abs(kb)) * sb
    qc, kc = p['qn_c'][l].astype(F32), p['kn_c'][l].astype(F32)
    bound_c = (1.01 * DC * jnp.max(jnp.abs(qc * kc)) * scc + 2.0 * c_bounded['bias_mag']
               + jnp.max(jnp.abs(p['sink_c'][l].astype(F32))) * LOG2E)
    return bound_a <= BOUND_LIMIT, bound_b <= BOUND_LIMIT, bound_c <= BOUND_LIMIT


def _prep_layer(l, p, shared):
    near_a, sc_far, bm_c, a_bounded, c_bounded = shared
    lam_init = 0.8 - 0.6 * math.exp(-0.3 * l)
    lam = (jnp.exp(jnp.sum(p['lam_q1'][l].astype(F32) * p['lam_k1'][l].astype(F32)))
           - jnp.exp(jnp.sum(p['lam_q2'][l].astype(F32) * p['lam_k2'][l].astype(F32))) + lam_init)
    w_uq = p['w_uq'][l]
    w_ukv = p['w_ukv'][l].reshape(KV_RANK, HB, DB_NOPE + DB_V)
    sa = DA_QK ** -0.5 * LOG2E
    sb = DB_QK ** -0.5 * LOG2E
    scc = DC ** -0.5 * LOG2E
    zpad = jnp.zeros((DB_PAD - DB_QK,), F32)
    lw = {
        'g_mix': p['norm_mix_g'][l].astype(F32).reshape(1, D_MODEL),
        'w_inT': p['w_in'][l].T.astype(BF16),
        'w_uqT': _pad_heads(w_uq.T, HB, DB_QK, DB_PAD).astype(BF16),
        'w_ukT': w_ukv[:, :, :DB_NOPE].reshape(KV_RANK, HB * DB_NOPE).T.astype(BF16),
        'w_uvT': w_ukv[:, :, DB_NOPE:].reshape(KV_RANK, HB * DB_V).T.astype(BF16),
        'gqa': _col(jnp.tile(p['qn_a'][l], 2 * HA) * sa),
        'gka': _col(jnp.tile(p['kn_a'][l], 2 * HA)),
        'gcq': _col(p['g_cq'][l]),
        'gckv': _col(p['g_ckv'][l]),
        'gqb': _col(jnp.tile(jnp.concatenate([p['qn_b'][l].astype(F32) * sb, zpad]), HB)),
        'gkb': _col(jnp.tile(jnp.concatenate([p['kn_b'][l].astype(F32), zpad]), HB)),
        'gqc': _col(jnp.tile(p['qn_c'][l], HC) * scc),
        'gkc': _col(jnp.tile(p['kn_c'][l], KVH_C)),
        'near_a': near_a,
        'sc_a': jnp.concatenate([sc_far, lam.reshape(1)]).astype(F32),
        'subln': _col(jnp.tile(p['subln_a'][l], HA) * (1.0 - lam_init)),
        'bm_c': bm_c,
        'sc_c': p['sink_c'][l].astype(F32) * LOG2E,
        'w_out': p['w_out'][l].astype(BF16),
        'g_ffn': p['norm_ffn_g'][l].astype(F32).reshape(1, D_MODEL),
        'w_r': jnp.pad(p['w_router'][l], ((0, 0), (0, 128 - N_EXPERTS))).astype(BF16),
        'w_gate': p['w_gate'][l].astype(BF16),
        'w_up': p['w_up'][l].astype(BF16),
        'w_down': p['w_down'][l].astype(BF16),
        'tri': jnp.triu(jnp.ones((128, 128), F32), k=1).astype(BF16),
        'near_a_bounded': a_bounded['near'],
        'sc_a_bounded': jnp.concatenate([a_bounded['sc'], lam.reshape(1)]).astype(F32),
    }
    lw['e_c'] = c_bounded['e']
    lw['sc_c_bounded'] = jnp.concatenate([lw['sc_c'], jnp.exp2(lw['sc_c'])])
    lw['bounded_a'], lw['bounded_b'], lw['bounded_c'] = _logit_bounds(l, p, a_bounded, c_bounded, sa, sb, scc)
    return lw


def _trunk(x, layers):
    B, S, D = x.shape
    x2 = x.reshape(B * S, D)
    cosT, sinT = _rope_tables(S)
    for lw in layers:
        lw = dict(lw, cosT=cosT, sinT=sinT)
        qaT, ka, vaT, qbT, kb, vbT, qcT, kc, vcT = _in_proj(x2, lw, S)
        ya = lax.cond(lw['bounded_a'],
                      lambda q, k, v: _attn_a(q, k, v, lw, B, S, True),
                      lambda q, k, v: _attn_a(q, k, v, lw, B, S, False), qaT, ka, vaT)
        yb = lax.cond(lw['bounded_b'],
                      lambda q, k, v: _attn_b(q, k, v, B, S, True),
                      lambda q, k, v: _attn_b(q, k, v, B, S, False), qbT, kb, vbT)
        yc = lax.cond(lw['bounded_c'],
                      lambda q, k, v: _attn_c(q, k, v, lw, B, S, True),
                      lambda q, k, v: _attn_c(q, k, v, lw, B, S, False), qcT, kc, vcT)
        xn, h2, affT = _out_proj(x2, ya, yb, yc, lw)
        thr, tcut = _topk_thresholds(affT)
        x2 = _moe(xn, h2, affT, thr, tcut, lw)
    return x2.reshape(B, S, D)


def kernel(x_prompt, x_sample, rel_bias, norm_mix_g, w_in, qn_a, kn_a, lam_q1, lam_k1, lam_q2, lam_k2, subln_a, g_cq, w_uq, g_ckv, w_ukv, qn_b, kn_b, qn_c, kn_c, sink_c, w_out, norm_ffn_g, w_router, w_gate, w_up, w_down):
    p = dict(norm_mix_g=norm_mix_g, w_in=w_in, qn_a=qn_a, kn_a=kn_a, lam_q1=lam_q1, lam_k1=lam_k1,
             lam_q2=lam_q2, lam_k2=lam_k2, subln_a=subln_a, g_cq=g_cq, w_uq=w_uq, g_ckv=g_ckv, w_ukv=w_ukv,
             qn_b=qn_b, kn_b=kn_b, qn_c=qn_c, kn_c=kn_c, sink_c=sink_c, w_out=w_out, norm_ffn_g=norm_ffn_g,
             w_router=w_router, w_gate=w_gate, w_up=w_up, w_down=w_down)
    shared = _prep_shared(rel_bias)
    layers = [_prep_layer(l, p, shared) for l in range(w_in.shape[0])]
    return (_trunk(x_prompt, layers), _trunk(x_sample, layers))
```

```python
import functools
import math

import jax
import jax.numpy as jnp
import numpy as np
from jax import lax
from jax.experimental import pallas as pl
from jax.experimental.pallas import tpu as pltpu

D_MODEL = 1024
BLOCK = 128
HA, DA_QK, DA_V = 4, 32, 64
HB, Q_RANK, KV_RANK, DB_NOPE, DB_ROPE, DB_V = 6, 384, 256, 64, 32, 64
ROPE_BASE = 10000.0
HC, KVH_C, DC, WINDOW = 6, 2, 64, 128
G_C = HC // KVH_C
NUM_BUCKETS, MAX_DISTANCE = 32, 128
N_EXPERTS, D_EXPERT, CAPACITY_FACTOR = 16, 1024, 2
EPS = 1e-6

DB_QK = DB_NOPE + DB_ROPE
DB_PAD = 128
VX = 80
BOUND_LIMIT = 80.0
ROPE_HALF = DB_ROPE // 2
IN_SPLITS = (HA * 2 * DA_QK, HA * 2 * DA_QK, HA * DA_V, Q_RANK, KV_RANK, DB_ROPE, HC * DC, KVH_C * DC, KVH_C * DC)
IN_OFFS = tuple(int(v) for v in np.cumsum((0,) + IN_SPLITS))
IN_COLS = IN_OFFS[-1]
LOG2E = 1.4426950408889634
NEG_BIG = -1e30

F32 = jnp.float32
BF16 = jnp.bfloat16

TM_IN = 1024
TM_OUT = 1024
BQ_A = 512
BK_A = 512
UNROLL_A = 4
NEAR_A = BQ_A // BK_A + 2
assert BQ_A % BK_A == 0 and BK_A >= MAX_DISTANCE
BQ_B = 512
BK_B = 512
UNROLL_B = 8
KW_C = 3 * BLOCK
QB_C = 256
KWB_C = QB_C + 2 * WINDOW
SUB_C = 2
TT_MOE = 1024
SUBS_MOE = 2
GATHER_GROUP = 8
CH_MOE = 160
CHP_MOE = 256
PREFETCH = 3
VMEM_LIMIT = 56 * 1024 * 1024


def _dot(a, b):
    return jnp.dot(a, b, preferred_element_type=F32)


def _group_norm_rows(y, gs, n_valid):
    rows, tm = y.shape
    y3 = y.reshape(rows // gs, gs, tm)
    ms = jnp.sum(y3 * y3, axis=1, keepdims=True) * (1.0 / n_valid)
    return (y3 * lax.rsqrt(ms + EPS)).reshape(rows, tm)


def _in_kernel(x_ref, g_ref, winT_ref, wuqT_ref, wukT_ref, wuvT_ref,
               gqa_ref, gka_ref, gcq_ref, gckv_ref, gqb_ref, gkb_ref, gqc_ref, gkc_ref,
               cos_ref, sin_ref,
               qaT_ref, ka_ref, vaT_ref, qbT_ref, kb_ref, vbT_ref, qcT_ref, kc_ref, vcT_ref):
    x = x_ref[...]
    h = x * lax.rsqrt(jnp.mean(x * x, axis=-1, keepdims=True) + EPS) * g_ref[...]
    projT = _dot(winT_ref[...], h.T.astype(BF16))

    def seg(i):
        return projT[IN_OFFS[i]:IN_OFFS[i + 1]]

    cos = cos_ref[...]
    sin = sin_ref[...]

    def with_ones_rows(v, n_heads, d):
        tm = v.shape[1]
        tail = jnp.where(lax.broadcasted_iota(jnp.int32, (VX - d, tm), 0) == 0, 1.0, 0.0)
        rows = []
        for hh in range(n_heads):
            rows += [v[hh * d:(hh + 1) * d], tail]
        return jnp.concatenate(rows, axis=0).astype(BF16)

    qaT_ref[...] = (_group_norm_rows(seg(0), DA_QK, DA_QK) * gqa_ref[...]).astype(BF16)
    kaT = _group_norm_rows(seg(1), DA_QK, DA_QK) * gka_ref[...]
    ka_ref[...] = kaT.T.astype(BF16)
    vaT_ref[...] = with_ones_rows(seg(2), HA, DA_V)

    def rope_store(dst_rows, y, h):
        o = h * DB_PAD
        x1 = y[o + DB_NOPE:o + DB_NOPE + ROPE_HALF]
        x2 = y[o + DB_NOPE + ROPE_HALF:o + DB_QK]
        dst_rows.append(y[o:o + DB_NOPE])
        dst_rows.append(x1 * cos - x2 * sin)
        dst_rows.append(x2 * cos + x1 * sin)
        dst_rows.append(jnp.zeros((DB_PAD - DB_QK, y.shape[1]), F32))

    cqT = seg(3)
    cqn = (cqT * lax.rsqrt(jnp.mean(cqT * cqT, axis=0, keepdims=True) + EPS) * gcq_ref[...]).astype(BF16)
    qf = _group_norm_rows(_dot(wuqT_ref[...], cqn), DB_PAD, DB_QK) * gqb_ref[...]
    rows = []
    for hh in range(HB):
        rope_store(rows, qf, hh)
    qbT_ref[...] = jnp.concatenate(rows, axis=0).astype(BF16)

    ckvT = seg(4)
    ckvn = (ckvT * lax.rsqrt(jnp.mean(ckvT * ckvT, axis=0, keepdims=True) + EPS) * gckv_ref[...]).astype(BF16)
    knope = _dot(wukT_ref[...], ckvn)
    vbT_ref[...] = with_ones_rows(_dot(wuvT_ref[...], ckvn), HB, DB_V)
    krope = seg(5)
    zpad = jnp.zeros((DB_PAD - DB_QK, krope.shape[1]), F32)
    rows = []
    for hh in range(HB):
        rows += [knope[hh * DB_NOPE:(hh + 1) * DB_NOPE], krope, zpad]
    kf = _group_norm_rows(jnp.concatenate(rows, axis=0), DB_PAD, DB_QK) * gkb_ref[...]
    rows = []
    for hh in range(HB):
        rope_store(rows, kf, hh)
    kb_ref[...] = jnp.concatenate(rows, axis=0).T.astype(BF16)

    qcT_ref[...] = (_group_norm_rows(seg(6), DC, DC) * gqc_ref[...]).astype(BF16)
    kcT = _group_norm_rows(seg(7), DC, DC) * gkc_ref[...]
    kc_ref[...] = kcT.T.astype(BF16)
    vcT_ref[...] = with_ones_rows(seg(8), KVH_C, DC)


def _in_proj(x2, lw, S):
    T = x2.shape[0]
    tm = TM_IN
    nt = T // tm
    spb = S // tm

    def full(a):
        return pl.BlockSpec(a.shape, lambda i: (0,) * a.ndim)

    consts = [lw['g_mix'], lw['w_inT'], lw['w_uqT'], lw['w_ukT'], lw['w_uvT'],
              lw['gqa'], lw['gka'], lw['gcq'], lw['gckv'], lw['gqb'], lw['gkb'], lw['gqc'], lw['gkc']]
    in_specs = ([pl.BlockSpec((tm, D_MODEL), lambda i: (i, 0))] + [full(a) for a in consts]
                + [pl.BlockSpec((ROPE_HALF, tm), lambda i: (0, i % spb))] * 2)

    def fm(rows):
        return jax.ShapeDtypeStruct((rows, T), BF16), pl.BlockSpec((rows, tm), lambda i: (0, i))

    def tk(cols):
        return jax.ShapeDtypeStruct((T, cols), BF16), pl.BlockSpec((tm, cols), lambda i: (i, 0))

    outs = [fm(HA * 2 * DA_QK), tk(HA * 2 * DA_QK), fm(HA * VX),
            fm(HB * DB_PAD), tk(HB * DB_PAD), fm(HB * VX),
            fm(HC * DC), tk(KVH_C * DC), fm(KVH_C * VX)]
    return pl.pallas_call(
        _in_kernel,
        grid=(nt,),
        in_specs=in_specs,
        out_specs=[o[1] for o in outs],
        out_shape=[o[0] for o in outs],
        compiler_params=pltpu.CompilerParams(dimension_semantics=("arbitrary",), vmem_limit_bytes=VMEM_LIMIT),
        name="in_proj",
    )(x2, *consts, lw['cosT'], lw['sinT'])


def _attn_a_kernel(sc_ref, qT_ref, k_ref, vT_ref, near_ref, subln_ref, o_ref, w_scr, m_scr, l_scr, acc_scr, *, nk):
    qb = pl.program_id(1)
    n_comb = 2 * HA
    bq = qT_ref.shape[1]
    near_lo = (BQ_A // BK_A) * qb - 1

    for hc in range(n_comb):
        g = hc // 4
        qg = qT_ref[g * 128:(g + 1) * 128, :]
        row = lax.broadcasted_iota(jnp.int32, (128, bq), 0)
        keep = (row // DA_QK) == (hc % 4)
        w_scr[hc] = jnp.where(keep, qg, jnp.zeros_like(qg))
    m_scr[...] = jnp.full(m_scr.shape, NEG_BIG, F32)
    l_scr[...] = jnp.zeros(l_scr.shape, F32)
    acc_scr[...] = jnp.zeros(acc_scr.shape, F32)

    def step(kb, mode):
        ks = pl.multiple_of(kb * BK_A, BK_A)

        def qk(hc):
            g = hc // 4
            return _dot(k_ref[pl.ds(ks, BK_A), g * 128:(g + 1) * 128], w_scr[hc])

        ahead = [qk(i) for i in range(PREFETCH)]
        for hc in range(n_comb):
            h = hc // 2
            s = ahead.pop(0)
            if hc + PREFETCH < n_comb:
                ahead.append(qk(hc + PREFETCH))
            c = sc_ref[(1 - mode) * HA + h]
            cmax = jnp.max(s, axis=0, keepdims=True) + c
            m_old = m_scr[hc]
            m_new = jnp.maximum(m_old, cmax)
            alpha = jnp.exp2(m_old - m_new)
            p = jnp.exp2(s - (m_new - c))
            if mode == 0:
                p = p * near_ref[kb - near_lo, h]
            l_scr[hc] = alpha * l_scr[hc] + jnp.sum(p, axis=0, keepdims=True)
            pv = _dot(vT_ref[h * VX:h * VX + DA_V, pl.ds(ks, BK_A)], p.astype(BF16))
            acc_scr[hc] = alpha * acc_scr[hc] + pv
            m_scr[hc] = m_new

    lo = jnp.maximum(near_lo, 0)
    hi = jnp.minimum(near_lo + NEAR_A, nk)
    lax.fori_loop(0, lo, lambda kb, c: (step(kb, -1), c)[1], 0)
    lax.fori_loop(lo, hi, lambda kb, c: (step(kb, 0), c)[1], 0)
    lax.fori_loop(hi, nk, lambda kb, c: (step(kb, 1), c)[1], 0)

    lam = sc_ref[3 * HA]
    outs = []
    for h in range(HA):
        a = acc_scr[2 * h] / l_scr[2 * h] - lam * (acc_scr[2 * h + 1] / l_scr[2 * h + 1])
        a = a * lax.rsqrt(jnp.mean(a * a, axis=0, keepdims=True) + EPS)
        outs.append(a)
    o = jnp.concatenate(outs, axis=0) * subln_ref[...]
    o_ref[...] = o.T.astype(BF16)


def _sublane_partial_sum(p):
    n, w = p.shape
    return jnp.sum(p.reshape(n // 8, 8, w), axis=0)


def _attn_a_bounded_kernel(sc_ref, qT_ref, k_ref, vT_ref, near_ref, subln_ref, o_ref, w_scr, acc_scr, l_scr, *, nk):
    qb = pl.program_id(1)
    n_comb = 2 * HA
    bq = qT_ref.shape[1]
    near_lo = (BQ_A // BK_A) * qb - 1
    for hc in range(n_comb):
        g = hc // 4
        qg = qT_ref[g * 128:(g + 1) * 128, :]
        row = lax.broadcasted_iota(jnp.int32, (128, bq), 0)
        w_scr[hc] = jnp.where((row // DA_QK) == (hc % 4), qg, jnp.zeros_like(qg))
    acc_scr[...] = jnp.zeros(acc_scr.shape, F32)
    l_scr[...] = jnp.zeros(l_scr.shape, F32)

    lo = jnp.maximum(near_lo, 0)
    hi = jnp.minimum(near_lo + NEAR_A, nk)
    n_far = lo + (nk - hi)

    def step(j0, n_blocks, near):
        def block(jj):
            j = j0 + jj
            if near:
                return j, None
            right = j >= lo
            return jnp.where(right, j - lo + hi, j), jnp.where(right, HA, 0)

        tiles = [(jj, hc) for jj in range(n_blocks) for hc in range(n_comb)]

        def qk(t):
            jj, hc = tiles[t]
            g = hc // 4
            ks = pl.multiple_of(block(jj)[0] * BK_A, BK_A)
            return _dot(k_ref[pl.ds(ks, BK_A), g * 128:(g + 1) * 128], w_scr[hc])

        ahead = [qk(i) for i in range(PREFETCH)]
        for t, (jj, hc) in enumerate(tiles):
            h = hc // 2
            s = ahead.pop(0)
            if t + PREFETCH < len(tiles):
                ahead.append(qk(t + PREFETCH))
            kb, side = block(jj)
            ks = pl.multiple_of(kb * BK_A, BK_A)
            p = jnp.exp2(s)
            if near:
                p = p * near_ref[kb - near_lo, h]
            psum = _sublane_partial_sum(p)
            pv = _dot(vT_ref[h * VX:h * VX + DA_V, pl.ds(ks, BK_A)], p.astype(BF16))
            if near:
                acc_scr[hc] += pv
                l_scr[hc] += psum
            else:
                f = sc_ref[side + h]
                acc_scr[hc] += pv * f
                l_scr[hc] += psum * f

    n_groups = n_far // UNROLL_A
    lax.fori_loop(0, n_groups, lambda i, c: (step(i * UNROLL_A, UNROLL_A, False), c)[1], 0)
    lax.fori_loop(n_groups * UNROLL_A, n_far, lambda j, c: (step(j, 1, False), c)[1], 0)
    full = (hi - lo) // NEAR_A
    lax.fori_loop(0, full, lambda i, c: (step(lo, NEAR_A, True), c)[1], 0)
    lax.fori_loop(lo + full * NEAR_A, hi, lambda kb, c: (step(kb, 1, True), c)[1], 0)

    lam = sc_ref[2 * HA]
    outs = []
    for h in range(HA):
        l0 = jnp.sum(l_scr[2 * h], axis=0, keepdims=True)
        l1 = jnp.sum(l_scr[2 * h + 1], axis=0, keepdims=True)
        a = acc_scr[2 * h] / l0 - lam * (acc_scr[2 * h + 1] / l1)
        outs.append(a * lax.rsqrt(jnp.mean(a * a, axis=0, keepdims=True) + EPS))
    o = jnp.concatenate(outs, axis=0) * subln_ref[...]
    o_ref[...] = o.T.astype(BF16)


def _attn_a(qT, k, vT, lw, B, S, bounded):
    nq = S // BQ_A
    nk = S // BK_A
    n_comb = 2 * HA
    if bounded:
        body = functools.partial(_attn_a_bounded_kernel, nk=nk)
        scratch = [pltpu.VMEM((n_comb, 128, BQ_A), BF16), pltpu.VMEM((n_comb, DA_V, BQ_A), F32),
                   pltpu.VMEM((n_comb, 8, BQ_A), F32)]
        sc, near = lw['sc_a_bounded'], lw['near_a_bounded']
    else:
        body = functools.partial(_attn_a_kernel, nk=nk)
        scratch = [pltpu.VMEM((n_comb, 128, BQ_A), BF16), pltpu.VMEM((n_comb, 1, BQ_A), F32),
                   pltpu.VMEM((n_comb, 1, BQ_A), F32), pltpu.VMEM((n_comb, DA_V, BQ_A), F32)]
        sc, near = lw['sc_a'], lw['near_a']
    return pl.pallas_call(
        body,
        grid=(B, nq),
        in_specs=[
            pl.BlockSpec(memory_space=pltpu.SMEM),
            pl.BlockSpec((n_comb * DA_QK, BQ_A), lambda b, q: (0, b * nq + q)),
            pl.BlockSpec((S, n_comb * DA_QK), lambda b, q: (b, 0)),
            pl.BlockSpec((HA * VX, S), lambda b, q: (0, b)),
            pl.BlockSpec(near.shape, lambda b, q: (0, 0, 0, 0)),
            pl.BlockSpec((HA * DA_V, 1), lambda b, q: (0, 0)),
        ],
        out_specs=pl.BlockSpec((BQ_A, HA * DA_V), lambda b, q: (b * nq + q, 0)),
        out_shape=jax.ShapeDtypeStruct((B * S, HA * DA_V), BF16),
        scratch_shapes=scratch,
        compiler_params=pltpu.CompilerParams(dimension_semantics=("arbitrary", "arbitrary"),
                                             vmem_limit_bytes=VMEM_LIMIT),
        name="attn_a_bounded" if bounded else "attn_a",
    )(sc, qT, k, vT, near, lw['subln'])


def _attn_b_kernel(qT_ref, k_ref, vT_ref, o_ref, m_scr, l_scr, acc_scr, *, nk):
    m_scr[...] = jnp.full(m_scr.shape, NEG_BIG, F32)
    l_scr[...] = jnp.zeros(l_scr.shape, F32)
    acc_scr[...] = jnp.zeros(acc_scr.shape, F32)

    def step(kb, carry):
        ks = pl.multiple_of(kb * BK_B, BK_B)

        def qk(h):
            return _dot(k_ref[pl.ds(ks, BK_B), h * DB_PAD:(h + 1) * DB_PAD], qT_ref[h * DB_PAD:(h + 1) * DB_PAD, :])

        ahead = [qk(i) for i in range(PREFETCH)]
        for h in range(HB):
            s = ahead.pop(0)
            if h + PREFETCH < HB:
                ahead.append(qk(h + PREFETCH))
            m_old = m_scr[h]
            m_new = jnp.maximum(m_old, jnp.max(s, axis=0, keepdims=True))
            alpha = jnp.exp2(m_old - m_new)
            p = jnp.exp2(s - m_new)
            l_scr[h] = alpha * l_scr[h] + jnp.sum(p, axis=0, keepdims=True)
            pv = _dot(vT_ref[h * VX:h * VX + DB_V, pl.ds(ks, BK_B)], p.astype(BF16))
            acc_scr[h] = alpha * acc_scr[h] + pv
            m_scr[h] = m_new
        return carry

    lax.fori_loop(0, nk, step, 0)
    o = jnp.concatenate([acc_scr[h] / l_scr[h] for h in range(HB)], axis=0)
    o_ref[...] = o.T.astype(BF16)


def _attn_b_bounded_kernel(qT_ref, k_ref, vT_ref, o_ref, acc_scr, l_scr, *, nk):
    acc_scr[...] = jnp.zeros(acc_scr.shape, F32)
    l_scr[...] = jnp.zeros(l_scr.shape, F32)

    assert nk % UNROLL_B == 0
    tiles = [(j, h) for j in range(UNROLL_B) for h in range(HB)]

    def step(it, carry):
        def ks(j):
            return pl.multiple_of((it * UNROLL_B + j) * BK_B, BK_B)

        def qk(t):
            j, h = tiles[t]
            return _dot(k_ref[pl.ds(ks(j), BK_B), h * DB_PAD:(h + 1) * DB_PAD],
                        qT_ref[h * DB_PAD:(h + 1) * DB_PAD, :])

        ahead = [qk(i) for i in range(PREFETCH)]
        for t, (j, h) in enumerate(tiles):
            s = ahead.pop(0)
            if t + PREFETCH < len(tiles):
                ahead.append(qk(t + PREFETCH))
            p = jnp.exp2(s)
            l_scr[h] += _sublane_partial_sum(p)
            acc_scr[h] += _dot(vT_ref[h * VX:h * VX + DB_V, pl.ds(ks(j), BK_B)], p.astype(BF16))
        return carry

    lax.fori_loop(0, nk // UNROLL_B, step, 0)
    o = jnp.concatenate([acc_scr[h] / jnp.sum(l_scr[h], axis=0, keepdims=True) for h in range(HB)], axis=0)
    o_ref[...] = o.T.astype(BF16)


def _attn_b(qT, k, vT, B, S, bounded):
    nq = S // BQ_B
    nk = S // BK_B
    if bounded:
        body = functools.partial(_attn_b_bounded_kernel, nk=nk)
        scratch = [pltpu.VMEM((HB, DB_V, BQ_B), F32), pltpu.VMEM((HB, 8, BQ_B), F32)]
    else:
        body = functools.partial(_attn_b_kernel, nk=nk)
        scratch = [pltpu.VMEM((HB, 1, BQ_B), F32), pltpu.VMEM((HB, 1, BQ_B), F32), pltpu.VMEM((HB, DB_V, BQ_B), F32)]
    return pl.pallas_call(
        body,
        grid=(B, nq),
        in_specs=[
            pl.BlockSpec((HB * DB_PAD, BQ_B), lambda b, q: (0, b * nq + q)),
            pl.BlockSpec((S, HB * DB_PAD), lambda b, q: (b, 0)),
            pl.BlockSpec((HB * VX, S), lambda b, q: (0, b)),
        ],
        out_specs=pl.BlockSpec((BQ_B, HB * DB_V), lambda b, q: (b * nq + q, 0)),
        out_shape=jax.ShapeDtypeStruct((B * S, HB * DB_V), BF16),
        scratch_shapes=scratch,
        compiler_params=pltpu.CompilerParams(dimension_semantics=("arbitrary", "arbitrary"),
                                             vmem_limit_bytes=VMEM_LIMIT),
        name="attn_b_bounded" if bounded else "attn_b",
    )(qT, k, vT)


def _attn_c_kernel(sc_ref, qT_ref, k_ref, vT_ref, bm_ref, o_ref, *, nb, S):
    n = pl.program_id(1)
    start = pl.multiple_of(jnp.clip((n - 1) * BLOCK, 0, S - KW_C), BLOCK)
    case = jnp.where(n == 0, 0, jnp.where(n == nb - 1, 2, 1))
    kwin = k_ref[pl.ds(start, KW_C), :]
    outs = []
    for h in range(HC):
        j = h // G_C
        qh = qT_ref[h * DC:(h + 1) * DC, :]
        zz = jnp.zeros_like(qh)
        w = jnp.concatenate([qh, zz] if j == 0 else [zz, qh], axis=0)
        s = _dot(kwin, w) + bm_ref[case, h]
        sink = sc_ref[h]
        m = jnp.maximum(jnp.max(s, axis=0, keepdims=True), sink)
        p = jnp.exp2(s - m)
        den = jnp.sum(p, axis=0, keepdims=True) + jnp.exp2(sink - m)
        pv = _dot(vT_ref[j * VX:j * VX + DC, pl.ds(start, KW_C)], p.astype(BF16))
        outs.append(pv / den)
    o_ref[...] = jnp.concatenate(outs, axis=0).T.astype(BF16)


def _attn_c_bounded_kernel(sc_ref, qT_ref, k_ref, vT_ref, e_ref, o_ref, *, n_sub, S):
    g = pl.program_id(1)
    tiles = [(u, h) for u in range(SUB_C) for h in range(HC)]

    def window(u):
        n = g * SUB_C + u
        start = pl.multiple_of(jnp.clip(n * QB_C - WINDOW, 0, S - KWB_C), WINDOW)
        case = jnp.where(n == 0, 0, jnp.where(n == n_sub - 1, 2, 1))
        return start, case

    def qk(t):
        u, h = tiles[t]
        start, _ = window(u)
        qh = qT_ref[h * DC:(h + 1) * DC, u * QB_C:(u + 1) * QB_C]
        zz = jnp.zeros_like(qh)
        w = jnp.concatenate([qh, zz] if h // G_C == 0 else [zz, qh], axis=0)
        return _dot(k_ref[pl.ds(start, KWB_C), :], w)

    ahead = [qk(i) for i in range(PREFETCH)]
    outs = []
    for t, (u, h) in enumerate(tiles):
        s = ahead.pop(0)
        if t + PREFETCH < len(tiles):
            ahead.append(qk(t + PREFETCH))
        start, case = window(u)
        j = h // G_C
        p = (jnp.exp2(s) * e_ref[case, h]).astype(BF16)
        pv = _dot(vT_ref[j * VX:(j + 1) * VX, pl.ds(start, KWB_C)], p)
        outs.append(pv[0:DC] / (pv[DC:DC + 1] + sc_ref[HC + h]))
        if h == HC - 1:
            o_ref[u * QB_C:(u + 1) * QB_C, :] = jnp.concatenate(outs, axis=0).T.astype(BF16)
            outs = []


def _attn_c(qT, k, vT, lw, B, S, bounded):
    if bounded:
        n_sub = S // QB_C
        assert n_sub >= 3 and n_sub % SUB_C == 0
        nb, bq = n_sub // SUB_C, QB_C * SUB_C
        body = functools.partial(_attn_c_bounded_kernel, n_sub=n_sub, S=S)
        tile, sc = lw['e_c'], lw['sc_c_bounded']
    else:
        nb, bq = S // BLOCK, BLOCK
        assert nb >= 3
        body = functools.partial(_attn_c_kernel, nb=nb, S=S)
        tile, sc = lw['bm_c'], lw['sc_c']
    return pl.pallas_call(
        body,
        grid=(B, nb),
        in_specs=[
            pl.BlockSpec(memory_space=pltpu.SMEM),
            pl.BlockSpec((HC * DC, bq), lambda b, n: (0, b * nb + n)),
            pl.BlockSpec((S, KVH_C * DC), lambda b, n: (b, 0)),
            pl.BlockSpec((KVH_C * VX, S), lambda b, n: (0, b)),
            pl.BlockSpec(tile.shape, lambda b, n: (0, 0, 0, 0)),
        ],
        out_specs=pl.BlockSpec((bq, HC * DC), lambda b, n: (b * nb + n, 0)),
        out_shape=jax.ShapeDtypeStruct((B * S, HC * DC), BF16),
        compiler_params=pltpu.CompilerParams(dimension_semantics=("arbitrary", "arbitrary"),
                                             vmem_limit_bytes=VMEM_LIMIT),
        name="attn_c_bounded" if bounded else "attn_c",
    )(sc, qT, k, vT, tile)


def _out_kernel(x_ref, ya_ref, yb_ref, yc_ref, wo_ref, g_ref, wr_ref, xn_ref, h2_ref, affT_ref):
    na, nb_ = HA * DA_V, HA * DA_V + HB * DB_V
    xn = (x_ref[...] + _dot(ya_ref[...], wo_ref[0:na, :]) + _dot(yb_ref[...], wo_ref[na:nb_, :])
          + _dot(yc_ref[...], wo_ref[nb_:, :]))
    xn_ref[...] = xn
    h2 = (xn * lax.rsqrt(jnp.mean(xn * xn, axis=-1, keepdims=True) + EPS) * g_ref[...]).astype(BF16)
    h2_ref[...] = h2
    lg = _dot(h2, wr_ref[...]).T[0:N_EXPERTS]
    e = jnp.exp(lg - jnp.max(lg, axis=0, keepdims=True))
    affT_ref[...] = e / jnp.sum(e, axis=0, keepdims=True)


def _out_proj(x2, ya, yb, yc, lw):
    T = x2.shape[0]
    tm = TM_OUT

    def full(a):
        return pl.BlockSpec(a.shape, lambda i: (0,) * a.ndim)

    return pl.pallas_call(
        _out_kernel,
        grid=(T // tm,),
        in_specs=[pl.BlockSpec((tm, D_MODEL), lambda i: (i, 0)),
                  pl.BlockSpec((tm, ya.shape[1]), lambda i: (i, 0)),
                  pl.BlockSpec((tm, yb.shape[1]), lambda i: (i, 0)),
                  pl.BlockSpec((tm, yc.shape[1]), lambda i: (i, 0)),
                  full(lw['w_out']), full(lw['g_ffn']), full(lw['w_r'])],
        out_specs=[pl.BlockSpec((tm, D_MODEL), lambda i: (i, 0)),
                   pl.BlockSpec((tm, D_MODEL), lambda i: (i, 0)),
                   pl.BlockSpec((N_EXPERTS, tm), lambda i: (0, i))],
        out_shape=[jax.ShapeDtypeStruct((T, D_MODEL), F32),
                   jax.ShapeDtypeStruct((T, D_MODEL), BF16),
                   jax.ShapeDtypeStruct((N_EXPERTS, T), F32)],
        compiler_params=pltpu.CompilerParams(dimension_semantics=("arbitrary",), vmem_limit_bytes=VMEM_LIMIT),
        name="out_proj",
    )(x2, ya, yb, yc, lw['w_out'], lw['g_ffn'], lw['w_r'])


def _topk_kernel(aff_ref, thr_ref, tcut_ref, *, cap, T):
    capf = float(cap)

    def bits():
        return lax.bitcast_convert_type(aff_ref[...], jnp.int32)

    def count(mask):
        return jnp.sum(jnp.where(mask, 1.0, 0.0), axis=1, keepdims=True)

    def vbody(i, v):
        cand = v | jnp.left_shift(jnp.int32(1), 30 - i)
        return jnp.where(count(bits() >= cand) >= capf, cand, v)

    thr = lax.fori_loop(0, 31, vbody, jnp.zeros((N_EXPERTS, 1), jnp.int32))
    need = capf - count(bits() > thr)
    nbits = max(1, int(math.ceil(math.log2(T))))

    def tbody(i, c):
        cand = c | jnp.left_shift(jnp.int32(1), nbits - 1 - i)
        idx = lax.broadcasted_iota(jnp.int32, (N_EXPERTS, T), 1)
        f = count((bits() == thr) & (idx < cand))
        return jnp.where(f < need, cand, c)

    tcut = lax.fori_loop(0, nbits, tbody, jnp.zeros((N_EXPERTS, 1), jnp.int32))
    thr_ref[...] = jnp.broadcast_to(thr, thr_ref.shape)
    tcut_ref[...] = jnp.broadcast_to(tcut, tcut_ref.shape)


def _topk_thresholds(affT):
    E, T = affT.shape
    cap = CAPACITY_FACTOR * T // N_EXPERTS
    return pl.pallas_call(
        functools.partial(_topk_kernel, cap=cap, T=T),
        grid=(1,),
        in_specs=[pl.BlockSpec((E, T), lambda i: (0, 0))],
        out_specs=[pl.BlockSpec((E, 128), lambda i: (0, 0))] * 2,
        out_shape=[jax.ShapeDtypeStruct((E, 128), jnp.int32)] * 2,
        compiler_params=pltpu.CompilerParams(dimension_semantics=("arbitrary",), vmem_limit_bytes=VMEM_LIMIT),
        name="topk_thr",
    )(affT)


def _exclusive_prefix_count(sel, tri):
    e, n = sel.shape
    ones = jnp.where(sel, 1.0, 0.0)
    out, before = [], jnp.zeros((e, 1), F32)
    for kt in range(n // 128):
        blk = ones[:, kt * 128:(kt + 1) * 128]
        out.append(_dot(blk.astype(BF16), tri) + before)
        before = before + jnp.sum(blk, axis=1, keepdims=True)
    return jnp.concatenate(out, axis=1)


def _moe_kernel(x_ref, h2_ref, affT_ref, thr_ref, tcut_ref, tri_ref, wg_ref, wu_ref, wd_ref, o_ref, pos_scr, xe_scr):
    i = pl.program_id(0)
    e = pl.program_id(1)
    tt = TT_MOE

    def one_hot_rows(posm, c):
        rel = posm - (c * CH_MOE).astype(F32)
        jj = lax.broadcasted_iota(jnp.int32, (CH_MOE, tt), 0).astype(F32)
        return jnp.broadcast_to(rel, (CH_MOE, tt)) == jj

    @pl.when(e == 0)
    def _():
        b = lax.bitcast_convert_type(affT_ref[...], jnp.int32)
        thr = thr_ref[:, 0:1]
        tc = tcut_ref[:, 0:1]
        tg = i * (tt * SUBS_MOE) + lax.broadcasted_iota(jnp.int32, b.shape, 1)
        sel = (b > thr) | ((b == thr) & (tg <= tc))
        for sub in range(SUBS_MOE):
            sl = slice(sub * tt, (sub + 1) * tt)
            pos = _exclusive_prefix_count(sel[:, sl], tri_ref[...])
            posm_all = jnp.where(sel[:, sl], pos, -1.0)
            pos_scr[:, sl] = posm_all
            for grp in range(N_EXPERTS // GATHER_GROUP):
                hits = [one_hot_rows(posm_all[ee:ee + 1], jnp.int32(0))
                        for ee in range(grp * GATHER_GROUP, (grp + 1) * GATHER_GROUP)]
                onehot = jnp.where(jnp.concatenate(hits, axis=0), 1.0, 0.0).astype(BF16)
                rows = slice(grp * GATHER_GROUP * CH_MOE, (grp + 1) * GATHER_GROUP * CH_MOE)
                xe_scr[sub, rows, :] = _dot(onehot, h2_ref[sl, :]).astype(BF16)
        o_ref[...] = x_ref[...]

    posm = [pos_scr[pl.ds(e, 1), sub * tt:(sub + 1) * tt] for sub in range(SUBS_MOE)]
    gate = [affT_ref[pl.ds(e, 1), sub * tt:(sub + 1) * tt] for sub in range(SUBS_MOE)]
    n_sel = jnp.max(jnp.concatenate(posm, axis=0)) + 1.0
    n_ch = (n_sel.astype(jnp.int32) + CH_MOE - 1) // CH_MOE

    def chunk(c, pregathered):
        hits, xes = [], []
        for sub in range(SUBS_MOE):
            hit = one_hot_rows(posm[sub], c)
            if pregathered:
                xes.append(xe_scr[sub, pl.ds(pl.multiple_of(e * CH_MOE, CH_MOE), CH_MOE), :])
            else:
                xes.append(_dot(jnp.where(hit, 1.0, 0.0).astype(BF16),
                                h2_ref[sub * tt:(sub + 1) * tt, :]).astype(BF16))
            hits.append(hit)
        xe = jnp.concatenate(xes, axis=0)
        g = _dot(xe, wg_ref[...])
        u = _dot(xe, wu_ref[...])
        hid = (g * jax.nn.sigmoid(g) * u).astype(BF16)
        ye = _dot(hid, wd_ref[...])
        for sub in range(SUBS_MOE):
            gc = jnp.sum(jnp.where(hits[sub], jnp.broadcast_to(gate[sub], hits[sub].shape), 0.0),
                         axis=1, keepdims=True)
            yeb = jnp.concatenate([(ye[sub * CH_MOE:(sub + 1) * CH_MOE] * gc).astype(BF16),
                                   jnp.zeros((CHP_MOE - CH_MOE, ye.shape[1]), BF16)], axis=0)
            onehot = jnp.concatenate([jnp.where(hits[sub], 1.0, 0.0).astype(BF16),
                                      jnp.zeros((CHP_MOE - CH_MOE, tt), BF16)], axis=0)
            o_ref[sub * tt:(sub + 1) * tt, :] += _dot(onehot.T, yeb)

    chunk(jnp.int32(0), True)
    lax.fori_loop(1, n_ch, lambda c, carry: (chunk(c, False), carry)[1], 0)


def _moe(xn, h2, affT, thr, tcut, lw):
    T = xn.shape[0]
    tt = TT_MOE * SUBS_MOE
    once = pl.Buffered(1)
    return pl.pallas_call(
        _moe_kernel,
        grid=(T // tt, N_EXPERTS),
        in_specs=[
            pl.BlockSpec((tt, D_MODEL), lambda i, e: (i, 0), pipeline_mode=once),
            pl.BlockSpec((tt, D_MODEL), lambda i, e: (i, 0), pipeline_mode=once),
            pl.BlockSpec((N_EXPERTS, tt), lambda i, e: (0, i)),
            pl.BlockSpec((N_EXPERTS, 128), lambda i, e: (0, 0)),
            pl.BlockSpec((N_EXPERTS, 128), lambda i, e: (0, 0)),
            pl.BlockSpec((128, 128), lambda i, e: (0, 0), pipeline_mode=once),
            pl.BlockSpec((None, D_MODEL, D_EXPERT), lambda i, e: (e, 0, 0)),
            pl.BlockSpec((None, D_MODEL, D_EXPERT), lambda i, e: (e, 0, 0)),
            pl.BlockSpec((None, D_EXPERT, D_MODEL), lambda i, e: (e, 0, 0)),
        ],
        out_specs=pl.BlockSpec((tt, D_MODEL), lambda i, e: (i, 0), pipeline_mode=once),
        out_shape=jax.ShapeDtypeStruct((T, D_MODEL), F32),
        scratch_shapes=[pltpu.VMEM((N_EXPERTS, tt), F32),
                        pltpu.VMEM((SUBS_MOE, N_EXPERTS * CH_MOE, D_MODEL), BF16)],
        compiler_params=pltpu.CompilerParams(dimension_semantics=("arbitrary", "arbitrary"),
                                             vmem_limit_bytes=VMEM_LIMIT),
        name="moe_ffn",
    )(xn, h2, affT, thr, tcut, lw['tri'], lw['w_gate'], lw['w_up'], lw['w_down'])


def _t5_bucket(rel):
    nb = NUM_BUCKETS // 2
    max_exact = nb // 2
    ret = jnp.where(rel > 0, nb, 0)
    n = jnp.abs(rel)
    nf = jnp.maximum(n, 1).astype(jnp.float32)
    large = max_exact + (jnp.log(nf / max_exact) / math.log(MAX_DISTANCE / max_exact) * (nb - max_exact)).astype(jnp.int32)
    large = jnp.minimum(large, nb - 1)
    return ret + jnp.where(n < max_exact, n, large)


def _col(v):
    return v.astype(F32).reshape(-1, 1)


def _pad_heads(w, n_heads, d, d_pad):
    k = w.shape[1]
    w3 = w.reshape(n_heads, d, k)
    return jnp.pad(w3, ((0, 0), (0, d_pad - d), (0, 0))).reshape(n_heads * d_pad, k)


def _toeplitz_tiles(fn, offsets, nk, nq):
    span = nk + nq
    out = []
    for off in offsets:
        u = fn(off + nk - 1 - jnp.arange(span, dtype=jnp.int32))
        rows = jnp.tile(u, (1, nk))[:, :nk * (span - 1)].reshape(u.shape[0], nk, span - 1)
        out.append(rows[:, :, nk - 1:nk - 1 + nq])
    return jnp.stack(out, axis=0)


def _prep_shared(rel_bias):
    rb = rel_bias.astype(F32) * LOG2E

    def bias_a(rel):
        return rb[:, :HA][_t5_bucket(rel)].T

    def bias_c(rel):
        return rb[:, HA:][_t5_bucket(rel)].T

    def in_band(rel):
        return (jnp.abs(rel) <= WINDOW)[None, :]

    bmax = jnp.max(rb[:, :HA], axis=0)
    near_offs = [d * BK_A for d in range(-1, NEAR_A - 1)]
    near_a = _toeplitz_tiles(lambda rel: jnp.exp2(bias_a(rel) - bmax[:, None]), near_offs, BK_A, BQ_A)
    far = _t5_bucket(jnp.array([-(1 << 20), 1 << 20], dtype=jnp.int32))
    sc_far = jnp.concatenate([rb[far[1], :HA], bmax, rb[far[0], :HA]])
    a_bounded = {
        'near': _toeplitz_tiles(lambda rel: jnp.exp2(bias_a(rel)), near_offs, BK_A, BQ_A),
        'sc': jnp.concatenate([jnp.exp2(rb[far[0], :HA]), jnp.exp2(rb[far[1], :HA])]),
        'bias_mag': jnp.max(jnp.abs(rb[:, :HA])),
    }
    bm_c = _toeplitz_tiles(lambda rel: jnp.where(in_band(rel), bias_c(rel), NEG_BIG),
                           [0, -BLOCK, -2 * BLOCK], KW_C, BLOCK)
    c_bounded = {
        'e': _toeplitz_tiles(lambda rel: jnp.where(in_band(rel), jnp.exp2(bias_c(rel)), 0.0),
                             [0, -WINDOW, -(KWB_C - QB_C)], KWB_C, QB_C),
        'bias_mag': jnp.max(jnp.abs(rb[:, HA:])),
    }
    return near_a, sc_far, bm_c, a_bounded, c_bounded


def _rope_tables(S):
    inv = 1.0 / (ROPE_BASE ** (jnp.arange(ROPE_HALF, dtype=jnp.float32) / ROPE_HALF))
    ang = jnp.arange(S, dtype=jnp.int32).astype(jnp.float32)[:, None] * inv[None, :]
    return jnp.cos(ang).T, jnp.sin(ang).T


def _logit_bounds(l, p, a_bounded, c_bounded, sa, sb, scc):
    qa, ka = p['qn_a'][l].astype(F32), p['kn_a'][l].astype(F32)
    bound_a = 1.01 * DA_QK * jnp.max(jnp.abs(qa * ka)) * sa + 2.0 * a_bounded['bias_mag']
    qb, kb = p['qn_b'][l].astype(F32), p['kn_b'][l].astype(F32)
    bound_b = 1.01 * DB_QK * jnp.max(jnp.abs(qb)) * jnp.max(jnp.abs(kb)) * sb
    qc, kc = p['qn_c'][l].astype(F32), p['kn_c'][l].astype(F32)
    bound_c = (1.01 * DC * jnp.max(jnp.abs(qc * kc)) * scc + 2.0 * c_bounded['bias_mag']
               + jnp.max(jnp.abs(p['sink_c'][l].astype(F32))) * LOG2E)
    return bound_a <= BOUND_LIMIT, bound_b <= BOUND_LIMIT, bound_c <= BOUND_LIMIT


def _prep_layer(l, p, shared):
    near_a, sc_far, bm_c, a_bounded, c_bounded = shared
    lam_init = 0.8 - 0.6 * math.exp(-0.3 * l)
    lam = (jnp.exp(jnp.sum(p['lam_q1'][l].astype(F32) * p['lam_k1'][l].astype(F32)))
           - jnp.exp(jnp.sum(p['lam_q2'][l].astype(F32) * p['lam_k2'][l].astype(F32))) + lam_init)
    w_uq = p['w_uq'][l]
    w_ukv = p['w_ukv'][l].reshape(KV_RANK, HB, DB_NOPE + DB_V)
    sa = DA_QK ** -0.5 * LOG2E
    sb = DB_QK ** -0.5 * LOG2E
    scc = DC ** -0.5 * LOG2E
    zpad = jnp.zeros((DB_PAD - DB_QK,), F32)
    lw = {
        'g_mix': p['norm_mix_g'][l].astype(F32).reshape(1, D_MODEL),
        'w_inT': p['w_in'][l].T.astype(BF16),
        'w_uqT': _pad_heads(w_uq.T, HB, DB_QK, DB_PAD).astype(BF16),
        'w_ukT': w_ukv[:, :, :DB_NOPE].reshape(KV_RANK, HB * DB_NOPE).T.astype(BF16),
        'w_uvT': w_ukv[:, :, DB_NOPE:].reshape(KV_RANK, HB * DB_V).T.astype(BF16),
        'gqa': _col(jnp.tile(p['qn_a'][l], 2 * HA) * sa),
        'gka': _col(jnp.tile(p['kn_a'][l], 2 * HA)),
        'gcq': _col(p['g_cq'][l]),
        'gckv': _col(p['g_ckv'][l]),
        'gqb': _col(jnp.tile(jnp.concatenate([p['qn_b'][l].astype(F32) * sb, zpad]), HB)),
        'gkb': _col(jnp.tile(jnp.concatenate([p['kn_b'][l].astype(F32), zpad]), HB)),
        'gqc': _col(jnp.tile(p['qn_c'][l], HC) * scc),
        'gkc': _col(jnp.tile(p['kn_c'][l], KVH_C)),
        'near_a': near_a,
        'sc_a': jnp.concatenate([sc_far, lam.reshape(1)]).astype(F32),
        'subln': _col(jnp.tile(p['subln_a'][l], HA) * (1.0 - lam_init)),
        'bm_c': bm_c,
        'sc_c': p['sink_c'][l].astype(F32) * LOG2E,
        'w_out': p['w_out'][l].astype(BF16),
        'g_ffn': p['norm_ffn_g'][l].astype(F32).reshape(1, D_MODEL),
        'w_r': jnp.pad(p['w_router'][l], ((0, 0), (0, 128 - N_EXPERTS))).astype(BF16),
        'w_gate': p['w_gate'][l].astype(BF16),
        'w_up': p['w_up'][l].astype(BF16),
        'w_down': p['w_down'][l].astype(BF16),
        'tri': jnp.triu(jnp.ones((128, 128), F32), k=1).astype(BF16),
        'near_a_bounded': a_bounded['near'],
        'sc_a_bounded': jnp.concatenate([a_bounded['sc'], lam.reshape(1)]).astype(F32),
    }
    lw['e_c'] = c_bounded['e']
    lw['sc_c_bounded'] = jnp.concatenate([lw['sc_c'], jnp.exp2(lw['sc_c'])])
    lw['bounded_a'], lw['bounded_b'], lw['bounded_c'] = _logit_bounds(l, p, a_bounded, c_bounded, sa, sb, scc)
    return lw


def _trunk(x, layers):
    B, S, D = x.shape
    x2 = x.reshape(B * S, D)
    cosT, sinT = _rope_tables(S)
    for lw in layers:
        lw = dict(lw, cosT=cosT, sinT=sinT)
        qaT, ka, vaT, qbT, kb, vbT, qcT, kc, vcT = _in_proj(x2, lw, S)
        ya = lax.cond(lw['bounded_a'],
                      lambda q, k, v: _attn_a(q, k, v, lw, B, S, True),
                      lambda q, k, v: _attn_a(q, k, v, lw, B, S, False), qaT, ka, vaT)
        yb = lax.cond(lw['bounded_b'],
                      lambda q, k, v: _attn_b(q, k, v, B, S, True),
                      lambda q, k, v: _attn_b(q, k, v, B, S, False), qbT, kb, vbT)
        yc = lax.cond(lw['bounded_c'],
                      lambda q, k, v: _attn_c(q, k, v, lw, B, S, True),
                      lambda q, k, v: _attn_c(q, k, v, lw, B, S, False), qcT, kc, vcT)
        xn, h2, affT = _out_proj(x2, ya, yb, yc, lw)
        thr, tcut = _topk_thresholds(affT)
        x2 = _moe(xn, h2, affT, thr, tcut, lw)
    return x2.reshape(B, S, D)


def kernel(x_prompt, x_sample, rel_bias, norm_mix_g, w_in, qn_a, kn_a, lam_q1, lam_k1, lam_q2, lam_k2, subln_a, g_cq, w_uq, g_ckv, w_ukv, qn_b, kn_b, qn_c, kn_c, sink_c, w_out, norm_ffn_g, w_router, w_gate, w_up, w_down):
    p = dict(norm_mix_g=norm_mix_g, w_in=w_in, qn_a=qn_a, kn_a=kn_a, lam_q1=lam_q1, lam_k1=lam_k1,
             lam_q2=lam_q2, lam_k2=lam_k2, subln_a=subln_a, g_cq=g_cq, w_uq=w_uq, g_ckv=g_ckv, w_ukv=w_ukv,
             qn_b=qn_b, kn_b=kn_b, qn_c=qn_c, kn_c=kn_c, sink_c=sink_c, w_out=w_out, norm_ffn_g=norm_ffn_g,
             w_router=w_router, w_gate=w_gate, w_up=w_up, w_down=w_down)
    shared = _prep_shared(rel_bias)
    layers = [_prep_layer(l, p, shared) for l in range(w_in.shape[0])]
    return (_trunk(x_prompt, layers), _trunk(x_sample, layers))
```

```python
import functools
import math

import jax
import jax.numpy as jnp
import numpy as np
from jax import lax
from jax.experimental import pallas as pl
from jax.experimental.pallas import tpu as pltpu

D_MODEL = 1024
BLOCK = 128
HA, DA_QK, DA_V = 4, 32, 64
HB, Q_RANK, KV_RANK, DB_NOPE, DB_ROPE, DB_V = 6, 384, 256, 64, 32, 64
ROPE_BASE = 10000.0
HC, KVH_C, DC, WINDOW = 6, 2, 64, 128
G_C = HC // KVH_C
NUM_BUCKETS, MAX_DISTANCE = 32, 128
N_EXPERTS, D_EXPERT, CAPACITY_FACTOR = 16, 1024, 2
EPS = 1e-6

DB_QK = DB_NOPE + DB_ROPE
DB_PAD = 128
VX = 80
BOUND_LIMIT = 80.0
ROPE_HALF = DB_ROPE // 2
IN_SPLITS = (HA * 2 * DA_QK, HA * 2 * DA_QK, HA * DA_V, Q_RANK, KV_RANK, DB_ROPE, HC * DC, KVH_C * DC, KVH_C * DC)
IN_OFFS = tuple(int(v) for v in np.cumsum((0,) + IN_SPLITS))
IN_COLS = IN_OFFS[-1]
LOG2E = 1.4426950408889634
NEG_BIG = -1e30

F32 = jnp.float32
BF16 = jnp.bfloat16

TM_IN = 1024
TM_OUT = 1024
BQ_A = 512
BK_A = 512
UNROLL_A = 4
NEAR_A = BQ_A // BK_A + 2
assert BQ_A % BK_A == 0 and BK_A >= MAX_DISTANCE
BQ_B = 512
BK_B = 512
UNROLL_B = 4
KW_C = 3 * BLOCK
QB_C = 256
KWB_C = QB_C + 2 * WINDOW
SUB_C = 2
TT_MOE = 1024
SUBS_MOE = 2
SCATTER_GROUP = 8
CH_MOE = 160
CHP_MOE = 256
PREFETCH = 3
VMEM_LIMIT = 56 * 1024 * 1024


def _dot(a, b):
    return jnp.dot(a, b, preferred_element_type=F32)


def _group_norm_rows(y, gs, n_valid):
    rows, tm = y.shape
    y3 = y.reshape(rows // gs, gs, tm)
    ms = jnp.sum(y3 * y3, axis=1, keepdims=True) * (1.0 / n_valid)
    return (y3 * lax.rsqrt(ms + EPS)).reshape(rows, tm)


def _in_kernel(x_ref, g_ref, winT_ref, wuqT_ref, wukT_ref, wuvT_ref,
               gqa_ref, gka_ref, gcq_ref, gckv_ref, gqb_ref, gkb_ref, gqc_ref, gkc_ref,
               cos_ref, sin_ref,
               qaT_ref, ka_ref, vaT_ref, qbT_ref, kb_ref, vbT_ref, qcT_ref, kc_ref, vcT_ref):
    x = x_ref[...]
    h = x * lax.rsqrt(jnp.mean(x * x, axis=-1, keepdims=True) + EPS) * g_ref[...]
    projT = _dot(winT_ref[...], h.T.astype(BF16))

    def seg(i):
        return projT[IN_OFFS[i]:IN_OFFS[i + 1]]

    cos = cos_ref[...]
    sin = sin_ref[...]

    def with_ones_rows(v, n_heads, d):
        tm = v.shape[1]
        tail = jnp.where(lax.broadcasted_iota(jnp.int32, (VX - d, tm), 0) == 0, 1.0, 0.0)
        rows = []
        for hh in range(n_heads):
            rows += [v[hh * d:(hh + 1) * d], tail]
        return jnp.concatenate(rows, axis=0).astype(BF16)

    qaT_ref[...] = (_group_norm_rows(seg(0), DA_QK, DA_QK) * gqa_ref[...]).astype(BF16)
    kaT = _group_norm_rows(seg(1), DA_QK, DA_QK) * gka_ref[...]
    ka_ref[...] = kaT.T.astype(BF16)
    vaT_ref[...] = with_ones_rows(seg(2), HA, DA_V)

    def rope_store(dst_rows, y, h):
        o = h * DB_PAD
        x1 = y[o + DB_NOPE:o + DB_NOPE + ROPE_HALF]
        x2 = y[o + DB_NOPE + ROPE_HALF:o + DB_QK]
        dst_rows.append(y[o:o + DB_NOPE])
        dst_rows.append(x1 * cos - x2 * sin)
        dst_rows.append(x2 * cos + x1 * sin)
        dst_rows.append(jnp.zeros((DB_PAD - DB_QK, y.shape[1]), F32))

    cqT = seg(3)
    cqn = (cqT * lax.rsqrt(jnp.mean(cqT * cqT, axis=0, keepdims=True) + EPS) * gcq_ref[...]).astype(BF16)
    qf = _group_norm_rows(_dot(wuqT_ref[...], cqn), DB_PAD, DB_QK) * gqb_ref[...]
    rows = []
    for hh in range(HB):
        rope_store(rows, qf, hh)
    qbT_ref[...] = jnp.concatenate(rows, axis=0).astype(BF16)

    ckvT = seg(4)
    ckvn = (ckvT * lax.rsqrt(jnp.mean(ckvT * ckvT, axis=0, keepdims=True) + EPS) * gckv_ref[...]).astype(BF16)
    knope = _dot(wukT_ref[...], ckvn)
    vbT_ref[...] = with_ones_rows(_dot(wuvT_ref[...], ckvn), HB, DB_V)
    krope = seg(5)
    zpad = jnp.zeros((DB_PAD - DB_QK, krope.shape[1]), F32)
    rows = []
    for hh in range(HB):
        rows += [knope[hh * DB_NOPE:(hh + 1) * DB_NOPE], krope, zpad]
    kf = _group_norm_rows(jnp.concatenate(rows, axis=0), DB_PAD, DB_QK) * gkb_ref[...]
    rows = []
    for hh in range(HB):
        rope_store(rows, kf, hh)
    kb_ref[...] = jnp.concatenate(rows, axis=0).T.astype(BF16)

    qcT_ref[...] = (_group_norm_rows(seg(6), DC, DC) * gqc_ref[...]).astype(BF16)
    kcT = _group_norm_rows(seg(7), DC, DC) * gkc_ref[...]
    kc_ref[...] = kcT.T.astype(BF16)
    vcT_ref[...] = with_ones_rows(seg(8), KVH_C, DC)


def _in_proj(x2, lw, S):
    T = x2.shape[0]
    tm = TM_IN
    nt = T // tm
    spb = S // tm

    def full(a):
        return pl.BlockSpec(a.shape, lambda i: (0,) * a.ndim)

    consts = [lw['g_mix'], lw['w_inT'], lw['w_uqT'], lw['w_ukT'], lw['w_uvT'],
              lw['gqa'], lw['gka'], lw['gcq'], lw['gckv'], lw['gqb'], lw['gkb'], lw['gqc'], lw['gkc']]
    in_specs = ([pl.BlockSpec((tm, D_MODEL), lambda i: (i, 0))] + [full(a) for a in consts]
                + [pl.BlockSpec((ROPE_HALF, tm), lambda i: (0, i % spb))] * 2)

    def fm(rows):
        return jax.ShapeDtypeStruct((rows, T), BF16), pl.BlockSpec((rows, tm), lambda i: (0, i))

    def tk(cols):
        return jax.ShapeDtypeStruct((T, cols), BF16), pl.BlockSpec((tm, cols), lambda i: (i, 0))

    outs = [fm(HA * 2 * DA_QK), tk(HA * 2 * DA_QK), fm(HA * VX),
            fm(HB * DB_PAD), tk(HB * DB_PAD), fm(HB * VX),
            fm(HC * DC), tk(KVH_C * DC), fm(KVH_C * VX)]
    return pl.pallas_call(
        _in_kernel,
        grid=(nt,),
        in_specs=in_specs,
        out_specs=[o[1] for o in outs],
        out_shape=[o[0] for o in outs],
        compiler_params=pltpu.CompilerParams(dimension_semantics=("arbitrary",), vmem_limit_bytes=VMEM_LIMIT),
        name="in_proj",
    )(x2, *consts, lw['cosT'], lw['sinT'])


def _attn_a_kernel(sc_ref, qT_ref, k_ref, vT_ref, near_ref, subln_ref, o_ref, w_scr, m_scr, l_scr, acc_scr, *, nk):
    qb = pl.program_id(1)
    n_comb = 2 * HA
    bq = qT_ref.shape[1]
    near_lo = (BQ_A // BK_A) * qb - 1

    for hc in range(n_comb):
        g = hc // 4
        qg = qT_ref[g * 128:(g + 1) * 128, :]
        row = lax.broadcasted_iota(jnp.int32, (128, bq), 0)
        keep = (row // DA_QK) == (hc % 4)
        w_scr[hc] = jnp.where(keep, qg, jnp.zeros_like(qg))
    m_scr[...] = jnp.full(m_scr.shape, NEG_BIG, F32)
    l_scr[...] = jnp.zeros(l_scr.shape, F32)
    acc_scr[...] = jnp.zeros(acc_scr.shape, F32)

    def step(kb, mode):
        ks = pl.multiple_of(kb * BK_A, BK_A)

        def qk(hc):
            g = hc // 4
            return _dot(k_ref[pl.ds(ks, BK_A), g * 128:(g + 1) * 128], w_scr[hc])

        ahead = [qk(i) for i in range(PREFETCH)]
        for hc in range(n_comb):
            h = hc // 2
            s = ahead.pop(0)
            if hc + PREFETCH < n_comb:
                ahead.append(qk(hc + PREFETCH))
            c = sc_ref[(1 - mode) * HA + h]
            cmax = jnp.max(s, axis=0, keepdims=True) + c
            m_old = m_scr[hc]
            m_new = jnp.maximum(m_old, cmax)
            alpha = jnp.exp2(m_old - m_new)
            p = jnp.exp2(s - (m_new - c))
            if mode == 0:
                p = p * near_ref[kb - near_lo, h]
            l_scr[hc] = alpha * l_scr[hc] + jnp.sum(p, axis=0, keepdims=True)
            pv = _dot(vT_ref[h * VX:h * VX + DA_V, pl.ds(ks, BK_A)], p.astype(BF16))
            acc_scr[hc] = alpha * acc_scr[hc] + pv
            m_scr[hc] = m_new

    lo = jnp.maximum(near_lo, 0)
    hi = jnp.minimum(near_lo + NEAR_A, nk)
    lax.fori_loop(0, lo, lambda kb, c: (step(kb, -1), c)[1], 0)
    lax.fori_loop(lo, hi, lambda kb, c: (step(kb, 0), c)[1], 0)
    lax.fori_loop(hi, nk, lambda kb, c: (step(kb, 1), c)[1], 0)

    lam = sc_ref[3 * HA]
    outs = []
    for h in range(HA):
        a = acc_scr[2 * h] / l_scr[2 * h] - lam * (acc_scr[2 * h + 1] / l_scr[2 * h + 1])
        a = a * lax.rsqrt(jnp.mean(a * a, axis=0, keepdims=True) + EPS)
        outs.append(a)
    o = jnp.concatenate(outs, axis=0) * subln_ref[...]
    o_ref[...] = o.T.astype(BF16)


def _sublane_partial_sum(p):
    n, w = p.shape
    return jnp.sum(p.reshape(n // 8, 8, w), axis=0)


def _attn_a_bounded_kernel(sc_ref, qT_ref, k_ref, vT_ref, near_ref, subln_ref, o_ref, w_scr, acc_scr, l_scr, *, nk):
    qb = pl.program_id(1)
    n_comb = 2 * HA
    bq = qT_ref.shape[1]
    near_lo = (BQ_A // BK_A) * qb - 1
    for hc in range(n_comb):
        g = hc // 4
        qg = qT_ref[g * 128:(g + 1) * 128, :]
        row = lax.broadcasted_iota(jnp.int32, (128, bq), 0)
        w_scr[hc] = jnp.where((row // DA_QK) == (hc % 4), qg, jnp.zeros_like(qg))
    acc_scr[...] = jnp.zeros(acc_scr.shape, F32)
    l_scr[...] = jnp.zeros(l_scr.shape, F32)

    lo = jnp.maximum(near_lo, 0)
    hi = jnp.minimum(near_lo + NEAR_A, nk)
    n_far = lo + (nk - hi)

    def step(j0, n_blocks, near):
        def block(jj):
            j = j0 + jj
            if near:
                return j, None
            right = j >= lo
            return jnp.where(right, j - lo + hi, j), jnp.where(right, HA, 0)

        tiles = [(jj, hc) for jj in range(n_blocks) for hc in range(n_comb)]

        def qk(t):
            jj, hc = tiles[t]
            g = hc // 4
            ks = pl.multiple_of(block(jj)[0] * BK_A, BK_A)
            return _dot(k_ref[pl.ds(ks, BK_A), g * 128:(g + 1) * 128], w_scr[hc])

        ahead = [qk(i) for i in range(PREFETCH)]
        for t, (jj, hc) in enumerate(tiles):
            h = hc // 2
            s = ahead.pop(0)
            if t + PREFETCH < len(tiles):
                ahead.append(qk(t + PREFETCH))
            kb, side = block(jj)
            ks = pl.multiple_of(kb * BK_A, BK_A)
            p = jnp.exp2(s)
            if near:
                p = p * near_ref[kb - near_lo, h]
            psum = _sublane_partial_sum(p)
            pv = _dot(vT_ref[h * VX:h * VX + DA_V, pl.ds(ks, BK_A)], p.astype(BF16))
            if near:
                acc_scr[hc] += pv
                l_scr[hc] += psum
            else:
                f = sc_ref[side + h]
                acc_scr[hc] += pv * f
                l_scr[hc] += psum * f

    n_pairs = n_far // UNROLL_A
    lax.fori_loop(0, n_pairs, lambda i, c: (step(i * UNROLL_A, UNROLL_A, False), c)[1], 0)
    lax.fori_loop(n_pairs * UNROLL_A, n_far, lambda j, c: (step(j, 1, False), c)[1], 0)
    lax.fori_loop(lo, hi, lambda kb, c: (step(kb, 1, True), c)[1], 0)

    lam = sc_ref[2 * HA]
    outs = []
    for h in range(HA):
        l0 = jnp.sum(l_scr[2 * h], axis=0, keepdims=True)
        l1 = jnp.sum(l_scr[2 * h + 1], axis=0, keepdims=True)
        a = acc_scr[2 * h] / l0 - lam * (acc_scr[2 * h + 1] / l1)
        outs.append(a * lax.rsqrt(jnp.mean(a * a, axis=0, keepdims=True) + EPS))
    o = jnp.concatenate(outs, axis=0) * subln_ref[...]
    o_ref[...] = o.T.astype(BF16)


def _attn_a(qT, k, vT, lw, B, S, bounded):
    nq = S // BQ_A
    nk = S // BK_A
    n_comb = 2 * HA
    if bounded:
        body = functools.partial(_attn_a_bounded_kernel, nk=nk)
        scratch = [pltpu.VMEM((n_comb, 128, BQ_A), BF16), pltpu.VMEM((n_comb, DA_V, BQ_A), F32),
                   pltpu.VMEM((n_comb, 8, BQ_A), F32)]
        sc, near = lw['sc_a_bounded'], lw['near_a_bounded']
    else:
        body = functools.partial(_attn_a_kernel, nk=nk)
        scratch = [pltpu.VMEM((n_comb, 128, BQ_A), BF16), pltpu.VMEM((n_comb, 1, BQ_A), F32),
                   pltpu.VMEM((n_comb, 1, BQ_A), F32), pltpu.VMEM((n_comb, DA_V, BQ_A), F32)]
        sc, near = lw['sc_a'], lw['near_a']
    return pl.pallas_call(
        body,
        grid=(B, nq),
        in_specs=[
            pl.BlockSpec(memory_space=pltpu.SMEM),
            pl.BlockSpec((n_comb * DA_QK, BQ_A), lambda b, q: (0, b * nq + q)),
            pl.BlockSpec((S, n_comb * DA_QK), lambda b, q: (b, 0)),
            pl.BlockSpec((HA * VX, S), lambda b, q: (0, b)),
            pl.BlockSpec(near.shape, lambda b, q: (0, 0, 0, 0)),
            pl.BlockSpec((HA * DA_V, 1), lambda b, q: (0, 0)),
        ],
        out_specs=pl.BlockSpec((BQ_A, HA * DA_V), lambda b, q: (b * nq + q, 0)),
        out_shape=jax.ShapeDtypeStruct((B * S, HA * DA_V), BF16),
        scratch_shapes=scratch,
        compiler_params=pltpu.CompilerParams(dimension_semantics=("arbitrary", "arbitrary"),
                                             vmem_limit_bytes=VMEM_LIMIT),
        name="attn_a_bounded" if bounded else "attn_a",
    )(sc, qT, k, vT, near, lw['subln'])


def _attn_b_kernel(qT_ref, k_ref, vT_ref, o_ref, m_scr, l_scr, acc_scr, *, nk):
    m_scr[...] = jnp.full(m_scr.shape, NEG_BIG, F32)
    l_scr[...] = jnp.zeros(l_scr.shape, F32)
    acc_scr[...] = jnp.zeros(acc_scr.shape, F32)

    def step(kb, carry):
        ks = pl.multiple_of(kb * BK_B, BK_B)

        def qk(h):
            return _dot(k_ref[pl.ds(ks, BK_B), h * DB_PAD:(h + 1) * DB_PAD], qT_ref[h * DB_PAD:(h + 1) * DB_PAD, :])

        ahead = [qk(i) for i in range(PREFETCH)]
        for h in range(HB):
            s = ahead.pop(0)
            if h + PREFETCH < HB:
                ahead.append(qk(h + PREFETCH))
            m_old = m_scr[h]
            m_new = jnp.maximum(m_old, jnp.max(s, axis=0, keepdims=True))
            alpha = jnp.exp2(m_old - m_new)
            p = jnp.exp2(s - m_new)
            l_scr[h] = alpha * l_scr[h] + jnp.sum(p, axis=0, keepdims=True)
            pv = _dot(vT_ref[h * VX:h * VX + DB_V, pl.ds(ks, BK_B)], p.astype(BF16))
            acc_scr[h] = alpha * acc_scr[h] + pv
            m_scr[h] = m_new
        return carry

    lax.fori_loop(0, nk, step, 0)
    o = jnp.concatenate([acc_scr[h] / l_scr[h] for h in range(HB)], axis=0)
    o_ref[...] = o.T.astype(BF16)


def _attn_b_bounded_kernel(qT_ref, k_ref, vT_ref, o_ref, acc_scr, l_scr, *, nk):
    acc_scr[...] = jnp.zeros(acc_scr.shape, F32)
    l_scr[...] = jnp.zeros(l_scr.shape, F32)

    assert nk % UNROLL_B == 0
    tiles = [(j, h) for j in range(UNROLL_B) for h in range(HB)]

    def step(it, carry):
        def ks(j):
            return pl.multiple_of((it * UNROLL_B + j) * BK_B, BK_B)

        def qk(t):
            j, h = tiles[t]
            return _dot(k_ref[pl.ds(ks(j), BK_B), h * DB_PAD:(h + 1) * DB_PAD],
                        qT_ref[h * DB_PAD:(h + 1) * DB_PAD, :])

        ahead = [qk(i) for i in range(PREFETCH)]
        for t, (j, h) in enumerate(tiles):
            s = ahead.pop(0)
            if t + PREFETCH < len(tiles):
                ahead.append(qk(t + PREFETCH))
            p = jnp.exp2(s)
            l_scr[h] += _sublane_partial_sum(p)
            acc_scr[h] += _dot(vT_ref[h * VX:h * VX + DB_V, pl.ds(ks(j), BK_B)], p.astype(BF16))
        return carry

    lax.fori_loop(0, nk // UNROLL_B, step, 0)
    o = jnp.concatenate([acc_scr[h] / jnp.sum(l_scr[h], axis=0, keepdims=True) for h in range(HB)], axis=0)
    o_ref[...] = o.T.astype(BF16)


def _attn_b(qT, k, vT, B, S, bounded):
    nq = S // BQ_B
    nk = S // BK_B
    if bounded:
        body = functools.partial(_attn_b_bounded_kernel, nk=nk)
        scratch = [pltpu.VMEM((HB, DB_V, BQ_B), F32), pltpu.VMEM((HB, 8, BQ_B), F32)]
    else:
        body = functools.partial(_attn_b_kernel, nk=nk)
        scratch = [pltpu.VMEM((HB, 1, BQ_B), F32), pltpu.VMEM((HB, 1, BQ_B), F32), pltpu.VMEM((HB, DB_V, BQ_B), F32)]
    return pl.pallas_call(
        body,
        grid=(B, nq),
        in_specs=[
            pl.BlockSpec((HB * DB_PAD, BQ_B), lambda b, q: (0, b * nq + q)),
            pl.BlockSpec((S, HB * DB_PAD), lambda b, q: (b, 0)),
            pl.BlockSpec((HB * VX, S), lambda b, q: (0, b)),
        ],
        out_specs=pl.BlockSpec((BQ_B, HB * DB_V), lambda b, q: (b * nq + q, 0)),
        out_shape=jax.ShapeDtypeStruct((B * S, HB * DB_V), BF16),
        scratch_shapes=scratch,
        compiler_params=pltpu.CompilerParams(dimension_semantics=("arbitrary", "arbitrary"),
                                             vmem_limit_bytes=VMEM_LIMIT),
        name="attn_b_bounded" if bounded else "attn_b",
    )(qT, k, vT)


def _attn_c_kernel(sc_ref, qT_ref, k_ref, vT_ref, bm_ref, o_ref, *, nb, S):
    n = pl.program_id(1)
    start = pl.multiple_of(jnp.clip((n - 1) * BLOCK, 0, S - KW_C), BLOCK)
    case = jnp.where(n == 0, 0, jnp.where(n == nb - 1, 2, 1))
    kwin = k_ref[pl.ds(start, KW_C), :]
    outs = []
    for h in range(HC):
        j = h // G_C
        qh = qT_ref[h * DC:(h + 1) * DC, :]
        zz = jnp.zeros_like(qh)
        w = jnp.concatenate([qh, zz] if j == 0 else [zz, qh], axis=0)
        s = _dot(kwin, w) + bm_ref[case, h]
        sink = sc_ref[h]
        m = jnp.maximum(jnp.max(s, axis=0, keepdims=True), sink)
        p = jnp.exp2(s - m)
        den = jnp.sum(p, axis=0, keepdims=True) + jnp.exp2(sink - m)
        pv = _dot(vT_ref[j * VX:j * VX + DC, pl.ds(start, KW_C)], p.astype(BF16))
        outs.append(pv / den)
    o_ref[...] = jnp.concatenate(outs, axis=0).T.astype(BF16)


def _attn_c_bounded_kernel(sc_ref, qT_ref, k_ref, vT_ref, e_ref, o_ref, *, n_sub, S):
    g = pl.program_id(1)
    tiles = [(u, h) for u in range(SUB_C) for h in range(HC)]

    def window(u):
        n = g * SUB_C + u
        start = pl.multiple_of(jnp.clip(n * QB_C - WINDOW, 0, S - KWB_C), WINDOW)
        case = jnp.where(n == 0, 0, jnp.where(n == n_sub - 1, 2, 1))
        return start, case

    def qk(t):
        u, h = tiles[t]
        start, _ = window(u)
        qh = qT_ref[h * DC:(h + 1) * DC, u * QB_C:(u + 1) * QB_C]
        zz = jnp.zeros_like(qh)
        w = jnp.concatenate([qh, zz] if h // G_C == 0 else [zz, qh], axis=0)
        return _dot(k_ref[pl.ds(start, KWB_C), :], w)

    ahead = [qk(i) for i in range(PREFETCH)]
    outs = []
    for t, (u, h) in enumerate(tiles):
        s = ahead.pop(0)
        if t + PREFETCH < len(tiles):
            ahead.append(qk(t + PREFETCH))
        start, case = window(u)
        j = h // G_C
        p = (jnp.exp2(s) * e_ref[case, h]).astype(BF16)
        pv = _dot(vT_ref[j * VX:(j + 1) * VX, pl.ds(start, KWB_C)], p)
        outs.append(pv[0:DC] / (pv[DC:DC + 1] + sc_ref[HC + h]))
        if h == HC - 1:
            o_ref[u * QB_C:(u + 1) * QB_C, :] = jnp.concatenate(outs, axis=0).T.astype(BF16)
            outs = []


def _attn_c(qT, k, vT, lw, B, S, bounded):
    if bounded:
        n_sub = S // QB_C
        assert n_sub >= 3 and n_sub % SUB_C == 0
        nb, bq = n_sub // SUB_C, QB_C * SUB_C
        body = functools.partial(_attn_c_bounded_kernel, n_sub=n_sub, S=S)
        tile, sc = lw['e_c'], lw['sc_c_bounded']
    else:
        nb, bq = S // BLOCK, BLOCK
        assert nb >= 3
        body = functools.partial(_attn_c_kernel, nb=nb, S=S)
        tile, sc = lw['bm_c'], lw['sc_c']
    return pl.pallas_call(
        body,
        grid=(B, nb),
        in_specs=[
            pl.BlockSpec(memory_space=pltpu.SMEM),
            pl.BlockSpec((HC * DC, bq), lambda b, n: (0, b * nb + n)),
            pl.BlockSpec((S, KVH_C * DC), lambda b, n: (b, 0)),
            pl.BlockSpec((KVH_C * VX, S), lambda b, n: (0, b)),
            pl.BlockSpec(tile.shape, lambda b, n: (0, 0, 0, 0)),
        ],
        out_specs=pl.BlockSpec((bq, HC * DC), lambda b, n: (b * nb + n, 0)),
        out_shape=jax.ShapeDtypeStruct((B * S, HC * DC), BF16),
        compiler_params=pltpu.CompilerParams(dimension_semantics=("arbitrary", "arbitrary"),
                                             vmem_limit_bytes=VMEM_LIMIT),
        name="attn_c_bounded" if bounded else "attn_c",
    )(sc, qT, k, vT, tile)


def _out_kernel(x_ref, ya_ref, yb_ref, yc_ref, wo_ref, g_ref, wr_ref, xn_ref, h2_ref, affT_ref):
    na, nb_ = HA * DA_V, HA * DA_V + HB * DB_V
    xn = (x_ref[...] + _dot(ya_ref[...], wo_ref[0:na, :]) + _dot(yb_ref[...], wo_ref[na:nb_, :])
          + _dot(yc_ref[...], wo_ref[nb_:, :]))
    xn_ref[...] = xn
    h2 = (xn * lax.rsqrt(jnp.mean(xn * xn, axis=-1, keepdims=True) + EPS) * g_ref[...]).astype(BF16)
    h2_ref[...] = h2
    lg = _dot(h2, wr_ref[...]).T[0:N_EXPERTS]
    e = jnp.exp(lg - jnp.max(lg, axis=0, keepdims=True))
    affT_ref[...] = e / jnp.sum(e, axis=0, keepdims=True)


def _out_proj(x2, ya, yb, yc, lw):
    T = x2.shape[0]
    tm = TM_OUT

    def full(a):
        return pl.BlockSpec(a.shape, lambda i: (0,) * a.ndim)

    return pl.pallas_call(
        _out_kernel,
        grid=(T // tm,),
        in_specs=[pl.BlockSpec((tm, D_MODEL), lambda i: (i, 0)),
                  pl.BlockSpec((tm, ya.shape[1]), lambda i: (i, 0)),
                  pl.BlockSpec((tm, yb.shape[1]), lambda i: (i, 0)),
                  pl.BlockSpec((tm, yc.shape[1]), lambda i: (i, 0)),
                  full(lw['w_out']), full(lw['g_ffn']), full(lw['w_r'])],
        out_specs=[pl.BlockSpec((tm, D_MODEL), lambda i: (i, 0)),
                   pl.BlockSpec((tm, D_MODEL), lambda i: (i, 0)),
                   pl.BlockSpec((N_EXPERTS, tm), lambda i: (0, i))],
        out_shape=[jax.ShapeDtypeStruct((T, D_MODEL), F32),
                   jax.ShapeDtypeStruct((T, D_MODEL), BF16),
                   jax.ShapeDtypeStruct((N_EXPERTS, T), F32)],
        compiler_params=pltpu.CompilerParams(dimension_semantics=("arbitrary",), vmem_limit_bytes=VMEM_LIMIT),
        name="out_proj",
    )(x2, ya, yb, yc, lw['w_out'], lw['g_ffn'], lw['w_r'])


def _topk_kernel(aff_ref, thr_ref, tcut_ref, *, cap, T):
    capf = float(cap)

    def bits():
        return lax.bitcast_convert_type(aff_ref[...], jnp.int32)

    def count(mask):
        return jnp.sum(jnp.where(mask, 1.0, 0.0), axis=1, keepdims=True)

    def vbody(i, v):
        cand = v | jnp.left_shift(jnp.int32(1), 30 - i)
        return jnp.where(count(bits() >= cand) >= capf, cand, v)

    thr = lax.fori_loop(0, 31, vbody, jnp.zeros((N_EXPERTS, 1), jnp.int32))
    need = capf - count(bits() > thr)
    nbits = max(1, int(math.ceil(math.log2(T))))

    def tbody(i, c):
        cand = c | jnp.left_shift(jnp.int32(1), nbits - 1 - i)
        idx = lax.broadcasted_iota(jnp.int32, (N_EXPERTS, T), 1)
        f = count((bits() == thr) & (idx < cand))
        return jnp.where(f < need, cand, c)

    tcut = lax.fori_loop(0, nbits, tbody, jnp.zeros((N_EXPERTS, 1), jnp.int32))
    thr_ref[...] = jnp.broadcast_to(thr, thr_ref.shape)
    tcut_ref[...] = jnp.broadcast_to(tcut, tcut_ref.shape)


def _topk_thresholds(affT):
    E, T = affT.shape
    cap = CAPACITY_FACTOR * T // N_EXPERTS
    return pl.pallas_call(
        functools.partial(_topk_kernel, cap=cap, T=T),
        grid=(1,),
        in_specs=[pl.BlockSpec((E, T), lambda i: (0, 0))],
        out_specs=[pl.BlockSpec((E, 128), lambda i: (0, 0))] * 2,
        out_shape=[jax.ShapeDtypeStruct((E, 128), jnp.int32)] * 2,
        compiler_params=pltpu.CompilerParams(dimension_semantics=("arbitrary",), vmem_limit_bytes=VMEM_LIMIT),
        name="topk_thr",
    )(affT)


def _exclusive_prefix_count(sel, tri):
    e, n = sel.shape
    ones = jnp.where(sel, 1.0, 0.0)
    out, before = [], jnp.zeros((e, 1), F32)
    for kt in range(n // 128):
        blk = ones[:, kt * 128:(kt + 1) * 128]
        out.append(_dot(blk.astype(BF16), tri) + before)
        before = before + jnp.sum(blk, axis=1, keepdims=True)
    return jnp.concatenate(out, axis=1)


def _moe_kernel(x_ref, h2_ref, affT_ref, thr_ref, tcut_ref, tri_ref, wg_ref, wu_ref, wd_ref, o_ref, pos_scr, y_scr):
    i = pl.program_id(0)
    e = pl.program_id(1)
    tt = TT_MOE

    def one_hot_rows(posm, c):
        rel = posm - (c * CH_MOE).astype(F32)
        jj = lax.broadcasted_iota(jnp.int32, (CH_MOE, tt), 0).astype(F32)
        return jnp.broadcast_to(rel, (CH_MOE, tt)) == jj

    @pl.when(e == 0)
    def _():
        b = lax.bitcast_convert_type(affT_ref[...], jnp.int32)
        thr = thr_ref[:, 0:1]
        tc = tcut_ref[:, 0:1]
        tg = i * (tt * SUBS_MOE) + lax.broadcasted_iota(jnp.int32, b.shape, 1)
        sel = (b > thr) | ((b == thr) & (tg <= tc))
        for sub in range(SUBS_MOE):
            sl = slice(sub * tt, (sub + 1) * tt)
            pos = _exclusive_prefix_count(sel[:, sl], tri_ref[...])
            posm_all = jnp.where(sel[:, sl], pos, -1.0)
            pos_scr[:, sl] = posm_all
        o_ref[...] = x_ref[...]

    posm = [pos_scr[pl.ds(e, 1), sub * tt:(sub + 1) * tt] for sub in range(SUBS_MOE)]
    gate = [affT_ref[pl.ds(e, 1), sub * tt:(sub + 1) * tt] for sub in range(SUBS_MOE)]
    n_sel = jnp.max(jnp.concatenate(posm, axis=0)) + 1.0
    n_ch = (n_sel.astype(jnp.int32) + CH_MOE - 1) // CH_MOE

    def chunk(c, first):
        hits, xes = [], []
        for sub in range(SUBS_MOE):
            hit = one_hot_rows(posm[sub], c)
            xes.append(_dot(jnp.where(hit, 1.0, 0.0).astype(BF16), h2_ref[sub * tt:(sub + 1) * tt, :]).astype(BF16))
            hits.append(hit)
        xe = jnp.concatenate(xes, axis=0)
        g = _dot(xe, wg_ref[...])
        u = _dot(xe, wu_ref[...])
        hid = (g * jax.nn.sigmoid(g) * u).astype(BF16)
        ye = _dot(hid, wd_ref[...])
        for sub in range(SUBS_MOE):
            gc = jnp.sum(jnp.where(hits[sub], jnp.broadcast_to(gate[sub], hits[sub].shape), 0.0),
                         axis=1, keepdims=True)
            yg = (ye[sub * CH_MOE:(sub + 1) * CH_MOE] * gc).astype(BF16)
            if first:
                y_scr[sub, pl.ds(pl.multiple_of(e * CH_MOE, CH_MOE), CH_MOE), :] = yg
            else:
                yeb = jnp.concatenate([yg, jnp.zeros((CHP_MOE - CH_MOE, yg.shape[1]), BF16)], axis=0)
                onehot = jnp.concatenate([jnp.where(hits[sub], 1.0, 0.0).astype(BF16),
                                          jnp.zeros((CHP_MOE - CH_MOE, tt), BF16)], axis=0)
                o_ref[sub * tt:(sub + 1) * tt, :] += _dot(onehot.T, yeb)

    chunk(jnp.int32(0), True)
    lax.fori_loop(1, n_ch, lambda c, carry: (chunk(c, False), carry)[1], 0)

    @pl.when(e == N_EXPERTS - 1)
    def _():
        for sub in range(SUBS_MOE):
            sl = slice(sub * tt, (sub + 1) * tt)
            for grp in range(N_EXPERTS // SCATTER_GROUP):
                hits = [one_hot_rows(pos_scr[ee:ee + 1, sl], jnp.int32(0))
                        for ee in range(grp * SCATTER_GROUP, (grp + 1) * SCATTER_GROUP)]
                onehot = jnp.where(jnp.concatenate(hits, axis=0), 1.0, 0.0).astype(BF16)
                rows = slice(grp * SCATTER_GROUP * CH_MOE, (grp + 1) * SCATTER_GROUP * CH_MOE)
                o_ref[sl, :] += _dot(onehot.T, y_scr[sub, rows, :])


def _moe(xn, h2, affT, thr, tcut, lw):
    T = xn.shape[0]
    tt = TT_MOE * SUBS_MOE
    once = pl.Buffered(1)
    return pl.pallas_call(
        _moe_kernel,
        grid=(T // tt, N_EXPERTS),
        in_specs=[
            pl.BlockSpec((tt, D_MODEL), lambda i, e: (i, 0), pipeline_mode=once),
            pl.BlockSpec((tt, D_MODEL), lambda i, e: (i, 0), pipeline_mode=once),
            pl.BlockSpec((N_EXPERTS, tt), lambda i, e: (0, i)),
            pl.BlockSpec((N_EXPERTS, 128), lambda i, e: (0, 0)),
            pl.BlockSpec((N_EXPERTS, 128), lambda i, e: (0, 0)),
            pl.BlockSpec((128, 128), lambda i, e: (0, 0), pipeline_mode=once),
            pl.BlockSpec((None, D_MODEL, D_EXPERT), lambda i, e: (e, 0, 0)),
            pl.BlockSpec((None, D_MODEL, D_EXPERT), lambda i, e: (e, 0, 0)),
            pl.BlockSpec((None, D_EXPERT, D_MODEL), lambda i, e: (e, 0, 0)),
        ],
        out_specs=pl.BlockSpec((tt, D_MODEL), lambda i, e: (i, 0), pipeline_mode=once),
        out_shape=jax.ShapeDtypeStruct((T, D_MODEL), F32),
        scratch_shapes=[pltpu.VMEM((N_EXPERTS, tt), F32),
                        pltpu.VMEM((SUBS_MOE, N_EXPERTS * CH_MOE, D_MODEL), BF16)],
        compiler_params=pltpu.CompilerParams(dimension_semantics=("arbitrary", "arbitrary"),
                                             vmem_limit_bytes=VMEM_LIMIT),
        name="moe_ffn",
    )(xn, h2, affT, thr, tcut, lw['tri'], lw['w_gate'], lw['w_up'], lw['w_down'])


def _t5_bucket(rel):
    nb = NUM_BUCKETS // 2
    max_exact = nb // 2
    ret = jnp.where(rel > 0, nb, 0)
    n = jnp.abs(rel)
    nf = jnp.maximum(n, 1).astype(jnp.float32)
    large = max_exact + (jnp.log(nf / max_exact) / math.log(MAX_DISTANCE / max_exact) * (nb - max_exact)).astype(jnp.int32)
    large = jnp.minimum(large, nb - 1)
    return ret + jnp.where(n < max_exact, n, large)


def _col(v):
    return v.astype(F32).reshape(-1, 1)


def _pad_heads(w, n_heads, d, d_pad):
    k = w.shape[1]
    w3 = w.reshape(n_heads, d, k)
    return jnp.pad(w3, ((0, 0), (0, d_pad - d), (0, 0))).reshape(n_heads * d_pad, k)


def _toeplitz_tiles(fn, offsets, nk, nq):
    span = nk + nq
    out = []
    for off in offsets:
        u = fn(off + nk - 1 - jnp.arange(span, dtype=jnp.int32))
        rows = jnp.tile(u, (1, nk))[:, :nk * (span - 1)].reshape(u.shape[0], nk, span - 1)
        out.append(rows[:, :, nk - 1:nk - 1 + nq])
    return jnp.stack(out, axis=0)


def _prep_shared(rel_bias):
    rb = rel_bias.astype(F32) * LOG2E

    def bias_a(rel):
        return rb[:, :HA][_t5_bucket(rel)].T

    def bias_c(rel):
        return rb[:, HA:][_t5_bucket(rel)].T

    def in_band(rel):
        return (jnp.abs(rel) <= WINDOW)[None, :]

    bmax = jnp.max(rb[:, :HA], axis=0)
    near_offs = [d * BK_A for d in range(-1, NEAR_A - 1)]
    near_a = _toeplitz_tiles(lambda rel: jnp.exp2(bias_a(rel) - bmax[:, None]), near_offs, BK_A, BQ_A)
    far = _t5_bucket(jnp.array([-(1 << 20), 1 << 20], dtype=jnp.int32))
    sc_far = jnp.concatenate([rb[far[1], :HA], bmax, rb[far[0], :HA]])
    a_bounded = {
        'near': _toeplitz_tiles(lambda rel: jnp.exp2(bias_a(rel)), near_offs, BK_A, BQ_A),
        'sc': jnp.concatenate([jnp.exp2(rb[far[0], :HA]), jnp.exp2(rb[far[1], :HA])]),
        'bias_mag': jnp.max(jnp.abs(rb[:, :HA])),
    }
    bm_c = _toeplitz_tiles(lambda rel: jnp.where(in_band(rel), bias_c(rel), NEG_BIG),
                           [0, -BLOCK, -2 * BLOCK], KW_C, BLOCK)
    c_bounded = {
        'e': _toeplitz_tiles(lambda rel: jnp.where(in_band(rel), jnp.exp2(bias_c(rel)), 0.0),
                             [0, -WINDOW, -(KWB_C - QB_C)], KWB_C, QB_C),
        'bias_mag': jnp.max(jnp.abs(rb[:, HA:])),
    }
    return near_a, sc_far, bm_c, a_bounded, c_bounded


def _rope_tables(S):
    inv = 1.0 / (ROPE_BASE ** (jnp.arange(ROPE_HALF, dtype=jnp.float32) / ROPE_HALF))
    ang = jnp.arange(S, dtype=jnp.int32).astype(jnp.float32)[:, None] * inv[None, :]
    return jnp.cos(ang).T, jnp.sin(ang).T


def _logit_bounds(l, p, a_bounded, c_bounded, sa, sb, scc):
    qa, ka = p['qn_a'][l].astype(F32), p['kn_a'][l].astype(F32)
    bound_a = 1.01 * DA_QK * jnp.max(jnp.abs(qa * ka)) * sa + 2.0 * a_bounded['bias_mag']
    qb, kb = p['qn_b'][l].astype(F32), p['kn_b'][l].astype(F32)
    bound_b = 1.01 * DB_QK * jnp.max(jnp.abs(qb)) * jnp.max(jnp.abs(kb)) * sb
    qc, kc = p['qn_c'][l].astype(F32), p['kn_c'][l].astype(F32)
    bound_c = (1.01 * DC * jnp.max(jnp.abs(qc * kc)) * scc + 2.0 * c_bounded['bias_mag']
               + jnp.max(jnp.abs(p['sink_c'][l].astype(F32))) * LOG2E)
    return bound_a <= BOUND_LIMIT, bound_b <= BOUND_LIMIT, bound_c <= BOUND_LIMIT


def _prep_layer(l, p, shared):
    near_a, sc_far, bm_c, a_bounded, c_bounded = shared
    lam_init = 0.8 - 0.6 * math.exp(-0.3 * l)
    lam = (jnp.exp(jnp.sum(p['lam_q1'][l].astype(F32) * p['lam_k1'][l].astype(F32)))
           - jnp.exp(jnp.sum(p['lam_q2'][l].astype(F32) * p['lam_k2'][l].astype(F32))) + lam_init)
    w_uq = p['w_uq'][l]
    w_ukv = p['w_ukv'][l].reshape(KV_RANK, HB, DB_NOPE + DB_V)
    sa = DA_QK ** -0.5 * LOG2E
    sb = DB_QK ** -0.5 * LOG2E
    scc = DC ** -0.5 * LOG2E
    zpad = jnp.zeros((DB_PAD - DB_QK,), F32)
    lw = {
        'g_mix': p['norm_mix_g'][l].astype(F32).reshape(1, D_MODEL),
        'w_inT': p['w_in'][l].T.astype(BF16),
        'w_uqT': _pad_heads(w_uq.T, HB, DB_QK, DB_PAD).astype(BF16),
        'w_ukT': w_ukv[:, :, :DB_NOPE].reshape(KV_RANK, HB * DB_NOPE).T.astype(BF16),
        'w_uvT': w_ukv[:, :, DB_NOPE:].reshape(KV_RANK, HB * DB_V).T.astype(BF16),
        'gqa': _col(jnp.tile(p['qn_a'][l], 2 * HA) * sa),
        'gka': _col(jnp.tile(p['kn_a'][l], 2 * HA)),
        'gcq': _col(p['g_cq'][l]),
        'gckv': _col(p['g_ckv'][l]),
        'gqb': _col(jnp.tile(jnp.concatenate([p['qn_b'][l].astype(F32) * sb, zpad]), HB)),
        'gkb': _col(jnp.tile(jnp.concatenate([p['kn_b'][l].astype(F32), zpad]), HB)),
        'gqc': _col(jnp.tile(p['qn_c'][l], HC) * scc),
        'gkc': _col(jnp.tile(p['kn_c'][l], KVH_C)),
        'near_a': near_a,
        'sc_a': jnp.concatenate([sc_far, lam.reshape(1)]).astype(F32),
        'subln': _col(jnp.tile(p['subln_a'][l], HA) * (1.0 - lam_init)),
        'bm_c': bm_c,
        'sc_c': p['sink_c'][l].astype(F32) * LOG2E,
        'w_out': p['w_out'][l].astype(BF16),
        'g_ffn': p['norm_ffn_g'][l].astype(F32).reshape(1, D_MODEL),
        'w_r': jnp.pad(p['w_router'][l], ((0, 0), (0, 128 - N_EXPERTS))).astype(BF16),
        'w_gate': p['w_gate'][l].astype(BF16),
        'w_up': p['w_up'][l].astype(BF16),
        'w_down': p['w_down'][l].astype(BF16),
        'tri': jnp.triu(jnp.ones((128, 128), F32), k=1).astype(BF16),
        'near_a_bounded': a_bounded['near'],
        'sc_a_bounded': jnp.concatenate([a_bounded['sc'], lam.reshape(1)]).astype(F32),
    }
    lw['e_c'] = c_bounded['e']
    lw['sc_c_bounded'] = jnp.concatenate([lw['sc_c'], jnp.exp2(lw['sc_c'])])
    lw['bounded_a'], lw['bounded_b'], lw['bounded_c'] = _logit_bounds(l, p, a_bounded, c_bounded, sa, sb, scc)
    return lw


def _trunk(x, layers):
    B, S, D = x.shape
    x2 = x.reshape(B * S, D)
    cosT, sinT = _rope_tables(S)
    for lw in layers:
        lw = dict(lw, cosT=cosT, sinT=sinT)
        qaT, ka, vaT, qbT, kb, vbT, qcT, kc, vcT = _in_proj(x2, lw, S)
        ya = lax.cond(lw['bounded_a'],
                      lambda q, k, v: _attn_a(q, k, v, lw, B, S, True),
                      lambda q, k, v: _attn_a(q, k, v, lw, B, S, False), qaT, ka, vaT)
        yb = lax.cond(lw['bounded_b'],
                      lambda q, k, v: _attn_b(q, k, v, B, S, True),
                      lambda q, k, v: _attn_b(q, k, v, B, S, False), qbT, kb, vbT)
        yc = lax.cond(lw['bounded_c'],
                      lambda q, k, v: _attn_c(q, k, v, lw, B, S, True),
                      lambda q, k, v: _attn_c(q, k, v, lw, B, S, False), qcT, kc, vcT)
        xn, h2, affT = _out_proj(x2, ya, yb, yc, lw)
        thr, tcut = _topk_thresholds(affT)
        x2 = _moe(xn, h2, affT, thr, tcut, lw)
    return x2.reshape(B, S, D)


def kernel(x_prompt, x_sample, rel_bias, norm_mix_g, w_in, qn_a, kn_a, lam_q1, lam_k1, lam_q2, lam_k2, subln_a, g_cq, w_uq, g_ckv, w_ukv, qn_b, kn_b, qn_c, kn_c, sink_c, w_out, norm_ffn_g, w_router, w_gate, w_up, w_down):
    p = dict(norm_mix_g=norm_mix_g, w_in=w_in, qn_a=qn_a, kn_a=kn_a, lam_q1=lam_q1, lam_k1=lam_k1,
             lam_q2=lam_q2, lam_k2=lam_k2, subln_a=subln_a, g_cq=g_cq, w_uq=w_uq, g_ckv=g_ckv, w_ukv=w_ukv,
             qn_b=qn_b, kn_b=kn_b, qn_c=qn_c, kn_c=kn_c, sink_c=sink_c, w_out=w_out, norm_ffn_g=norm_ffn_g,
             w_router=w_router, w_gate=w_gate, w_up=w_up, w_down=w_down)
    shared = _prep_shared(rel_bias)
    layers = [_prep_layer(l, p, shared) for l in range(w_in.shape[0])]
    return (_trunk(x_prompt, layers), _trunk(x_sample, layers))
```

```python
import functools
import math

import jax
import jax.numpy as jnp
import numpy as np
from jax import lax
from jax.experimental import pallas as pl
from jax.experimental.pallas import tpu as pltpu

D_MODEL = 1024
BLOCK = 128
HA, DA_QK, DA_V = 4, 32, 64
HB, Q_RANK, KV_RANK, DB_NOPE, DB_ROPE, DB_V = 6, 384, 256, 64, 32, 64
ROPE_BASE = 10000.0
HC, KVH_C, DC, WINDOW = 6, 2, 64, 128
G_C = HC // KVH_C
NUM_BUCKETS, MAX_DISTANCE = 32, 128
N_EXPERTS, D_EXPERT, CAPACITY_FACTOR = 16, 1024, 2
EPS = 1e-6

DB_QK = DB_NOPE + DB_ROPE
DB_PAD = 128
VX = 80
BOUND_LIMIT = 80.0
ROPE_HALF = DB_ROPE // 2
IN_SPLITS = (HA * 2 * DA_QK, HA * 2 * DA_QK, HA * DA_V, Q_RANK, KV_RANK, DB_ROPE, HC * DC, KVH_C * DC, KVH_C * DC)
IN_OFFS = tuple(int(v) for v in np.cumsum((0,) + IN_SPLITS))
IN_COLS = IN_OFFS[-1]
LOG2E = 1.4426950408889634
NEG_BIG = -1e30

F32 = jnp.float32
BF16 = jnp.bfloat16

TM_IN = 1024
TM_OUT = 1024
BQ_A = 512
BK_A = 512
UNROLL_A = 4
NEAR_A = BQ_A // BK_A + 2
assert BQ_A % BK_A == 0 and BK_A >= MAX_DISTANCE
BQ_B = 512
BK_B = 512
UNROLL_B = 4
KW_C = 3 * BLOCK
QB_C = 256
KWB_C = QB_C + 2 * WINDOW
SUB_C = 2
TT_MOE = 1024
SUBS_MOE = 2
SCATTER_GROUP = 8
CH_MOE = 160
CHP_MOE = 256
PREFETCH = 3
PREFETCH_AB = 2
VMEM_LIMIT = 56 * 1024 * 1024


def _dot(a, b):
    return jnp.dot(a, b, preferred_element_type=F32)


def _group_norm_rows(y, gs, n_valid):
    rows, tm = y.shape
    y3 = y.reshape(rows // gs, gs, tm)
    ms = jnp.sum(y3 * y3, axis=1, keepdims=True) * (1.0 / n_valid)
    return (y3 * lax.rsqrt(ms + EPS)).reshape(rows, tm)


def _in_kernel(x_ref, g_ref, winT_ref, wuqT_ref, wukT_ref, wuvT_ref,
               gqa_ref, gka_ref, gcq_ref, gckv_ref, gqb_ref, gkb_ref, gqc_ref, gkc_ref,
               cos_ref, sin_ref,
               qaT_ref, ka_ref, vaT_ref, qbT_ref, kb_ref, vbT_ref, qcT_ref, kc_ref, vcT_ref):
    x = x_ref[...]
    h = x * lax.rsqrt(jnp.mean(x * x, axis=-1, keepdims=True) + EPS) * g_ref[...]
    projT = _dot(winT_ref[...], h.T.astype(BF16))

    def seg(i):
        return projT[IN_OFFS[i]:IN_OFFS[i + 1]]

    cos = cos_ref[...]
    sin = sin_ref[...]

    def with_ones_rows(v, n_heads, d):
        tm = v.shape[1]
        tail = jnp.where(lax.broadcasted_iota(jnp.int32, (VX - d, tm), 0) == 0, 1.0, 0.0)
        rows = []
        for hh in range(n_heads):
            rows += [v[hh * d:(hh + 1) * d], tail]
        return jnp.concatenate(rows, axis=0).astype(BF16)

    qaT_ref[...] = (_group_norm_rows(seg(0), DA_QK, DA_QK) * gqa_ref[...]).astype(BF16)
    kaT = _group_norm_rows(seg(1), DA_QK, DA_QK) * gka_ref[...]
    ka_ref[...] = kaT.T.astype(BF16)
    vaT_ref[...] = with_ones_rows(seg(2), HA, DA_V)

    def rope_store(dst_rows, y, h):
        o = h * DB_PAD
        x1 = y[o + DB_NOPE:o + DB_NOPE + ROPE_HALF]
        x2 = y[o + DB_NOPE + ROPE_HALF:o + DB_QK]
        dst_rows.append(y[o:o + DB_NOPE])
        dst_rows.append(x1 * cos - x2 * sin)
        dst_rows.append(x2 * cos + x1 * sin)
        dst_rows.append(jnp.zeros((DB_PAD - DB_QK, y.shape[1]), F32))

    cqT = seg(3)
    cqn = (cqT * lax.rsqrt(jnp.mean(cqT * cqT, axis=0, keepdims=True) + EPS) * gcq_ref[...]).astype(BF16)
    qf = _group_norm_rows(_dot(wuqT_ref[...], cqn), DB_PAD, DB_QK) * gqb_ref[...]
    rows = []
    for hh in range(HB):
        rope_store(rows, qf, hh)
    qbT_ref[...] = jnp.concatenate(rows, axis=0).astype(BF16)

    ckvT = seg(4)
    ckvn = (ckvT * lax.rsqrt(jnp.mean(ckvT * ckvT, axis=0, keepdims=True) + EPS) * gckv_ref[...]).astype(BF16)
    knope = _dot(wukT_ref[...], ckvn)
    vbT_ref[...] = with_ones_rows(_dot(wuvT_ref[...], ckvn), HB, DB_V)
    krope = seg(5)
    zpad = jnp.zeros((DB_PAD - DB_QK, krope.shape[1]), F32)
    rows = []
    for hh in range(HB):
        rows += [knope[hh * DB_NOPE:(hh + 1) * DB_NOPE], krope, zpad]
    kf = _group_norm_rows(jnp.concatenate(rows, axis=0), DB_PAD, DB_QK) * gkb_ref[...]
    rows = []
    for hh in range(HB):
        rope_store(rows, kf, hh)
    kb_ref[...] = jnp.concatenate(rows, axis=0).T.astype(BF16)

    qcT_ref[...] = (_group_norm_rows(seg(6), DC, DC) * gqc_ref[...]).astype(BF16)
    kcT = _group_norm_rows(seg(7), DC, DC) * gkc_ref[...]
    kc_ref[...] = kcT.T.astype(BF16)
    vcT_ref[...] = with_ones_rows(seg(8), KVH_C, DC)


def _in_proj(x2, lw, S):
    T = x2.shape[0]
    tm = TM_IN
    nt = T // tm
    spb = S // tm

    def full(a):
        return pl.BlockSpec(a.shape, lambda i: (0,) * a.ndim)

    consts = [lw['g_mix'], lw['w_inT'], lw['w_uqT'], lw['w_ukT'], lw['w_uvT'],
              lw['gqa'], lw['gka'], lw['gcq'], lw['gckv'], lw['gqb'], lw['gkb'], lw['gqc'], lw['gkc']]
    in_specs = ([pl.BlockSpec((tm, D_MODEL), lambda i: (i, 0))] + [full(a) for a in consts]
                + [pl.BlockSpec((ROPE_HALF, tm), lambda i: (0, i % spb))] * 2)

    def fm(rows):
        return jax.ShapeDtypeStruct((rows, T), BF16), pl.BlockSpec((rows, tm), lambda i: (0, i))

    def tk(cols):
        return jax.ShapeDtypeStruct((T, cols), BF16), pl.BlockSpec((tm, cols), lambda i: (i, 0))

    outs = [fm(HA * 2 * DA_QK), tk(HA * 2 * DA_QK), fm(HA * VX),
            fm(HB * DB_PAD), tk(HB * DB_PAD), fm(HB * VX),
            fm(HC * DC), tk(KVH_C * DC), fm(KVH_C * VX)]
    return pl.pallas_call(
        _in_kernel,
        grid=(nt,),
        in_specs=in_specs,
        out_specs=[o[1] for o in outs],
        out_shape=[o[0] for o in outs],
        compiler_params=pltpu.CompilerParams(dimension_semantics=("arbitrary",), vmem_limit_bytes=VMEM_LIMIT),
        name="in_proj",
    )(x2, *consts, lw['cosT'], lw['sinT'])


def _attn_a_kernel(sc_ref, qT_ref, k_ref, vT_ref, near_ref, subln_ref, o_ref, w_scr, m_scr, l_scr, acc_scr, *, nk):
    qb = pl.program_id(1)
    n_comb = 2 * HA
    bq = qT_ref.shape[1]
    near_lo = (BQ_A // BK_A) * qb - 1

    for hc in range(n_comb):
        g = hc // 4
        qg = qT_ref[g * 128:(g + 1) * 128, :]
        row = lax.broadcasted_iota(jnp.int32, (128, bq), 0)
        keep = (row // DA_QK) == (hc % 4)
        w_scr[hc] = jnp.where(keep, qg, jnp.zeros_like(qg))
    m_scr[...] = jnp.full(m_scr.shape, NEG_BIG, F32)
    l_scr[...] = jnp.zeros(l_scr.shape, F32)
    acc_scr[...] = jnp.zeros(acc_scr.shape, F32)

    def step(kb, mode):
        ks = pl.multiple_of(kb * BK_A, BK_A)

        def qk(hc):
            g = hc // 4
            return _dot(k_ref[pl.ds(ks, BK_A), g * 128:(g + 1) * 128], w_scr[hc])

        ahead = [qk(i) for i in range(PREFETCH)]
        for hc in range(n_comb):
            h = hc // 2
            s = ahead.pop(0)
            if hc + PREFETCH < n_comb:
                ahead.append(qk(hc + PREFETCH))
            c = sc_ref[(1 - mode) * HA + h]
            cmax = jnp.max(s, axis=0, keepdims=True) + c
            m_old = m_scr[hc]
            m_new = jnp.maximum(m_old, cmax)
            alpha = jnp.exp2(m_old - m_new)
            p = jnp.exp2(s - (m_new - c))
            if mode == 0:
                p = p * near_ref[kb - near_lo, h]
            l_scr[hc] = alpha * l_scr[hc] + jnp.sum(p, axis=0, keepdims=True)
            pv = _dot(vT_ref[h * VX:h * VX + DA_V, pl.ds(ks, BK_A)], p.astype(BF16))
            acc_scr[hc] = alpha * acc_scr[hc] + pv
            m_scr[hc] = m_new

    lo = jnp.maximum(near_lo, 0)
    hi = jnp.minimum(near_lo + NEAR_A, nk)
    lax.fori_loop(0, lo, lambda kb, c: (step(kb, -1), c)[1], 0)
    lax.fori_loop(lo, hi, lambda kb, c: (step(kb, 0), c)[1], 0)
    lax.fori_loop(hi, nk, lambda kb, c: (step(kb, 1), c)[1], 0)

    lam = sc_ref[3 * HA]
    outs = []
    for h in range(HA):
        a = acc_scr[2 * h] / l_scr[2 * h] - lam * (acc_scr[2 * h + 1] / l_scr[2 * h + 1])
        a = a * lax.rsqrt(jnp.mean(a * a, axis=0, keepdims=True) + EPS)
        outs.append(a)
    o = jnp.concatenate(outs, axis=0) * subln_ref[...]
    o_ref[...] = o.T.astype(BF16)


def _sublane_partial_sum(p):
    n, w = p.shape
    return jnp.sum(p.reshape(n // 8, 8, w), axis=0)


def _attn_a_bounded_kernel(sc_ref, qT_ref, k_ref, vT_ref, near_ref, subln_ref, o_ref, w_scr, acc_scr, l_scr, *, nk):
    qb = pl.program_id(1)
    n_comb = 2 * HA
    bq = qT_ref.shape[1]
    near_lo = (BQ_A // BK_A) * qb - 1
    for hc in range(n_comb):
        g = hc // 4
        qg = qT_ref[g * 128:(g + 1) * 128, :]
        row = lax.broadcasted_iota(jnp.int32, (128, bq), 0)
        w_scr[hc] = jnp.where((row // DA_QK) == (hc % 4), qg, jnp.zeros_like(qg))
    acc_scr[...] = jnp.zeros(acc_scr.shape, F32)
    l_scr[...] = jnp.zeros(l_scr.shape, F32)

    lo = jnp.maximum(near_lo, 0)
    hi = jnp.minimum(near_lo + NEAR_A, nk)
    n_far = lo + (nk - hi)

    def step(j0, n_blocks, near):
        def block(jj):
            j = j0 + jj
            if near:
                return j, None
            right = j >= lo
            return jnp.where(right, j - lo + hi, j), jnp.where(right, HA, 0)

        tiles = [(jj, hc) for jj in range(n_blocks) for hc in range(n_comb)]

        def qk(t):
            jj, hc = tiles[t]
            g = hc // 4
            ks = pl.multiple_of(block(jj)[0] * BK_A, BK_A)
            return _dot(k_ref[pl.ds(ks, BK_A), g * 128:(g + 1) * 128], w_scr[hc])

        ahead = [qk(i) for i in range(PREFETCH_AB)]
        for t, (jj, hc) in enumerate(tiles):
            h = hc // 2
            s = ahead.pop(0)
            if t + PREFETCH_AB < len(tiles):
                ahead.append(qk(t + PREFETCH_AB))
            kb, side = block(jj)
            ks = pl.multiple_of(kb * BK_A, BK_A)
            p = jnp.exp2(s)
            if near:
                p = p * near_ref[kb - near_lo, h]
            psum = _sublane_partial_sum(p)
            pv = _dot(vT_ref[h * VX:h * VX + DA_V, pl.ds(ks, BK_A)], p.astype(BF16))
            if near:
                acc_scr[hc] += pv
                l_scr[hc] += psum
            else:
                f = sc_ref[side + h]
                acc_scr[hc] += pv * f
                l_scr[hc] += psum * f

    n_pairs = n_far // UNROLL_A
    lax.fori_loop(0, n_pairs, lambda i, c: (step(i * UNROLL_A, UNROLL_A, False), c)[1], 0)
    lax.fori_loop(n_pairs * UNROLL_A, n_far, lambda j, c: (step(j, 1, False), c)[1], 0)
    lax.fori_loop(lo, hi, lambda kb, c: (step(kb, 1, True), c)[1], 0)

    lam = sc_ref[2 * HA]
    outs = []
    for h in range(HA):
        l0 = jnp.sum(l_scr[2 * h], axis=0, keepdims=True)
        l1 = jnp.sum(l_scr[2 * h + 1], axis=0, keepdims=True)
        a = acc_scr[2 * h] / l0 - lam * (acc_scr[2 * h + 1] / l1)
        outs.append(a * lax.rsqrt(jnp.mean(a * a, axis=0, keepdims=True) + EPS))
    o = jnp.concatenate(outs, axis=0) * subln_ref[...]
    o_ref[...] = o.T.astype(BF16)


def _attn_a(qT, k, vT, lw, B, S, bounded):
    nq = S // BQ_A
    nk = S // BK_A
    n_comb = 2 * HA
    if bounded:
        body = functools.partial(_attn_a_bounded_kernel, nk=nk)
        scratch = [pltpu.VMEM((n_comb, 128, BQ_A), BF16), pltpu.VMEM((n_comb, DA_V, BQ_A), F32),
                   pltpu.VMEM((n_comb, 8, BQ_A), F32)]
        sc, near = lw['sc_a_bounded'], lw['near_a_bounded']
    else:
        body = functools.partial(_attn_a_kernel, nk=nk)
        scratch = [pltpu.VMEM((n_comb, 128, BQ_A), BF16), pltpu.VMEM((n_comb, 1, BQ_A), F32),
                   pltpu.VMEM((n_comb, 1, BQ_A), F32), pltpu.VMEM((n_comb, DA_V, BQ_A), F32)]
        sc, near = lw['sc_a'], lw['near_a']
    return pl.pallas_call(
        body,
        grid=(B, nq),
        in_specs=[
            pl.BlockSpec(memory_space=pltpu.SMEM),
            pl.BlockSpec((n_comb * DA_QK, BQ_A), lambda b, q: (0, b * nq + q)),
            pl.BlockSpec((S, n_comb * DA_QK), lambda b, q: (b, 0)),
            pl.BlockSpec((HA * VX, S), lambda b, q: (0, b)),
            pl.BlockSpec(near.shape, lambda b, q: (0, 0, 0, 0)),
            pl.BlockSpec((HA * DA_V, 1), lambda b, q: (0, 0)),
        ],
        out_specs=pl.BlockSpec((BQ_A, HA * DA_V), lambda b, q: (b * nq + q, 0)),
        out_shape=jax.ShapeDtypeStruct((B * S, HA * DA_V), BF16),
        scratch_shapes=scratch,
        compiler_params=pltpu.CompilerParams(dimension_semantics=("arbitrary", "arbitrary"),
                                             vmem_limit_bytes=VMEM_LIMIT),
        name="attn_a_bounded" if bounded else "attn_a",
    )(sc, qT, k, vT, near, lw['subln'])


def _attn_b_kernel(qT_ref, k_ref, vT_ref, o_ref, m_scr, l_scr, acc_scr, *, nk):
    m_scr[...] = jnp.full(m_scr.shape, NEG_BIG, F32)
    l_scr[...] = jnp.zeros(l_scr.shape, F32)
    acc_scr[...] = jnp.zeros(acc_scr.shape, F32)

    def step(kb, carry):
        ks = pl.multiple_of(kb * BK_B, BK_B)

        def qk(h):
            return _dot(k_ref[pl.ds(ks, BK_B), h * DB_PAD:(h + 1) * DB_PAD], qT_ref[h * DB_PAD:(h + 1) * DB_PAD, :])

        ahead = [qk(i) for i in range(PREFETCH)]
        for h in range(HB):
            s = ahead.pop(0)
            if h + PREFETCH < HB:
                ahead.append(qk(h + PREFETCH))
            m_old = m_scr[h]
            m_new = jnp.maximum(m_old, jnp.max(s, axis=0, keepdims=True))
            alpha = jnp.exp2(m_old - m_new)
            p = jnp.exp2(s - m_new)
            l_scr[h] = alpha * l_scr[h] + jnp.sum(p, axis=0, keepdims=True)
            pv = _dot(vT_ref[h * VX:h * VX + DB_V, pl.ds(ks, BK_B)], p.astype(BF16))
            acc_scr[h] = alpha * acc_scr[h] + pv
            m_scr[h] = m_new
        return carry

    lax.fori_loop(0, nk, step, 0)
    o = jnp.concatenate([acc_scr[h] / l_scr[h] for h in range(HB)], axis=0)
    o_ref[...] = o.T.astype(BF16)


def _attn_b_bounded_kernel(qT_ref, k_ref, vT_ref, o_ref, acc_scr, l_scr, *, nk):
    acc_scr[...] = jnp.zeros(acc_scr.shape, F32)
    l_scr[...] = jnp.zeros(l_scr.shape, F32)

    assert nk % UNROLL_B == 0
    tiles = [(j, h) for j in range(UNROLL_B) for h in range(HB)]

    def step(it, carry):
        def ks(j):
            return pl.multiple_of((it * UNROLL_B + j) * BK_B, BK_B)

        def qk(t):
            j, h = tiles[t]
            return _dot(k_ref[pl.ds(ks(j), BK_B), h * DB_PAD:(h + 1) * DB_PAD],
                        qT_ref[h * DB_PAD:(h + 1) * DB_PAD, :])

        ahead = [qk(i) for i in range(PREFETCH_AB)]
        for t, (j, h) in enumerate(tiles):
            s = ahead.pop(0)
            if t + PREFETCH_AB < len(tiles):
                ahead.append(qk(t + PREFETCH_AB))
            p = jnp.exp2(s)
            l_scr[h] += _sublane_partial_sum(p)
            acc_scr[h] += _dot(vT_ref[h * VX:h * VX + DB_V, pl.ds(ks(j), BK_B)], p.astype(BF16))
        return carry

    lax.fori_loop(0, nk // UNROLL_B, step, 0)
    o = jnp.concatenate([acc_scr[h] / jnp.sum(l_scr[h], axis=0, keepdims=True) for h in range(HB)], axis=0)
    o_ref[...] = o.T.astype(BF16)


def _attn_b(qT, k, vT, B, S, bounded):
    nq = S // BQ_B
    nk = S // BK_B
    if bounded:
        body = functools.partial(_attn_b_bounded_kernel, nk=nk)
        scratch = [pltpu.VMEM((HB, DB_V, BQ_B), F32), pltpu.VMEM((HB, 8, BQ_B), F32)]
    else:
        body = functools.partial(_attn_b_kernel, nk=nk)
        scratch = [pltpu.VMEM((HB, 1, BQ_B), F32), pltpu.VMEM((HB, 1, BQ_B), F32), pltpu.VMEM((HB, DB_V, BQ_B), F32)]
    return pl.pallas_call(
        body,
        grid=(B, nq),
        in_specs=[
            pl.BlockSpec((HB * DB_PAD, BQ_B), lambda b, q: (0, b * nq + q)),
            pl.BlockSpec((S, HB * DB_PAD), lambda b, q: (b, 0)),
            pl.BlockSpec((HB * VX, S), lambda b, q: (0, b)),
        ],
        out_specs=pl.BlockSpec((BQ_B, HB * DB_V), lambda b, q: (b * nq + q, 0)),
        out_shape=jax.ShapeDtypeStruct((B * S, HB * DB_V), BF16),
        scratch_shapes=scratch,
        compiler_params=pltpu.CompilerParams(dimension_semantics=("arbitrary", "arbitrary"),
                                             vmem_limit_bytes=VMEM_LIMIT),
        name="attn_b_bounded" if bounded else "attn_b",
    )(qT, k, vT)


def _attn_c_kernel(sc_ref, qT_ref, k_ref, vT_ref, bm_ref, o_ref, *, nb, S):
    n = pl.program_id(1)
    start = pl.multiple_of(jnp.clip((n - 1) * BLOCK, 0, S - KW_C), BLOCK)
    case = jnp.where(n == 0, 0, jnp.where(n == nb - 1, 2, 1))
    kwin = k_ref[pl.ds(start, KW_C), :]
    outs = []
    for h in range(HC):
        j = h // G_C
        qh = qT_ref[h * DC:(h + 1) * DC, :]
        zz = jnp.zeros_like(qh)
        w = jnp.concatenate([qh, zz] if j == 0 else [zz, qh], axis=0)
        s = _dot(kwin, w) + bm_ref[case, h]
        sink = sc_ref[h]
        m = jnp.maximum(jnp.max(s, axis=0, keepdims=True), sink)
        p = jnp.exp2(s - m)
        den = jnp.sum(p, axis=0, keepdims=True) + jnp.exp2(sink - m)
        pv = _dot(vT_ref[j * VX:j * VX + DC, pl.ds(start, KW_C)], p.astype(BF16))
        outs.append(pv / den)
    o_ref[...] = jnp.concatenate(outs, axis=0).T.astype(BF16)


def _attn_c_bounded_kernel(sc_ref, qT_ref, k_ref, vT_ref, e_ref, o_ref, *, n_sub, S):
    g = pl.program_id(1)
    tiles = [(u, h) for u in range(SUB_C) for h in range(HC)]

    def window(u):
        n = g * SUB_C + u
        start = pl.multiple_of(jnp.clip(n * QB_C - WINDOW, 0, S - KWB_C), WINDOW)
        case = jnp.where(n == 0, 0, jnp.where(n == n_sub - 1, 2, 1))
        return start, case

    def qk(t):
        u, h = tiles[t]
        start, _ = window(u)
        qh = qT_ref[h * DC:(h + 1) * DC, u * QB_C:(u + 1) * QB_C]
        zz = jnp.zeros_like(qh)
        w = jnp.concatenate([qh, zz] if h // G_C == 0 else [zz, qh], axis=0)
        return _dot(k_ref[pl.ds(start, KWB_C), :], w)

    ahead = [qk(i) for i in range(PREFETCH)]
    outs = []
    for t, (u, h) in enumerate(tiles):
        s = ahead.pop(0)
        if t + PREFETCH < len(tiles):
            ahead.append(qk(t + PREFETCH))
        start, case = window(u)
        j = h // G_C
        p = (jnp.exp2(s) * e_ref[case, h]).astype(BF16)
        pv = _dot(vT_ref[j * VX:(j + 1) * VX, pl.ds(start, KWB_C)], p)
        outs.append(pv[0:DC] / (pv[DC:DC + 1] + sc_ref[HC + h]))
        if h == HC - 1:
            o_ref[u * QB_C:(u + 1) * QB_C, :] = jnp.concatenate(outs, axis=0).T.astype(BF16)
            outs = []


def _attn_c(qT, k, vT, lw, B, S, bounded):
    if bounded:
        n_sub = S // QB_C
        assert n_sub >= 3 and n_sub % SUB_C == 0
        nb, bq = n_sub // SUB_C, QB_C * SUB_C
        body = functools.partial(_attn_c_bounded_kernel, n_sub=n_sub, S=S)
        tile, sc = lw['e_c'], lw['sc_c_bounded']
    else:
        nb, bq = S // BLOCK, BLOCK
        assert nb >= 3
        body = functools.partial(_attn_c_kernel, nb=nb, S=S)
        tile, sc = lw['bm_c'], lw['sc_c']
    return pl.pallas_call(
        body,
        grid=(B, nb),
        in_specs=[
            pl.BlockSpec(memory_space=pltpu.SMEM),
            pl.BlockSpec((HC * DC, bq), lambda b, n: (0, b * nb + n)),
            pl.BlockSpec((S, KVH_C * DC), lambda b, n: (b, 0)),
            pl.BlockSpec((KVH_C * VX, S), lambda b, n: (0, b)),
            pl.BlockSpec(tile.shape, lambda b, n: (0, 0, 0, 0)),
        ],
        out_specs=pl.BlockSpec((bq, HC * DC), lambda b, n: (b * nb + n, 0)),
        out_shape=jax.ShapeDtypeStruct((B * S, HC * DC), BF16),
        compiler_params=pltpu.CompilerParams(dimension_semantics=("arbitrary", "arbitrary"),
                                             vmem_limit_bytes=VMEM_LIMIT),
        name="attn_c_bounded" if bounded else "attn_c",
    )(sc, qT, k, vT, tile)


def _out_kernel(x_ref, ya_ref, yb_ref, yc_ref, wo_ref, g_ref, wr_ref, xn_ref, h2_ref, affT_ref):
    na, nb_ = HA * DA_V, HA * DA_V + HB * DB_V
    xn = (x_ref[...] + _dot(ya_ref[...], wo_ref[0:na, :]) + _dot(yb_ref[...], wo_ref[na:nb_, :])
          + _dot(yc_ref[...], wo_ref[nb_:, :]))
    xn_ref[...] = xn
    h2 = (xn * lax.rsqrt(jnp.mean(xn * xn, axis=-1, keepdims=True) + EPS) * g_ref[...]).astype(BF16)
    h2_ref[...] = h2
    lg = _dot(h2, wr_ref[...]).T[0:N_EXPERTS]
    e = jnp.exp(lg - jnp.max(lg, axis=0, keepdims=True))
    affT_ref[...] = e / jnp.sum(e, axis=0, keepdims=True)


def _out_proj(x2, ya, yb, yc, lw):
    T = x2.shape[0]
    tm = TM_OUT

    def full(a):
        return pl.BlockSpec(a.shape, lambda i: (0,) * a.ndim)

    return pl.pallas_call(
        _out_kernel,
        grid=(T // tm,),
        in_specs=[pl.BlockSpec((tm, D_MODEL), lambda i: (i, 0)),
                  pl.BlockSpec((tm, ya.shape[1]), lambda i: (i, 0)),
                  pl.BlockSpec((tm, yb.shape[1]), lambda i: (i, 0)),
                  pl.BlockSpec((tm, yc.shape[1]), lambda i: (i, 0)),
                  full(lw['w_out']), full(lw['g_ffn']), full(lw['w_r'])],
        out_specs=[pl.BlockSpec((tm, D_MODEL), lambda i: (i, 0)),
                   pl.BlockSpec((tm, D_MODEL), lambda i: (i, 0)),
                   pl.BlockSpec((N_EXPERTS, tm), lambda i: (0, i))],
        out_shape=[jax.ShapeDtypeStruct((T, D_MODEL), F32),
                   jax.ShapeDtypeStruct((T, D_MODEL), BF16),
                   jax.ShapeDtypeStruct((N_EXPERTS, T), F32)],
        compiler_params=pltpu.CompilerParams(dimension_semantics=("arbitrary",), vmem_limit_bytes=VMEM_LIMIT),
        name="out_proj",
    )(x2, ya, yb, yc, lw['w_out'], lw['g_ffn'], lw['w_r'])


def _topk_kernel(aff_ref, thr_ref, tcut_ref, *, cap, T):
    capf = float(cap)

    def bits():
        return lax.bitcast_convert_type(aff_ref[...], jnp.int32)

    def count(mask):
        return jnp.sum(jnp.where(mask, 1.0, 0.0), axis=1, keepdims=True)

    def vbody(i, v):
        cand = v | jnp.left_shift(jnp.int32(1), 30 - i)
        return jnp.where(count(bits() >= cand) >= capf, cand, v)

    thr = lax.fori_loop(0, 31, vbody, jnp.zeros((N_EXPERTS, 1), jnp.int32))
    need = capf - count(bits() > thr)
    nbits = max(1, int(math.ceil(math.log2(T))))

    def tbody(i, c):
        cand = c | jnp.left_shift(jnp.int32(1), nbits - 1 - i)
        idx = lax.broadcasted_iota(jnp.int32, (N_EXPERTS, T), 1)
        f = count((bits() == thr) & (idx < cand))
        return jnp.where(f < need, cand, c)

    tcut = lax.fori_loop(0, nbits, tbody, jnp.zeros((N_EXPERTS, 1), jnp.int32))
    thr_ref[...] = jnp.broadcast_to(thr, thr_ref.shape)
    tcut_ref[...] = jnp.broadcast_to(tcut, tcut_ref.shape)


def _topk_thresholds(affT):
    E, T = affT.shape
    cap = CAPACITY_FACTOR * T // N_EXPERTS
    return pl.pallas_call(
        functools.partial(_topk_kernel, cap=cap, T=T),
        grid=(1,),
        in_specs=[pl.BlockSpec((E, T), lambda i: (0, 0))],
        out_specs=[pl.BlockSpec((E, 128), lambda i: (0, 0))] * 2,
        out_shape=[jax.ShapeDtypeStruct((E, 128), jnp.int32)] * 2,
        compiler_params=pltpu.CompilerParams(dimension_semantics=("arbitrary",), vmem_limit_bytes=VMEM_LIMIT),
        name="topk_thr",
    )(affT)


def _exclusive_prefix_count(sel, tri):
    e, n = sel.shape
    ones = jnp.where(sel, 1.0, 0.0)
    out, before = [], jnp.zeros((e, 1), F32)
    for kt in range(n // 128):
        blk = ones[:, kt * 128:(kt + 1) * 128]
        out.append(_dot(blk.astype(BF16), tri) + before)
        before = before + jnp.sum(blk, axis=1, keepdims=True)
    return jnp.concatenate(out, axis=1)


def _moe_kernel(x_ref, h2_ref, affT_ref, thr_ref, tcut_ref, tri_ref, wg_ref, wu_ref, wd_ref, o_ref, pos_scr, y_scr):
    i = pl.program_id(0)
    e = pl.program_id(1)
    tt = TT_MOE

    def one_hot_rows(posm, c):
        rel = posm - (c * CH_MOE).astype(F32)
        jj = lax.broadcasted_iota(jnp.int32, (CH_MOE, tt), 0).astype(F32)
        return jnp.broadcast_to(rel, (CH_MOE, tt)) == jj

    @pl.when(e == 0)
    def _():
        b = lax.bitcast_convert_type(affT_ref[...], jnp.int32)
        thr = thr_ref[:, 0:1]
        tc = tcut_ref[:, 0:1]
        tg = i * (tt * SUBS_MOE) + lax.broadcasted_iota(jnp.int32, b.shape, 1)
        sel = (b > thr) | ((b == thr) & (tg <= tc))
        for sub in range(SUBS_MOE):
            sl = slice(sub * tt, (sub + 1) * tt)
            pos = _exclusive_prefix_count(sel[:, sl], tri_ref[...])
            posm_all = jnp.where(sel[:, sl], pos, -1.0)
            pos_scr[:, sl] = posm_all
        o_ref[...] = x_ref[...]

    posm = [pos_scr[pl.ds(e, 1), sub * tt:(sub + 1) * tt] for sub in range(SUBS_MOE)]
    gate = [affT_ref[pl.ds(e, 1), sub * tt:(sub + 1) * tt] for sub in range(SUBS_MOE)]
    n_sel = jnp.max(jnp.concatenate(posm, axis=0)) + 1.0
    n_ch = (n_sel.astype(jnp.int32) + CH_MOE - 1) // CH_MOE

    def chunk(c, first):
        hits, xes = [], []
        for sub in range(SUBS_MOE):
            hit = one_hot_rows(posm[sub], c)
            xes.append(_dot(jnp.where(hit, 1.0, 0.0).astype(BF16), h2_ref[sub * tt:(sub + 1) * tt, :]).astype(BF16))
            hits.append(hit)
        xe = jnp.concatenate(xes, axis=0)
        g = _dot(xe, wg_ref[...])
        u = _dot(xe, wu_ref[...])
        hid = (g * jax.nn.sigmoid(g) * u).astype(BF16)
        ye = _dot(hid, wd_ref[...])
        for sub in range(SUBS_MOE):
            gc = jnp.sum(jnp.where(hits[sub], jnp.broadcast_to(gate[sub], hits[sub].shape), 0.0),
                         axis=1, keepdims=True)
            yg = (ye[sub * CH_MOE:(sub + 1) * CH_MOE] * gc).astype(BF16)
            if first:
                y_scr[sub, pl.ds(pl.multiple_of(e * CH_MOE, CH_MOE), CH_MOE), :] = yg
            else:
                yeb = jnp.concatenate([yg, jnp.zeros((CHP_MOE - CH_MOE, yg.shape[1]), BF16)], axis=0)
                onehot = jnp.concatenate([jnp.where(hits[sub], 1.0, 0.0).astype(BF16),
                                          jnp.zeros((CHP_MOE - CH_MOE, tt), BF16)], axis=0)
                o_ref[sub * tt:(sub + 1) * tt, :] += _dot(onehot.T, yeb)

    chunk(jnp.int32(0), True)
    lax.fori_loop(1, n_ch, lambda c, carry: (chunk(c, False), carry)[1], 0)

    @pl.when(e == N_EXPERTS - 1)
    def _():
        for sub in range(SUBS_MOE):
            sl = slice(sub * tt, (sub + 1) * tt)
            for grp in range(N_EXPERTS // SCATTER_GROUP):
                hits = [one_hot_rows(pos_scr[ee:ee + 1, sl], jnp.int32(0))
                        for ee in range(grp * SCATTER_GROUP, (grp + 1) * SCATTER_GROUP)]
                onehot = jnp.where(jnp.concatenate(hits, axis=0), 1.0, 0.0).astype(BF16)
                rows = slice(grp * SCATTER_GROUP * CH_MOE, (grp + 1) * SCATTER_GROUP * CH_MOE)
                o_ref[sl, :] += _dot(onehot.T, y_scr[sub, rows, :])


def _moe(xn, h2, affT, thr, tcut, lw):
    T = xn.shape[0]
    tt = TT_MOE * SUBS_MOE
    once = pl.Buffered(1)
    return pl.pallas_call(
        _moe_kernel,
        grid=(T // tt, N_EXPERTS),
        in_specs=[
            pl.BlockSpec((tt, D_MODEL), lambda i, e: (i, 0), pipeline_mode=once),
            pl.BlockSpec((tt, D_MODEL), lambda i, e: (i, 0), pipeline_mode=once),
            pl.BlockSpec((N_EXPERTS, tt), lambda i, e: (0, i)),
            pl.BlockSpec((N_EXPERTS, 128), lambda i, e: (0, 0)),
            pl.BlockSpec((N_EXPERTS, 128), lambda i, e: (0, 0)),
            pl.BlockSpec((128, 128), lambda i, e: (0, 0), pipeline_mode=once),
            pl.BlockSpec((None, D_MODEL, D_EXPERT), lambda i, e: (e, 0, 0)),
            pl.BlockSpec((None, D_MODEL, D_EXPERT), lambda i, e: (e, 0, 0)),
            pl.BlockSpec((None, D_EXPERT, D_MODEL), lambda i, e: (e, 0, 0)),
        ],
        out_specs=pl.BlockSpec((tt, D_MODEL), lambda i, e: (i, 0), pipeline_mode=once),
        out_shape=jax.ShapeDtypeStruct((T, D_MODEL), F32),
        scratch_shapes=[pltpu.VMEM((N_EXPERTS, tt), F32),
                        pltpu.VMEM((SUBS_MOE, N_EXPERTS * CH_MOE, D_MODEL), BF16)],
        compiler_params=pltpu.CompilerParams(dimension_semantics=("arbitrary", "arbitrary"),
                                             vmem_limit_bytes=VMEM_LIMIT),
        name="moe_ffn",
    )(xn, h2, affT, thr, tcut, lw['tri'], lw['w_gate'], lw['w_up'], lw['w_down'])


def _t5_bucket(rel):
    nb = NUM_BUCKETS // 2
    max_exact = nb // 2
    ret = jnp.where(rel > 0, nb, 0)
    n = jnp.abs(rel)
    nf = jnp.maximum(n, 1).astype(jnp.float32)
    large = max_exact + (jnp.log(nf / max_exact) / math.log(MAX_DISTANCE / max_exact) * (nb - max_exact)).astype(jnp.int32)
    large = jnp.minimum(large, nb - 1)
    return ret + jnp.where(n < max_exact, n, large)


def _col(v):
    return v.astype(F32).reshape(-1, 1)


def _pad_heads(w, n_heads, d, d_pad):
    k = w.shape[1]
    w3 = w.reshape(n_heads, d, k)
    return jnp.pad(w3, ((0, 0), (0, d_pad - d), (0, 0))).reshape(n_heads * d_pad, k)


def _toeplitz_tiles(fn, offsets, nk, nq):
    span = nk + nq
    out = []
    for off in offsets:
        u = fn(off + nk - 1 - jnp.arange(span, dtype=jnp.int32))
        rows = jnp.tile(u, (1, nk))[:, :nk * (span - 1)].reshape(u.shape[0], nk, span - 1)
        out.append(rows[:, :, nk - 1:nk - 1 + nq])
    return jnp.stack(out, axis=0)


def _prep_shared(rel_bias):
    rb = rel_bias.astype(F32) * LOG2E

    def bias_a(rel):
        return rb[:, :HA][_t5_bucket(rel)].T

    def bias_c(rel):
        return rb[:, HA:][_t5_bucket(rel)].T

    def in_band(rel):
        return (jnp.abs(rel) <= WINDOW)[None, :]

    bmax = jnp.max(rb[:, :HA], axis=0)
    near_offs = [d * BK_A for d in range(-1, NEAR_A - 1)]
    near_a = _toeplitz_tiles(lambda rel: jnp.exp2(bias_a(rel) - bmax[:, None]), near_offs, BK_A, BQ_A)
    far = _t5_bucket(jnp.array([-(1 << 20), 1 << 20], dtype=jnp.int32))
    sc_far = jnp.concatenate([rb[far[1], :HA], bmax, rb[far[0], :HA]])
    a_bounded = {
        'near': _toeplitz_tiles(lambda rel: jnp.exp2(bias_a(rel)), near_offs, BK_A, BQ_A),
        'sc': jnp.concatenate([jnp.exp2(rb[far[0], :HA]), jnp.exp2(rb[far[1], :HA])]),
        'bias_mag': jnp.max(jnp.abs(rb[:, :HA])),
    }
    bm_c = _toeplitz_tiles(lambda rel: jnp.where(in_band(rel), bias_c(rel), NEG_BIG),
                           [0, -BLOCK, -2 * BLOCK], KW_C, BLOCK)
    c_bounded = {
        'e': _toeplitz_tiles(lambda rel: jnp.where(in_band(rel), jnp.exp2(bias_c(rel)), 0.0),
                             [0, -WINDOW, -(KWB_C - QB_C)], KWB_C, QB_C),
        'bias_mag': jnp.max(jnp.abs(rb[:, HA:])),
    }
    return near_a, sc_far, bm_c, a_bounded, c_bounded


def _rope_tables(S):
    inv = 1.0 / (ROPE_BASE ** (jnp.arange(ROPE_HALF, dtype=jnp.float32) / ROPE_HALF))
    ang = jnp.arange(S, dtype=jnp.int32).astype(jnp.float32)[:, None] * inv[None, :]
    return jnp.cos(ang).T, jnp.sin(ang).T


def _logit_bounds(l, p, a_bounded, c_bounded, sa, sb, scc):
    qa, ka = p['qn_a'][l].astype(F32), p['kn_a'][l].astype(F32)
    bound_a = 1.01 * DA_QK * jnp.max(jnp.abs(qa * ka)) * sa + 2.0 * a_bounded['bias_mag']
    qb, kb = p['qn_b'][l].astype(F32), p['kn_b'][l].astype(F32)
    bound_b = 1.01 * DB_QK * jnp.max(jnp.abs(qb)) * jnp.max(jnp.abs(kb)) * sb
    qc, kc = p['qn_c'][l].astype(F32), p['kn_c'][l].astype(F32)
    bound_c = (1.01 * DC * jnp.max(jnp.abs(qc * kc)) * scc + 2.0 * c_bounded['bias_mag']
               + jnp.max(jnp.abs(p['sink_c'][l].astype(F32))) * LOG2E)
    return bound_a <= BOUND_LIMIT, bound_b <= BOUND_LIMIT, bound_c <= BOUND_LIMIT


def _prep_layer(l, p, shared):
    near_a, sc_far, bm_c, a_bounded, c_bounded = shared
    lam_init = 0.8 - 0.6 * math.exp(-0.3 * l)
    lam = (jnp.exp(jnp.sum(p['lam_q1'][l].astype(F32) * p['lam_k1'][l].astype(F32)))
           - jnp.exp(jnp.sum(p['lam_q2'][l].astype(F32) * p['lam_k2'][l].astype(F32))) + lam_init)
    w_uq = p['w_uq'][l]
    w_ukv = p['w_ukv'][l].reshape(KV_RANK, HB, DB_NOPE + DB_V)
    sa = DA_QK ** -0.5 * LOG2E
    sb = DB_QK ** -0.5 * LOG2E
    scc = DC ** -0.5 * LOG2E
    zpad = jnp.zeros((DB_PAD - DB_QK,), F32)
    lw = {
        'g_mix': p['norm_mix_g'][l].astype(F32).reshape(1, D_MODEL),
        'w_inT': p['w_in'][l].T.astype(BF16),
        'w_uqT': _pad_heads(w_uq.T, HB, DB_QK, DB_PAD).astype(BF16),
        'w_ukT': w_ukv[:, :, :DB_NOPE].reshape(KV_RANK, HB * DB_NOPE).T.astype(BF16),
        'w_uvT': w_ukv[:, :, DB_NOPE:].reshape(KV_RANK, HB * DB_V).T.astype(BF16),
        'gqa': _col(jnp.tile(p['qn_a'][l], 2 * HA) * sa),
        'gka': _col(jnp.tile(p['kn_a'][l], 2 * HA)),
        'gcq': _col(p['g_cq'][l]),
        'gckv': _col(p['g_ckv'][l]),
        'gqb': _col(jnp.tile(jnp.concatenate([p['qn_b'][l].astype(F32) * sb, zpad]), HB)),
        'gkb': _col(jnp.tile(jnp.concatenate([p['kn_b'][l].astype(F32), zpad]), HB)),
        'gqc': _col(jnp.tile(p['qn_c'][l], HC) * scc),
        'gkc': _col(jnp.tile(p['kn_c'][l], KVH_C)),
        'near_a': near_a,
        'sc_a': jnp.concatenate([sc_far, lam.reshape(1)]).astype(F32),
        'subln': _col(jnp.tile(p['subln_a'][l], HA) * (1.0 - lam_init)),
        'bm_c': bm_c,
        'sc_c': p['sink_c'][l].astype(F32) * LOG2E,
        'w_out': p['w_out'][l].astype(BF16),
        'g_ffn': p['norm_ffn_g'][l].astype(F32).reshape(1, D_MODEL),
        'w_r': jnp.pad(p['w_router'][l], ((0, 0), (0, 128 - N_EXPERTS))).astype(BF16),
        'w_gate': p['w_gate'][l].astype(BF16),
        'w_up': p['w_up'][l].astype(BF16),
        'w_down': p['w_down'][l].astype(BF16),
        'tri': jnp.triu(jnp.ones((128, 128), F32), k=1).astype(BF16),
        'near_a_bounded': a_bounded['near'],
        'sc_a_bounded': jnp.concatenate([a_bounded['sc'], lam.reshape(1)]).astype(F32),
    }
    lw['e_c'] = c_bounded['e']
    lw['sc_c_bounded'] = jnp.concatenate([lw['sc_c'], jnp.exp2(lw['sc_c'])])
    lw['bounded_a'], lw['bounded_b'], lw['bounded_c'] = _logit_bounds(l, p, a_bounded, c_bounded, sa, sb, scc)
    return lw


def _trunk(x, layers):
    B, S, D = x.shape
    x2 = x.reshape(B * S, D)
    cosT, sinT = _rope_tables(S)
    for lw in layers:
        lw = dict(lw, cosT=cosT, sinT=sinT)
        qaT, ka, vaT, qbT, kb, vbT, qcT, kc, vcT = _in_proj(x2, lw, S)
        ya = lax.cond(lw['bounded_a'],
                      lambda q, k, v: _attn_a(q, k, v, lw, B, S, True),
                      lambda q, k, v: _attn_a(q, k, v, lw, B, S, False), qaT, ka, vaT)
        yb = lax.cond(lw['bounded_b'],
                      lambda q, k, v: _attn_b(q, k, v, B, S, True),
                      lambda q, k, v: _attn_b(q, k, v, B, S, False), qbT, kb, vbT)
        yc = lax.cond(lw['bounded_c'],
                      lambda q, k, v: _attn_c(q, k, v, lw, B, S, True),
                      lambda q, k, v: _attn_c(q, k, v, lw, B, S, False), qcT, kc, vcT)
        xn, h2, affT = _out_proj(x2, ya, yb, yc, lw)
        thr, tcut = _topk_thresholds(affT)
        x2 = _moe(xn, h2, affT, thr, tcut, lw)
    return x2.reshape(B, S, D)


def kernel(x_prompt, x_sample, rel_bias, norm_mix_g, w_in, qn_a, kn_a, lam_q1, lam_k1, lam_q2, lam_k2, subln_a, g_cq, w_uq, g_ckv, w_ukv, qn_b, kn_b, qn_c, kn_c, sink_c, w_out, norm_ffn_g, w_router, w_gate, w_up, w_down):
    p = dict(norm_mix_g=norm_mix_g, w_in=w_in, qn_a=qn_a, kn_a=kn_a, lam_q1=lam_q1, lam_k1=lam_k1,
             lam_q2=lam_q2, lam_k2=lam_k2, subln_a=subln_a, g_cq=g_cq, w_uq=w_uq, g_ckv=g_ckv, w_ukv=w_ukv,
             qn_b=qn_b, kn_b=kn_b, qn_c=qn_c, kn_c=kn_c, sink_c=sink_c, w_out=w_out, norm_ffn_g=norm_ffn_g,
             w_router=w_router, w_gate=w_gate, w_up=w_up, w_down=w_down)
    shared = _prep_shared(rel_bias)
    layers = [_prep_layer(l, p, shared) for l in range(w_in.shape[0])]
    return (_trunk(x_prompt, layers), _trunk(x_sample, layers))
```
